```python
import math
import jax, jax.numpy as jnp
from jax import lax
import numpy as np

D_MODEL = 1024
BATCH = 8
SEQ = 2048
DEPTH = 4
DEC_BATCH = 16
DEC_SEQ = 16
PAST_LEN = 4096

CHUNK = 64
QBLOCK = 128
N_MIXERS = 4
ROPE_THETA = 10000.0
NORM_EPS = 1e-6

MLA_HEADS = 8
MLA_Q_LORA = 384
MLA_KV_LORA = 256
MLA_NOPE = 128
MLA_ROPE = 64
MLA_V = 128

FOX_HEADS = 16
FOX_DH = 64
FORGET_BIAS_INIT = 3.0

DIFF_HEADS = 8
DIFF_DH = 64

BAND_HEADS = 16
BAND_DH = 64
BAND_LEFT_CHUNKS = 8
BAND_LEFT = BAND_LEFT_CHUNKS * CHUNK
BAND_KEYS = BAND_LEFT + CHUNK
REL_CLIP = 128

D_FF = 2816
CONV_W = 3

kernel_name = 'hybrid_streaming_encoder_step'


def rms_norm(x, g):
    xf = x.astype(jnp.float32)
    y = xf * lax.rsqrt(jnp.mean(xf * xf, axis=-1, keepdims=True) + NORM_EPS)
    return (y * g.astype(jnp.float32)).astype(x.dtype)


def rope_tables(pos, dim):
    inv = 1.0 / (ROPE_THETA ** (jnp.arange(0, dim, 2, dtype=jnp.float32) / dim))
    ang = pos.astype(jnp.float32)[:, None] * inv[None, :]
    return jnp.cos(ang), jnp.sin(ang)


def apply_rope(x, cos, sin):
    shape = (cos.shape[0],) + (1,) * (x.ndim - 3) + (cos.shape[1],)
    c = cos.reshape(shape)
    s = sin.reshape(shape)
    x1, x2 = jnp.split(x.astype(jnp.float32), 2, axis=-1)
    return jnp.concatenate([x1 * c - x2 * s, x2 * c + x1 * s], axis=-1).astype(x.dtype)


def masked_softmax(s, allowed):
    return jax.nn.softmax(jnp.where(allowed, s, -jnp.inf), axis=-1)


def chunk_causal(q_pos, k_pos):
    return (k_pos[None, :] // CHUNK) <= (q_pos[:, None] // CHUNK)


def chunk_band(q_pos, k_pos):
    qc = q_pos[:, None] // CHUNK
    kc = k_pos[None, :] // CHUNK
    return (k_pos[None, :] >= 0) & (kc <= qc) & (kc >= qc - BAND_LEFT_CHUNKS)


def rel_position_bias(table, q_pos, k_pos):
    idx = jnp.clip(q_pos[:, None] - k_pos[None, :], -REL_CLIP, REL_CLIP) + REL_CLIP
    return jnp.take(table.astype(jnp.float32), idx, axis=1)


def over_query_blocks(fn, q_side, q_pos):
    T = q_pos.shape[0]
    if T <= QBLOCK:
        return fn(q_side, q_pos)
    nb = T // QBLOCK
    blocks = tuple(jnp.swapaxes(a.reshape(a.shape[0], nb, QBLOCK, *a.shape[2:]), 0, 1) for a in q_side)
    out = lax.map(lambda args: fn(args[0], args[1]), (blocks, q_pos.reshape(nb, QBLOCK)))
    out = jnp.swapaxes(out, 0, 1)
    return out.reshape(out.shape[0], T, *out.shape[3:])


def mla_mixer(xn, pos, past, w_dq, g_q, w_uq, w_dkv, g_kv, w_uk, w_uv, g_qn, g_qr, g_kn, g_kr, w_o):
    B, T, _ = xn.shape
    cos, sin = rope_tables(pos, MLA_ROPE)
    q = (rms_norm(xn @ w_dq, g_q) @ w_uq).reshape(B, T, MLA_HEADS, MLA_NOPE + MLA_ROPE)
    q = jnp.concatenate([rms_norm(q[..., :MLA_NOPE], g_qn),
                         apply_rope(rms_norm(q[..., MLA_NOPE:], g_qr), cos, sin)], axis=-1)
    kv = xn @ w_dkv
    ckv = rms_norm(kv[..., :MLA_KV_LORA], g_kv)
    krope = apply_rope(rms_norm(kv[..., MLA_KV_LORA:], g_kr), cos, sin)
    if past is None:
        ckv_all, krope_all, k_pos = ckv, krope, pos
    else:
        ckv_all = jnp.concatenate([past[0].astype(ckv.dtype), ckv], axis=1)
        krope_all = jnp.concatenate([past[1].astype(krope.dtype), krope], axis=1)
        k_pos = jnp.arange(ckv_all.shape[1], dtype=jnp.int32)
    S = ckv_all.shape[1]
    k_nope = rms_norm((ckv_all @ w_uk).reshape(B, S, MLA_HEADS, MLA_NOPE), g_kn)
    v = (ckv_all @ w_uv).reshape(B, S, MLA_HEADS, MLA_V)
    k = jnp.concatenate([k_nope, jnp.broadcast_to(krope_all[:, :, None, :],
                                                  (B, S, MLA_HEADS, MLA_ROPE)).astype(k_nope.dtype)], axis=-1)
    scale = (MLA_NOPE + MLA_ROPE) ** -0.5

    def block(qs, qp):
        (qb,) = qs
        s = jnp.einsum('bqhd,bkhd->bhqk', qb, k, preferred_element_type=jnp.float32) * scale
        p = masked_softmax(s, chunk_causal(qp, k_pos))
        return jnp.einsum('bhqk,bkhd->bqhd', p.astype(v.dtype), v)

    o = over_query_blocks(block, (q,), pos)
    return o.reshape(B, T, MLA_HEADS * MLA_V) @ w_o, (ckv, krope)


def fox_mixer(xn, pos, past, w_qkv, w_f, b_f, g_q, g_k, w_o):
    B, T, _ = xn.shape
    qkv = (xn @ w_qkv).reshape(B, T, 3, FOX_HEADS, FOX_DH)
    q = rms_norm(qkv[:, :, 0], g_q)
    k = rms_norm(qkv[:, :, 1], g_k)
    v = qkv[:, :, 2]
    logf = jax.nn.log_sigmoid((xn @ w_f + b_f).astype(jnp.float32)).astype(xn.dtype)
    if past is None:
        k_all, v_all, logf_all, k_pos = k, v, logf, pos
    else:
        k_all = jnp.concatenate([past[0].astype(k.dtype), k], axis=1)
        v_all = jnp.concatenate([past[1].astype(v.dtype), v], axis=1)
        logf_all = jnp.concatenate([past[2].astype(logf.dtype), logf], axis=1)
        k_pos = jnp.arange(k_all.shape[1], dtype=jnp.int32)
    F = jnp.cumsum(logf_all.astype(jnp.float32), axis=1)
    F_q = F[:, -T:]
    F_k = jnp.swapaxes(F, 1, 2)
    scale = FOX_DH ** -0.5

    def block(qs, qp):
        qb, fq = qs
        s = jnp.einsum('bqhd,bkhd->bhqk', qb, k_all, preferred_element_type=jnp.float32) * scale
        s = s + (jnp.swapaxes(fq, 1, 2)[..., None] - F_k[:, :, None, :])
        p = masked_softmax(s, k_pos[None, :] <= qp[:, None])
        return jnp.einsum('bhqk,bkhd->bqhd', p.astype(v_all.dtype), v_all)

    o = over_query_blocks(block, (q, F_q), pos)
    return o.reshape(B, T, FOX_HEADS * FOX_DH) @ w_o, (k, v, logf)


def diff_mixer(xn, pos, past, layer_idx, w_qkv, g_q, g_k, lq1, lk1, lq2, lk2, g_sub, w_o):
    B, T, _ = xn.shape
    qk_w = DIFF_HEADS * 2 * DIFF_DH
    proj = xn @ w_qkv
    q = proj[..., :qk_w].reshape(B, T, DIFF_HEADS, 2, DIFF_DH)
    k = proj[..., qk_w:2 * qk_w].reshape(B, T, DIFF_HEADS, 2, DIFF_DH)
    v = proj[..., 2 * qk_w:].reshape(B, T, DIFF_HEADS, 2 * DIFF_DH)
    cos, sin = rope_tables(pos, DIFF_DH)
    q = apply_rope(rms_norm(q, g_q), cos, sin)
    k = apply_rope(rms_norm(k, g_k), cos, sin)
    lam_init = 0.8 - 0.6 * math.exp(-0.3 * layer_idx)
    lam = (jnp.exp(jnp.sum(lq1.astype(jnp.float32) * lk1.astype(jnp.float32)))
           - jnp.exp(jnp.sum(lq2.astype(jnp.float32) * lk2.astype(jnp.float32))) + lam_init)
    if past is None:
        k_all, v_all, k_pos = k, v, pos
    else:
        k_all = jnp.concatenate([past[0].astype(k.dtype), k], axis=1)
        v_all = jnp.concatenate([past[1].astype(v.dtype), v], axis=1)
        k_pos = jnp.arange(k_all.shape[1], dtype=jnp.int32)
    scale = DIFF_DH ** -0.5

    def block(qs, qp):
        (qb,) = qs
        s = jnp.einsum('bqhcd,bkhcd->bhcqk', qb, k_all, preferred_element_type=jnp.float32) * scale
        p = masked_softmax(s, chunk_causal(qp, k_pos))
        a = p[:, :, 0] - lam * p[:, :, 1]
        return jnp.einsum('bhqk,bkhd->bqhd', a.astype(v_all.dtype), v_all)

    o = over_query_blocks(block, (q,), pos)
    o = rms_norm(o, g_sub) * (1.0 - lam_init)
    return o.reshape(B, T, DIFF_HEADS * 2 * DIFF_DH) @ w_o, (k, v)


def band_mixer(xn, pos, past, w_qkv, g_q, g_k, rel_table, w_o):
    B, T, _ = xn.shape
    qkv = (xn @ w_qkv).reshape(B, T, 3, BAND_HEADS, BAND_DH)
    q = rms_norm(qkv[:, :, 0], g_q)
    k = rms_norm(qkv[:, :, 1], g_k)
    v = qkv[:, :, 2]
    scale = BAND_DH ** -0.5

    def attend(qb, kb, vb, qp, kp):
        s = jnp.einsum('bqhd,bkhd->bhqk', qb, kb, preferred_element_type=jnp.float32) * scale
        s = s + rel_position_bias(rel_table, qp, kp)[None]
        p = masked_softmax(s, chunk_band(qp, kp))
        return jnp.einsum('bhqk,bkhd->bqhd', p.astype(vb.dtype), vb)

    if past is None:
        n_chunks = T // CHUNK
        pad = ((0, 0), (BAND_LEFT, 0), (0, 0), (0, 0))
        k_pad = jnp.pad(k, pad)
        v_pad = jnp.pad(v, pad)

        def one_chunk(c):
            start = c * CHUNK
            qb = lax.dynamic_slice_in_dim(q, start, CHUNK, axis=1)
            kb = lax.dynamic_slice_in_dim(k_pad, start, BAND_KEYS, axis=1)
            vb = lax.dynamic_slice_in_dim(v_pad, start, BAND_KEYS, axis=1)
            qp = start + jnp.arange(CHUNK, dtype=jnp.int32)
            kp = start - BAND_LEFT + jnp.arange(BAND_KEYS, dtype=jnp.int32)
            return attend(qb, kb, vb, qp, kp)

        o = lax.map(one_chunk, jnp.arange(n_chunks, dtype=jnp.int32))
        o = jnp.swapaxes(o, 0, 1).reshape(B, T, BAND_HEADS, BAND_DH)
        keep = min(BAND_LEFT, T)
        new_k, new_v = k[:, T - keep:], v[:, T - keep:]
    else:
        W = past[0].shape[1]
        k_all = jnp.concatenate([past[0].astype(k.dtype), k], axis=1)
        v_all = jnp.concatenate([past[1].astype(v.dtype), v], axis=1)
        kp = (pos[0] - W) + jnp.arange(W + T, dtype=jnp.int32)
        o = attend(q, k_all, v_all, pos, kp)
        new_k, new_v = k_all[:, T:], v_all[:, T:]
    return o.reshape(B, T, BAND_HEADS * BAND_DH) @ w_o, (new_k, new_v)


def conv_ffn(xn, buf, w_up, conv_w, conv_b, w_down):
    B, T, _ = xn.shape
    u = xn @ w_up
    if buf is None:
        buf = jnp.zeros((B, CONV_W - 1, u.shape[-1]), u.dtype)
    ue = jnp.concatenate([buf.astype(u.dtype), u], axis=1)
    c = conv_b + ue[:, CONV_W - 1:] * conv_w[CONV_W - 1]
    for j in range(CONV_W - 1):
        c = c + ue[:, j:j + T] * conv_w[j]
    g, h = jnp.split(c, 2, axis=-1)
    return (jax.nn.silu(g) * h) @ w_down, ue[:, T:]


def setup_inputs(seed: int = 0) -> dict:
    key = jax.random.key(seed)
    keys = iter(jax.random.split(key, 64))

    def normal(shape, scale=1.0):
        return scale * jax.random.normal(next(keys), shape, jnp.float32)

    def dense(shape):
        return normal(shape, shape[-2] ** -0.5)

    def gain(shape):
        return 1.0 + normal(shape, 0.05)

    band_w = min(BAND_LEFT, PAST_LEN)
    d = {}
    d['x_prompt'] = normal((BATCH, SEQ, D_MODEL))
    d['x_sample'] = normal((DEC_BATCH, DEC_SEQ, D_MODEL))
    d['cache_mla_ckv'] = normal((DEC_BATCH, PAST_LEN, MLA_KV_LORA))
    d['cache_mla_krope'] = normal((DEC_BATCH, PAST_LEN, MLA_ROPE))
    d['cache_fox_k'] = normal((DEC_BATCH, PAST_LEN, FOX_HEADS, FOX_DH))
    d['cache_fox_v'] = normal((DEC_BATCH, PAST_LEN, FOX_HEADS, FOX_DH))
    d['cache_fox_logf'] = jax.nn.log_sigmoid(FORGET_BIAS_INIT + normal((DEC_BATCH, PAST_LEN, FOX_HEADS)))
    d['cache_diff_k'] = normal((DEC_BATCH, PAST_LEN, DIFF_HEADS, 2, DIFF_DH))
    d['cache_diff_v'] = normal((DEC_BATCH, PAST_LEN, DIFF_HEADS, 2 * DIFF_DH))
    d['cache_band_k'] = normal((DEC_BATCH, band_w, BAND_HEADS, BAND_DH))
    d['cache_band_v'] = normal((DEC_BATCH, band_w, BAND_HEADS, BAND_DH))
    d['state_ffn_conv'] = normal((DEPTH, DEC_BATCH, CONV_W - 1, 2 * D_FF), 0.5)
    d['attn_norm_g'] = gain((DEPTH, D_MODEL))
    d['ffn_norm_g'] = gain((DEPTH, D_MODEL))
    d['final_norm_g'] = gain((D_MODEL,))
    d['mla_w_dq'] = dense((D_MODEL, MLA_Q_LORA))
    d['mla_g_q'] = gain((MLA_Q_LORA,))
    d['mla_w_uq'] = dense((MLA_Q_LORA, MLA_HEADS * (MLA_NOPE + MLA_ROPE)))
    d['mla_w_dkv'] = dense((D_MODEL, MLA_KV_LORA + MLA_ROPE))
    d['mla_g_kv'] = gain((MLA_KV_LORA,))
    d['mla_w_uk'] = dense((MLA_KV_LORA, MLA_HEADS * MLA_NOPE))
    d['mla_w_uv'] = dense((MLA_KV_LORA, MLA_HEADS * MLA_V))
    d['mla_g_qn'] = gain((MLA_NOPE,))
    d['mla_g_qr'] = gain((MLA_ROPE,))
    d['mla_g_kn'] = gain((MLA_NOPE,))
    d['mla_g_kr'] = gain((MLA_ROPE,))
    d['mla_w_o'] = dense((MLA_HEADS * MLA_V, D_MODEL))
    d['fox_w_qkv'] = dense((D_MODEL, 3 * FOX_HEADS * FOX_DH))
    d['fox_w_f'] = dense((D_MODEL, FOX_HEADS))
    d['fox_b_f'] = FORGET_BIAS_INIT + normal((FOX_HEADS,), 0.1)
    d['fox_g_q'] = gain((FOX_DH,))
    d['fox_g_k'] = gain((FOX_DH,))
    d['fox_w_o'] = dense((FOX_HEADS * FOX_DH, D_MODEL))
    d['diff_w_qkv'] = dense((D_MODEL, 3 * DIFF_HEADS * 2 * DIFF_DH))
    d['diff_g_q'] = gain((DIFF_DH,))
    d['diff_g_k'] = gain((DIFF_DH,))
    d['diff_lq1'] = normal((DIFF_DH,), 0.1)
    d['diff_lk1'] = normal((DIFF_DH,), 0.1)
    d['diff_lq2'] = normal((DIFF_DH,), 0.1)
    d['diff_lk2'] = normal((DIFF_DH,), 0.1)
    d['diff_g_sub'] = gain((2 * DIFF_DH,))
    d['diff_w_o'] = dense((DIFF_HEADS * 2 * DIFF_DH, D_MODEL))
    d['band_w_qkv'] = dense((D_MODEL, 3 * BAND_HEADS * BAND_DH))
    d['band_g_q'] = gain((BAND_DH,))
    d['band_g_k'] = gain((BAND_DH,))
    d['band_rel_bias'] = normal((BAND_HEADS, 2 * REL_CLIP + 1), 0.5)
    d['band_w_o'] = dense((BAND_HEADS * BAND_DH, D_MODEL))
    d['ffn_w_up'] = dense((DEPTH, D_MODEL, 2 * D_FF))
    d['ffn_conv_w'] = normal((DEPTH, CONV_W, 2 * D_FF), CONV_W ** -0.5)
    d['ffn_conv_b'] = normal((DEPTH, 2 * D_FF), 0.02)
    d['ffn_w_down'] = dense((DEPTH, D_FF, D_MODEL))
    return d


def reference(x_prompt, x_sample, cache_mla_ckv, cache_mla_krope, cache_fox_k, cache_fox_v, cache_fox_logf,
              cache_diff_k, cache_diff_v, cache_band_k, cache_band_v, state_ffn_conv,
              attn_norm_g, ffn_norm_g, final_norm_g,
              mla_w_dq, mla_g_q, mla_w_uq, mla_w_dkv, mla_g_kv, mla_w_uk, mla_w_uv,
              mla_g_qn, mla_g_qr, mla_g_kn, mla_g_kr, mla_w_o,
              fox_w_qkv, fox_w_f, fox_b_f, fox_g_q, fox_g_k, fox_w_o,
              diff_w_qkv, diff_g_q, diff_g_k, diff_lq1, diff_lk1, diff_lq2, diff_lk2, diff_g_sub, diff_w_o,
              band_w_qkv, band_g_q, band_g_k, band_rel_bias, band_w_o,
              ffn_w_up, ffn_conv_w, ffn_conv_b, ffn_w_down):
    past_len = cache_mla_ckv.shape[1]
    pos_p = jnp.arange(x_prompt.shape[1], dtype=jnp.int32)
    pos_s = past_len + jnp.arange(x_sample.shape[1], dtype=jnp.int32)
    mla_w = (mla_w_dq, mla_g_q, mla_w_uq, mla_w_dkv, mla_g_kv, mla_w_uk, mla_w_uv,
             mla_g_qn, mla_g_qr, mla_g_kn, mla_g_kr, mla_w_o)
    fox_w = (fox_w_qkv, fox_w_f, fox_b_f, fox_g_q, fox_g_k, fox_w_o)
    diff_w = (diff_w_qkv, diff_g_q, diff_g_k, diff_lq1, diff_lk1, diff_lq2, diff_lk2, diff_g_sub, diff_w_o)
    band_w = (band_w_qkv, band_g_q, band_g_k, band_rel_bias, band_w_o)

    h_p, h_s = x_prompt, x_sample
    conv_p, conv_s = [], []
    for i in range(DEPTH):
        a_p = rms_norm(h_p, attn_norm_g[i])
        a_s = rms_norm(h_s, attn_norm_g[i])
        kind = i % N_MIXERS
        if kind == 0:
            m_p, (mla_ckv_p, mla_krope_p) = mla_mixer(a_p, pos_p, None, *mla_w)
            m_s, (mla_ckv_s, mla_krope_s) = mla_mixer(a_s, pos_s, (cache_mla_ckv, cache_mla_krope), *mla_w)
        elif kind == 1:
            m_p, (fox_k_p, fox_v_p, fox_logf_p) = fox_mixer(a_p, pos_p, None, *fox_w)
            m_s, (fox_k_s, fox_v_s, fox_logf_s) = fox_mixer(
                a_s, pos_s, (cache_fox_k, cache_fox_v, cache_fox_logf), *fox_w)
        elif kind == 2:
            m_p, (diff_k_p, diff_v_p) = diff_mixer(a_p, pos_p, None, i, *diff_w)
            m_s, (diff_k_s, diff_v_s) = diff_mixer(a_s, pos_s, (cache_diff_k, cache_diff_v), i, *diff_w)
        else:
            m_p, (band_k_p, band_v_p) = band_mixer(a_p, pos_p, None, *band_w)
            m_s, (band_k_s, band_v_s) = band_mixer(a_s, pos_s, (cache_band_k, cache_band_v), *band_w)
        h_p = h_p + m_p
        h_s = h_s + m_s
        f_p, b_p = conv_ffn(rms_norm(h_p, ffn_norm_g[i]), None,
                            ffn_w_up[i], ffn_conv_w[i], ffn_conv_b[i], ffn_w_down[i])
        f_s, b_s = conv_ffn(rms_norm(h_s, ffn_norm_g[i]), state_ffn_conv[i],
                            ffn_w_up[i], ffn_conv_w[i], ffn_conv_b[i], ffn_w_down[i])
        h_p = h_p + f_p
        h_s = h_s + f_s
        conv_p.append(b_p)
        conv_s.append(b_s)

    y_prompt = rms_norm(h_p, final_norm_g)
    y_sample = rms_norm(h_s, final_norm_g)
    ffn_conv_p = jnp.stack(conv_p, axis=0)
    ffn_conv_s = jnp.stack(conv_s, axis=0)
    return (y_prompt, y_sample,
            mla_ckv_p, mla_krope_p, mla_ckv_s, mla_krope_s,
            fox_k_p, fox_v_p, fox_logf_p, fox_k_s, fox_v_s, fox_logf_s,
            diff_k_p, diff_v_p, diff_k_s, diff_v_s,
            band_k_p, band_v_p, band_k_s, band_v_s,
            ffn_conv_p, ffn_conv_s)
```

```python
import functools
import math

import jax
import jax.numpy as jnp
from jax import lax
from jax.experimental import pallas as pl
from jax.experimental.pallas import tpu as pltpu

F32 = jnp.float32
BF16 = jnp.bfloat16

D_MODEL = 1024
CHUNK = 64
ROPE_THETA = 10000.0
NORM_EPS = 1e-6
MLA_HEADS, MLA_Q_LORA, MLA_KV_LORA, MLA_NOPE, MLA_ROPE, MLA_V = 8, 384, 256, 128, 64, 128
FOX_HEADS, FOX_DH = 16, 64
DIFF_HEADS, DIFF_DH = 8, 64
BAND_HEADS, BAND_DH = 16, 64
BAND_LEFT_CHUNKS = 8
BAND_LEFT = BAND_LEFT_CHUNKS * CHUNK
REL_CLIP = 128
D_FF = 2816
CONV_W = 3

LANES = 128
SUBLANES = 8
NEG = -1e30
REL_PAD = 384
VMEM_LIMIT = 56 * 1024 * 1024


def _cp(n_axes):
    return pltpu.CompilerParams(dimension_semantics=("arbitrary",) * n_axes, vmem_limit_bytes=VMEM_LIMIT)


def _dot(a, b):
    return jnp.dot(a, b, preferred_element_type=F32)


def _dot_nt(a, b):
    return lax.dot_general(a, b, (((1,), (1,)), ((), ())), preferred_element_type=F32)


def _rms(x, g):
    ms = jnp.mean(x * x, axis=-1, keepdims=True)
    return x * lax.rsqrt(ms + NORM_EPS) * g


def _lane_iota(shape):
    return lax.broadcasted_iota(jnp.int32, shape, len(shape) - 1)


def _row_iota(shape):
    return lax.broadcasted_iota(jnp.int32, shape, len(shape) - 2)


def _head_rms(x, g, seg):
    n = x.shape[1]
    outs = []
    for c in range(n // LANES):
        xs = x[:, LANES * c:LANES * (c + 1)]
        sq = xs * xs
        if seg == LANES:
            r = lax.rsqrt(jnp.sum(sq, axis=-1, keepdims=True) * (1.0 / seg) + NORM_EPS)
        else:
            lo = _lane_iota(xs.shape) < seg
            s_lo = jnp.sum(jnp.where(lo, sq, 0.0), axis=-1, keepdims=True)
            s_hi = jnp.sum(jnp.where(lo, 0.0, sq), axis=-1, keepdims=True)
            r = jnp.where(lo, lax.rsqrt(s_lo * (1.0 / seg) + NORM_EPS), lax.rsqrt(s_hi * (1.0 / seg) + NORM_EPS))
        outs.append(xs * r * g[:, LANES * c:LANES * (c + 1)])
    return jnp.concatenate(outs, axis=-1) if len(outs) > 1 else outs[0]


def _rope(x, cos, sinp):
    lo32 = (_lane_iota((x.shape[0], LANES)) & 63) < 32
    outs = []
    for c in range(x.shape[1] // LANES):
        xs = x[:, LANES * c:LANES * (c + 1)]
        fwd = pltpu.roll(xs, 32, axis=1)
        bwd = pltpu.roll(xs, 96, axis=1)
        outs.append(xs * cos + jnp.where(lo32, bwd, fwd) * sinp)
    return jnp.concatenate(outs, axis=-1) if len(outs) > 1 else outs[0]


def _log2(n):
    assert n & (n - 1) == 0, n
    return n.bit_length() - 1


def _split3(x):
    hi = x.astype(BF16)
    r = x - hi.astype(F32)
    mid = r.astype(BF16)
    lo = (r - mid.astype(F32)).astype(BF16)
    return hi, mid, lo


def _softmax_update(s, v, m_ref, l_ref, acc_ref):
    m_old = m_ref[...]
    m_new = jnp.maximum(m_old, jnp.max(s, axis=-1, keepdims=True))
    alpha = jnp.exp(m_old - m_new)
    p = jnp.exp(s - m_new)
    l_ref[...] = alpha * l_ref[...] + jnp.sum(p, axis=-1, keepdims=True)
    acc_ref[...] = alpha * acc_ref[...] + _dot(p.astype(BF16), v)
    m_ref[...] = m_new


def _qkv_proj_kernel(*refs, rope, logf, scale):
    it = iter(refs)
    h_ref, g_ref, w_ref, gq_ref, gk_ref = next(it), next(it), next(it), next(it), next(it)
    cos_ref = sin_ref = wf_ref = bf_ref = None
    if rope:
        cos_ref, sin_ref = next(it), next(it)
    if logf:
        wf_ref, bf_ref = next(it), next(it)
    q_out, k_out, v_out = next(it), next(it), next(it)
    a = _rms(h_ref[...], g_ref[...]).astype(BF16)
    qkv = _dot(a, w_ref[...])
    q = _head_rms(qkv[:, :D_MODEL], gq_ref[...], 64)
    k = _head_rms(qkv[:, D_MODEL:2 * D_MODEL], gk_ref[...], 64)
    if rope:
        q = _rope(q, cos_ref[...], sin_ref[...])
        k = _rope(k, cos_ref[...], sin_ref[...])
    q_out[...] = (q * scale).astype(BF16)
    k_out[...] = k
    v_out[...] = qkv[:, 2 * D_MODEL:]
    if logf:
        logf_out = next(it)
        z = _dot(a, wf_ref[...]) + bf_ref[...]
        lf = jnp.minimum(z, 0.0) - jnp.log1p(jnp.exp(-jnp.abs(z)))
        logf_out[...] = lf[:, :FOX_HEADS]


def _qkv_proj(h, g, w, gq, gk, *, tm, rope_tabs=None, wf=None, bf=None, scale, name):
    m = h.shape[0]
    grid = (m // tm,)
    row = lambda n: pl.BlockSpec((tm, n), lambda i: (i, 0))
    full = lambda a: pl.BlockSpec(a.shape, lambda i: (0,) * a.ndim)
    ins = [h, g, w, gq, gk]
    specs = [row(D_MODEL), full(g), full(w), full(gq), full(gk)]
    if rope_tabs is not None:
        cos, sinp = rope_tabs
        nt = cos.shape[0] // tm
        tab = pl.BlockSpec((tm, LANES), lambda i: (i % nt, 0))
        ins += [cos, sinp]
        specs += [tab, tab]
    if wf is not None:
        ins += [wf, bf]
        specs += [full(wf), full(bf)]
    outs = [jax.ShapeDtypeStruct((m, D_MODEL), BF16), jax.ShapeDtypeStruct((m, D_MODEL), F32),
            jax.ShapeDtypeStruct((m, D_MODEL), F32)]
    ospecs = [row(D_MODEL), row(D_MODEL), row(D_MODEL)]
    if wf is not None:
        outs.append(jax.ShapeDtypeStruct((m, FOX_HEADS), F32))
        ospecs.append(row(FOX_HEADS))
    kern = functools.partial(_qkv_proj_kernel, rope=rope_tabs is not None, logf=wf is not None, scale=scale)
    return pl.pallas_call(kern, out_shape=outs, grid=grid, in_specs=specs, out_specs=ospecs,
                          compiler_params=_cp(1), name=name)(*ins)


def _mla_proj_kernel(*refs, with_kv, scale):
    it = iter(refs)
    (h_ref, g_ref, wdq_ref, gq_ref, wuq_ref, wc_ref, wr_ref, gkv_ref, gqn_ref, gqr_ref, gkr_ref,
     cos_ref, sin_ref) = [next(it) for _ in range(13)]
    if with_kv:
        wuk_ref, wuv_ref, gkn_ref = next(it), next(it), next(it)
    q_out, ckv_out, kr_out, krd_out = next(it), next(it), next(it), next(it)
    a = _rms(h_ref[...], g_ref[...]).astype(BF16)
    cq = _rms(_dot(a, wdq_ref[...]), gq_ref[...]).astype(BF16)
    q = _dot(cq, wuq_ref[...])
    n_nope = MLA_HEADS * MLA_NOPE
    qn = _head_rms(q[:, :n_nope], gqn_ref[...], MLA_NOPE) * scale
    qr = _rope(_head_rms(q[:, n_nope:], gqr_ref[...], MLA_ROPE), cos_ref[...], sin_ref[...]) * scale
    lo = _lane_iota((q.shape[0], LANES)) < MLA_ROPE
    pieces = []
    for hd in range(MLA_HEADS):
        slab = qr[:, LANES * (hd // 2):LANES * (hd // 2 + 1)]
        keep = lo if hd % 2 == 0 else jnp.logical_not(lo)
        pieces += [qn[:, LANES * hd:LANES * (hd + 1)], jnp.where(keep, slab, 0.0)]
    q_out[...] = jnp.concatenate(pieces, axis=-1).astype(BF16)
    ckv = _rms(_dot(a, wc_ref[...]), gkv_ref[...])
    ckv_out[...] = ckv
    kr2 = _rope(_head_rms(_dot(a, wr_ref[...]), gkr_ref[...], MLA_ROPE), cos_ref[...], sin_ref[...])
    kr_out[...] = kr2[:, :MLA_ROPE]
    krd_out[...] = kr2
    if with_kv:
        kc_out, v_out = next(it), next(it)
        c = ckv.astype(BF16)
        kn = _head_rms(_dot(c, wuk_ref[...]), gkn_ref[...], MLA_NOPE)
        pieces = []
        for hd in range(MLA_HEADS):
            pieces += [kn[:, LANES * hd:LANES * (hd + 1)], kr2]
        kc_out[...] = jnp.concatenate(pieces, axis=-1).astype(BF16)
        v_out[...] = _dot(c, wuv_ref[...]).astype(BF16)


def _mla_proj(h, g, w, rope_tabs, *, tm, with_kv, scale, name):
    m = h.shape[0]
    cos, sinp = rope_tabs
    nt = cos.shape[0] // tm
    row = lambda n: pl.BlockSpec((tm, n), lambda i: (i, 0))
    full = lambda a: pl.BlockSpec(a.shape, lambda i: (0,) * a.ndim)
    tab = pl.BlockSpec((tm, LANES), lambda i: (i % nt, 0))
    ins = [h, g, w["wdq"], w["gq"], w["wuq"], w["wc"], w["wr"], w["gkv"], w["gqn"], w["gqr"], w["gkr"], cos, sinp]
    specs = [row(D_MODEL)] + [full(x) for x in ins[1:11]] + [tab, tab]
    if with_kv:
        ins += [w["wuk"], w["wuv"], w["gkn"]]
        specs += [full(w["wuk"]), full(w["wuv"]), full(w["gkn"])]
    qw = MLA_HEADS * 2 * LANES
    outs = [jax.ShapeDtypeStruct((m, qw), BF16), jax.ShapeDtypeStruct((m, MLA_KV_LORA), F32),
            jax.ShapeDtypeStruct((m, MLA_ROPE), F32), jax.ShapeDtypeStruct((m, LANES), F32)]
    ospecs = [row(qw), row(MLA_KV_LORA), row(MLA_ROPE), row(LANES)]
    if with_kv:
        outs += [jax.ShapeDtypeStruct((m, qw), BF16), jax.ShapeDtypeStruct((m, MLA_HEADS * MLA_V), BF16)]
        ospecs += [row(qw), row(MLA_HEADS * MLA_V)]
    kern = functools.partial(_mla_proj_kernel, with_kv=with_kv, scale=scale)
    return pl.pallas_call(kern, out_shape=outs, grid=(m // tm,), in_specs=specs, out_specs=ospecs,
                          compiler_params=_cp(1), name=name)(*ins)


def _cumsum_kernel(x_ref, o_ref, carry_ref):
    @pl.when(pl.program_id(1) == 0)
    def _():
        carry_ref[...] = jnp.zeros_like(carry_ref)

    x = x_ref[...]
    tk = x.shape[1]
    hi, mid, lo = _split3(x)
    tri = (_row_iota((tk, tk)) <= _lane_iota((tk, tk))).astype(BF16)
    y = _dot(jnp.concatenate([hi, mid, lo], axis=0), tri)
    nh = x.shape[0]
    f = (y[:nh] + y[nh:2 * nh]) + y[2 * nh:] + carry_ref[...]
    o_ref[...] = f
    carry_ref[...] = f[:, tk - 1:tk]


def _cumsum_last(x, *, tk, name):
    b, nh, t = x.shape
    spec = pl.BlockSpec((None, nh, tk), lambda i, j: (i, 0, j))
    return pl.pallas_call(_cumsum_kernel, out_shape=jax.ShapeDtypeStruct(x.shape, F32), grid=(b, t // tk),
                          in_specs=[spec], out_specs=spec, scratch_shapes=[pltpu.VMEM((nh, 1), F32)],
                          compiler_params=_cp(2), name=name)(x)


def _mla_attn_kernel(q_ref, k_ref, v_ref, o_ref, m_ref, l_ref, acc_ref, *, tq, tk):
    qi = pl.program_id(2)
    q = q_ref[...]
    m_ref[...] = jnp.full_like(m_ref, NEG)
    l_ref[...] = jnp.zeros_like(l_ref)
    acc_ref[...] = jnp.zeros_like(acc_ref)
    limit = (((qi * tq + _row_iota((tq, 1))) >> 6) + 1) << 6

    def body(kt, c):
        off = pl.multiple_of(kt * tk, tk)
        s = _dot_nt(q, k_ref[pl.ds(off, tk), :])
        s = jnp.where(off + _lane_iota((1, tk)) < limit, s, NEG)
        _softmax_update(s, v_ref[pl.ds(off, tk), :], m_ref, l_ref, acc_ref)
        return c

    lax.fori_loop(0, (qi * tq + tq + tk - 1) // tk, body, 0)
    o_ref[...] = (acc_ref[...] / l_ref[...]).astype(BF16)


def _mla_attn(q, kc, v, *, b, t, tq, tk, name):
    nq = t // tq
    kw = 2 * LANES
    return pl.pallas_call(
        functools.partial(_mla_attn_kernel, tq=tq, tk=tk),
        out_shape=jax.ShapeDtypeStruct((b * t, MLA_HEADS * MLA_V), BF16),
        grid=(b, MLA_HEADS, nq),
        in_specs=[pl.BlockSpec((tq, kw), lambda i, h, j: (i * nq + j, h)),
                  pl.BlockSpec((t, kw), lambda i, h, j: (i, h)),
                  pl.BlockSpec((t, MLA_V), lambda i, h, j: (i, h))],
        out_specs=pl.BlockSpec((tq, MLA_V), lambda i, h, j: (i * nq + j, h)),
        scratch_shapes=[pltpu.VMEM((tq, 1), F32), pltpu.VMEM((tq, 1), F32), pltpu.VMEM((tq, MLA_V), F32)],
        compiler_params=_cp(3), name=name)(q, kc, v)


def _pair_attn_kernel(*refs, mode, tq, tk, lam_init):
    if mode == "fox":
        q_ref, k_ref, v_ref, fq_ref, fk_ref, o_ref, kb, vb, m0, l0, a0, m1, l1, a1 = refs
    else:
        q_ref, k_ref, v_ref, lam_ref, gsub_ref, o_ref, kb, vb, m0, l0, a0, m1, l1, a1 = refs
    pr = pl.program_id(1)
    qi = pl.program_id(2)

    @pl.when(qi == 0)
    def _():
        kb[...] = k_ref[...].astype(BF16)
        vb[...] = v_ref[...].astype(BF16)

    q = q_ref[...]
    lo = _lane_iota((tq, LANES)) < 64
    zero = jnp.zeros_like(q)
    q0 = jnp.where(lo, q, zero)
    q1 = jnp.where(lo, zero, q)
    qpos = qi * tq + _row_iota((tq, 1))
    if mode == "fox":
        limit = qpos + 1
        fq = fq_ref[...]
        hl = _lane_iota(fq.shape)
        fq0 = jnp.sum(jnp.where(hl == 2 * pr, fq, 0.0), axis=-1, keepdims=True)
        fq1 = jnp.sum(jnp.where(hl == 2 * pr + 1, fq, 0.0), axis=-1, keepdims=True)
    else:
        limit = ((qpos >> 6) + 1) << 6
    for r in (m0, m1):
        r[...] = jnp.full_like(r, NEG)
    for r in (l0, l1, a0, a1):
        r[...] = jnp.zeros_like(r)

    def body(kt, c):
        off = pl.multiple_of(kt * tk, tk)
        k = kb[pl.ds(off, tk), :]
        v = vb[pl.ds(off, tk), :]
        allowed = off + _lane_iota((1, tk)) < limit
        s0 = _dot_nt(q0, k)
        s1 = _dot_nt(q1, k)
        if mode == "fox":
            fk = fk_ref[:, pl.ds(off, tk)]
            s0 = s0 + (fq0 - fk[0:1])
            s1 = s1 + (fq1 - fk[1:2])
        _softmax_update(jnp.where(allowed, s0, NEG), v, m0, l0, a0)
        _softmax_update(jnp.where(allowed, s1, NEG), v, m1, l1, a1)
        return c

    lax.fori_loop(0, (qi * tq + tq + tk - 1) // tk, body, 0)
    o0 = a0[...] / l0[...]
    o1 = a1[...] / l1[...]
    if mode == "fox":
        o = jnp.where(lo, o0, o1)
    else:
        lv = lam_ref[...]
        lam = (jnp.exp(jnp.sum(lv[0:1] * lv[1:2], axis=-1, keepdims=True))
               - jnp.exp(jnp.sum(lv[2:3] * lv[3:4], axis=-1, keepdims=True)) + lam_init)
        o = _rms(o0 - lam * o1, gsub_ref[...]) * (1.0 - lam_init)
    o_ref[...] = o.astype(BF16)


def _pair_attn(q, k, v, extra, *, mode, b, t, tq, tk, lam_init=0.0, name):
    nq = t // tq
    npair = D_MODEL // LANES
    qspec = pl.BlockSpec((tq, LANES), lambda i, p, j: (i * nq + j, p))
    kvspec = pl.BlockSpec((t, LANES), lambda i, p, j: (i, p))
    if mode == "fox":
        fq, fk = extra
        especs = [pl.BlockSpec((tq, FOX_HEADS), lambda i, p, j: (i * nq + j, 0)),
                  pl.BlockSpec((None, None, 2, t), lambda i, p, j: (i, p, 0, 0))]
    else:
        lamv, gsub = extra
        especs = [pl.BlockSpec(lamv.shape, lambda i, p, j: (0, 0)), pl.BlockSpec(gsub.shape, lambda i, p, j: (0, 0))]
    stat = lambda: pltpu.VMEM((tq, 1), F32)
    accs = lambda: pltpu.VMEM((tq, LANES), F32)
    return pl.pallas_call(
        functools.partial(_pair_attn_kernel, mode=mode, tq=tq, tk=tk, lam_init=lam_init),
        out_shape=jax.ShapeDtypeStruct((b * t, D_MODEL), BF16),
        grid=(b, npair, nq),
        in_specs=[qspec, kvspec, kvspec] + especs,
        out_specs=qspec,
        scratch_shapes=[pltpu.VMEM((t, LANES), BF16), pltpu.VMEM((t, LANES), BF16),
                        stat(), stat(), accs(), stat(), stat(), accs()],
        compiler_params=_cp(3), name=name)(q, k, v, *extra)


def _rel_gather(tab, width, center):
    idx = jnp.clip(center - _lane_iota((REL_PAD, width)), -REL_CLIP, REL_CLIP) + REL_CLIP
    onehot = (_row_iota((REL_PAD, width)) == idx).astype(BF16)
    hi, mid, lo = _split3(tab)
    return (_dot(hi, onehot) + _dot(mid, onehot)) + _dot(lo, onehot)


def _band_attn_kernel(q_ref, k_ref, v_ref, tab_ref, o_ref, kb, vb, bias_ref, *, tq, win, bw):
    b = pl.program_id(1)
    qi = pl.program_id(2)
    gw = bw + tq

    @pl.when((b == 0) & (qi == 0))
    def _():
        g = _rel_gather(tab_ref[...], gw, BAND_LEFT + tq)
        ii = _row_iota((tq, bw)) >> 6
        jj = _lane_iota((tq, bw)) >> 6
        allowed = (jj >= ii) & (jj <= ii + BAND_LEFT_CHUNKS)
        for c in range(2):
            rows = jnp.broadcast_to(g[c:c + 1, :], (tq, gw))
            skew = pltpu.roll(rows, gw - tq, axis=1, stride=1, stride_axis=0)
            bias_ref[c] = jnp.where(allowed, skew[:, :bw], NEG)

    @pl.when(qi == 0)
    def _():
        kb[...] = k_ref[...].astype(BF16)
        vb[...] = v_ref[...].astype(BF16)

    q0 = qi * tq
    ws = pl.multiple_of(jnp.maximum(q0 - BAND_LEFT, 0), tq)
    d = pl.multiple_of(BAND_LEFT - q0 + ws, LANES)
    k = kb[pl.ds(ws, win), :]
    v = vb[pl.ds(ws, win), :]
    q = q_ref[...]
    lo = _lane_iota((tq, LANES)) < 64
    zero = jnp.zeros_like(q)
    outs = []
    for c in range(2):
        qc = jnp.where(lo, q, zero) if c == 0 else jnp.where(lo, zero, q)
        s = _dot_nt(qc, k) + bias_ref[c, :, pl.ds(d, win)]
        p = jnp.exp(s - jnp.max(s, axis=-1, keepdims=True))
        outs.append(_dot(p.astype(BF16), v) / jnp.sum(p, axis=-1, keepdims=True))
    o_ref[...] = jnp.where(lo, outs[0], outs[1]).astype(BF16)


def _band_attn(q, k, v, tab, *, b, t, tq, name):
    nq = t // tq
    npair = D_MODEL // LANES
    win = BAND_LEFT + tq
    bw = win + BAND_LEFT
    assert t >= win and tq % CHUNK == 0 and BAND_LEFT % tq == 0
    qspec = pl.BlockSpec((tq, LANES), lambda p, i, j: (i * nq + j, p))
    kvspec = pl.BlockSpec((t, LANES), lambda p, i, j: (i, p))
    return pl.pallas_call(
        functools.partial(_band_attn_kernel, tq=tq, win=win, bw=bw),
        out_shape=jax.ShapeDtypeStruct((b * t, D_MODEL), BF16),
        grid=(npair, b, nq),
        in_specs=[qspec, kvspec, kvspec, pl.BlockSpec((None, SUBLANES, REL_PAD), lambda p, i, j: (p, 0, 0))],
        out_specs=qspec,
        scratch_shapes=[pltpu.VMEM((t, LANES), BF16), pltpu.VMEM((t, LANES), BF16), pltpu.VMEM((2, tq, bw), F32)],
        compiler_params=_cp(3), name=name)(q, k, v, tab)


def _block_diag_q(q, nh, width):
    tq = q.shape[0]
    rep = jnp.concatenate([q] * nh, axis=0)
    keep = (_row_iota(rep.shape) >> _log2(tq)) == (_lane_iota(rep.shape) >> _log2(width))
    return jnp.where(keep, rep, jnp.zeros_like(rep))


def _dec_attn_kernel(*refs, mode, n_cache, tq, past, lam_init):
    if mode == "fox":
        (q_ref, kc_ref, vc_ref, kn_ref, vn_ref, fkc_ref, fkn_ref, fq_ref, o_ref, qb, m_ref, l_ref, acc_ref) = refs
    else:
        (q_ref, kc_ref, vc_ref, kn_ref, vn_ref, lam_ref, gsub_ref, o_ref, qb, m_ref, l_ref, acc_ref) = refs
    t = pl.program_id(1)
    nh = D_MODEL // 64
    rows = nh * tq

    @pl.when(t == 0)
    def _():
        qb[...] = _block_diag_q(q_ref[...], nh, 64)
        m_ref[...] = jnp.full_like(m_ref, NEG)
        l_ref[...] = jnp.zeros_like(l_ref)
        acc_ref[...] = jnp.zeros_like(acc_ref)

    def expand(f):
        return jnp.concatenate([jnp.broadcast_to(f[h:h + 1, :], (tq, f.shape[1])) for h in range(nh)], axis=0)

    @pl.when(t < n_cache)
    def _():
        s = _dot_nt(qb[...], kc_ref[...].astype(BF16))
        if mode == "fox":
            s = s + (fq_ref[...] - expand(fkc_ref[...]))
        _softmax_update(s, vc_ref[...].astype(BF16), m_ref, l_ref, acc_ref)

    @pl.when(t == n_cache)
    def _():
        s = _dot_nt(qb[...], kn_ref[...].astype(BF16))
        qpos = past + (_row_iota((rows, tq)) & (tq - 1))
        kpos = past + _lane_iota((rows, tq))
        if mode == "fox":
            s = s + (fq_ref[...] - expand(fkn_ref[...][:, :tq]))
            allowed = kpos <= qpos
        else:
            allowed = (kpos >> 6) <= (qpos >> 6)
        _softmax_update(jnp.where(allowed, s, NEG), vn_ref[...].astype(BF16), m_ref, l_ref, acc_ref)
        o_all = acc_ref[...] / l_ref[...]
        if mode == "fox":
            o = jnp.zeros((tq, D_MODEL), F32)
            hl = _lane_iota((tq, D_MODEL)) >> 6
            for h in range(nh):
                o = jnp.where(hl == h, o_all[h * tq:(h + 1) * tq, :], o)
        else:
            lv = lam_ref[...]
            lam = (jnp.exp(jnp.sum(lv[0:1] * lv[1:2], axis=-1, keepdims=True))
                   - jnp.exp(jnp.sum(lv[2:3] * lv[3:4], axis=-1, keepdims=True)) + lam_init)
            pieces = []
            for h in range(nh // 2):
                a0 = o_all[(2 * h) * tq:(2 * h + 1) * tq, LANES * h:LANES * (h + 1)]
                a1 = o_all[(2 * h + 1) * tq:(2 * h + 2) * tq, LANES * h:LANES * (h + 1)]
                pieces.append(_rms(a0 - lam * a1, gsub_ref[...]) * (1.0 - lam_init))
            o = jnp.concatenate(pieces, axis=-1)
        o_ref[...] = o.astype(BF16)


def _dec_attn(q, kc, vc, kn, vn, extra, *, mode, b, tq, past, tk, lam_init=0.0, name):
    n_cache = past // tk
    nh = D_MODEL // 64
    last = n_cache - 1
    new = pl.BlockSpec((tq, D_MODEL), lambda i, t: (i, 0))
    cache = pl.BlockSpec((tk, D_MODEL), lambda i, t: (i * n_cache + jnp.minimum(t, last), 0))
    if mode == "fox":
        fk, fq = extra
        especs = [pl.BlockSpec((None, nh, tk), lambda i, t: (i, 0, jnp.minimum(t, last))),
                  pl.BlockSpec((None, nh, LANES), lambda i, t: (i, 0, past // LANES)),
                  pl.BlockSpec((None, nh * tq, 1), lambda i, t: (i, 0, 0))]
        ins = [fk, fk, fq]
    else:
        lamv, gsub = extra
        especs = [pl.BlockSpec(lamv.shape, lambda i, t: (0, 0)), pl.BlockSpec(gsub.shape, lambda i, t: (0, 0))]
        ins = [lamv, gsub]
    return pl.pallas_call(
        functools.partial(_dec_attn_kernel, mode=mode, n_cache=n_cache, tq=tq, past=past, lam_init=lam_init),
        out_shape=jax.ShapeDtypeStruct((b * tq, D_MODEL), BF16),
        grid=(b, n_cache + 1),
        in_specs=[new, cache, cache, new, new] + especs,
        out_specs=new,
        scratch_shapes=[pltpu.VMEM((nh * tq, D_MODEL), BF16), pltpu.VMEM((nh * tq, 1), F32),
                        pltpu.VMEM((nh * tq, 1), F32), pltpu.VMEM((nh * tq, D_MODEL), F32)],
        compiler_params=_cp(2), name=name)(q, kc, vc, kn, vn, *ins)


def _mla_dec_kernel(q_ref, cc_ref, rc_ref, cn_ref, rn_ref, wuk_ref, wuv_ref, gkn_ref, o_ref,
                    qb, m_ref, l_ref, acc_ref, *, n_cache, tq, past):
    t = pl.program_id(1)
    nh = MLA_HEADS
    rows = nh * tq

    @pl.when(t == 0)
    def _():
        q = q_ref[...]
        zero = jnp.zeros((tq, LANES), BF16)
        blocks = []
        for h in range(nh):
            row = [q[:, 2 * LANES * h:2 * LANES * h + LANES] if c == h else zero for c in range(nh)]
            row.append(q[:, 2 * LANES * h + LANES:2 * LANES * (h + 1)])
            blocks.append(jnp.concatenate(row, axis=-1))
        qb[...] = jnp.concatenate(blocks, axis=0)
        m_ref[...] = jnp.full_like(m_ref, NEG)
        l_ref[...] = jnp.zeros_like(l_ref)
        acc_ref[...] = jnp.zeros_like(acc_ref)

    def step(ckv, krd, mask_new):
        c = ckv.astype(BF16)
        kn = _head_rms(_dot(c, wuk_ref[...]), gkn_ref[...], MLA_NOPE).astype(BF16)
        kcat = jnp.concatenate([kn, krd.astype(BF16)], axis=-1)
        v = _dot(c, wuv_ref[...]).astype(BF16)
        s = _dot_nt(qb[...], kcat)
        if mask_new:
            n = ckv.shape[0]
            qpos = past + (_row_iota((rows, n)) & (tq - 1))
            kpos = past + _lane_iota((rows, n))
            s = jnp.where((kpos >> 6) <= (qpos >> 6), s, NEG)
        _softmax_update(s, v, m_ref, l_ref, acc_ref)

    @pl.when(t < n_cache)
    def _():
        step(cc_ref[...], rc_ref[...], False)

    @pl.when(t == n_cache)
    def _():
        step(cn_ref[...], rn_ref[...], True)
        o_all = acc_ref[...] / l_ref[...]
        o_ref[...] = jnp.concatenate(
            [o_all[h * tq:(h + 1) * tq, MLA_V * h:MLA_V * (h + 1)] for h in range(nh)], axis=-1).astype(BF16)


def _mla_dec(q, cc, rc, cn, rn, wuk, wuv, gkn, *, b, tq, past, tk, name):
    n_cache = past // tk
    last = n_cache - 1
    nh = MLA_HEADS
    kdim = nh * MLA_NOPE + LANES
    new = lambda n: pl.BlockSpec((tq, n), lambda i, t: (i, 0))
    cache = lambda n: pl.BlockSpec((tk, n), lambda i, t: (i * n_cache + jnp.minimum(t, last), 0))
    full = lambda a: pl.BlockSpec(a.shape, lambda i, t: (0,) * a.ndim)
    return pl.pallas_call(
        functools.partial(_mla_dec_kernel, n_cache=n_cache, tq=tq, past=past),
        out_shape=jax.ShapeDtypeStruct((b * tq, nh * MLA_V), BF16),
        grid=(b, n_cache + 1),
        in_specs=[new(nh * 2 * LANES), cache(MLA_KV_LORA), cache(LANES), new(MLA_KV_LORA), new(LANES),
                  full(wuk), full(wuv), full(gkn)],
        out_specs=new(nh * MLA_V),
        scratch_shapes=[pltpu.VMEM((nh * tq, kdim), BF16), pltpu.VMEM((nh * tq, 1), F32),
                        pltpu.VMEM((nh * tq, 1), F32), pltpu.VMEM((nh * tq, nh * MLA_V), F32)],
        compiler_params=_cp(2), name=name)(q, cc, rc, cn, rn, wuk, wuv, gkn)


def _band_dec_kernel(q_ref, kc_ref, vc_ref, kn_ref, vn_ref, tab_ref, o_ref, *, tq, w, past):
    nh = BAND_HEADS
    rows = nh * tq
    bwid = ((w + tq + LANES - 1) // LANES) * LANES
    qb = _block_diag_q(q_ref[...], nh, BAND_DH)
    g = _rel_gather(tab_ref[...], bwid, w + tq)
    bias = jnp.concatenate(
        [pltpu.roll(jnp.broadcast_to(g[h:h + 1, :], (tq, bwid)), bwid - tq, axis=1, stride=1, stride_axis=0)
         for h in range(nh)], axis=0)
    qc = (past + (_row_iota((rows, bwid)) & (tq - 1))) >> 6
    kc = (past - w + _lane_iota((rows, bwid))) >> 6
    bias = jnp.where((kc <= qc) & (kc >= qc - BAND_LEFT_CHUNKS), bias, NEG)
    s_c = _dot_nt(qb, kc_ref[...].astype(BF16)) + bias[:, :w]
    s_n = _dot_nt(qb, kn_ref[...].astype(BF16)) + bias[:, w:w + tq]
    m = jnp.maximum(jnp.max(s_c, axis=-1, keepdims=True), jnp.max(s_n, axis=-1, keepdims=True))
    p_c = jnp.exp(s_c - m)
    p_n = jnp.exp(s_n - m)
    l = jnp.sum(p_c, axis=-1, keepdims=True) + jnp.sum(p_n, axis=-1, keepdims=True)
    o_all = (_dot(p_c.astype(BF16), vc_ref[...].astype(BF16)) + _dot(p_n.astype(BF16), vn_ref[...].astype(BF16))) / l
    o = jnp.zeros((tq, D_MODEL), F32)
    hl = _lane_iota((tq, D_MODEL)) >> _log2(BAND_DH)
    for h in range(nh):
        o = jnp.where(hl == h, o_all[h * tq:(h + 1) * tq, :], o)
    o_ref[...] = o.astype(BF16)


def _band_dec(q, kc, vc, kn, vn, tab, *, b, tq, w, past, name):
    new = pl.BlockSpec((tq, D_MODEL), lambda i: (i, 0))
    cache = pl.BlockSpec((w, D_MODEL), lambda i: (i, 0))
    return pl.pallas_call(
        functools.partial(_band_dec_kernel, tq=tq, w=w, past=past),
        out_shape=jax.ShapeDtypeStruct((b * tq, D_MODEL), BF16),
        grid=(b,),
        in_specs=[new, cache, cache, new, new, pl.BlockSpec(tab.shape, lambda i: (0, 0))],
        out_specs=new,
        compiler_params=_cp(1), name=name)(q, kc, vc, kn, vn, tab)


def _ffn_kernel(*refs, mode, tps, seq, final):
    it = iter(refs)
    h_ref, o_ref, wo_ref, gf_ref, wg_ref, wh_ref, cwg_ref, cwh_ref, cbg_ref, cbh_ref, wd_ref = [
        next(it) for _ in range(11)]
    if mode == "state":
        s1g_ref, s1h_ref, s2g_ref, s2h_ref = [next(it) for _ in range(4)]
    if final:
        gfin_ref = next(it)
    out_ref, ug_out, uh_out = next(it), next(it), next(it)
    xn_ref, acc_ref = next(it), next(it)
    if mode == "carry":
        carry_ref = next(it)
    i = pl.program_id(0)
    j = pl.program_id(1)
    nj = pl.num_programs(1)
    tm = h_ref.shape[0]

    @pl.when(j == 0)
    def _():
        h1 = h_ref[...] + _dot(o_ref[...], wo_ref[...])
        acc_ref[...] = h1
        xn_ref[...] = _rms(h1, gf_ref[...]).astype(BF16)

    if mode == "carry":
        @pl.when(i % tps == 0)
        def _():
            carry_ref[0, j] = jnp.zeros((SUBLANES, carry_ref.shape[-1]), F32)
            carry_ref[1, j] = jnp.zeros((SUBLANES, carry_ref.shape[-1]), F32)

    xn = xn_ref[...]

    def conv(u, cw, cb, part):
        if mode == "carry":
            ext = jnp.concatenate([carry_ref[part, j], u], axis=0)
            um1 = pltpu.roll(ext, 1, axis=0)[SUBLANES:]
            um2 = pltpu.roll(ext, 2, axis=0)[SUBLANES:]
            carry_ref[part, j] = u[tm - SUBLANES:]
        else:
            s1, s2 = (s1g_ref, s2g_ref) if part == 0 else (s1h_ref, s2h_ref)
            tpos = _row_iota(u.shape) & (seq - 1)
            um1 = jnp.where(tpos >= 1, pltpu.roll(u, 1, axis=0), s1[...])
            um2 = jnp.where(tpos >= 2, pltpu.roll(u, 2, axis=0), s2[...])
        return ((cb + u * cw[2:3]) + um2 * cw[0:1]) + um1 * cw[1:2]

    ug = _dot(xn, wg_ref[...])
    uh = _dot(xn, wh_ref[...])
    if mode == "carry":
        ug_out[...] = ug[tm - SUBLANES:]
        uh_out[...] = uh[tm - SUBLANES:]
    else:
        ug_out[...] = ug
        uh_out[...] = uh
    cg = conv(ug, cwg_ref[...], cbg_ref[...], 0)
    ch = conv(uh, cwh_ref[...], cbh_ref[...], 1)
    act = (cg * jax.nn.sigmoid(cg) * ch).astype(BF16)
    acc_ref[...] += _dot(act, wd_ref[...])

    @pl.when(j == nj - 1)
    def _():
        y = acc_ref[...]
        if final:
            y = _rms(y, gfin_ref[...])
        out_ref[...] = y


def _ffn(h, o, wo, gf, wup, cw, cb, wd, *, tm, tf, seq, state=None, final_g=None, name):
    m = h.shape[0]
    nff = D_FF // tf
    mode = "carry" if state is None else "state"
    tps = seq // tm if mode == "carry" else 1
    row = pl.BlockSpec((tm, D_MODEL), lambda i, j: (i, 0))
    full = lambda a: pl.BlockSpec(a.shape, lambda i, j: (0,) * a.ndim)
    colg = lambda r: pl.BlockSpec((r, tf), lambda i, j: (0, j))
    colh = lambda r: pl.BlockSpec((r, tf), lambda i, j: (0, nff + j))
    ins = [h, o, wo, gf, wup, wup, cw, cw, cb, cb, wd]
    specs = [row, row, full(wo), full(gf), colg(D_MODEL), colh(D_MODEL), colg(CONV_W), colh(CONV_W), colg(1), colh(1),
             pl.BlockSpec((tf, D_MODEL), lambda i, j: (j, 0))]
    scratch = [pltpu.VMEM((tm, D_MODEL), BF16), pltpu.VMEM((tm, D_MODEL), F32)]
    if mode == "state":
        assert m == tm and seq & (seq - 1) == 0
        s1, s2 = state
        sg = pl.BlockSpec((tm, tf), lambda i, j: (i, j))
        sh = pl.BlockSpec((tm, tf), lambda i, j: (i, nff + j))
        ins += [s1, s1, s2, s2]
        specs += [sg, sh, sg, sh]
        u_shape = jax.ShapeDtypeStruct((m, D_FF), F32)
        u_spec = pl.BlockSpec((tm, tf), lambda i, j: (i, j))
    else:
        assert seq % tm == 0
        scratch.append(pltpu.VMEM((2, nff, SUBLANES, tf), F32))
        u_shape = jax.ShapeDtypeStruct((m // seq, SUBLANES, D_FF), F32)
        u_spec = pl.BlockSpec((None, SUBLANES, tf), lambda i, j: (i // tps, 0, j))
    if final_g is not None:
        ins.append(final_g)
        specs.append(full(final_g))
    kern = functools.partial(_ffn_kernel, mode=mode, tps=tps, seq=seq, final=final_g is not None)
    return pl.pallas_call(
        kern, out_shape=[jax.ShapeDtypeStruct((m, D_MODEL), F32), u_shape, u_shape],
        grid=(m // tm, nff), in_specs=specs, out_specs=[row, u_spec, u_spec], scratch_shapes=scratch,
        compiler_params=_cp(2), name=name)(*ins)


def _rope_tables(pos):
    inv = 1.0 / (ROPE_THETA ** (jnp.arange(0, 64, 2, dtype=F32) / 64))
    ang = pos.astype(F32)[:, None] * inv[None, :]
    c, s = jnp.cos(ang), jnp.sin(ang)
    return jnp.tile(c, (1, 4)), jnp.tile(jnp.concatenate([-s, s], axis=1), (1, 2))


def _tile_gain(g, n):
    return jnp.tile(g.astype(F32), n // g.shape[0]).reshape(1, n)


def kernel(x_prompt, x_sample, cache_mla_ckv, cache_mla_krope, cache_fox_k, cache_fox_v, cache_fox_logf,
           cache_diff_k, cache_diff_v, cache_band_k, cache_band_v, state_ffn_conv,
           attn_norm_g, ffn_norm_g, final_norm_g,
           mla_w_dq, mla_g_q, mla_w_uq, mla_w_dkv, mla_g_kv, mla_w_uk, mla_w_uv,
           mla_g_qn, mla_g_qr, mla_g_kn, mla_g_kr, mla_w_o,
           fox_w_qkv, fox_w_f, fox_b_f, fox_g_q, fox_g_k, fox_w_o,
           diff_w_qkv, diff_g_q, diff_g_k, diff_lq1, diff_lk1, diff_lq2, diff_lk2, diff_g_sub, diff_w_o,
           band_w_qkv, band_g_q, band_g_k, band_rel_bias, band_w_o,
           ffn_w_up, ffn_conv_w, ffn_conv_b, ffn_w_down):
    bp, tp, d = x_prompt.shape
    bs, ts, _ = x_sample.shape
    past = cache_mla_ckv.shape[1]
    depth = attn_norm_g.shape[0]
    mp, ms = bp * tp, bs * ts
    tm_p = min(512, tp)
    tm_f = min(1024, tp)
    tq = 256
    tk = 512
    tkd = 512
    tf = 256
    assert tp % tq == 0 and tp % tk == 0 and past % tkd == 0 and past % CHUNK == 0

    tobf = lambda a: a.astype(BF16)
    rowv = lambda g: g.astype(F32).reshape(1, -1)
    pos_p = jnp.arange(tp, dtype=jnp.int32)
    pos_s = past + jnp.arange(ts, dtype=jnp.int32)
    tabs_p = _rope_tables(pos_p)
    tabs_s = tuple(jnp.tile(a, (bs, 1)) for a in _rope_tables(pos_s))

    nope_cols = jnp.arange(MLA_HEADS)[:, None] * (MLA_NOPE + MLA_ROPE) + jnp.arange(MLA_NOPE)[None, :]
    rope_cols = jnp.arange(MLA_HEADS)[:, None] * (MLA_NOPE + MLA_ROPE) + MLA_NOPE + jnp.arange(MLA_ROPE)[None, :]
    wuq_perm = jnp.concatenate([mla_w_uq[:, nope_cols.reshape(-1)], mla_w_uq[:, rope_cols.reshape(-1)]], axis=1)
    wr = mla_w_dkv[:, MLA_KV_LORA:]
    mla_w = dict(
        wdq=tobf(mla_w_dq), gq=rowv(mla_g_q), wuq=tobf(wuq_perm), wc=tobf(mla_w_dkv[:, :MLA_KV_LORA]),
        wr=tobf(jnp.concatenate([wr, wr], axis=1)), gkv=rowv(mla_g_kv),
        gqn=_tile_gain(mla_g_qn, MLA_HEADS * MLA_NOPE), gqr=_tile_gain(mla_g_qr, MLA_HEADS * MLA_ROPE),
        gkr=_tile_gain(mla_g_kr, LANES), wuk=tobf(mla_w_uk), wuv=tobf(mla_w_uv),
        gkn=_tile_gain(mla_g_kn, MLA_HEADS * MLA_NOPE))
    fox_wf = tobf(jnp.pad(fox_w_f, ((0, 0), (0, LANES - FOX_HEADS))))
    fox_bf = jnp.pad(fox_b_f.astype(F32), (0, LANES - FOX_HEADS)).reshape(1, LANES)
    lamv = jnp.stack([diff_lq1, diff_lk1, diff_lq2, diff_lk2]).astype(F32)
    gsub = rowv(diff_g_sub)
    tab_pad = jnp.pad(band_rel_bias.astype(F32), ((0, 0), (0, REL_PAD - band_rel_bias.shape[1])))
    wo = [tobf(mla_w_o), tobf(fox_w_o), tobf(diff_w_o), tobf(band_w_o)]
    wup, wdn = tobf(ffn_w_up), tobf(ffn_w_down)
    cwf, cbf = ffn_conv_w.astype(F32), ffn_conv_b.astype(F32)

    h_p = x_prompt.reshape(mp, d)
    h_s = x_sample.reshape(ms, d)
    outs = {}
    conv_p, conv_s = [], []
    for i in range(depth):
        kind = i % 4
        ga = rowv(attn_norm_g[i])
        if kind == 0:
            scale = (MLA_NOPE + MLA_ROPE) ** -0.5
            q_p, ckv_p, kr_p, _, kc_p, v_p = _mla_proj(h_p, ga, mla_w, tabs_p, tm=tm_p, with_kv=True, scale=scale,
                                                       name="mla_proj_p")
            o_p = _mla_attn(q_p, kc_p, v_p, b=bp, t=tp, tq=tq, tk=tk, name="mla_attn_p")
            q_s, ckv_s, kr_s, krd_s = _mla_proj(h_s, ga, mla_w, tabs_s, tm=ms, with_kv=False, scale=scale,
                                                name="mla_proj_s")
            krd_c = jnp.concatenate([cache_mla_krope, cache_mla_krope], axis=-1).astype(F32)
            o_s = _mla_dec(q_s, cache_mla_ckv.astype(F32).reshape(bs * past, MLA_KV_LORA),
                           krd_c.reshape(bs * past, LANES), ckv_s, krd_s, mla_w["wuk"], mla_w["wuv"], mla_w["gkn"],
                           b=bs, tq=ts, past=past, tk=tkd, name="mla_attn_s")
            outs["mla"] = (ckv_p.reshape(bp, tp, MLA_KV_LORA), kr_p.reshape(bp, tp, MLA_ROPE),
                           ckv_s.reshape(bs, ts, MLA_KV_LORA), kr_s.reshape(bs, ts, MLA_ROPE))
        elif kind == 1:
            scale = FOX_DH ** -0.5
            gq, gk = _tile_gain(fox_g_q, d), _tile_gain(fox_g_k, d)
            q_p, k_p, v_p, lf_p = _qkv_proj(h_p, ga, tobf(fox_w_qkv), gq, gk, tm=tm_p, wf=fox_wf, bf=fox_bf,
                                            scale=scale, name="fox_proj_p")
            lft = jnp.swapaxes(lf_p.reshape(bp, tp, FOX_HEADS), 1, 2)
            fk_p = _cumsum_last(lft, tk=tk, name="fox_cumsum_p")
            fq_p = jnp.swapaxes(fk_p, 1, 2).reshape(mp, FOX_HEADS)
            o_p = _pair_attn(q_p, k_p, v_p, (fq_p, fk_p.reshape(bp, FOX_HEADS // 2, 2, tp)), mode="fox",
                             b=bp, t=tp, tq=tq, tk=tk, name="fox_attn_p")
            q_s, k_s, v_s, lf_s = _qkv_proj(h_s, ga, tobf(fox_w_qkv), gq, gk, tm=ms, wf=fox_wf, bf=fox_bf,
                                            scale=scale, name="fox_proj_s")
            lf_all = jnp.concatenate([cache_fox_logf.astype(F32), lf_s.reshape(bs, ts, FOX_HEADS)], axis=1)
            tot = past + tkd
            lf_all = jnp.pad(jnp.swapaxes(lf_all, 1, 2), ((0, 0), (0, 0), (0, tot - past - ts)))
            f_all = _cumsum_last(lf_all, tk=tkd, name="fox_cumsum_s")
            fq_s = f_all[:, :, past:past + ts].reshape(bs, FOX_HEADS * ts, 1)
            o_s = _dec_attn(q_s, cache_fox_k.astype(F32).reshape(bs * past, d),
                            cache_fox_v.astype(F32).reshape(bs * past, d), k_s, v_s, (f_all, fq_s), mode="fox",
                            b=bs, tq=ts, past=past, tk=tkd, name="fox_attn_s")
            sh = (FOX_HEADS, FOX_DH)
            outs["fox"] = (k_p.reshape(bp, tp, *sh), v_p.reshape(bp, tp, *sh), lf_p.reshape(bp, tp, FOX_HEADS),
                           k_s.reshape(bs, ts, *sh), v_s.reshape(bs, ts, *sh), lf_s.reshape(bs, ts, FOX_HEADS))
        elif kind == 2:
            scale = DIFF_DH ** -0.5
            lam_init = 0.8 - 0.6 * math.exp(-0.3 * i)
            gq, gk = _tile_gain(diff_g_q, d), _tile_gain(diff_g_k, d)
            q_p, k_p, v_p = _qkv_proj(h_p, ga, tobf(diff_w_qkv), gq, gk, tm=tm_p, rope_tabs=tabs_p, scale=scale,
                                      name="diff_proj_p")
            o_p = _pair_attn(q_p, k_p, v_p, (lamv, gsub), mode="diff", b=bp, t=tp, tq=tq, tk=tk, lam_init=lam_init,
                             name="diff_attn_p")
            q_s, k_s, v_s = _qkv_proj(h_s, ga, tobf(diff_w_qkv), gq, gk, tm=ms, rope_tabs=tabs_s, scale=scale,
                                      name="diff_proj_s")
            o_s = _dec_attn(q_s, cache_diff_k.astype(F32).reshape(bs * past, d),
                            cache_diff_v.astype(F32).reshape(bs * past, d), k_s, v_s, (lamv, gsub), mode="diff",
                            b=bs, tq=ts, past=past, tk=tkd, lam_init=lam_init, name="diff_attn_s")
            outs["diff"] = (k_p.reshape(bp, tp, DIFF_HEADS, 2, DIFF_DH), v_p.reshape(bp, tp, DIFF_HEADS, 2 * DIFF_DH),
                            k_s.reshape(bs, ts, DIFF_HEADS, 2, DIFF_DH), v_s.reshape(bs, ts, DIFF_HEADS, 2 * DIFF_DH))
        else:
            scale = BAND_DH ** -0.5
            gq, gk = _tile_gain(band_g_q, d), _tile_gain(band_g_k, d)
            q_p, k_p, v_p = _qkv_proj(h_p, ga, tobf(band_w_qkv), gq, gk, tm=tm_p, scale=scale, name="band_proj_p")
            tab_pairs = jnp.pad(tab_pad.reshape(BAND_HEADS // 2, 2, REL_PAD), ((0, 0), (0, SUBLANES - 2), (0, 0)))
            o_p = _band_attn(q_p, k_p, v_p, tab_pairs, b=bp, t=tp, tq=tq, name="band_attn_p")
            q_s, k_s, v_s = _qkv_proj(h_s, ga, tobf(band_w_qkv), gq, gk, tm=ms, scale=scale, name="band_proj_s")
            w = cache_band_k.shape[1]
            kc = cache_band_k.astype(F32).reshape(bs * w, d)
            vc = cache_band_v.astype(F32).reshape(bs * w, d)
            o_s = _band_dec(q_s, kc, vc, k_s, v_s, tab_pad, b=bs, tq=ts, w=w, past=past, name="band_attn_s")
            sh = (BAND_HEADS, BAND_DH)
            keep = min(BAND_LEFT, tp)
            outs["band"] = (
                k_p.reshape(bp, tp, *sh)[:, tp - keep:], v_p.reshape(bp, tp, *sh)[:, tp - keep:],
                jnp.concatenate([cache_band_k.astype(F32), k_s.reshape(bs, ts, *sh)], axis=1)[:, ts:],
                jnp.concatenate([cache_band_v.astype(F32), v_s.reshape(bs, ts, *sh)], axis=1)[:, ts:])

        last = i == depth - 1
        gfin = rowv(final_norm_g) if last else None
        gfn = rowv(ffn_norm_g[i])
        h_p, ug, uh = _ffn(h_p, o_p, wo[kind], gfn, wup[i], cwf[i], cbf[i].reshape(1, -1), wdn[i], tm=tm_f, tf=tf,
                           seq=tp, final_g=gfin, name="ffn_p")
        conv_p.append(jnp.concatenate([ug, uh], axis=-1)[:, SUBLANES - (CONV_W - 1):])
        st = state_ffn_conv[i].astype(F32)
        zeros = jnp.zeros((bs, ts, 2 * D_FF), F32)
        s1 = zeros.at[:, 0].set(st[:, 1]).reshape(ms, 2 * D_FF)
        s2 = zeros.at[:, 0].set(st[:, 0]).at[:, 1].set(st[:, 1]).reshape(ms, 2 * D_FF)
        h_s, ug, uh = _ffn(h_s, o_s, wo[kind], gfn, wup[i], cwf[i], cbf[i].reshape(1, -1), wdn[i], tm=ms, tf=tf,
                           seq=ts, state=(s1, s2), final_g=gfin, name="ffn_s")
        u_s = jnp.concatenate([ug, uh], axis=-1).reshape(bs, ts, 2 * D_FF)
        conv_s.append(jnp.concatenate([st, u_s], axis=1)[:, ts:])

    y_prompt = h_p.reshape(bp, tp, d)
    y_sample = h_s.reshape(bs, ts, d)
    return (y_prompt, y_sample) + outs["mla"] + outs["fox"] + outs["diff"] + outs["band"] + (
        jnp.stack(conv_p, axis=0), jnp.stack(conv_s, axis=0))
```

```python
import functools
import math

import jax
import jax.numpy as jnp
from jax import lax
from jax.experimental import pallas as pl
from jax.experimental.pallas import tpu as pltpu

F32 = jnp.float32
BF16 = jnp.bfloat16

D_MODEL = 1024
CHUNK = 64
ROPE_THETA = 10000.0
NORM_EPS = 1e-6
MLA_HEADS, MLA_Q_LORA, MLA_KV_LORA, MLA_NOPE, MLA_ROPE, MLA_V = 8, 384, 256, 128, 64, 128
FOX_HEADS, FOX_DH = 16, 64
DIFF_HEADS, DIFF_DH = 8, 64
BAND_HEADS, BAND_DH = 16, 64
BAND_LEFT_CHUNKS = 8
BAND_LEFT = BAND_LEFT_CHUNKS * CHUNK
REL_CLIP = 128
D_FF = 2816
CONV_W = 3

LANES = 128
SUBLANES = 8
NEG = -1e30
REL_PAD = 384
LOG2E = math.log2(math.e)
VMEM_LIMIT = 56 * 1024 * 1024


def _cp(n_axes):
    return pltpu.CompilerParams(dimension_semantics=("arbitrary",) * n_axes, vmem_limit_bytes=VMEM_LIMIT)


def _dot(a, b):
    return jnp.dot(a, b, preferred_element_type=F32)


def _dot_nt(a, b):
    return lax.dot_general(a, b, (((1,), (1,)), ((), ())), preferred_element_type=F32)


def _rms(x, g):
    ms = jnp.mean(x * x, axis=-1, keepdims=True)
    return x * lax.rsqrt(ms + NORM_EPS) * g


def _lane_iota(shape):
    return lax.broadcasted_iota(jnp.int32, shape, len(shape) - 1)


def _row_iota(shape):
    return lax.broadcasted_iota(jnp.int32, shape, len(shape) - 2)


def _log2(n):
    assert n & (n - 1) == 0, n
    return n.bit_length() - 1


def _head_rms(x, g, seg):
    n = x.shape[1]
    outs = []
    for c in range(n // LANES):
        xs = x[:, LANES * c:LANES * (c + 1)]
        sq = xs * xs
        if seg == LANES:
            r = lax.rsqrt(jnp.sum(sq, axis=-1, keepdims=True) * (1.0 / seg) + NORM_EPS)
        else:
            lo = _lane_iota(xs.shape) < seg
            s_lo = jnp.sum(jnp.where(lo, sq, 0.0), axis=-1, keepdims=True)
            s_hi = jnp.sum(jnp.where(lo, 0.0, sq), axis=-1, keepdims=True)
            r = jnp.where(lo, lax.rsqrt(s_lo * (1.0 / seg) + NORM_EPS), lax.rsqrt(s_hi * (1.0 / seg) + NORM_EPS))
        outs.append(xs * r * g[:, LANES * c:LANES * (c + 1)])
    return jnp.concatenate(outs, axis=-1) if len(outs) > 1 else outs[0]


def _head_rms_t(x, g, seg=64):
    outs = []
    for hd in range(x.shape[0] // seg):
        xs = x[seg * hd:seg * (hd + 1)]
        r = lax.rsqrt(jnp.sum(xs * xs, axis=0, keepdims=True) * (1.0 / seg) + NORM_EPS)
        outs.append(xs * r * g[seg * hd:seg * (hd + 1)])
    return jnp.concatenate(outs, axis=0) if len(outs) > 1 else outs[0]


def _rope(x, cos, sinp):
    lo32 = (_lane_iota((x.shape[0], LANES)) & 63) < 32
    outs = []
    for c in range(x.shape[1] // LANES):
        xs = x[:, LANES * c:LANES * (c + 1)]
        fwd = pltpu.roll(xs, 32, axis=1)
        bwd = pltpu.roll(xs, 96, axis=1)
        outs.append(xs * cos + jnp.where(lo32, bwd, fwd) * sinp)
    return jnp.concatenate(outs, axis=-1) if len(outs) > 1 else outs[0]


def _rope_t(x, cos_t, sin_t):
    outs = []
    for hd in range(x.shape[0] // 64):
        x1 = x[64 * hd:64 * hd + 32]
        x2 = x[64 * hd + 32:64 * (hd + 1)]
        outs += [x1 * cos_t - x2 * sin_t, x2 * cos_t + x1 * sin_t]
    return jnp.concatenate(outs, axis=0)


def _split3(x):
    hi = x.astype(BF16)
    r = x - hi.astype(F32)
    mid = r.astype(BF16)
    lo = (r - mid.astype(F32)).astype(BF16)
    return hi, mid, lo


def _softmax_step(s, pv, m, l, acc):
    m_new = jnp.maximum(m, jnp.max(s, axis=-1, keepdims=True))
    alpha = jnp.exp2(m - m_new)
    p = jnp.exp2(s - m_new)
    if l is not None:
        l = alpha * l + jnp.sum(p, axis=-1, keepdims=True)
    return m_new, l, alpha * acc + pv(p.astype(BF16))


def _softmax_update(s, pv, m_ref, l_ref, acc_ref):
    m, l, acc = _softmax_step(s, pv, m_ref[...], l_ref[...], acc_ref[...])
    m_ref[...] = m
    l_ref[...] = l
    acc_ref[...] = acc


def _log_sigmoid(z):
    return jnp.minimum(z, 0.0) - jnp.log1p(jnp.exp(-jnp.abs(z)))


def _qkv_proj_kernel(*refs, rope, logf, scale):
    it = iter(refs)
    h_ref, g_ref, w_ref, gq_ref, gk_ref = next(it), next(it), next(it), next(it), next(it)
    cos_ref = sin_ref = wf_ref = bf_ref = None
    if rope:
        cos_ref, sin_ref = next(it), next(it)
    if logf:
        wf_ref, bf_ref = next(it), next(it)
    q_out, k_out, v_out = next(it), next(it), next(it)
    a = _rms(h_ref[...], g_ref[...]).astype(BF16)
    qkv = _dot(a, w_ref[...])
    q = _head_rms(qkv[:, :D_MODEL], gq_ref[...], 64)
    k = _head_rms(qkv[:, D_MODEL:2 * D_MODEL], gk_ref[...], 64)
    if rope:
        q = _rope(q, cos_ref[...], sin_ref[...])
        k = _rope(k, cos_ref[...], sin_ref[...])
    q_out[...] = (q * scale).astype(BF16)
    k_out[...] = k
    v_out[...] = qkv[:, 2 * D_MODEL:]
    if logf:
        logf_out = next(it)
        lf = _log_sigmoid(_dot(a, wf_ref[...]) + bf_ref[...])
        logf_out[...] = lf[:, :FOX_HEADS]


def _qkv_proj(h, g, w, gq, gk, *, tm, rope_tabs=None, wf=None, bf=None, scale, name):
    m = h.shape[0]
    grid = (m // tm,)
    row = lambda n: pl.BlockSpec((tm, n), lambda i: (i, 0))
    full = lambda a: pl.BlockSpec(a.shape, lambda i: (0,) * a.ndim)
    ins = [h, g, w, gq, gk]
    specs = [row(D_MODEL), full(g), full(w), full(gq), full(gk)]
    if rope_tabs is not None:
        cos, sinp = rope_tabs
        nt = cos.shape[0] // tm
        tab = pl.BlockSpec((tm, LANES), lambda i: (i % nt, 0))
        ins += [cos, sinp]
        specs += [tab, tab]
    if wf is not None:
        ins += [wf, bf]
        specs += [full(wf), full(bf)]
    outs = [jax.ShapeDtypeStruct((m, D_MODEL), BF16), jax.ShapeDtypeStruct((m, D_MODEL), F32),
            jax.ShapeDtypeStruct((m, D_MODEL), F32)]
    ospecs = [row(D_MODEL), row(D_MODEL), row(D_MODEL)]
    if wf is not None:
        outs.append(jax.ShapeDtypeStruct((m, FOX_HEADS), F32))
        ospecs.append(row(FOX_HEADS))
    kern = functools.partial(_qkv_proj_kernel, rope=rope_tabs is not None, logf=wf is not None, scale=scale)
    return pl.pallas_call(kern, out_shape=outs, grid=grid, in_specs=specs, out_specs=ospecs,
                          compiler_params=_cp(1), name=name)(*ins)


def _qkv_proj_t_kernel(*refs, rope, logf, v_tokens, scale):
    it = iter(refs)
    h_ref, g_ref, wq_ref, wkt_ref, wv_ref, gq_ref, gkt_ref = [next(it) for _ in range(7)]
    if rope:
        cos_ref, sin_ref, cost_ref, sint_ref = [next(it) for _ in range(4)]
    if logf:
        wft_ref, bft_ref = next(it), next(it)
    q_out, kt_out, v_out = next(it), next(it), next(it)
    a = _rms(h_ref[...], g_ref[...]).astype(BF16)
    tm = a.shape[0]
    q = _head_rms(_dot(a, wq_ref[...]), gq_ref[...], 64)
    kt = _head_rms_t(_dot_nt(wkt_ref[...], a), gkt_ref[...])
    if rope:
        q = _rope(q, cos_ref[...], sin_ref[...])
        kt = _rope_t(kt, cost_ref[...], sint_ref[...])
    q_out[...] = (q * scale).astype(BF16)
    kt_out[...] = kt
    if v_tokens:
        v = _dot(a, wv_ref[...])
        nh = D_MODEL // LANES
        for hd in range(nh):
            v_out[pl.ds(hd, tm, stride=nh), :] = v[:, LANES * hd:LANES * (hd + 1)]
    else:
        v_out[...] = _dot_nt(wv_ref[...], a)
    if logf:
        lft_out = next(it)
        lft_out[...] = _log_sigmoid(_dot_nt(wft_ref[...], a) + bft_ref[...])


def _qkv_proj_t(h, g, wq, wkt, wv, gq, gkt, *, b, t, tm, rope_tabs=None, wft=None, bft=None, v_tokens=False,
                scale, name):
    m = b * t
    nt = t // tm
    nh = D_MODEL // LANES
    row = lambda n: pl.BlockSpec((tm, n), lambda i: (i, 0))
    full = lambda a: pl.BlockSpec(a.shape, lambda i: (0,) * a.ndim)
    feat = lambda n: pl.BlockSpec((None, n, tm), lambda i: (i // nt, 0, i % nt))
    ins = [h, g, wq, wkt, wv, gq, gkt]
    specs = [row(D_MODEL)] + [full(x) for x in ins[1:]]
    if rope_tabs is not None:
        cos, sinp, cos_t, sin_t = rope_tabs
        ins += [cos, sinp, cos_t, sin_t]
        tab = pl.BlockSpec((tm, LANES), lambda i: (i % nt, 0))
        tab_t = pl.BlockSpec((32, tm), lambda i: (0, i % nt))
        specs += [tab, tab, tab_t, tab_t]
    if wft is not None:
        ins += [wft, bft]
        specs += [full(wft), full(bft)]
    outs = [jax.ShapeDtypeStruct((m, D_MODEL), BF16), jax.ShapeDtypeStruct((b, D_MODEL, t), F32)]
    ospecs = [row(D_MODEL), feat(D_MODEL)]
    if v_tokens:
        outs.append(jax.ShapeDtypeStruct((m * nh, LANES), F32))
        ospecs.append(pl.BlockSpec((tm * nh, LANES), lambda i: (i, 0)))
    else:
        outs.append(jax.ShapeDtypeStruct((b, D_MODEL, t), F32))
        ospecs.append(feat(D_MODEL))
    if wft is not None:
        outs.append(jax.ShapeDtypeStruct((b, FOX_HEADS, t), F32))
        ospecs.append(feat(FOX_HEADS))
    kern = functools.partial(_qkv_proj_t_kernel, rope=rope_tabs is not None, logf=wft is not None,
                             v_tokens=v_tokens, scale=scale)
    return pl.pallas_call(kern, out_shape=outs, grid=(m // tm,), in_specs=specs, out_specs=ospecs,
                          compiler_params=_cp(1), name=name)(*ins)


def _mla_proj_kernel(*refs, with_kv, scale):
    it = iter(refs)
    (h_ref, g_ref, wdq_ref, gq_ref, wuq_ref, wc_ref, wr_ref, gkv_ref, gqn_ref, gqr_ref, gkr_ref,
     cos_ref, sin_ref) = [next(it) for _ in range(13)]
    if with_kv:
        wuk_ref, wuv_ref, gkn_ref, wrt_ref, gkrt_ref, cost_ref, sint_ref = [next(it) for _ in range(7)]
    q_out, ckv_out, kr_out, krd_out = next(it), next(it), next(it), next(it)
    a = _rms(h_ref[...], g_ref[...]).astype(BF16)
    cq = _rms(_dot(a, wdq_ref[...]), gq_ref[...]).astype(BF16)
    q = _dot(cq, wuq_ref[...])
    n_nope = MLA_HEADS * MLA_NOPE
    qn = _head_rms(q[:, :n_nope], gqn_ref[...], MLA_NOPE) * scale
    qr = _rope(_head_rms(q[:, n_nope:], gqr_ref[...], MLA_ROPE), cos_ref[...], sin_ref[...]) * scale
    lo = _lane_iota((q.shape[0], LANES)) < MLA_ROPE
    pieces = []
    for hd in range(MLA_HEADS):
        slab = qr[:, LANES * (hd // 2):LANES * (hd // 2 + 1)]
        keep = lo if hd % 2 == 0 else jnp.logical_not(lo)
        pieces += [qn[:, LANES * hd:LANES * (hd + 1)], jnp.where(keep, slab, 0.0)]
    q_out[...] = jnp.concatenate(pieces, axis=-1).astype(BF16)
    ckv = _rms(_dot(a, wc_ref[...]), gkv_ref[...])
    ckv_out[...] = ckv
    kr2 = _rope(_head_rms(_dot(a, wr_ref[...]), gkr_ref[...], MLA_ROPE), cos_ref[...], sin_ref[...])
    kr_out[...] = kr2[:, :MLA_ROPE]
    krd_out[...] = kr2
    if with_kv:
        kc_out, v_out, krt_out = next(it), next(it), next(it)
        c = ckv.astype(BF16)
        kn = _head_rms(_dot(c, wuk_ref[...]), gkn_ref[...], MLA_NOPE)
        pieces = []
        for hd in range(MLA_HEADS):
            pieces += [kn[:, LANES * hd:LANES * (hd + 1)], kr2]
        kc_out[...] = jnp.concatenate(pieces, axis=-1).astype(BF16)
        v_out[...] = _dot(c, wuv_ref[...]).astype(BF16)
        krt_out[...] = _rope_t(_head_rms_t(_dot_nt(wrt_ref[...], a), gkrt_ref[...]), cost_ref[...], sint_ref[...])


def _mla_proj(h, g, w, rope_tabs, *, b, t, tm, with_kv, scale, name):
    m = h.shape[0]
    cos, sinp = rope_tabs[:2]
    nt = cos.shape[0] // tm
    row = lambda n: pl.BlockSpec((tm, n), lambda i: (i, 0))
    full = lambda a: pl.BlockSpec(a.shape, lambda i: (0,) * a.ndim)
    tab = pl.BlockSpec((tm, LANES), lambda i: (i % nt, 0))
    ins = [h, g, w["wdq"], w["gq"], w["wuq"], w["wc"], w["wr"], w["gkv"], w["gqn"], w["gqr"], w["gkr"], cos, sinp]
    specs = [row(D_MODEL)] + [full(x) for x in ins[1:11]] + [tab, tab]
    qw = MLA_HEADS * 2 * LANES
    outs = [jax.ShapeDtypeStruct((m, qw), BF16), jax.ShapeDtypeStruct((m, MLA_KV_LORA), F32),
            jax.ShapeDtypeStruct((m, MLA_ROPE), F32), jax.ShapeDtypeStruct((m, LANES), F32)]
    ospecs = [row(qw), row(MLA_KV_LORA), row(MLA_ROPE), row(LANES)]
    if with_kv:
        cos_t, sin_t = rope_tabs[2:]
        tab_t = pl.BlockSpec((32, tm), lambda i: (0, i % nt))
        extra = [w["wuk"], w["wuv"], w["gkn"], w["wrt"], w["gkrt"]]
        ins += extra + [cos_t, sin_t]
        specs += [full(x) for x in extra] + [tab_t, tab_t]
        outs += [jax.ShapeDtypeStruct((m, qw), BF16), jax.ShapeDtypeStruct((m, MLA_HEADS * MLA_V), BF16),
                 jax.ShapeDtypeStruct((b, MLA_ROPE, t), F32)]
        ospecs += [row(qw), row(MLA_HEADS * MLA_V),
                   pl.BlockSpec((None, MLA_ROPE, tm), lambda i: (i // nt, 0, i % nt))]
    kern = functools.partial(_mla_proj_kernel, with_kv=with_kv, scale=scale)
    return pl.pallas_call(kern, out_shape=outs, grid=(m // tm,), in_specs=specs, out_specs=ospecs,
                          compiler_params=_cp(1), name=name)(*ins)


def _cumsum_kernel(x_ref, o_ref, carry_ref):
    @pl.when(pl.program_id(1) == 0)
    def _():
        carry_ref[...] = jnp.zeros_like(carry_ref)

    x = x_ref[...]
    tk = x.shape[1]
    hi, mid, lo = _split3(x)
    tri = (_row_iota((tk, tk)) <= _lane_iota((tk, tk))).astype(BF16)
    y = _dot(jnp.concatenate([hi, mid, lo], axis=0), tri)
    nh = x.shape[0]
    f = (y[:nh] + y[nh:2 * nh]) + y[2 * nh:] + carry_ref[...]
    o_ref[...] = f
    carry_ref[...] = f[:, tk - 1:tk]


def _cumsum_last(x, *, tk, name):
    b, nh, t = x.shape
    spec = pl.BlockSpec((None, nh, tk), lambda i, j: (i, 0, j))
    return pl.pallas_call(_cumsum_kernel, out_shape=jax.ShapeDtypeStruct(x.shape, F32), grid=(b, t // tk),
                          in_specs=[spec], out_specs=spec, scratch_shapes=[pltpu.VMEM((nh, 1), F32)],
                          compiler_params=_cp(2), name=name)(x)


def _causal_loop(qi, tile, init):
    carry = lax.fori_loop(0, qi, lambda kt, c: tile(kt, c, False), init)
    return tile(qi, carry, True)


def _mla_attn_kernel(q_ref, k_ref, v_ref, o_ref, *, tq, hps):
    qi = pl.program_id(2)
    kw = 2 * LANES
    limit = (((qi * tq + _row_iota((tq, 1))) >> 6) + 1) << 6
    qs = [q_ref[:, kw * h:kw * (h + 1)] for h in range(hps)]

    def tile(kt, carry, masked):
        off = pl.multiple_of(kt * tq, tq)
        new = []
        for h in range(hps):
            m, l, acc = carry[h]
            s = _dot_nt(qs[h], k_ref[pl.ds(off, tq), kw * h:kw * (h + 1)])
            if masked:
                s = jnp.where(off + _lane_iota((1, tq)) < limit, s, NEG)
            v = v_ref[pl.ds(off, tq), MLA_V * h:MLA_V * (h + 1)]
            new.append(_softmax_step(s, lambda pb, v=v: _dot(pb, v), m, l, acc))
        return tuple(new)

    init = tuple((jnp.full((tq, 1), NEG, F32), jnp.zeros((tq, 1), F32), jnp.zeros((tq, MLA_V), F32))
                 for _ in range(hps))
    res = _causal_loop(qi, tile, init)
    o_ref[...] = jnp.concatenate([acc / l for _, l, acc in res], axis=-1).astype(BF16)


def _mla_attn(q, kc, v, *, b, t, tq, hps, name):
    nq = t // tq
    kw = 2 * LANES * hps
    vw = MLA_V * hps
    return pl.pallas_call(
        functools.partial(_mla_attn_kernel, tq=tq, hps=hps),
        out_shape=jax.ShapeDtypeStruct((b * t, MLA_HEADS * MLA_V), BF16),
        grid=(b, MLA_HEADS // hps, nq),
        in_specs=[pl.BlockSpec((tq, kw), lambda i, h, j: (i * nq + j, h)),
                  pl.BlockSpec((t, kw), lambda i, h, j: (i, h)),
                  pl.BlockSpec((t, vw), lambda i, h, j: (i, h))],
        out_specs=pl.BlockSpec((tq, vw), lambda i, h, j: (i * nq + j, h)),
        compiler_params=_cp(3), name=name)(q, kc, v)


def _ones_row_values(vt, c):
    row = _row_iota(vt.shape)
    if c == 0:
        return jnp.where(row < 64, vt, jnp.where(row == 64, 1.0, 0.0)).astype(BF16)
    return jnp.where(row >= 64, vt, jnp.where(row == 0, 1.0, 0.0)).astype(BF16)


def _merge_pair(acc0, acc1):
    lo = _lane_iota(acc0.shape) < 64
    return jnp.where(lo, acc0 / acc0[:, 64:65], acc1 / acc1[:, 0:1])


def _pair_attn_kernel(*refs, mode, tq, lam_init):
    if mode == "fox":
        q_ref, kt_ref, vt_ref, fk_ref, o_ref, ka, va = refs
    else:
        q_ref, kt_ref, v_ref, lam_ref, gsub_ref, o_ref, ka, vb = refs
    qi = pl.program_id(2)
    t = kt_ref.shape[1]

    @pl.when(qi == 0)
    def _():
        if mode == "fox":
            ka[0:LANES, :] = kt_ref[...].astype(BF16)
            f = fk_ref[...] * (-LOG2E)
            terms = [x.astype(F32) for x in _split3(f)]
            nr = 2 * SUBLANES
            row = _row_iota((nr, t))
            aug = jnp.zeros((nr, t), F32)
            for c in range(2):
                for j in range(3):
                    aug = jnp.where(row == 3 * c + j, terms[j][c:c + 1, :], aug)
            ka[LANES:LANES + nr, :] = aug.astype(BF16)
            ka[LANES + nr:, :] = jnp.zeros((LANES - nr, t), BF16)
            vt = vt_ref[...]
            va[0] = _ones_row_values(vt, 0)
            va[1] = _ones_row_values(vt, 1)
        else:
            ka[...] = kt_ref[...].astype(BF16)
            nhv = v_ref.shape[0] // t
            vb[...] = v_ref[pl.ds(pl.program_id(1), t, stride=nhv), :].astype(BF16)

    q = q_ref[...]
    lane = _lane_iota((tq, LANES))
    zero = jnp.zeros_like(q)
    qa = [jnp.where(lane < 64, q, zero), jnp.where(lane < 64, zero, q)]
    qpos = qi * tq + _row_iota((tq, 1))
    if mode == "fox":
        limit = qpos + 1
        pick = [jnp.where(lane < 3 * c, 0.0, jnp.where(lane < 3 * c + 3, 1.0, 0.0)).astype(BF16) for c in range(2)]
        qa = [jnp.concatenate([qa[c], pick[c]], axis=-1) for c in range(2)]
    else:
        limit = ((qpos >> 6) + 1) << 6

    def tile(kt, carry, masked):
        off = pl.multiple_of(kt * tq, tq)
        k = ka[:, pl.ds(off, tq)]
        new = []
        for c in range(2):
            m, l, acc = carry[c]
            s = _dot(qa[c], k)
            if masked:
                s = jnp.where(off + _lane_iota((1, tq)) < limit, s, NEG)
            if mode == "fox":
                v = va[c, :, pl.ds(off, tq)]
                new.append(_softmax_step(s, lambda pb, v=v: _dot_nt(pb, v), m, None, acc))
            else:
                v = vb[pl.ds(off, tq), :]
                new.append(_softmax_step(s, lambda pb, v=v: _dot(pb, v), m, l, acc))
        return tuple(new)

    init = tuple((jnp.full((tq, 1), NEG, F32), None if mode == "fox" else jnp.zeros((tq, 1), F32),
                  jnp.zeros((tq, LANES), F32)) for _ in range(2))
    (_, l0, a0), (_, l1, a1) = _causal_loop(qi, tile, init)
    if mode == "fox":
        o = _merge_pair(a0, a1)
    else:
        lv = lam_ref[...]
        lam = (jnp.exp(jnp.sum(lv[0:1] * lv[1:2], axis=-1, keepdims=True))
               - jnp.exp(jnp.sum(lv[2:3] * lv[3:4], axis=-1, keepdims=True)) + lam_init)
        o = _rms(a0 / l0 - lam * (a1 / l1), gsub_ref[...]) * (1.0 - lam_init)
    o_ref[...] = o.astype(BF16)


def _pair_attn(q, kt, v, extra, *, mode, b, t, tq, lam_init=0.0, name):
    nq = t // tq
    npair = D_MODEL // LANES
    qspec = pl.BlockSpec((tq, LANES), lambda i, p, j: (i * nq + j, p))
    ktspec = pl.BlockSpec((None, LANES, t), lambda i, p, j: (i, p, 0))
    if mode == "fox":
        vspec = ktspec
        especs = [pl.BlockSpec((None, None, 2, t), lambda i, p, j: (i, p, 0, 0))]
        scratch = [pltpu.VMEM((2 * LANES, t), BF16), pltpu.VMEM((2, LANES, t), BF16)]
    else:
        vspec = pl.BlockSpec((t * npair, LANES), lambda i, p, j: (i, 0))
        especs = [pl.BlockSpec(x.shape, lambda i, p, j: (0, 0)) for x in extra]
        scratch = [pltpu.VMEM((LANES, t), BF16), pltpu.VMEM((t, LANES), BF16)]
    return pl.pallas_call(
        functools.partial(_pair_attn_kernel, mode=mode, tq=tq, lam_init=lam_init),
        out_shape=jax.ShapeDtypeStruct((b * t, D_MODEL), BF16),
        grid=(b, npair, nq),
        in_specs=[qspec, ktspec, vspec] + especs,
        out_specs=qspec,
        scratch_shapes=scratch,
        compiler_params=_cp(3), name=name)(q, kt, v, *extra)


def _rel_gather(tab, width, center):
    idx = jnp.clip(center - _lane_iota((REL_PAD, width)), -REL_CLIP, REL_CLIP) + REL_CLIP
    onehot = (_row_iota((REL_PAD, width)) == idx).astype(BF16)
    hi, mid, lo = _split3(tab)
    return (_dot(hi, onehot) + _dot(mid, onehot)) + _dot(lo, onehot)


def _band_attn_kernel(q_ref, kt_ref, vt_ref, tab_ref, o_ref, kb, va, bias_ref, *, tq, win, bw):
    b = pl.program_id(1)
    qi = pl.program_id(2)
    gw = bw + tq

    @pl.when((b == 0) & (qi == 0))
    def _():
        g = _rel_gather(tab_ref[...], gw, BAND_LEFT + tq) * LOG2E
        ii = _row_iota((tq, bw)) >> 6
        jj = _lane_iota((tq, bw)) >> 6
        allowed = (jj >= ii) & (jj <= ii + BAND_LEFT_CHUNKS)
        for c in range(2):
            rows = jnp.broadcast_to(g[c:c + 1, :], (tq, gw))
            skew = pltpu.roll(rows, gw - tq, axis=1, stride=1, stride_axis=0)
            bias_ref[c] = jnp.where(allowed, skew[:, :bw], NEG)

    @pl.when(qi == 0)
    def _():
        kb[...] = kt_ref[...].astype(BF16)
        vt = vt_ref[...]
        va[0] = _ones_row_values(vt, 0)
        va[1] = _ones_row_values(vt, 1)

    q0 = qi * tq
    ws = pl.multiple_of(jnp.maximum(q0 - BAND_LEFT, 0), tq)
    d = pl.multiple_of(BAND_LEFT - q0 + ws, LANES)
    k = kb[:, pl.ds(ws, win)]
    q = q_ref[...]
    lo = _lane_iota((tq, LANES)) < 64
    zero = jnp.zeros_like(q)
    accs = []
    for c in range(2):
        qc = jnp.where(lo, q, zero) if c == 0 else jnp.where(lo, zero, q)
        s = _dot(qc, k) + bias_ref[c, :, pl.ds(d, win)]
        p = jnp.exp2(s - jnp.max(s, axis=-1, keepdims=True))
        accs.append(_dot_nt(p.astype(BF16), va[c, :, pl.ds(ws, win)]))
    o_ref[...] = _merge_pair(*accs).astype(BF16)


def _band_attn(q, kt, vt, tab, *, b, t, tq, name):
    nq = t // tq
    npair = D_MODEL // LANES
    win = BAND_LEFT + tq
    bw = win + BAND_LEFT
    assert t >= win and tq % CHUNK == 0 and BAND_LEFT % tq == 0
    qspec = pl.BlockSpec((tq, LANES), lambda p, i, j: (i * nq + j, p))
    kvspec = pl.BlockSpec((None, LANES, t), lambda p, i, j: (i, p, 0))
    return pl.pallas_call(
        functools.partial(_band_attn_kernel, tq=tq, win=win, bw=bw),
        out_shape=jax.ShapeDtypeStruct((b * t, D_MODEL), BF16),
        grid=(npair, b, nq),
        in_specs=[qspec, kvspec, kvspec, pl.BlockSpec((None, SUBLANES, REL_PAD), lambda p, i, j: (p, 0, 0))],
        out_specs=qspec,
        scratch_shapes=[pltpu.VMEM((LANES, t), BF16), pltpu.VMEM((2, LANES, t), BF16), pltpu.VMEM((2, tq, bw), F32)],
        compiler_params=_cp(3), name=name)(q, kt, vt, tab)


def _block_diag_q(q, nh, width):
    tq = q.shape[0]
    rep = jnp.concatenate([q] * nh, axis=0)
    keep = (_row_iota(rep.shape) >> _log2(tq)) == (_lane_iota(rep.shape) >> _log2(width))
    return jnp.where(keep, rep, jnp.zeros_like(rep))


def _expand_rows(f, tq):
    return jnp.concatenate([jnp.broadcast_to(f[h:h + 1, :], (tq, f.shape[1])) for h in range(f.shape[0])], axis=0)


def _dec_attn_kernel(*refs, mode, n_cache, tq, past, lam_init):
    if mode == "fox":
        (q_ref, kc_ref, vc_ref, kn_ref, vn_ref, fkc_ref, fkn_ref, o_ref, qb, m_ref, l_ref, acc_ref) = refs
    else:
        (q_ref, kc_ref, vc_ref, kn_ref, vn_ref, lam_ref, gsub_ref, o_ref, qb, m_ref, l_ref, acc_ref) = refs
    t = pl.program_id(1)
    nh = D_MODEL // 64
    rows = nh * tq

    @pl.when(t == 0)
    def _():
        qb[...] = _block_diag_q(q_ref[...], nh, 64)
        m_ref[...] = jnp.full_like(m_ref, NEG)
        l_ref[...] = jnp.zeros_like(l_ref)
        acc_ref[...] = jnp.zeros_like(acc_ref)

    @pl.when(t < n_cache)
    def _():
        s = _dot(qb[...], kc_ref[...].astype(BF16))
        if mode == "fox":
            s = s - _expand_rows(fkc_ref[...] * LOG2E, tq)
            v = vc_ref[...].astype(BF16)
            _softmax_update(s, lambda pb: _dot_nt(pb, v), m_ref, l_ref, acc_ref)
        else:
            tk = kc_ref.shape[1]
            nhv = D_MODEL // LANES
            v = jnp.concatenate([vc_ref[pl.ds(h, tk, stride=nhv), :].astype(BF16) for h in range(nhv)], axis=-1)
            _softmax_update(s, lambda pb: _dot(pb, v), m_ref, l_ref, acc_ref)

    @pl.when(t == n_cache)
    def _():
        s = _dot_nt(qb[...], kn_ref[...].astype(BF16))
        qpos = past + (_row_iota((rows, tq)) & (tq - 1))
        kpos = past + _lane_iota((rows, tq))
        if mode == "fox":
            s = s - _expand_rows(fkn_ref[...][:, :tq] * LOG2E, tq)
            allowed = kpos <= qpos
        else:
            allowed = (kpos >> 6) <= (qpos >> 6)
        vn = vn_ref[...].astype(BF16)
        _softmax_update(jnp.where(allowed, s, NEG), lambda pb: _dot(pb, vn), m_ref, l_ref, acc_ref)
        o_all = acc_ref[...] / l_ref[...]
        if mode == "fox":
            o = jnp.zeros((tq, D_MODEL), F32)
            hl = _lane_iota((tq, D_MODEL)) >> 6
            for h in range(nh):
                o = jnp.where(hl == h, o_all[h * tq:(h + 1) * tq, :], o)
        else:
            lv = lam_ref[...]
            lam = (jnp.exp(jnp.sum(lv[0:1] * lv[1:2], axis=-1, keepdims=True))
                   - jnp.exp(jnp.sum(lv[2:3] * lv[3:4], axis=-1, keepdims=True)) + lam_init)
            pieces = []
            for h in range(nh // 2):
                a0 = o_all[(2 * h) * tq:(2 * h + 1) * tq, LANES * h:LANES * (h + 1)]
                a1 = o_all[(2 * h + 1) * tq:(2 * h + 2) * tq, LANES * h:LANES * (h + 1)]
                pieces.append(_rms(a0 - lam * a1, gsub_ref[...]) * (1.0 - lam_init))
            o = jnp.concatenate(pieces, axis=-1)
        o_ref[...] = o.astype(BF16)


def _dec_attn(q, kc, vc, kn, vn, extra, *, mode, b, tq, past, tk, lam_init=0.0, name):
    n_cache = past // tk
    nh = D_MODEL // 64
    last = n_cache - 1
    new = pl.BlockSpec((tq, D_MODEL), lambda i, t: (i, 0))
    cache_t = pl.BlockSpec((None, D_MODEL, tk), lambda i, t: (i, 0, jnp.minimum(t, last)))
    if mode == "fox":
        (fk,) = extra
        vspec = cache_t
        especs = [pl.BlockSpec((None, nh, tk), lambda i, t: (i, 0, jnp.minimum(t, last))),
                  pl.BlockSpec((None, nh, LANES), lambda i, t: (i, 0, past // LANES))]
        ins = [fk, fk]
    else:
        nhv = D_MODEL // LANES
        vspec = pl.BlockSpec((tk * nhv, LANES), lambda i, t: (i * n_cache + jnp.minimum(t, last), 0))
        especs = [pl.BlockSpec(x.shape, lambda i, t: (0, 0)) for x in extra]
        ins = list(extra)
    return pl.pallas_call(
        functools.partial(_dec_attn_kernel, mode=mode, n_cache=n_cache, tq=tq, past=past, lam_init=lam_init),
        out_shape=jax.ShapeDtypeStruct((b * tq, D_MODEL), BF16),
        grid=(b, n_cache + 1),
        in_specs=[new, cache_t, vspec, new, new] + especs,
        out_specs=new,
        scratch_shapes=[pltpu.VMEM((nh * tq, D_MODEL), BF16), pltpu.VMEM((nh * tq, 1), F32),
                        pltpu.VMEM((nh * tq, 1), F32), pltpu.VMEM((nh * tq, D_MODEL), F32)],
        compiler_params=_cp(2), name=name)(q, kc, vc, kn, vn, *ins)


def _mla_dec_kernel(q_ref, cc_ref, rc_ref, cn_ref, rn_ref, wuk_ref, wuv_ref, gkn_ref, o_ref,
                    qn, qr, m_ref, l_ref, acc_ref, *, n_cache, tq, past):
    t = pl.program_id(1)
    nh = MLA_HEADS
    rows = nh * tq

    @pl.when(t == 0)
    def _():
        q = q_ref[...]
        zero = jnp.zeros((tq, LANES), BF16)
        for h in range(nh):
            qn[h * tq:(h + 1) * tq, :] = jnp.concatenate(
                [q[:, 2 * LANES * h:2 * LANES * h + LANES] if c == h else zero for c in range(nh)], axis=-1)
            qr[h * tq:(h + 1) * tq, :] = q[:, 2 * LANES * h + LANES:2 * LANES * (h + 1)]
        m_ref[...] = jnp.full_like(m_ref, NEG)
        l_ref[...] = jnp.zeros_like(l_ref)
        acc_ref[...] = jnp.zeros_like(acc_ref)

    def step(ckv, s_rope, mask_new):
        c = ckv.astype(BF16)
        kn = _head_rms(_dot(c, wuk_ref[...]), gkn_ref[...], MLA_NOPE).astype(BF16)
        v = _dot(c, wuv_ref[...]).astype(BF16)
        s = _dot_nt(qn[...], kn) + s_rope
        if mask_new:
            n = ckv.shape[0]
            qpos = past + (_row_iota((rows, n)) & (tq - 1))
            kpos = past + _lane_iota((rows, n))
            s = jnp.where((kpos >> 6) <= (qpos >> 6), s, NEG)
        _softmax_update(s, lambda pb: _dot(pb, v), m_ref, l_ref, acc_ref)

    @pl.when(t < n_cache)
    def _():
        kr = rc_ref[...].astype(BF16)
        step(cc_ref[...], _dot(qr[...], jnp.concatenate([kr, kr], axis=0)), False)

    @pl.when(t == n_cache)
    def _():
        step(cn_ref[...], _dot_nt(qr[...], rn_ref[...].astype(BF16)), True)
        o_all = acc_ref[...] / l_ref[...]
        o_ref[...] = jnp.concatenate(
            [o_all[h * tq:(h + 1) * tq, MLA_V * h:MLA_V * (h + 1)] for h in range(nh)], axis=-1).astype(BF16)


def _mla_dec(q, cc, rc, cn, rn, wuk, wuv, gkn, *, b, tq, past, tk, name):
    n_cache = past // tk
    last = n_cache - 1
    nh = MLA_HEADS
    new = lambda n: pl.BlockSpec((tq, n), lambda i, t: (i, 0))
    full = lambda a: pl.BlockSpec(a.shape, lambda i, t: (0,) * a.ndim)
    return pl.pallas_call(
        functools.partial(_mla_dec_kernel, n_cache=n_cache, tq=tq, past=past),
        out_shape=jax.ShapeDtypeStruct((b * tq, nh * MLA_V), BF16),
        grid=(b, n_cache + 1),
        in_specs=[new(nh * 2 * LANES),
                  pl.BlockSpec((tk, MLA_KV_LORA), lambda i, t: (i * n_cache + jnp.minimum(t, last), 0)),
                  pl.BlockSpec((None, MLA_ROPE, tk), lambda i, t: (i, 0, jnp.minimum(t, last))),
                  new(MLA_KV_LORA), new(LANES), full(wuk), full(wuv), full(gkn)],
        out_specs=new(nh * MLA_V),
        scratch_shapes=[pltpu.VMEM((nh * tq, nh * MLA_NOPE), BF16), pltpu.VMEM((nh * tq, LANES), BF16),
                        pltpu.VMEM((nh * tq, 1), F32), pltpu.VMEM((nh * tq, 1), F32),
                        pltpu.VMEM((nh * tq, nh * MLA_V), F32)],
        compiler_params=_cp(2), name=name)(q, cc, rc, cn, rn, wuk, wuv, gkn)


def _band_dec_kernel(q_ref, kc_ref, vc_ref, kn_ref, vn_ref, tab_ref, o_ref, *, tq, w, past):
    nh = BAND_HEADS
    rows = nh * tq
    bwid = ((w + tq + LANES - 1) // LANES) * LANES
    qb = _block_diag_q(q_ref[...], nh, BAND_DH)
    g = _rel_gather(tab_ref[...], bwid, w + tq) * LOG2E
    bias = jnp.concatenate(
        [pltpu.roll(jnp.broadcast_to(g[h:h + 1, :], (tq, bwid)), bwid - tq, axis=1, stride=1, stride_axis=0)
         for h in range(nh)], axis=0)
    qc = (past + (_row_iota((rows, bwid)) & (tq - 1))) >> 6
    kc = (past - w + _lane_iota((rows, bwid))) >> 6
    bias = jnp.where((kc <= qc) & (kc >= qc - BAND_LEFT_CHUNKS), bias, NEG)
    s_c = _dot(qb, kc_ref[...].astype(BF16)) + bias[:, :w]
    s_n = _dot_nt(qb, kn_ref[...].astype(BF16)) + bias[:, w:w + tq]
    m = jnp.maximum(jnp.max(s_c, axis=-1, keepdims=True), jnp.max(s_n, axis=-1, keepdims=True))
    p_c = jnp.exp2(s_c - m)
    p_n = jnp.exp2(s_n - m)
    l = jnp.sum(p_c, axis=-1, keepdims=True) + jnp.sum(p_n, axis=-1, keepdims=True)
    o_all = (_dot_nt(p_c.astype(BF16), vc_ref[...].astype(BF16))
             + _dot(p_n.astype(BF16), vn_ref[...].astype(BF16))) / l
    o = jnp.zeros((tq, D_MODEL), F32)
    hl = _lane_iota((tq, D_MODEL)) >> _log2(BAND_DH)
    for h in range(nh):
        o = jnp.where(hl == h, o_all[h * tq:(h + 1) * tq, :], o)
    o_ref[...] = o.astype(BF16)


def _band_dec(q, kc, vc, kn, vn, tab, *, b, tq, w, past, name):
    new = pl.BlockSpec((tq, D_MODEL), lambda i: (i, 0))
    cache = pl.BlockSpec((None, D_MODEL, w), lambda i: (i, 0, 0))
    return pl.pallas_call(
        functools.partial(_band_dec_kernel, tq=tq, w=w, past=past),
        out_shape=jax.ShapeDtypeStruct((b * tq, D_MODEL), BF16),
        grid=(b,),
        in_specs=[new, cache, cache, new, new, pl.BlockSpec(tab.shape, lambda i: (0, 0))],
        out_specs=new,
        compiler_params=_cp(1), name=name)(q, kc, vc, kn, vn, tab)


def _ffn_kernel(*refs, mode, tps, seq, final):
    it = iter(refs)
    h_ref, o_ref, wo_ref, gf_ref, wg_ref, wh_ref, cwg_ref, cwh_ref, cbg_ref, cbh_ref, wd_ref = [
        next(it) for _ in range(11)]
    if mode == "state":
        s1g_ref, s1h_ref, s2g_ref, s2h_ref = [next(it) for _ in range(4)]
    if final:
        gfin_ref = next(it)
    out_ref, ug_out, uh_out = next(it), next(it), next(it)
    xn_ref, acc_ref = next(it), next(it)
    if mode == "carry":
        carry_ref = next(it)
    i = pl.program_id(0)
    j = pl.program_id(1)
    nj = pl.num_programs(1)
    tm = h_ref.shape[0]

    @pl.when(j == 0)
    def _():
        h1 = h_ref[...] + _dot(o_ref[...], wo_ref[...])
        acc_ref[...] = h1
        xn_ref[...] = _rms(h1, gf_ref[...]).astype(BF16)

    if mode == "carry":
        @pl.when(i % tps == 0)
        def _():
            carry_ref[0, j] = jnp.zeros((SUBLANES, carry_ref.shape[-1]), F32)
            carry_ref[1, j] = jnp.zeros((SUBLANES, carry_ref.shape[-1]), F32)

    xn = xn_ref[...]

    def conv(u, cw, cb, part):
        if mode == "carry":
            ext = jnp.concatenate([carry_ref[part, j], u], axis=0)
            um1 = pltpu.roll(ext, 1, axis=0)[SUBLANES:]
            um2 = pltpu.roll(ext, 2, axis=0)[SUBLANES:]
            carry_ref[part, j] = u[tm - SUBLANES:]
        else:
            s1, s2 = (s1g_ref, s2g_ref) if part == 0 else (s1h_ref, s2h_ref)
            tpos = _row_iota(u.shape) & (seq - 1)
            um1 = jnp.where(tpos >= 1, pltpu.roll(u, 1, axis=0), s1[...])
            um2 = jnp.where(tpos >= 2, pltpu.roll(u, 2, axis=0), s2[...])
        return ((cb + u * cw[2:3]) + um2 * cw[0:1]) + um1 * cw[1:2]

    ug = _dot(xn, wg_ref[...])
    uh = _dot(xn, wh_ref[...])
    if mode == "carry":
        ug_out[...] = ug[tm - SUBLANES:]
        uh_out[...] = uh[tm - SUBLANES:]
    else:
        ug_out[...] = ug
        uh_out[...] = uh
    cg = conv(ug, cwg_ref[...], cbg_ref[...], 0)
    ch = conv(uh, cwh_ref[...], cbh_ref[...], 1)
    act = (cg * jax.nn.sigmoid(cg) * ch).astype(BF16)
    acc_ref[...] += _dot(act, wd_ref[...])

    @pl.when(j == nj - 1)
    def _():
        y = acc_ref[...]
        if final:
            y = _rms(y, gfin_ref[...])
        out_ref[...] = y


def _ffn(h, o, wo, gf, wup, cw, cb, wd, *, tm, tf, seq, state=None, final_g=None, name):
    m = h.shape[0]
    nff = D_FF // tf
    mode = "carry" if state is None else "state"
    tps = seq // tm if mode == "carry" else 1
    row = pl.BlockSpec((tm, D_MODEL), lambda i, j: (i, 0))
    full = lambda a: pl.BlockSpec(a.shape, lambda i, j: (0,) * a.ndim)
    colg = lambda r: pl.BlockSpec((r, tf), lambda i, j: (0, j))
    colh = lambda r: pl.BlockSpec((r, tf), lambda i, j: (0, nff + j))
    ins = [h, o, wo, gf, wup, wup, cw, cw, cb, cb, wd]
    specs = [row, row, full(wo), full(gf), colg(D_MODEL), colh(D_MODEL), colg(CONV_W), colh(CONV_W), colg(1), colh(1),
             pl.BlockSpec((tf, D_MODEL), lambda i, j: (j, 0))]
    scratch = [pltpu.VMEM((tm, D_MODEL), BF16), pltpu.VMEM((tm, D_MODEL), F32)]
    if mode == "state":
        assert m == tm and seq & (seq - 1) == 0
        s1, s2 = state
        sg = pl.BlockSpec((tm, tf), lambda i, j: (i, j))
        sh = pl.BlockSpec((tm, tf), lambda i, j: (i, nff + j))
        ins += [s1, s1, s2, s2]
        specs += [sg, sh, sg, sh]
        u_shape = jax.ShapeDtypeStruct((m, D_FF), F32)
        u_spec = pl.BlockSpec((tm, tf), lambda i, j: (i, j))
    else:
        assert seq % tm == 0
        scratch.append(pltpu.VMEM((2, nff, SUBLANES, tf), F32))
        u_shape = jax.ShapeDtypeStruct((m // tm, SUBLANES, D_FF), F32)
        u_spec = pl.BlockSpec((None, SUBLANES, tf), lambda i, j: (i, 0, j))
    if final_g is not None:
        ins.append(final_g)
        specs.append(full(final_g))
    kern = functools.partial(_ffn_kernel, mode=mode, tps=tps, seq=seq, final=final_g is not None)
    return pl.pallas_call(
        kern, out_shape=[jax.ShapeDtypeStruct((m, D_MODEL), F32), u_shape, u_shape],
        grid=(m // tm, nff), in_specs=specs, out_specs=[row, u_spec, u_spec], scratch_shapes=scratch,
        compiler_params=_cp(2), name=name)(*ins)


def _rope_tables(pos):
    inv = 1.0 / (ROPE_THETA ** (jnp.arange(0, 64, 2, dtype=F32) / 64))
    ang = pos.astype(F32)[:, None] * inv[None, :]
    c, s = jnp.cos(ang), jnp.sin(ang)
    return jnp.tile(c, (1, 4)), jnp.tile(jnp.concatenate([-s, s], axis=1), (1, 2)), c.T, s.T


def _tile_gain(g, n):
    return jnp.tile(g.astype(F32), n // g.shape[0]).reshape(1, n)


def _gain_t(g, n, w):
    return jnp.broadcast_to(jnp.tile(g.astype(F32), n // g.shape[0])[:, None], (n, w))


def _to_tokens(xt, heads):
    b, n, t = xt.shape
    dh = n // math.prod(heads)
    nd = len(heads)
    return xt.reshape(b, *heads, dh, t).transpose(0, nd + 2, *range(1, nd + 2))


def _to_features(x):
    b, t = x.shape[:2]
    nd = x.ndim
    return x.transpose(0, *range(2, nd), 1).reshape(b, -1, t)


def kernel(x_prompt, x_sample, cache_mla_ckv, cache_mla_krope, cache_fox_k, cache_fox_v, cache_fox_logf,
           cache_diff_k, cache_diff_v, cache_band_k, cache_band_v, state_ffn_conv,
           attn_norm_g, ffn_norm_g, final_norm_g,
           mla_w_dq, mla_g_q, mla_w_uq, mla_w_dkv, mla_g_kv, mla_w_uk, mla_w_uv,
           mla_g_qn, mla_g_qr, mla_g_kn, mla_g_kr, mla_w_o,
           fox_w_qkv, fox_w_f, fox_b_f, fox_g_q, fox_g_k, fox_w_o,
           diff_w_qkv, diff_g_q, diff_g_k, diff_lq1, diff_lk1, diff_lq2, diff_lk2, diff_g_sub, diff_w_o,
           band_w_qkv, band_g_q, band_g_k, band_rel_bias, band_w_o,
           ffn_w_up, ffn_conv_w, ffn_conv_b, ffn_w_down):
    bp, tp, d = x_prompt.shape
    bs, ts, _ = x_sample.shape
    past = cache_mla_ckv.shape[1]
    depth = attn_norm_g.shape[0]
    mp, ms = bp * tp, bs * ts
    tm_p = min(512, tp)
    tm_f = min(1024, tp)
    tq = 512
    tq_band = 256
    tkd = 512
    tf = 256
    assert tp % tq == 0 and past % tkd == 0 and past % CHUNK == 0

    tobf = lambda a: a.astype(BF16)
    rowv = lambda g: g.astype(F32).reshape(1, -1)
    pos_p = jnp.arange(tp, dtype=jnp.int32)
    pos_s = past + jnp.arange(ts, dtype=jnp.int32)
    tabs_p = _rope_tables(pos_p)
    tabs_s = tuple(jnp.tile(a, (bs, 1)) for a in _rope_tables(pos_s)[:2])

    nope_cols = jnp.arange(MLA_HEADS)[:, None] * (MLA_NOPE + MLA_ROPE) + jnp.arange(MLA_NOPE)[None, :]
    rope_cols = jnp.arange(MLA_HEADS)[:, None] * (MLA_NOPE + MLA_ROPE) + MLA_NOPE + jnp.arange(MLA_ROPE)[None, :]
    wuq_perm = jnp.concatenate([mla_w_uq[:, nope_cols.reshape(-1)], mla_w_uq[:, rope_cols.reshape(-1)]], axis=1)
    wr = mla_w_dkv[:, MLA_KV_LORA:]
    mla_w = dict(
        wdq=tobf(mla_w_dq), gq=rowv(mla_g_q), wuq=tobf(wuq_perm), wc=tobf(mla_w_dkv[:, :MLA_KV_LORA]),
        wr=tobf(jnp.concatenate([wr, wr], axis=1)), gkv=rowv(mla_g_kv),
        gqn=_tile_gain(mla_g_qn, MLA_HEADS * MLA_NOPE), gqr=_tile_gain(mla_g_qr, MLA_HEADS * MLA_ROPE),
        gkr=_tile_gain(mla_g_kr, LANES), wuk=tobf(mla_w_uk), wuv=tobf(mla_w_uv),
        gkn=_tile_gain(mla_g_kn, MLA_HEADS * MLA_NOPE), wrt=tobf(wr.T), gkrt=_gain_t(mla_g_kr, MLA_ROPE, tm_p))

    def split_qkv(w):
        wq, wk, wv = w[:, :d], w[:, d:2 * d], w[:, 2 * d:]
        return tobf(wq), tobf(wk.T), tobf(wv), tobf(wv.T)

    fox_wf = tobf(jnp.pad(fox_w_f, ((0, 0), (0, LANES - FOX_HEADS))))
    fox_bf = jnp.pad(fox_b_f.astype(F32), (0, LANES - FOX_HEADS)).reshape(1, LANES)
    fox_wft = tobf(fox_w_f.T)
    fox_bft = jnp.broadcast_to(fox_b_f.astype(F32)[:, None], (FOX_HEADS, tm_p))
    lamv = jnp.stack([diff_lq1, diff_lk1, diff_lq2, diff_lk2]).astype(F32)
    gsub = rowv(diff_g_sub)
    tab_pad = jnp.pad(band_rel_bias.astype(F32), ((0, 0), (0, REL_PAD - band_rel_bias.shape[1])))
    wo = [tobf(mla_w_o), tobf(fox_w_o), tobf(diff_w_o), tobf(band_w_o)]
    wup, wdn = tobf(ffn_w_up), tobf(ffn_w_down)
    cwf, cbf = ffn_conv_w.astype(F32), ffn_conv_b.astype(F32)

    h_p = x_prompt.reshape(mp, d)
    h_s = x_sample.reshape(ms, d)
    outs = {}
    conv_p, conv_s = [], []
    for i in range(depth):
        kind = i % 4
        ga = rowv(attn_norm_g[i])
        if kind == 0:
            scale = (MLA_NOPE + MLA_ROPE) ** -0.5 * LOG2E
            q_p, ckv_p, _, _, kc_p, v_p, krt_p = _mla_proj(h_p, ga, mla_w, tabs_p, b=bp, t=tp, tm=tm_p, with_kv=True,
                                                           scale=scale, name="mla_proj_p")
            o_p = _mla_attn(q_p, kc_p, v_p, b=bp, t=tp, tq=tq, hps=2, name="mla_attn_p")
            q_s, ckv_s, kr_s, krd_s = _mla_proj(h_s, ga, mla_w, tabs_s, b=bs, t=ts, tm=ms, with_kv=False,
                                                scale=scale, name="mla_proj_s")
            o_s = _mla_dec(q_s, cache_mla_ckv.astype(F32).reshape(bs * past, MLA_KV_LORA),
                           _to_features(cache_mla_krope.astype(F32)), ckv_s, krd_s,
                           mla_w["wuk"], mla_w["wuv"], mla_w["gkn"], b=bs, tq=ts, past=past, tk=tkd,
                           name="mla_attn_s")
            outs["mla"] = (ckv_p.reshape(bp, tp, MLA_KV_LORA), jnp.swapaxes(krt_p, 1, 2),
                           ckv_s.reshape(bs, ts, MLA_KV_LORA), kr_s.reshape(bs, ts, MLA_ROPE))
        elif kind == 1:
            scale = FOX_DH ** -0.5 * LOG2E
            gq, gk = _tile_gain(fox_g_q, d), _tile_gain(fox_g_k, d)
            wq, wkt, wv, wvt = split_qkv(fox_w_qkv)
            q_p, kt_p, vt_p, lft_p = _qkv_proj_t(h_p, ga, wq, wkt, wvt, gq, _gain_t(fox_g_k, d, tm_p), b=bp, t=tp,
                                                 tm=tm_p, wft=fox_wft, bft=fox_bft, scale=scale, name="fox_proj_p")
            fk_p = _cumsum_last(lft_p, tk=tq, name="fox_cumsum_p")
            o_p = _pair_attn(q_p, kt_p, vt_p, (fk_p.reshape(bp, FOX_HEADS // 2, 2, tp),), mode="fox",
                             b=bp, t=tp, tq=tq, name="fox_attn_p")
            q_s, k_s, v_s, lf_s = _qkv_proj(h_s, ga, tobf(fox_w_qkv), gq, gk, tm=ms, wf=fox_wf, bf=fox_bf,
                                            scale=scale, name="fox_proj_s")
            lf_all = jnp.concatenate([jnp.swapaxes(cache_fox_logf.astype(F32), 1, 2),
                                      jnp.swapaxes(lf_s.reshape(bs, ts, FOX_HEADS), 1, 2)], axis=2)
            lf_all = jnp.pad(lf_all, ((0, 0), (0, 0), (0, tkd - ts)))
            f_all = _cumsum_last(lf_all, tk=tkd, name="fox_cumsum_s")
            o_s = _dec_attn(q_s, _to_features(cache_fox_k.astype(F32)), _to_features(cache_fox_v.astype(F32)),
                            k_s, v_s, (f_all,), mode="fox", b=bs, tq=ts, past=past, tk=tkd, name="fox_attn_s")
            sh = (FOX_HEADS, FOX_DH)
            outs["fox"] = (_to_tokens(kt_p, (FOX_HEADS,)), _to_tokens(vt_p, (FOX_HEADS,)),
                           jnp.swapaxes(lft_p, 1, 2),
                           k_s.reshape(bs, ts, *sh), v_s.reshape(bs, ts, *sh), lf_s.reshape(bs, ts, FOX_HEADS))
        elif kind == 2:
            scale = DIFF_DH ** -0.5 * LOG2E
            lam_init = 0.8 - 0.6 * math.exp(-0.3 * i)
            gq, gk = _tile_gain(diff_g_q, d), _tile_gain(diff_g_k, d)
            wq, wkt, wv, wvt = split_qkv(diff_w_qkv)
            q_p, kt_p, v4_p = _qkv_proj_t(h_p, ga, wq, wkt, wv, gq, _gain_t(diff_g_k, d, tm_p), b=bp, t=tp, tm=tm_p,
                                          rope_tabs=tabs_p, v_tokens=True, scale=scale, name="diff_proj_p")
            v_p = v4_p.reshape(bp, tp, DIFF_HEADS, 2 * DIFF_DH)
            o_p = _pair_attn(q_p, kt_p, v4_p, (lamv, gsub), mode="diff", b=bp, t=tp, tq=tq, lam_init=lam_init,
                             name="diff_attn_p")
            q_s, k_s, v_s = _qkv_proj(h_s, ga, tobf(diff_w_qkv), gq, gk, tm=ms, rope_tabs=tabs_s, scale=scale,
                                      name="diff_proj_s")
            o_s = _dec_attn(q_s, _to_features(cache_diff_k.astype(F32)),
                            cache_diff_v.astype(F32).reshape(bs * past * DIFF_HEADS, 2 * DIFF_DH), k_s, v_s,
                            (lamv, gsub), mode="diff", b=bs, tq=ts, past=past, tk=tkd, lam_init=lam_init,
                            name="diff_attn_s")
            outs["diff"] = (_to_tokens(kt_p, (DIFF_HEADS, 2)), v_p,
                            k_s.reshape(bs, ts, DIFF_HEADS, 2, DIFF_DH), v_s.reshape(bs, ts, DIFF_HEADS, 2 * DIFF_DH))
        else:
            scale = BAND_DH ** -0.5 * LOG2E
            gq, gk = _tile_gain(band_g_q, d), _tile_gain(band_g_k, d)
            wq, wkt, wv, wvt = split_qkv(band_w_qkv)
            q_p, kt_p, vt_p = _qkv_proj_t(h_p, ga, wq, wkt, wvt, gq, _gain_t(band_g_k, d, tm_p), b=bp, t=tp, tm=tm_p,
                                          scale=scale, name="band_proj_p")
            tab_pairs = jnp.pad(tab_pad.reshape(BAND_HEADS // 2, 2, REL_PAD), ((0, 0), (0, SUBLANES - 2), (0, 0)))
            o_p = _band_attn(q_p, kt_p, vt_p, tab_pairs, b=bp, t=tp, tq=tq_band, name="band_attn_p")
            q_s, k_s, v_s = _qkv_proj(h_s, ga, tobf(band_w_qkv), gq, gk, tm=ms, scale=scale, name="band_proj_s")
            w = cache_band_k.shape[1]
            kct = _to_features(cache_band_k.astype(F32))
            vct = _to_features(cache_band_v.astype(F32))
            o_s = _band_dec(q_s, kct, vct, k_s, v_s, tab_pad, b=bs, tq=ts, w=w, past=past, name="band_attn_s")
            keep = min(BAND_LEFT, tp)
            roll_in = lambda ct, new: jnp.concatenate(
                [ct, jnp.swapaxes(new.reshape(bs, ts, d), 1, 2)], axis=2)[:, :, ts:]
            outs["band"] = (_to_tokens(kt_p[:, :, tp - keep:], (BAND_HEADS,)),
                            _to_tokens(vt_p[:, :, tp - keep:], (BAND_HEADS,)),
                            _to_tokens(roll_in(kct, k_s), (BAND_HEADS,)), _to_tokens(roll_in(vct, v_s), (BAND_HEADS,)))

        last = i == depth - 1
        gfin = rowv(final_norm_g) if last else None
        gfn = rowv(ffn_norm_g[i])
        h_p, ug, uh = _ffn(h_p, o_p, wo[kind], gfn, wup[i], cwf[i], cbf[i].reshape(1, -1), wdn[i], tm=tm_f, tf=tf,
                           seq=tp, final_g=gfin, name="ffn_p")
        tps = tp // tm_f
        conv_p.append(jnp.concatenate([ug, uh], axis=-1)[tps - 1::tps, SUBLANES - (CONV_W - 1):])
        st = state_ffn_conv[i].astype(F32)
        zeros = jnp.zeros((bs, ts, 2 * D_FF), F32)
        s1 = zeros.at[:, 0].set(st[:, 1]).reshape(ms, 2 * D_FF)
        s2 = zeros.at[:, 0].set(st[:, 0]).at[:, 1].set(st[:, 1]).reshape(ms, 2 * D_FF)
        h_s, ug, uh = _ffn(h_s, o_s, wo[kind], gfn, wup[i], cwf[i], cbf[i].reshape(1, -1), wdn[i], tm=ms, tf=tf,
                           seq=ts, state=(s1, s2), final_g=gfin, name="ffn_s")
        u_s = jnp.concatenate([ug, uh], axis=-1).reshape(bs, ts, 2 * D_FF)
        conv_s.append(jnp.concatenate([st, u_s], axis=1)[:, ts:])

    y_prompt = h_p.reshape(bp, tp, d)
    y_sample = h_s.reshape(bs, ts, d)
    return (y_prompt, y_sample) + outs["mla"] + outs["fox"] + outs["diff"] + outs["band"] + (
        jnp.stack(conv_p, axis=0), jnp.stack(conv_s, axis=0))
```

```python
import functools
import math

import jax
import jax.numpy as jnp
from jax import lax
from jax.experimental import pallas as pl
from jax.experimental.pallas import tpu as pltpu

F32 = jnp.float32
BF16 = jnp.bfloat16

D_MODEL = 1024
CHUNK = 64
ROPE_THETA = 10000.0
NORM_EPS = 1e-6
MLA_HEADS, MLA_Q_LORA, MLA_KV_LORA, MLA_NOPE, MLA_ROPE, MLA_V = 8, 384, 256, 128, 64, 128
FOX_HEADS, FOX_DH = 16, 64
DIFF_HEADS, DIFF_DH = 8, 64
BAND_HEADS, BAND_DH = 16, 64
BAND_LEFT_CHUNKS = 8
BAND_LEFT = BAND_LEFT_CHUNKS * CHUNK
REL_CLIP = 128
D_FF = 2816
CONV_W = 3

LANES = 128
SUBLANES = 8
NEG = -1e30
REL_PAD = 384
LOG2E = math.log2(math.e)
VMEM_LIMIT = 56 * 1024 * 1024


def _cp(n_axes):
    return pltpu.CompilerParams(dimension_semantics=("arbitrary",) * n_axes, vmem_limit_bytes=VMEM_LIMIT)


def _dot(a, b):
    return jnp.dot(a, b, preferred_element_type=F32)


def _dot_nt(a, b):
    return lax.dot_general(a, b, (((1,), (1,)), ((), ())), preferred_element_type=F32)


def _rms(x, g):
    ms = jnp.mean(x * x, axis=-1, keepdims=True)
    return x * lax.rsqrt(ms + NORM_EPS) * g


def _lane_iota(shape):
    return lax.broadcasted_iota(jnp.int32, shape, len(shape) - 1)


def _row_iota(shape):
    return lax.broadcasted_iota(jnp.int32, shape, len(shape) - 2)


def _log2(n):
    assert n & (n - 1) == 0, n
    return n.bit_length() - 1


def _head_rms(x, g, seg):
    n = x.shape[1]
    outs = []
    for c in range(n // LANES):
        xs = x[:, LANES * c:LANES * (c + 1)]
        sq = xs * xs
        if seg == LANES:
            r = lax.rsqrt(jnp.sum(sq, axis=-1, keepdims=True) * (1.0 / seg) + NORM_EPS)
        else:
            lo = _lane_iota(xs.shape) < seg
            s_lo = jnp.sum(jnp.where(lo, sq, 0.0), axis=-1, keepdims=True)
            s_hi = jnp.sum(jnp.where(lo, 0.0, sq), axis=-1, keepdims=True)
            r = jnp.where(lo, lax.rsqrt(s_lo * (1.0 / seg) + NORM_EPS), lax.rsqrt(s_hi * (1.0 / seg) + NORM_EPS))
        outs.append(xs * r * g[:, LANES * c:LANES * (c + 1)])
    return jnp.concatenate(outs, axis=-1) if len(outs) > 1 else outs[0]


def _head_rms_t(x, g, seg=64):
    outs = []
    for hd in range(x.shape[0] // seg):
        xs = x[seg * hd:seg * (hd + 1)]
        r = lax.rsqrt(jnp.sum(xs * xs, axis=0, keepdims=True) * (1.0 / seg) + NORM_EPS)
        outs.append(xs * r * g[seg * hd:seg * (hd + 1)])
    return jnp.concatenate(outs, axis=0) if len(outs) > 1 else outs[0]


def _rope(x, cos, sinp):
    lo32 = (_lane_iota((x.shape[0], LANES)) & 63) < 32
    outs = []
    for c in range(x.shape[1] // LANES):
        xs = x[:, LANES * c:LANES * (c + 1)]
        fwd = pltpu.roll(xs, 32, axis=1)
        bwd = pltpu.roll(xs, 96, axis=1)
        outs.append(xs * cos + jnp.where(lo32, bwd, fwd) * sinp)
    return jnp.concatenate(outs, axis=-1) if len(outs) > 1 else outs[0]


def _rope_t(x, cos_t, sin_t):
    outs = []
    for hd in range(x.shape[0] // 64):
        x1 = x[64 * hd:64 * hd + 32]
        x2 = x[64 * hd + 32:64 * (hd + 1)]
        outs += [x1 * cos_t - x2 * sin_t, x2 * cos_t + x1 * sin_t]
    return jnp.concatenate(outs, axis=0)


def _split3(x):
    hi = x.astype(BF16)
    r = x - hi.astype(F32)
    mid = r.astype(BF16)
    lo = (r - mid.astype(F32)).astype(BF16)
    return hi, mid, lo


def _softmax_step(s, pv, m, l, acc):
    m_new = jnp.maximum(m, jnp.max(s, axis=-1, keepdims=True))
    alpha = jnp.exp2(m - m_new)
    p = jnp.exp2(s - m_new)
    if l is not None:
        l = alpha * l + jnp.sum(p, axis=-1, keepdims=True)
    return m_new, l, alpha * acc + pv(p.astype(BF16))


def _softmax_update(s, pv, m_ref, l_ref, acc_ref):
    m, l, acc = _softmax_step(s, pv, m_ref[...], l_ref[...], acc_ref[...])
    m_ref[...] = m
    l_ref[...] = l
    acc_ref[...] = acc


def _log_sigmoid(z):
    return jnp.minimum(z, 0.0) - jnp.log1p(jnp.exp(-jnp.abs(z)))


def _qkv_proj_kernel(*refs, rope, logf, scale):
    it = iter(refs)
    h_ref, g_ref, w_ref, gq_ref, gk_ref = next(it), next(it), next(it), next(it), next(it)
    cos_ref = sin_ref = wf_ref = bf_ref = None
    if rope:
        cos_ref, sin_ref = next(it), next(it)
    if logf:
        wf_ref, bf_ref = next(it), next(it)
    q_out, k_out, v_out = next(it), next(it), next(it)
    a = _rms(h_ref[...], g_ref[...]).astype(BF16)
    qkv = _dot(a, w_ref[...])
    q = _head_rms(qkv[:, :D_MODEL], gq_ref[...], 64)
    k = _head_rms(qkv[:, D_MODEL:2 * D_MODEL], gk_ref[...], 64)
    if rope:
        q = _rope(q, cos_ref[...], sin_ref[...])
        k = _rope(k, cos_ref[...], sin_ref[...])
    q_out[...] = (q * scale).astype(BF16)
    k_out[...] = k
    v_out[...] = qkv[:, 2 * D_MODEL:]
    if logf:
        logf_out = next(it)
        lf = _log_sigmoid(_dot(a, wf_ref[...]) + bf_ref[...])
        logf_out[...] = lf[:, :FOX_HEADS]


def _qkv_proj(h, g, w, gq, gk, *, tm, rope_tabs=None, wf=None, bf=None, scale, name):
    m = h.shape[0]
    grid = (m // tm,)
    row = lambda n: pl.BlockSpec((tm, n), lambda i: (i, 0))
    full = lambda a: pl.BlockSpec(a.shape, lambda i: (0,) * a.ndim)
    ins = [h, g, w, gq, gk]
    specs = [row(D_MODEL), full(g), full(w), full(gq), full(gk)]
    if rope_tabs is not None:
        cos, sinp = rope_tabs
        nt = cos.shape[0] // tm
        tab = pl.BlockSpec((tm, LANES), lambda i: (i % nt, 0))
        ins += [cos, sinp]
        specs += [tab, tab]
    if wf is not None:
        ins += [wf, bf]
        specs += [full(wf), full(bf)]
    outs = [jax.ShapeDtypeStruct((m, D_MODEL), BF16), jax.ShapeDtypeStruct((m, D_MODEL), F32),
            jax.ShapeDtypeStruct((m, D_MODEL), F32)]
    ospecs = [row(D_MODEL), row(D_MODEL), row(D_MODEL)]
    if wf is not None:
        outs.append(jax.ShapeDtypeStruct((m, FOX_HEADS), F32))
        ospecs.append(row(FOX_HEADS))
    kern = functools.partial(_qkv_proj_kernel, rope=rope_tabs is not None, logf=wf is not None, scale=scale)
    return pl.pallas_call(kern, out_shape=outs, grid=grid, in_specs=specs, out_specs=ospecs,
                          compiler_params=_cp(1), name=name)(*ins)


def _qkv_proj_t_kernel(*refs, rope, logf, v_tokens, scale):
    it = iter(refs)
    h_ref, g_ref, wq_ref, wkt_ref, wv_ref, gq_ref, gkt_ref = [next(it) for _ in range(7)]
    if rope:
        cos_ref, sin_ref, cost_ref, sint_ref = [next(it) for _ in range(4)]
    if logf:
        wft_ref, bft_ref = next(it), next(it)
    q_out, kt_out, v_out = next(it), next(it), next(it)
    a = _rms(h_ref[...], g_ref[...]).astype(BF16)
    tm = a.shape[0]
    q = _head_rms(_dot(a, wq_ref[...]), gq_ref[...], 64)
    kt = _head_rms_t(_dot_nt(wkt_ref[...], a), gkt_ref[...])
    if rope:
        q = _rope(q, cos_ref[...], sin_ref[...])
        kt = _rope_t(kt, cost_ref[...], sint_ref[...])
    q_out[...] = (q * scale).astype(BF16)
    kt_out[...] = kt
    if v_tokens:
        v = _dot(a, wv_ref[...])
        nh = D_MODEL // LANES
        for hd in range(nh):
            v_out[pl.ds(hd, tm, stride=nh), :] = v[:, LANES * hd:LANES * (hd + 1)]
    else:
        v_out[...] = _dot_nt(wv_ref[...], a)
    if logf:
        lft_out = next(it)
        lft_out[...] = _log_sigmoid(_dot_nt(wft_ref[...], a) + bft_ref[...])


def _qkv_proj_t(h, g, wq, wkt, wv, gq, gkt, *, b, t, tm, rope_tabs=None, wft=None, bft=None, v_tokens=False,
                scale, name):
    m = b * t
    nt = t // tm
    nh = D_MODEL // LANES
    row = lambda n: pl.BlockSpec((tm, n), lambda i: (i, 0))
    full = lambda a: pl.BlockSpec(a.shape, lambda i: (0,) * a.ndim)
    feat = lambda n: pl.BlockSpec((None, n, tm), lambda i: (i // nt, 0, i % nt))
    ins = [h, g, wq, wkt, wv, gq, gkt]
    specs = [row(D_MODEL)] + [full(x) for x in ins[1:]]
    if rope_tabs is not None:
        cos, sinp, cos_t, sin_t = rope_tabs
        ins += [cos, sinp, cos_t, sin_t]
        tab = pl.BlockSpec((tm, LANES), lambda i: (i % nt, 0))
        tab_t = pl.BlockSpec((32, tm), lambda i: (0, i % nt))
        specs += [tab, tab, tab_t, tab_t]
    if wft is not None:
        ins += [wft, bft]
        specs += [full(wft), full(bft)]
    outs = [jax.ShapeDtypeStruct((m, D_MODEL), BF16), jax.ShapeDtypeStruct((b, D_MODEL, t), F32)]
    ospecs = [row(D_MODEL), feat(D_MODEL)]
    if v_tokens:
        outs.append(jax.ShapeDtypeStruct((m * nh, LANES), F32))
        ospecs.append(pl.BlockSpec((tm * nh, LANES), lambda i: (i, 0)))
    else:
        outs.append(jax.ShapeDtypeStruct((b, D_MODEL, t), F32))
        ospecs.append(feat(D_MODEL))
    if wft is not None:
        outs.append(jax.ShapeDtypeStruct((b, FOX_HEADS, t), F32))
        ospecs.append(feat(FOX_HEADS))
    kern = functools.partial(_qkv_proj_t_kernel, rope=rope_tabs is not None, logf=wft is not None,
                             v_tokens=v_tokens, scale=scale)
    return pl.pallas_call(kern, out_shape=outs, grid=(m // tm,), in_specs=specs, out_specs=ospecs,
                          compiler_params=_cp(1), name=name)(*ins)


def _mla_proj_kernel(*refs, with_kv, scale):
    it = iter(refs)
    (h_ref, g_ref, wdq_ref, gq_ref, wuq_ref, wc_ref, wr_ref, gkv_ref, gqn_ref, gqr_ref, gkr_ref,
     cos_ref, sin_ref) = [next(it) for _ in range(13)]
    if with_kv:
        wuk_ref, wuv_ref, gkn_ref, wrt_ref, gkrt_ref, cost_ref, sint_ref = [next(it) for _ in range(7)]
    q_out, ckv_out, kr_out, krd_out = next(it), next(it), next(it), next(it)
    a = _rms(h_ref[...], g_ref[...]).astype(BF16)
    cq = _rms(_dot(a, wdq_ref[...]), gq_ref[...]).astype(BF16)
    q = _dot(cq, wuq_ref[...])
    n_nope = MLA_HEADS * MLA_NOPE
    qn = _head_rms(q[:, :n_nope], gqn_ref[...], MLA_NOPE) * scale
    qr = _rope(_head_rms(q[:, n_nope:], gqr_ref[...], MLA_ROPE), cos_ref[...], sin_ref[...]) * scale
    lo = _lane_iota((q.shape[0], LANES)) < MLA_ROPE
    pieces = []
    for hd in range(MLA_HEADS):
        slab = qr[:, LANES * (hd // 2):LANES * (hd // 2 + 1)]
        keep = lo if hd % 2 == 0 else jnp.logical_not(lo)
        pieces += [qn[:, LANES * hd:LANES * (hd + 1)], jnp.where(keep, slab, 0.0)]
    q_out[...] = jnp.concatenate(pieces, axis=-1).astype(BF16)
    ckv = _rms(_dot(a, wc_ref[...]), gkv_ref[...])
    ckv_out[...] = ckv
    kr2 = _rope(_head_rms(_dot(a, wr_ref[...]), gkr_ref[...], MLA_ROPE), cos_ref[...], sin_ref[...])
    kr_out[...] = kr2[:, :MLA_ROPE]
    krd_out[...] = kr2
    if with_kv:
        kc_out, v_out, krt_out = next(it), next(it), next(it)
        c = ckv.astype(BF16)
        kn = _head_rms(_dot(c, wuk_ref[...]), gkn_ref[...], MLA_NOPE)
        pieces = []
        for hd in range(MLA_HEADS):
            pieces += [kn[:, LANES * hd:LANES * (hd + 1)], kr2]
        kc_out[...] = jnp.concatenate(pieces, axis=-1).astype(BF16)
        v_out[...] = _dot(c, wuv_ref[...]).astype(BF16)
        krt_out[...] = _rope_t(_head_rms_t(_dot_nt(wrt_ref[...], a), gkrt_ref[...]), cost_ref[...], sint_ref[...])


def _mla_proj(h, g, w, rope_tabs, *, b, t, tm, with_kv, scale, name):
    m = h.shape[0]
    cos, sinp = rope_tabs[:2]
    nt = cos.shape[0] // tm
    row = lambda n: pl.BlockSpec((tm, n), lambda i: (i, 0))
    full = lambda a: pl.BlockSpec(a.shape, lambda i: (0,) * a.ndim)
    tab = pl.BlockSpec((tm, LANES), lambda i: (i % nt, 0))
    ins = [h, g, w["wdq"], w["gq"], w["wuq"], w["wc"], w["wr"], w["gkv"], w["gqn"], w["gqr"], w["gkr"], cos, sinp]
    specs = [row(D_MODEL)] + [full(x) for x in ins[1:11]] + [tab, tab]
    qw = MLA_HEADS * 2 * LANES
    outs = [jax.ShapeDtypeStruct((m, qw), BF16), jax.ShapeDtypeStruct((m, MLA_KV_LORA), F32),
            jax.ShapeDtypeStruct((m, MLA_ROPE), F32), jax.ShapeDtypeStruct((m, LANES), F32)]
    ospecs = [row(qw), row(MLA_KV_LORA), row(MLA_ROPE), row(LANES)]
    if with_kv:
        cos_t, sin_t = rope_tabs[2:]
        tab_t = pl.BlockSpec((32, tm), lambda i: (0, i % nt))
        extra = [w["wuk"], w["wuv"], w["gkn"], w["wrt"], w["gkrt"]]
        ins += extra + [cos_t, sin_t]
        specs += [full(x) for x in extra] + [tab_t, tab_t]
        outs += [jax.ShapeDtypeStruct((m, qw), BF16), jax.ShapeDtypeStruct((m, MLA_HEADS * MLA_V), BF16),
                 jax.ShapeDtypeStruct((b, MLA_ROPE, t), F32)]
        ospecs += [row(qw), row(MLA_HEADS * MLA_V),
                   pl.BlockSpec((None, MLA_ROPE, tm), lambda i: (i // nt, 0, i % nt))]
    kern = functools.partial(_mla_proj_kernel, with_kv=with_kv, scale=scale)
    return pl.pallas_call(kern, out_shape=outs, grid=(m // tm,), in_specs=specs, out_specs=ospecs,
                          compiler_params=_cp(1), name=name)(*ins)


def _cumsum_kernel(x_ref, o_ref, *, tk):
    nh, t = x_ref.shape
    tri = (_row_iota((tk, tk)) <= _lane_iota((tk, tk))).astype(BF16)
    carry = jnp.zeros((nh, 1), F32)
    for c in range(t // tk):
        hi, mid, lo = _split3(x_ref[:, tk * c:tk * (c + 1)])
        y = _dot(jnp.concatenate([hi, mid, lo], axis=0), tri)
        f = (y[:nh] + y[nh:2 * nh]) + y[2 * nh:] + carry
        o_ref[:, tk * c:tk * (c + 1)] = f
        carry = f[:, tk - 1:tk]


def _cumsum_last(x, *, tk, name):
    b, nh, t = x.shape
    spec = pl.BlockSpec((None, nh, t), lambda i: (i, 0, 0))
    return pl.pallas_call(functools.partial(_cumsum_kernel, tk=tk), out_shape=jax.ShapeDtypeStruct(x.shape, F32),
                          grid=(b,), in_specs=[spec], out_specs=spec, compiler_params=_cp(1), name=name)(x)


def _mla_attn_kernel(q_ref, k_ref, v_ref, o_ref, *, tq, hps):
    kw = 2 * LANES
    t = k_ref.shape[0]
    for qi in range(t // tq):
        rows = slice(tq * qi, tq * (qi + 1))
        limit = (((qi * tq + _row_iota((tq, 1))) >> 6) + 1) << 6
        outs = []
        for h in range(hps):
            q = q_ref[rows, kw * h:kw * (h + 1)]
            state = (jnp.full((tq, 1), NEG, F32), jnp.zeros((tq, 1), F32), jnp.zeros((tq, MLA_V), F32))
            for kt in range(qi + 1):
                off = kt * tq
                s = _dot_nt(q, k_ref[off:off + tq, kw * h:kw * (h + 1)])
                if kt == qi:
                    s = jnp.where(off + _lane_iota((1, tq)) < limit, s, NEG)
                v = v_ref[off:off + tq, MLA_V * h:MLA_V * (h + 1)]
                state = _softmax_step(s, lambda pb, v=v: _dot(pb, v), *state)
            outs.append(state[2] / state[1])
        o_ref[rows, :] = jnp.concatenate(outs, axis=-1).astype(BF16)


def _mla_attn(q, kc, v, *, b, t, tq, hps, name):
    kw = 2 * LANES * hps
    vw = MLA_V * hps
    return pl.pallas_call(
        functools.partial(_mla_attn_kernel, tq=tq, hps=hps),
        out_shape=jax.ShapeDtypeStruct((b * t, MLA_HEADS * MLA_V), BF16),
        grid=(b, MLA_HEADS // hps),
        in_specs=[pl.BlockSpec((t, kw), lambda i, h: (i, h)),
                  pl.BlockSpec((t, kw), lambda i, h: (i, h)),
                  pl.BlockSpec((t, vw), lambda i, h: (i, h))],
        out_specs=pl.BlockSpec((t, vw), lambda i, h: (i, h)),
        compiler_params=_cp(2), name=name)(q, kc, v)


def _ones_row_values(vt, c):
    row = _row_iota(vt.shape)
    if c == 0:
        return jnp.where(row < 64, vt, jnp.where(row == 64, 1.0, 0.0)).astype(BF16)
    return jnp.where(row >= 64, vt, jnp.where(row == 0, 1.0, 0.0)).astype(BF16)


def _merge_pair(acc0, acc1):
    lo = _lane_iota(acc0.shape) < 64
    return jnp.where(lo, acc0 / acc0[:, 64:65], acc1 / acc1[:, 0:1])


def _pair_attn_kernel(*refs, mode, tq, lam_init):
    if mode == "fox":
        q_ref, kt_ref, vt_ref, fk_ref, o_ref, ka, va = refs
    else:
        q_ref, kt_ref, v_ref, lam_ref, gsub_ref, o_ref, ka, vb = refs
    t = kt_ref.shape[1]
    if mode == "fox":
        ka[0:LANES, :] = kt_ref[...].astype(BF16)
        f = fk_ref[...] * (-LOG2E)
        terms = [x.astype(F32) for x in _split3(f)]
        nr = 2 * SUBLANES
        row = _row_iota((nr, t))
        aug = jnp.zeros((nr, t), F32)
        for c in range(2):
            for j in range(3):
                aug = jnp.where(row == 3 * c + j, terms[j][c:c + 1, :], aug)
        ka[LANES:LANES + nr, :] = aug.astype(BF16)
        ka[LANES + nr:, :] = jnp.zeros((LANES - nr, t), BF16)
        vt = vt_ref[...]
        va[0] = _ones_row_values(vt, 0)
        va[1] = _ones_row_values(vt, 1)
    else:
        ka[...] = kt_ref[...].astype(BF16)
        nhv = v_ref.shape[0] // t
        vb[...] = v_ref[pl.ds(pl.program_id(1), t, stride=nhv), :].astype(BF16)

    def q_tile(qi, q):
        lane = _lane_iota((tq, LANES))
        zero = jnp.zeros_like(q)
        qa = [jnp.where(lane < 64, q, zero), jnp.where(lane < 64, zero, q)]
        qpos = qi * tq + _row_iota((tq, 1))
        if mode == "fox":
            limit = qpos + 1
            pick = [jnp.where(lane < 3 * c, 0.0, jnp.where(lane < 3 * c + 3, 1.0, 0.0)).astype(BF16)
                    for c in range(2)]
            qa = [jnp.concatenate([qa[c], pick[c]], axis=-1) for c in range(2)]
        else:
            limit = ((qpos >> 6) + 1) << 6

        def tile(kt, carry, masked):
            off = kt * tq
            k = ka[:, pl.ds(off, tq)]
            new = []
            for c in range(2):
                m, l, acc = carry[c]
                s = _dot(qa[c], k)
                if masked:
                    s = jnp.where(off + _lane_iota((1, tq)) < limit, s, NEG)
                if mode == "fox":
                    v = va[c, :, pl.ds(off, tq)]
                    new.append(_softmax_step(s, lambda pb, v=v: _dot_nt(pb, v), m, None, acc))
                else:
                    v = vb[pl.ds(off, tq), :]
                    new.append(_softmax_step(s, lambda pb, v=v: _dot(pb, v), m, l, acc))
            return tuple(new)

        carry = tuple((jnp.full((tq, 1), NEG, F32), None if mode == "fox" else jnp.zeros((tq, 1), F32),
                       jnp.zeros((tq, LANES), F32)) for _ in range(2))
        for kt in range(qi):
            carry = tile(kt, carry, False)
        (_, l0, a0), (_, l1, a1) = tile(qi, carry, True)
        if mode == "fox":
            o = _merge_pair(a0, a1)
        else:
            lv = lam_ref[...]
            lam = (jnp.exp(jnp.sum(lv[0:1] * lv[1:2], axis=-1, keepdims=True))
                   - jnp.exp(jnp.sum(lv[2:3] * lv[3:4], axis=-1, keepdims=True)) + lam_init)
            o = _rms(a0 / l0 - lam * (a1 / l1), gsub_ref[...]) * (1.0 - lam_init)
        return o.astype(BF16)

    for qi in range(t // tq):
        o_ref[tq * qi:tq * (qi + 1), :] = q_tile(qi, q_ref[tq * qi:tq * (qi + 1), :])


def _pair_attn(q, kt, v, extra, *, mode, b, t, tq, lam_init=0.0, name):
    npair = D_MODEL // LANES
    qspec = pl.BlockSpec((t, LANES), lambda i, p: (i, p))
    ktspec = pl.BlockSpec((None, LANES, t), lambda i, p: (i, p, 0))
    if mode == "fox":
        vspec = ktspec
        especs = [pl.BlockSpec((None, None, 2, t), lambda i, p: (i, p, 0, 0))]
        scratch = [pltpu.VMEM((2 * LANES, t), BF16), pltpu.VMEM((2, LANES, t), BF16)]
    else:
        vspec = pl.BlockSpec((t * npair, LANES), lambda i, p: (i, 0))
        especs = [pl.BlockSpec(x.shape, lambda i, p: (0, 0)) for x in extra]
        scratch = [pltpu.VMEM((LANES, t), BF16), pltpu.VMEM((t, LANES), BF16)]
    return pl.pallas_call(
        functools.partial(_pair_attn_kernel, mode=mode, tq=tq, lam_init=lam_init),
        out_shape=jax.ShapeDtypeStruct((b * t, D_MODEL), BF16),
        grid=(b, npair),
        in_specs=[qspec, ktspec, vspec] + especs,
        out_specs=qspec,
        scratch_shapes=scratch,
        compiler_params=_cp(2), name=name)(q, kt, v, *extra)


def _rel_gather(tab, width, center):
    idx = jnp.clip(center - _lane_iota((REL_PAD, width)), -REL_CLIP, REL_CLIP) + REL_CLIP
    onehot = (_row_iota((REL_PAD, width)) == idx).astype(BF16)
    hi, mid, lo = _split3(tab)
    return (_dot(hi, onehot) + _dot(mid, onehot)) + _dot(lo, onehot)


def _band_attn_kernel(q_ref, kt_ref, vt_ref, tab_ref, o_ref, kb, va, bias_ref, *, tq, win, bw):
    gw = bw + tq

    @pl.when(pl.program_id(1) == 0)
    def _():
        g = _rel_gather(tab_ref[...], gw, BAND_LEFT + tq) * LOG2E
        ii = _row_iota((tq, bw)) >> 6
        jj = _lane_iota((tq, bw)) >> 6
        allowed = (jj >= ii) & (jj <= ii + BAND_LEFT_CHUNKS)
        for c in range(2):
            rows = jnp.broadcast_to(g[c:c + 1, :], (tq, gw))
            skew = pltpu.roll(rows, gw - tq, axis=1, stride=1, stride_axis=0)
            bias_ref[c] = jnp.where(allowed, skew[:, :bw], NEG)

    kb[...] = kt_ref[...].astype(BF16)
    vt = vt_ref[...]
    va[0] = _ones_row_values(vt, 0)
    va[1] = _ones_row_values(vt, 1)

    lo = _lane_iota((tq, LANES)) < 64
    for r in range(q_ref.shape[0] // tq):
        q0 = r * tq
        ws = max(q0 - BAND_LEFT, 0)
        d = BAND_LEFT - q0 + ws
        k = kb[:, pl.ds(ws, win)]
        q = q_ref[tq * r:tq * (r + 1), :]
        zero = jnp.zeros_like(q)
        accs = []
        for c in range(2):
            qc = jnp.where(lo, q, zero) if c == 0 else jnp.where(lo, zero, q)
            s = _dot(qc, k) + bias_ref[c, :, pl.ds(d, win)]
            p = jnp.exp2(s - jnp.max(s, axis=-1, keepdims=True))
            accs.append(_dot_nt(p.astype(BF16), va[c, :, pl.ds(ws, win)]))
        o_ref[tq * r:tq * (r + 1), :] = _merge_pair(*accs).astype(BF16)


def _band_attn(q, kt, vt, tab, *, b, t, tq, name):
    npair = D_MODEL // LANES
    win = BAND_LEFT + tq
    bw = win + BAND_LEFT
    assert t >= win and tq % CHUNK == 0 and BAND_LEFT % tq == 0
    qspec = pl.BlockSpec((t, LANES), lambda p, i: (i, p))
    kvspec = pl.BlockSpec((None, LANES, t), lambda p, i: (i, p, 0))
    return pl.pallas_call(
        functools.partial(_band_attn_kernel, tq=tq, win=win, bw=bw),
        out_shape=jax.ShapeDtypeStruct((b * t, D_MODEL), BF16),
        grid=(npair, b),
        in_specs=[qspec, kvspec, kvspec, pl.BlockSpec((None, SUBLANES, REL_PAD), lambda p, i: (p, 0, 0))],
        out_specs=qspec,
        scratch_shapes=[pltpu.VMEM((LANES, t), BF16), pltpu.VMEM((2, LANES, t), BF16), pltpu.VMEM((2, tq, bw), F32)],
        compiler_params=_cp(2), name=name)(q, kt, vt, tab)


def _block_diag_q(q, nh, width):
    tq = q.shape[0]
    rep = jnp.concatenate([q] * nh, axis=0)
    keep = (_row_iota(rep.shape) >> _log2(tq)) == (_lane_iota(rep.shape) >> _log2(width))
    return jnp.where(keep, rep, jnp.zeros_like(rep))


def _expand_rows(f, tq):
    return jnp.concatenate([jnp.broadcast_to(f[h:h + 1, :], (tq, f.shape[1])) for h in range(f.shape[0])], axis=0)


def _dec_attn_kernel(*refs, mode, n_cache, tq, past, lam_init):
    if mode == "fox":
        (q_ref, kc_ref, vc_ref, kn_ref, vn_ref, fkc_ref, fkn_ref, o_ref, qb, m_ref, l_ref, acc_ref) = refs
    else:
        (q_ref, kc_ref, vc_ref, kn_ref, vn_ref, lam_ref, gsub_ref, o_ref, qb, m_ref, l_ref, acc_ref) = refs
    t = pl.program_id(1)
    nh = D_MODEL // 64
    rows = nh * tq

    @pl.when(t == 0)
    def _():
        qb[...] = _block_diag_q(q_ref[...], nh, 64)
        m_ref[...] = jnp.full_like(m_ref, NEG)
        l_ref[...] = jnp.zeros_like(l_ref)
        acc_ref[...] = jnp.zeros_like(acc_ref)

    @pl.when(t < n_cache)
    def _():
        s = _dot(qb[...], kc_ref[...].astype(BF16))
        if mode == "fox":
            s = s - _expand_rows(fkc_ref[...] * LOG2E, tq)
            v = vc_ref[...].astype(BF16)
            _softmax_update(s, lambda pb: _dot_nt(pb, v), m_ref, l_ref, acc_ref)
        else:
            tk = kc_ref.shape[1]
            nhv = D_MODEL // LANES
            v = jnp.concatenate([vc_ref[pl.ds(h, tk, stride=nhv), :].astype(BF16) for h in range(nhv)], axis=-1)
            _softmax_update(s, lambda pb: _dot(pb, v), m_ref, l_ref, acc_ref)

    @pl.when(t == n_cache)
    def _():
        s = _dot_nt(qb[...], kn_ref[...].astype(BF16))
        qpos = past + (_row_iota((rows, tq)) & (tq - 1))
        kpos = past + _lane_iota((rows, tq))
        if mode == "fox":
            s = s - _expand_rows(fkn_ref[...][:, :tq] * LOG2E, tq)
            allowed = kpos <= qpos
        else:
            allowed = (kpos >> 6) <= (qpos >> 6)
        vn = vn_ref[...].astype(BF16)
        _softmax_update(jnp.where(allowed, s, NEG), lambda pb: _dot(pb, vn), m_ref, l_ref, acc_ref)
        o_all = acc_ref[...] / l_ref[...]
        if mode == "fox":
            o = jnp.zeros((tq, D_MODEL), F32)
            hl = _lane_iota((tq, D_MODEL)) >> 6
            for h in range(nh):
                o = jnp.where(hl == h, o_all[h * tq:(h + 1) * tq, :], o)
        else:
            lv = lam_ref[...]
            lam = (jnp.exp(jnp.sum(lv[0:1] * lv[1:2], axis=-1, keepdims=True))
                   - jnp.exp(jnp.sum(lv[2:3] * lv[3:4], axis=-1, keepdims=True)) + lam_init)
            pieces = []
            for h in range(nh // 2):
                a0 = o_all[(2 * h) * tq:(2 * h + 1) * tq, LANES * h:LANES * (h + 1)]
                a1 = o_all[(2 * h + 1) * tq:(2 * h + 2) * tq, LANES * h:LANES * (h + 1)]
                pieces.append(_rms(a0 - lam * a1, gsub_ref[...]) * (1.0 - lam_init))
            o = jnp.concatenate(pieces, axis=-1)
        o_ref[...] = o.astype(BF16)


def _dec_attn(q, kc, vc, kn, vn, extra, *, mode, b, tq, past, tk, lam_init=0.0, name):
    n_cache = past // tk
    nh = D_MODEL // 64
    last = n_cache - 1
    new = pl.BlockSpec((tq, D_MODEL), lambda i, t: (i, 0))
    cache_t = pl.BlockSpec((None, D_MODEL, tk), lambda i, t: (i, 0, jnp.minimum(t, last)))
    if mode == "fox":
        (fk,) = extra
        vspec = cache_t
        especs = [pl.BlockSpec((None, nh, tk), lambda i, t: (i, 0, jnp.minimum(t, last))),
                  pl.BlockSpec((None, nh, LANES), lambda i, t: (i, 0, past // LANES))]
        ins = [fk, fk]
    else:
        nhv = D_MODEL // LANES
        vspec = pl.BlockSpec((tk * nhv, LANES), lambda i, t: (i * n_cache + jnp.minimum(t, last), 0))
        especs = [pl.BlockSpec(x.shape, lambda i, t: (0, 0)) for x in extra]
        ins = list(extra)
    return pl.pallas_call(
        functools.partial(_dec_attn_kernel, mode=mode, n_cache=n_cache, tq=tq, past=past, lam_init=lam_init),
        out_shape=jax.ShapeDtypeStruct((b * tq, D_MODEL), BF16),
        grid=(b, n_cache + 1),
        in_specs=[new, cache_t, vspec, new, new] + especs,
        out_specs=new,
        scratch_shapes=[pltpu.VMEM((nh * tq, D_MODEL), BF16), pltpu.VMEM((nh * tq, 1), F32),
                        pltpu.VMEM((nh * tq, 1), F32), pltpu.VMEM((nh * tq, D_MODEL), F32)],
        compiler_params=_cp(2), name=name)(q, kc, vc, kn, vn, *ins)


def _mla_dec_kernel(q_ref, cc_ref, rc_ref, cn_ref, rn_ref, wuk_ref, wuv_ref, gkn_ref, o_ref,
                    qn, qr, m_ref, l_ref, acc_ref, *, n_cache, tq, past):
    t = pl.program_id(1)
    nh = MLA_HEADS
    rows = nh * tq

    @pl.when(t == 0)
    def _():
        q = q_ref[...]
        zero = jnp.zeros((tq, LANES), BF16)
        for h in range(nh):
            qn[h * tq:(h + 1) * tq, :] = jnp.concatenate(
                [q[:, 2 * LANES * h:2 * LANES * h + LANES] if c == h else zero for c in range(nh)], axis=-1)
            qr[h * tq:(h + 1) * tq, :] = q[:, 2 * LANES * h + LANES:2 * LANES * (h + 1)]
        m_ref[...] = jnp.full_like(m_ref, NEG)
        l_ref[...] = jnp.zeros_like(l_ref)
        acc_ref[...] = jnp.zeros_like(acc_ref)

    def step(ckv, s_rope, mask_new):
        c = ckv.astype(BF16)
        kn = _head_rms(_dot(c, wuk_ref[...]), gkn_ref[...], MLA_NOPE).astype(BF16)
        v = _dot(c, wuv_ref[...]).astype(BF16)
        s = _dot_nt(qn[...], kn) + s_rope
        if mask_new:
            n = ckv.shape[0]
            qpos = past + (_row_iota((rows, n)) & (tq - 1))
            kpos = past + _lane_iota((rows, n))
            s = jnp.where((kpos >> 6) <= (qpos >> 6), s, NEG)
        _softmax_update(s, lambda pb: _dot(pb, v), m_ref, l_ref, acc_ref)

    @pl.when(t < n_cache)
    def _():
        kr = rc_ref[...].astype(BF16)
        step(cc_ref[...], _dot(qr[...], jnp.concatenate([kr, kr], axis=0)), False)

    @pl.when(t == n_cache)
    def _():
        step(cn_ref[...], _dot_nt(qr[...], rn_ref[...].astype(BF16)), True)
        o_all = acc_ref[...] / l_ref[...]
        o_ref[...] = jnp.concatenate(
            [o_all[h * tq:(h + 1) * tq, MLA_V * h:MLA_V * (h + 1)] for h in range(nh)], axis=-1).astype(BF16)


def _mla_dec(q, cc, rc, cn, rn, wuk, wuv, gkn, *, b, tq, past, tk, name):
    n_cache = past // tk
    last = n_cache - 1
    nh = MLA_HEADS
    new = lambda n: pl.BlockSpec((tq, n), lambda i, t: (i, 0))
    full = lambda a: pl.BlockSpec(a.shape, lambda i, t: (0,) * a.ndim)
    return pl.pallas_call(
        functools.partial(_mla_dec_kernel, n_cache=n_cache, tq=tq, past=past),
        out_shape=jax.ShapeDtypeStruct((b * tq, nh * MLA_V), BF16),
        grid=(b, n_cache + 1),
        in_specs=[new(nh * 2 * LANES),
                  pl.BlockSpec((tk, MLA_KV_LORA), lambda i, t: (i * n_cache + jnp.minimum(t, last), 0)),
                  pl.BlockSpec((None, MLA_ROPE, tk), lambda i, t: (i, 0, jnp.minimum(t, last))),
                  new(MLA_KV_LORA), new(LANES), full(wuk), full(wuv), full(gkn)],
        out_specs=new(nh * MLA_V),
        scratch_shapes=[pltpu.VMEM((nh * tq, nh * MLA_NOPE), BF16), pltpu.VMEM((nh * tq, LANES), BF16),
                        pltpu.VMEM((nh * tq, 1), F32), pltpu.VMEM((nh * tq, 1), F32),
                        pltpu.VMEM((nh * tq, nh * MLA_V), F32)],
        compiler_params=_cp(2), name=name)(q, cc, rc, cn, rn, wuk, wuv, gkn)


def _band_dec_kernel(q_ref, kc_ref, vc_ref, kn_ref, vn_ref, tab_ref, o_ref, *, tq, w, past):
    nh = BAND_HEADS
    rows = nh * tq
    bwid = ((w + tq + LANES - 1) // LANES) * LANES
    qb = _block_diag_q(q_ref[...], nh, BAND_DH)
    g = _rel_gather(tab_ref[...], bwid, w + tq) * LOG2E
    bias = jnp.concatenate(
        [pltpu.roll(jnp.broadcast_to(g[h:h + 1, :], (tq, bwid)), bwid - tq, axis=1, stride=1, stride_axis=0)
         for h in range(nh)], axis=0)
    qc = (past + (_row_iota((rows, bwid)) & (tq - 1))) >> 6
    kc = (past - w + _lane_iota((rows, bwid))) >> 6
    bias = jnp.where((kc <= qc) & (kc >= qc - BAND_LEFT_CHUNKS), bias, NEG)
    s_c = _dot(qb, kc_ref[...].astype(BF16)) + bias[:, :w]
    s_n = _dot_nt(qb, kn_ref[...].astype(BF16)) + bias[:, w:w + tq]
    m = jnp.maximum(jnp.max(s_c, axis=-1, keepdims=True), jnp.max(s_n, axis=-1, keepdims=True))
    p_c = jnp.exp2(s_c - m)
    p_n = jnp.exp2(s_n - m)
    l = jnp.sum(p_c, axis=-1, keepdims=True) + jnp.sum(p_n, axis=-1, keepdims=True)
    o_all = (_dot_nt(p_c.astype(BF16), vc_ref[...].astype(BF16))
             + _dot(p_n.astype(BF16), vn_ref[...].astype(BF16))) / l
    o = jnp.zeros((tq, D_MODEL), F32)
    hl = _lane_iota((tq, D_MODEL)) >> _log2(BAND_DH)
    for h in range(nh):
        o = jnp.where(hl == h, o_all[h * tq:(h + 1) * tq, :], o)
    o_ref[...] = o.astype(BF16)


def _band_dec(q, kc, vc, kn, vn, tab, *, b, tq, w, past, name):
    new = pl.BlockSpec((tq, D_MODEL), lambda i: (i, 0))
    cache = pl.BlockSpec((None, D_MODEL, w), lambda i: (i, 0, 0))
    return pl.pallas_call(
        functools.partial(_band_dec_kernel, tq=tq, w=w, past=past),
        out_shape=jax.ShapeDtypeStruct((b * tq, D_MODEL), BF16),
        grid=(b,),
        in_specs=[new, cache, cache, new, new, pl.BlockSpec(tab.shape, lambda i: (0, 0))],
        out_specs=new,
        compiler_params=_cp(1), name=name)(q, kc, vc, kn, vn, tab)


def _ffn_rows_kernel(*refs, mode, tps, seq, final, tf):
    it = iter(refs)
    h_ref, o_ref, wo_ref, gf_ref, wup_ref, cw_ref, cb_ref, wd_ref = [next(it) for _ in range(8)]
    if mode == "state":
        s1_ref, s2_ref = next(it), next(it)
    if final:
        gfin_ref = next(it)
    out_ref, ug_out, uh_out = next(it), next(it), next(it)
    if mode == "carry":
        carry_ref = next(it)
    tm = h_ref.shape[0]
    h1 = h_ref[...] + _dot(o_ref[...], wo_ref[...])
    xn = _rms(h1, gf_ref[...]).astype(BF16)
    if mode == "carry":
        @pl.when(pl.program_id(0) % tps == 0)
        def _():
            carry_ref[...] = jnp.zeros_like(carry_ref)
    else:
        tpos = _row_iota((tm, tf)) & (seq - 1)
    acc = h1
    for j in range(D_FF // tf):
        cs = []
        for part, u_out in enumerate((ug_out, uh_out)):
            tile = slice(tf * j, tf * (j + 1))
            cols = slice(part * D_FF + tf * j, part * D_FF + tf * (j + 1))
            u = _dot(xn, wup_ref[:, cols])
            if mode == "carry":
                tail = u[tm - SUBLANES:]
                u_out[:, tile] = tail
                ext = jnp.concatenate([carry_ref[part, :, tile], u], axis=0)
                carry_ref[part, :, tile] = tail
                um1 = pltpu.roll(ext, 1, axis=0)[SUBLANES:]
                um2 = pltpu.roll(ext, 2, axis=0)[SUBLANES:]
            else:
                u_out[:, tile] = u
                um1 = jnp.where(tpos >= 1, pltpu.roll(u, 1, axis=0), s1_ref[:, cols])
                um2 = jnp.where(tpos >= 2, pltpu.roll(u, 2, axis=0), s2_ref[:, cols])
            cw = cw_ref[:, cols]
            cs.append(((cb_ref[:, cols] + u * cw[2:3]) + um2 * cw[0:1]) + um1 * cw[1:2])
        act = (cs[0] * jax.nn.sigmoid(cs[0]) * cs[1]).astype(BF16)
        acc = acc + _dot(act, wd_ref[tf * j:tf * (j + 1), :])
    if final:
        acc = _rms(acc, gfin_ref[...])
    out_ref[...] = acc


def _ffn_rows(h, o, wo, gf, wup, cw, cb, wd, *, tm, tf, seq, state=None, final_g=None, name):
    m = h.shape[0]
    mode = "carry" if state is None else "state"
    tps = seq // tm if mode == "carry" else 1
    row = lambda n: pl.BlockSpec((tm, n), lambda i: (i, 0))
    once = lambda a: pl.BlockSpec(a.shape, lambda i: (0,) * a.ndim, pipeline_mode=pl.Buffered(1))
    ins = [h, o, wo, gf, wup, cw, cb, wd]
    specs = [row(D_MODEL), row(D_MODEL)] + [once(x) for x in ins[2:]]
    scratch = []
    if mode == "state":
        assert m == tm and seq & (seq - 1) == 0
        ins += list(state)
        specs += [row(2 * D_FF), row(2 * D_FF)]
        u_shape = jax.ShapeDtypeStruct((m, D_FF), F32)
        u_spec = row(D_FF)
    else:
        assert seq % tm == 0
        scratch.append(pltpu.VMEM((2, SUBLANES, D_FF), F32))
        u_shape = jax.ShapeDtypeStruct((m // tm, SUBLANES, D_FF), F32)
        u_spec = pl.BlockSpec((None, SUBLANES, D_FF), lambda i: (i, 0, 0))
    if final_g is not None:
        ins.append(final_g)
        specs.append(once(final_g))
    kern = functools.partial(_ffn_rows_kernel, mode=mode, tps=tps, seq=seq, final=final_g is not None, tf=tf)
    return pl.pallas_call(
        kern, out_shape=[jax.ShapeDtypeStruct((m, D_MODEL), F32), u_shape, u_shape],
        grid=(m // tm,), in_specs=specs, out_specs=[row(D_MODEL), u_spec, u_spec], scratch_shapes=scratch,
        compiler_params=_cp(1), name=name)(*ins)


def _rope_tables(pos):
    inv = 1.0 / (ROPE_THETA ** (jnp.arange(0, 64, 2, dtype=F32) / 64))
    ang = pos.astype(F32)[:, None] * inv[None, :]
    c, s = jnp.cos(ang), jnp.sin(ang)
    return jnp.tile(c, (1, 4)), jnp.tile(jnp.concatenate([-s, s], axis=1), (1, 2)), c.T, s.T


def _tile_gain(g, n):
    return jnp.tile(g.astype(F32), n // g.shape[0]).reshape(1, n)


def _gain_t(g, n, w):
    return jnp.broadcast_to(jnp.tile(g.astype(F32), n // g.shape[0])[:, None], (n, w))


def _to_tokens(xt, heads):
    b, n, t = xt.shape
    dh = n // math.prod(heads)
    nd = len(heads)
    return xt.reshape(b, *heads, dh, t).transpose(0, nd + 2, *range(1, nd + 2))


def _to_features(x):
    b, t = x.shape[:2]
    nd = x.ndim
    return x.transpose(0, *range(2, nd), 1).reshape(b, -1, t)


def kernel(x_prompt, x_sample, cache_mla_ckv, cache_mla_krope, cache_fox_k, cache_fox_v, cache_fox_logf,
           cache_diff_k, cache_diff_v, cache_band_k, cache_band_v, state_ffn_conv,
           attn_norm_g, ffn_norm_g, final_norm_g,
           mla_w_dq, mla_g_q, mla_w_uq, mla_w_dkv, mla_g_kv, mla_w_uk, mla_w_uv,
           mla_g_qn, mla_g_qr, mla_g_kn, mla_g_kr, mla_w_o,
           fox_w_qkv, fox_w_f, fox_b_f, fox_g_q, fox_g_k, fox_w_o,
           diff_w_qkv, diff_g_q, diff_g_k, diff_lq1, diff_lk1, diff_lq2, diff_lk2, diff_g_sub, diff_w_o,
           band_w_qkv, band_g_q, band_g_k, band_rel_bias, band_w_o,
           ffn_w_up, ffn_conv_w, ffn_conv_b, ffn_w_down):
    bp, tp, d = x_prompt.shape
    bs, ts, _ = x_sample.shape
    past = cache_mla_ckv.shape[1]
    depth = attn_norm_g.shape[0]
    mp, ms = bp * tp, bs * ts
    tm_p = min(512, tp)
    tm_f = min(512, tp)
    tq = 512
    tq_band = 256
    tkd = min(1024, past)
    tf = 256
    assert tp % tq == 0 and past % tkd == 0 and past % CHUNK == 0

    tobf = lambda a: a.astype(BF16)
    rowv = lambda g: g.astype(F32).reshape(1, -1)
    pos_p = jnp.arange(tp, dtype=jnp.int32)
    pos_s = past + jnp.arange(ts, dtype=jnp.int32)
    tabs_p = _rope_tables(pos_p)
    tabs_s = tuple(jnp.tile(a, (bs, 1)) for a in _rope_tables(pos_s)[:2])

    nope_cols = jnp.arange(MLA_HEADS)[:, None] * (MLA_NOPE + MLA_ROPE) + jnp.arange(MLA_NOPE)[None, :]
    rope_cols = jnp.arange(MLA_HEADS)[:, None] * (MLA_NOPE + MLA_ROPE) + MLA_NOPE + jnp.arange(MLA_ROPE)[None, :]
    wuq_perm = jnp.concatenate([mla_w_uq[:, nope_cols.reshape(-1)], mla_w_uq[:, rope_cols.reshape(-1)]], axis=1)
    wr = mla_w_dkv[:, MLA_KV_LORA:]
    mla_w = dict(
        wdq=tobf(mla_w_dq), gq=rowv(mla_g_q), wuq=tobf(wuq_perm), wc=tobf(mla_w_dkv[:, :MLA_KV_LORA]),
        wr=tobf(jnp.concatenate([wr, wr], axis=1)), gkv=rowv(mla_g_kv),
        gqn=_tile_gain(mla_g_qn, MLA_HEADS * MLA_NOPE), gqr=_tile_gain(mla_g_qr, MLA_HEADS * MLA_ROPE),
        gkr=_tile_gain(mla_g_kr, LANES), wuk=tobf(mla_w_uk), wuv=tobf(mla_w_uv),
        gkn=_tile_gain(mla_g_kn, MLA_HEADS * MLA_NOPE), wrt=tobf(wr.T), gkrt=_gain_t(mla_g_kr, MLA_ROPE, tm_p))

    def split_qkv(w):
        wq, wk, wv = w[:, :d], w[:, d:2 * d], w[:, 2 * d:]
        return tobf(wq), tobf(wk.T), tobf(wv), tobf(wv.T)

    fox_wf = tobf(jnp.pad(fox_w_f, ((0, 0), (0, LANES - FOX_HEADS))))
    fox_bf = jnp.pad(fox_b_f.astype(F32), (0, LANES - FOX_HEADS)).reshape(1, LANES)
    fox_wft = tobf(fox_w_f.T)
    fox_bft = jnp.broadcast_to(fox_b_f.astype(F32)[:, None], (FOX_HEADS, tm_p))
    lamv = jnp.stack([diff_lq1, diff_lk1, diff_lq2, diff_lk2]).astype(F32)
    gsub = rowv(diff_g_sub)
    tab_pad = jnp.pad(band_rel_bias.astype(F32), ((0, 0), (0, REL_PAD - band_rel_bias.shape[1])))
    wo = [tobf(mla_w_o), tobf(fox_w_o), tobf(diff_w_o), tobf(band_w_o)]
    wup, wdn = tobf(ffn_w_up), tobf(ffn_w_down)
    cwf, cbf = ffn_conv_w.astype(F32), ffn_conv_b.astype(F32)

    h_p = x_prompt.reshape(mp, d)
    h_s = x_sample.reshape(ms, d)
    outs = {}
    conv_p, conv_s = [], []
    for i in range(depth):
        kind = i % 4
        ga = rowv(attn_norm_g[i])
        if kind == 0:
            scale = (MLA_NOPE + MLA_ROPE) ** -0.5 * LOG2E
            q_p, ckv_p, _, _, kc_p, v_p, krt_p = _mla_proj(h_p, ga, mla_w, tabs_p, b=bp, t=tp, tm=tm_p, with_kv=True,
                                                           scale=scale, name="mla_proj_p")
            o_p = _mla_attn(q_p, kc_p, v_p, b=bp, t=tp, tq=tq, hps=2, name="mla_attn_p")
            q_s, ckv_s, kr_s, krd_s = _mla_proj(h_s, ga, mla_w, tabs_s, b=bs, t=ts, tm=ms, with_kv=False,
                                                scale=scale, name="mla_proj_s")
            o_s = _mla_dec(q_s, cache_mla_ckv.astype(F32).reshape(bs * past, MLA_KV_LORA),
                           _to_features(cache_mla_krope.astype(F32)), ckv_s, krd_s,
                           mla_w["wuk"], mla_w["wuv"], mla_w["gkn"], b=bs, tq=ts, past=past, tk=tkd,
                           name="mla_attn_s")
            outs["mla"] = (ckv_p.reshape(bp, tp, MLA_KV_LORA), jnp.swapaxes(krt_p, 1, 2),
                           ckv_s.reshape(bs, ts, MLA_KV_LORA), kr_s.reshape(bs, ts, MLA_ROPE))
        elif kind == 1:
            scale = FOX_DH ** -0.5 * LOG2E
            gq, gk = _tile_gain(fox_g_q, d), _tile_gain(fox_g_k, d)
            wq, wkt, wv, wvt = split_qkv(fox_w_qkv)
            q_p, kt_p, vt_p, lft_p = _qkv_proj_t(h_p, ga, wq, wkt, wvt, gq, _gain_t(fox_g_k, d, tm_p), b=bp, t=tp,
                                                 tm=tm_p, wft=fox_wft, bft=fox_bft, scale=scale, name="fox_proj_p")
            fk_p = _cumsum_last(lft_p, tk=tq, name="fox_cumsum_p")
            o_p = _pair_attn(q_p, kt_p, vt_p, (fk_p.reshape(bp, FOX_HEADS // 2, 2, tp),), mode="fox",
                             b=bp, t=tp, tq=tq, name="fox_attn_p")
            q_s, k_s, v_s, lf_s = _qkv_proj(h_s, ga, tobf(fox_w_qkv), gq, gk, tm=ms, wf=fox_wf, bf=fox_bf,
                                            scale=scale, name="fox_proj_s")
            lf_all = jnp.concatenate([jnp.swapaxes(cache_fox_logf.astype(F32), 1, 2),
                                      jnp.swapaxes(lf_s.reshape(bs, ts, FOX_HEADS), 1, 2)], axis=2)
            lf_all = jnp.pad(lf_all, ((0, 0), (0, 0), (0, tkd - ts)))
            f_all = _cumsum_last(lf_all, tk=tq, name="fox_cumsum_s")
            o_s = _dec_attn(q_s, _to_features(cache_fox_k.astype(F32)), _to_features(cache_fox_v.astype(F32)),
                            k_s, v_s, (f_all,), mode="fox", b=bs, tq=ts, past=past, tk=tkd, name="fox_attn_s")
            sh = (FOX_HEADS, FOX_DH)
            outs["fox"] = (_to_tokens(kt_p, (FOX_HEADS,)), _to_tokens(vt_p, (FOX_HEADS,)),
                           jnp.swapaxes(lft_p, 1, 2),
                           k_s.reshape(bs, ts, *sh), v_s.reshape(bs, ts, *sh), lf_s.reshape(bs, ts, FOX_HEADS))
        elif kind == 2:
            scale = DIFF_DH ** -0.5 * LOG2E
            lam_init = 0.8 - 0.6 * math.exp(-0.3 * i)
            gq, gk = _tile_gain(diff_g_q, d), _tile_gain(diff_g_k, d)
            wq, wkt, wv, wvt = split_qkv(diff_w_qkv)
            q_p, kt_p, v4_p = _qkv_proj_t(h_p, ga, wq, wkt, wv, gq, _gain_t(diff_g_k, d, tm_p), b=bp, t=tp, tm=tm_p,
                                          rope_tabs=tabs_p, v_tokens=True, scale=scale, name="diff_proj_p")
            v_p = v4_p.reshape(bp, tp, DIFF_HEADS, 2 * DIFF_DH)
            o_p = _pair_attn(q_p, kt_p, v4_p, (lamv, gsub), mode="diff", b=bp, t=tp, tq=tq, lam_init=lam_init,
                             name="diff_attn_p")
            q_s, k_s, v_s = _qkv_proj(h_s, ga, tobf(diff_w_qkv), gq, gk, tm=ms, rope_tabs=tabs_s, scale=scale,
                                      name="diff_proj_s")
            o_s = _dec_attn(q_s, _to_features(cache_diff_k.astype(F32)),
                            cache_diff_v.astype(F32).reshape(bs * past * DIFF_HEADS, 2 * DIFF_DH), k_s, v_s,
                            (lamv, gsub), mode="diff", b=bs, tq=ts, past=past, tk=tkd, lam_init=lam_init,
                            name="diff_attn_s")
            outs["diff"] = (_to_tokens(kt_p, (DIFF_HEADS, 2)), v_p,
                            k_s.reshape(bs, ts, DIFF_HEADS, 2, DIFF_DH), v_s.reshape(bs, ts, DIFF_HEADS, 2 * DIFF_DH))
        else:
            scale = BAND_DH ** -0.5 * LOG2E
            gq, gk = _tile_gain(band_g_q, d), _tile_gain(band_g_k, d)
            wq, wkt, wv, wvt = split_qkv(band_w_qkv)
            q_p, kt_p, vt_p = _qkv_proj_t(h_p, ga, wq, wkt, wvt, gq, _gain_t(band_g_k, d, tm_p), b=bp, t=tp, tm=tm_p,
                                          scale=scale, name="band_proj_p")
            tab_pairs = jnp.pad(tab_pad.reshape(BAND_HEADS // 2, 2, REL_PAD), ((0, 0), (0, SUBLANES - 2), (0, 0)))
            o_p = _band_attn(q_p, kt_p, vt_p, tab_pairs, b=bp, t=tp, tq=tq_band, name="band_attn_p")
            q_s, k_s, v_s = _qkv_proj(h_s, ga, tobf(band_w_qkv), gq, gk, tm=ms, scale=scale, name="band_proj_s")
            w = cache_band_k.shape[1]
            kct = _to_features(cache_band_k.astype(F32))
            vct = _to_features(cache_band_v.astype(F32))
            o_s = _band_dec(q_s, kct, vct, k_s, v_s, tab_pad, b=bs, tq=ts, w=w, past=past, name="band_attn_s")
            keep = min(BAND_LEFT, tp)
            roll_in = lambda ct, new: jnp.concatenate(
                [ct, jnp.swapaxes(new.reshape(bs, ts, d), 1, 2)], axis=2)[:, :, ts:]
            outs["band"] = (_to_tokens(kt_p[:, :, tp - keep:], (BAND_HEADS,)),
                            _to_tokens(vt_p[:, :, tp - keep:], (BAND_HEADS,)),
                            _to_tokens(roll_in(kct, k_s), (BAND_HEADS,)), _to_tokens(roll_in(vct, v_s), (BAND_HEADS,)))

        last = i == depth - 1
        gfin = rowv(final_norm_g) if last else None
        gfn = rowv(ffn_norm_g[i])
        h_p, ug, uh = _ffn_rows(h_p, o_p, wo[kind], gfn, wup[i], cwf[i], cbf[i].reshape(1, -1), wdn[i], tm=tm_f,
                                tf=tf, seq=tp, final_g=gfin, name="ffn_p")
        tps = tp // tm_f
        conv_p.append(jnp.concatenate([ug, uh], axis=-1)[tps - 1::tps, SUBLANES - (CONV_W - 1):])
        st = state_ffn_conv[i].astype(F32)
        zeros = jnp.zeros((bs, ts, 2 * D_FF), F32)
        s1 = zeros.at[:, 0].set(st[:, 1]).reshape(ms, 2 * D_FF)
        s2 = zeros.at[:, 0].set(st[:, 0]).at[:, 1].set(st[:, 1]).reshape(ms, 2 * D_FF)
        h_s, ug, uh = _ffn_rows(h_s, o_s, wo[kind], gfn, wup[i], cwf[i], cbf[i].reshape(1, -1), wdn[i], tm=ms,
                                tf=tf, seq=ts, state=(s1, s2), final_g=gfin, name="ffn_s")
        u_s = jnp.concatenate([ug, uh], axis=-1).reshape(bs, ts, 2 * D_FF)
        conv_s.append(jnp.concatenate([st, u_s], axis=1)[:, ts:])

    y_prompt = h_p.reshape(bp, tp, d)
    y_sample = h_s.reshape(bs, ts, d)
    return (y_prompt, y_sample) + outs["mla"] + outs["fox"] + outs["diff"] + outs["band"] + (
        jnp.stack(conv_p, axis=0), jnp.stack(conv_s, axis=0))
```

```python
import functools
import math

import jax
import jax.numpy as jnp
from jax import lax
from jax.experimental import pallas as pl
from jax.experimental.pallas import tpu as pltpu

F32 = jnp.float32
BF16 = jnp.bfloat16

D_MODEL = 1024
CHUNK = 64
ROPE_THETA = 10000.0
NORM_EPS = 1e-6
MLA_HEADS, MLA_Q_LORA, MLA_KV_LORA, MLA_NOPE, MLA_ROPE, MLA_V = 8, 384, 256, 128, 64, 128
FOX_HEADS, FOX_DH = 16, 64
DIFF_HEADS, DIFF_DH = 8, 64
BAND_HEADS, BAND_DH = 16, 64
BAND_LEFT_CHUNKS = 8
BAND_LEFT = BAND_LEFT_CHUNKS * CHUNK
REL_CLIP = 128
D_FF = 2816
CONV_W = 3

LANES = 128
SUBLANES = 8
NEG = -1e30
REL_PAD = 384
LOG2E = math.log2(math.e)
VMEM_LIMIT = 56 * 1024 * 1024


def _cp(n_axes):
    return pltpu.CompilerParams(dimension_semantics=("arbitrary",) * n_axes, vmem_limit_bytes=VMEM_LIMIT)


def _dot(a, b):
    return jnp.dot(a, b, preferred_element_type=F32)


def _dot_nt(a, b):
    return lax.dot_general(a, b, (((1,), (1,)), ((), ())), preferred_element_type=F32)


def _rms(x, g):
    ms = jnp.mean(x * x, axis=-1, keepdims=True)
    return x * lax.rsqrt(ms + NORM_EPS) * g


def _lane_iota(shape):
    return lax.broadcasted_iota(jnp.int32, shape, len(shape) - 1)


def _row_iota(shape):
    return lax.broadcasted_iota(jnp.int32, shape, len(shape) - 2)


def _log2(n):
    assert n & (n - 1) == 0, n
    return n.bit_length() - 1


def _head_rms(x, g, seg):
    n = x.shape[1]
    outs = []
    for c in range(n // LANES):
        xs = x[:, LANES * c:LANES * (c + 1)]
        sq = xs * xs
        if seg == LANES:
            r = lax.rsqrt(jnp.sum(sq, axis=-1, keepdims=True) * (1.0 / seg) + NORM_EPS)
        else:
            lo = _lane_iota(xs.shape) < seg
            s_lo = jnp.sum(jnp.where(lo, sq, 0.0), axis=-1, keepdims=True)
            s_hi = jnp.sum(jnp.where(lo, 0.0, sq), axis=-1, keepdims=True)
            r = jnp.where(lo, lax.rsqrt(s_lo * (1.0 / seg) + NORM_EPS), lax.rsqrt(s_hi * (1.0 / seg) + NORM_EPS))
        outs.append(xs * r * g[:, LANES * c:LANES * (c + 1)])
    return jnp.concatenate(outs, axis=-1) if len(outs) > 1 else outs[0]


def _head_rms_t(x, g, seg=64):
    outs = []
    for hd in range(x.shape[0] // seg):
        xs = x[seg * hd:seg * (hd + 1)]
        r = lax.rsqrt(jnp.sum(xs * xs, axis=0, keepdims=True) * (1.0 / seg) + NORM_EPS)
        outs.append(xs * r * g[seg * hd:seg * (hd + 1)])
    return jnp.concatenate(outs, axis=0) if len(outs) > 1 else outs[0]


def _rope(x, cos, sinp):
    lo32 = (_lane_iota((x.shape[0], LANES)) & 63) < 32
    outs = []
    for c in range(x.shape[1] // LANES):
        xs = x[:, LANES * c:LANES * (c + 1)]
        fwd = pltpu.roll(xs, 32, axis=1)
        bwd = pltpu.roll(xs, 96, axis=1)
        outs.append(xs * cos + jnp.where(lo32, bwd, fwd) * sinp)
    return jnp.concatenate(outs, axis=-1) if len(outs) > 1 else outs[0]


def _rope_t(x, cos_t, sin_t):
    outs = []
    for hd in range(x.shape[0] // 64):
        x1 = x[64 * hd:64 * hd + 32]
        x2 = x[64 * hd + 32:64 * (hd + 1)]
        outs += [x1 * cos_t - x2 * sin_t, x2 * cos_t + x1 * sin_t]
    return jnp.concatenate(outs, axis=0)


def _split3(x):
    hi = x.astype(BF16)
    r = x - hi.astype(F32)
    mid = r.astype(BF16)
    lo = (r - mid.astype(F32)).astype(BF16)
    return hi, mid, lo


def _softmax_step(s, pv, m, l, acc):
    m_new = jnp.maximum(m, jnp.max(s, axis=-1, keepdims=True))
    alpha = jnp.exp2(m - m_new)
    p = jnp.exp2(s - m_new)
    if l is not None:
        l = alpha * l + jnp.sum(p, axis=-1, keepdims=True)
    return m_new, l, alpha * acc + pv(p.astype(BF16))


def _softmax_update(s, pv, m_ref, l_ref, acc_ref):
    m, l, acc = _softmax_step(s, pv, m_ref[...], l_ref[...], acc_ref[...])
    m_ref[...] = m
    l_ref[...] = l
    acc_ref[...] = acc


def _log_sigmoid(z):
    return jnp.minimum(z, 0.0) - jnp.log1p(jnp.exp(-jnp.abs(z)))


def _qkv_proj_kernel(*refs, rope, logf, scale):
    it = iter(refs)
    h_ref, g_ref, w_ref, gq_ref, gk_ref = next(it), next(it), next(it), next(it), next(it)
    cos_ref = sin_ref = wf_ref = bf_ref = None
    if rope:
        cos_ref, sin_ref = next(it), next(it)
    if logf:
        wf_ref, bf_ref = next(it), next(it)
    q_out, k_out, v_out = next(it), next(it), next(it)
    a = _rms(h_ref[...], g_ref[...]).astype(BF16)
    qkv = _dot(a, w_ref[...])
    q = _head_rms(qkv[:, :D_MODEL], gq_ref[...], 64)
    k = _head_rms(qkv[:, D_MODEL:2 * D_MODEL], gk_ref[...], 64)
    if rope:
        q = _rope(q, cos_ref[...], sin_ref[...])
        k = _rope(k, cos_ref[...], sin_ref[...])
    q_out[...] = (q * scale).astype(BF16)
    k_out[...] = k
    v_out[...] = qkv[:, 2 * D_MODEL:]
    if logf:
        logf_out = next(it)
        lf = _log_sigmoid(_dot(a, wf_ref[...]) + bf_ref[...])
        logf_out[...] = lf[:, :FOX_HEADS]


def _qkv_proj(h, g, w, gq, gk, *, tm, rope_tabs=None, wf=None, bf=None, scale, name):
    m = h.shape[0]
    grid = (m // tm,)
    row = lambda n: pl.BlockSpec((tm, n), lambda i: (i, 0))
    full = lambda a: pl.BlockSpec(a.shape, lambda i: (0,) * a.ndim)
    ins = [h, g, w, gq, gk]
    specs = [row(D_MODEL), full(g), full(w), full(gq), full(gk)]
    if rope_tabs is not None:
        cos, sinp = rope_tabs
        nt = cos.shape[0] // tm
        tab = pl.BlockSpec((tm, LANES), lambda i: (i % nt, 0))
        ins += [cos, sinp]
        specs += [tab, tab]
    if wf is not None:
        ins += [wf, bf]
        specs += [full(wf), full(bf)]
    outs = [jax.ShapeDtypeStruct((m, D_MODEL), BF16), jax.ShapeDtypeStruct((m, D_MODEL), F32),
            jax.ShapeDtypeStruct((m, D_MODEL), F32)]
    ospecs = [row(D_MODEL), row(D_MODEL), row(D_MODEL)]
    if wf is not None:
        outs.append(jax.ShapeDtypeStruct((m, FOX_HEADS), F32))
        ospecs.append(row(FOX_HEADS))
    kern = functools.partial(_qkv_proj_kernel, rope=rope_tabs is not None, logf=wf is not None, scale=scale)
    return pl.pallas_call(kern, out_shape=outs, grid=grid, in_specs=specs, out_specs=ospecs,
                          compiler_params=_cp(1), name=name)(*ins)


def _qkv_proj_t_kernel(*refs, rope, logf, v_tokens, scale):
    it = iter(refs)
    h_ref, g_ref, wq_ref, wkt_ref, wv_ref, gq_ref, gkt_ref = [next(it) for _ in range(7)]
    if rope:
        cos_ref, sin_ref, cost_ref, sint_ref = [next(it) for _ in range(4)]
    if logf:
        wft_ref, bft_ref = next(it), next(it)
    q_out, kt_out, v_out = next(it), next(it), next(it)
    a = _rms(h_ref[...], g_ref[...]).astype(BF16)
    tm = a.shape[0]
    q = _head_rms(_dot(a, wq_ref[...]), gq_ref[...], 64)
    kt = _head_rms_t(_dot_nt(wkt_ref[...], a), gkt_ref[...])
    if rope:
        q = _rope(q, cos_ref[...], sin_ref[...])
        kt = _rope_t(kt, cost_ref[...], sint_ref[...])
    q_out[...] = (q * scale).astype(BF16)
    kt_out[...] = kt
    if v_tokens:
        v = _dot(a, wv_ref[...])
        nh = D_MODEL // LANES
        for hd in range(nh):
            v_out[pl.ds(hd, tm, stride=nh), :] = v[:, LANES * hd:LANES * (hd + 1)]
    else:
        v_out[...] = _dot_nt(wv_ref[...], a)
    if logf:
        lft_out = next(it)
        lft_out[...] = _log_sigmoid(_dot_nt(wft_ref[...], a) + bft_ref[...])


def _qkv_proj_t(h, g, wq, wkt, wv, gq, gkt, *, b, t, tm, rope_tabs=None, wft=None, bft=None, v_tokens=False,
                scale, name):
    m = b * t
    nt = t // tm
    nh = D_MODEL // LANES
    row = lambda n: pl.BlockSpec((tm, n), lambda i: (i, 0))
    full = lambda a: pl.BlockSpec(a.shape, lambda i: (0,) * a.ndim)
    feat = lambda n: pl.BlockSpec((None, n, tm), lambda i: (i // nt, 0, i % nt))
    ins = [h, g, wq, wkt, wv, gq, gkt]
    specs = [row(D_MODEL)] + [full(x) for x in ins[1:]]
    if rope_tabs is not None:
        cos, sinp, cos_t, sin_t = rope_tabs
        ins += [cos, sinp, cos_t, sin_t]
        tab = pl.BlockSpec((tm, LANES), lambda i: (i % nt, 0))
        tab_t = pl.BlockSpec((32, tm), lambda i: (0, i % nt))
        specs += [tab, tab, tab_t, tab_t]
    if wft is not None:
        ins += [wft, bft]
        specs += [full(wft), full(bft)]
    outs = [jax.ShapeDtypeStruct((m, D_MODEL), BF16), jax.ShapeDtypeStruct((b, D_MODEL, t), F32)]
    ospecs = [row(D_MODEL), feat(D_MODEL)]
    if v_tokens:
        outs.append(jax.ShapeDtypeStruct((m * nh, LANES), F32))
        ospecs.append(pl.BlockSpec((tm * nh, LANES), lambda i: (i, 0)))
    else:
        outs.append(jax.ShapeDtypeStruct((b, D_MODEL, t), F32))
        ospecs.append(feat(D_MODEL))
    if wft is not None:
        outs.append(jax.ShapeDtypeStruct((b, FOX_HEADS, t), F32))
        ospecs.append(feat(FOX_HEADS))
    kern = functools.partial(_qkv_proj_t_kernel, rope=rope_tabs is not None, logf=wft is not None,
                             v_tokens=v_tokens, scale=scale)
    return pl.pallas_call(kern, out_shape=outs, grid=(m // tm,), in_specs=specs, out_specs=ospecs,
                          compiler_params=_cp(1), name=name)(*ins)


def _mla_proj_kernel(*refs, with_kv, scale):
    it = iter(refs)
    (h_ref, g_ref, wdq_ref, gq_ref, wuq_ref, wc_ref, wr_ref, gkv_ref, gqn_ref, gqr_ref, gkr_ref,
     cos_ref, sin_ref) = [next(it) for _ in range(13)]
    if with_kv:
        wuk_ref, wuv_ref, gkn_ref, wrt_ref, gkrt_ref, cost_ref, sint_ref = [next(it) for _ in range(7)]
    q_out, ckv_out, kr_out, krd_out = next(it), next(it), next(it), next(it)
    a = _rms(h_ref[...], g_ref[...]).astype(BF16)
    cq = _rms(_dot(a, wdq_ref[...]), gq_ref[...]).astype(BF16)
    q = _dot(cq, wuq_ref[...])
    n_nope = MLA_HEADS * MLA_NOPE
    qn = _head_rms(q[:, :n_nope], gqn_ref[...], MLA_NOPE) * scale
    qr = _rope(_head_rms(q[:, n_nope:], gqr_ref[...], MLA_ROPE), cos_ref[...], sin_ref[...]) * scale
    lo = _lane_iota((q.shape[0], LANES)) < MLA_ROPE
    pieces = []
    for hd in range(MLA_HEADS):
        slab = qr[:, LANES * (hd // 2):LANES * (hd // 2 + 1)]
        keep = lo if hd % 2 == 0 else jnp.logical_not(lo)
        pieces += [qn[:, LANES * hd:LANES * (hd + 1)], jnp.where(keep, slab, 0.0)]
    q_out[...] = jnp.concatenate(pieces, axis=-1).astype(BF16)
    ckv = _rms(_dot(a, wc_ref[...]), gkv_ref[...])
    ckv_out[...] = ckv
    kr2 = _rope(_head_rms(_dot(a, wr_ref[...]), gkr_ref[...], MLA_ROPE), cos_ref[...], sin_ref[...])
    kr_out[...] = kr2[:, :MLA_ROPE]
    krd_out[...] = kr2
    if with_kv:
        kc_out, v_out, krt_out = next(it), next(it), next(it)
        c = ckv.astype(BF16)
        kn = _head_rms(_dot(c, wuk_ref[...]), gkn_ref[...], MLA_NOPE)
        pieces = []
        for hd in range(MLA_HEADS):
            pieces += [kn[:, LANES * hd:LANES * (hd + 1)], kr2]
        kc_out[...] = jnp.concatenate(pieces, axis=-1).astype(BF16)
        v_out[...] = _dot(c, wuv_ref[...]).astype(BF16)
        krt_out[...] = _rope_t(_head_rms_t(_dot_nt(wrt_ref[...], a), gkrt_ref[...]), cost_ref[...], sint_ref[...])


def _mla_proj(h, g, w, rope_tabs, *, b, t, tm, with_kv, scale, name):
    m = h.shape[0]
    cos, sinp = rope_tabs[:2]
    nt = cos.shape[0] // tm
    row = lambda n: pl.BlockSpec((tm, n), lambda i: (i, 0))
    full = lambda a: pl.BlockSpec(a.shape, lambda i: (0,) * a.ndim)
    tab = pl.BlockSpec((tm, LANES), lambda i: (i % nt, 0))
    ins = [h, g, w["wdq"], w["gq"], w["wuq"], w["wc"], w["wr"], w["gkv"], w["gqn"], w["gqr"], w["gkr"], cos, sinp]
    specs = [row(D_MODEL)] + [full(x) for x in ins[1:11]] + [tab, tab]
    qw = MLA_HEADS * 2 * LANES
    outs = [jax.ShapeDtypeStruct((m, qw), BF16), jax.ShapeDtypeStruct((m, MLA_KV_LORA), F32),
            jax.ShapeDtypeStruct((m, MLA_ROPE), F32), jax.ShapeDtypeStruct((m, LANES), F32)]
    ospecs = [row(qw), row(MLA_KV_LORA), row(MLA_ROPE), row(LANES)]
    if with_kv:
        cos_t, sin_t = rope_tabs[2:]
        tab_t = pl.BlockSpec((32, tm), lambda i: (0, i % nt))
        extra = [w["wuk"], w["wuv"], w["gkn"], w["wrt"], w["gkrt"]]
        ins += extra + [cos_t, sin_t]
        specs += [full(x) for x in extra] + [tab_t, tab_t]
        outs += [jax.ShapeDtypeStruct((m, qw), BF16), jax.ShapeDtypeStruct((m, MLA_HEADS * MLA_V), BF16),
                 jax.ShapeDtypeStruct((b, MLA_ROPE, t), F32)]
        ospecs += [row(qw), row(MLA_HEADS * MLA_V),
                   pl.BlockSpec((None, MLA_ROPE, tm), lambda i: (i // nt, 0, i % nt))]
    kern = functools.partial(_mla_proj_kernel, with_kv=with_kv, scale=scale)
    return pl.pallas_call(kern, out_shape=outs, grid=(m // tm,), in_specs=specs, out_specs=ospecs,
                          compiler_params=_cp(1), name=name)(*ins)


def _cumsum_kernel(x_ref, o_ref, *, tk):
    nh, t = x_ref.shape
    tri = (_row_iota((tk, tk)) <= _lane_iota((tk, tk))).astype(BF16)
    carry = jnp.zeros((nh, 1), F32)
    for c in range(t // tk):
        hi, mid, lo = _split3(x_ref[:, tk * c:tk * (c + 1)])
        y = _dot(jnp.concatenate([hi, mid, lo], axis=0), tri)
        f = (y[:nh] + y[nh:2 * nh]) + y[2 * nh:] + carry
        o_ref[:, tk * c:tk * (c + 1)] = f
        carry = f[:, tk - 1:tk]


def _cumsum_last(x, *, tk, name):
    b, nh, t = x.shape
    spec = pl.BlockSpec((None, nh, t), lambda i: (i, 0, 0))
    return pl.pallas_call(functools.partial(_cumsum_kernel, tk=tk), out_shape=jax.ShapeDtypeStruct(x.shape, F32),
                          grid=(b,), in_specs=[spec], out_specs=spec, compiler_params=_cp(1), name=name)(x)


def _mla_attn_kernel(q_ref, k_ref, v_ref, o_ref, *, tq, hps):
    kw = 2 * LANES
    t = k_ref.shape[0]
    for qi in range(t // tq):
        rows = slice(tq * qi, tq * (qi + 1))
        limit = (((qi * tq + _row_iota((tq, 1))) >> 6) + 1) << 6
        outs = []
        n0 = qi * tq
        for h in range(hps):
            q = q_ref[rows, kw * h:kw * (h + 1)]
            kcols = slice(kw * h, kw * (h + 1))
            vcols = slice(MLA_V * h, MLA_V * (h + 1))
            s_d = jnp.where(n0 + _lane_iota((1, tq)) < limit, _dot_nt(q, k_ref[n0:n0 + tq, kcols]), NEG)
            m = jnp.max(s_d, axis=-1, keepdims=True)
            if qi:
                s_f = _dot_nt(q, k_ref[0:n0, kcols])
                m = jnp.maximum(m, jnp.max(s_f, axis=-1, keepdims=True))
            p_d = jnp.exp2(s_d - m)
            acc = _dot(p_d.astype(BF16), v_ref[n0:n0 + tq, vcols])
            l = jnp.sum(p_d, axis=-1, keepdims=True)
            if qi:
                p_f = jnp.exp2(s_f - m)
                acc = acc + _dot(p_f.astype(BF16), v_ref[0:n0, vcols])
                l = l + jnp.sum(p_f, axis=-1, keepdims=True)
            outs.append(acc / l)
        o_ref[rows, :] = jnp.concatenate(outs, axis=-1).astype(BF16)


def _mla_attn(q, kc, v, *, b, t, tq, hps, name):
    kw = 2 * LANES * hps
    vw = MLA_V * hps
    return pl.pallas_call(
        functools.partial(_mla_attn_kernel, tq=tq, hps=hps),
        out_shape=jax.ShapeDtypeStruct((b * t, MLA_HEADS * MLA_V), BF16),
        grid=(b, MLA_HEADS // hps),
        in_specs=[pl.BlockSpec((t, kw), lambda i, h: (i, h)),
                  pl.BlockSpec((t, kw), lambda i, h: (i, h)),
                  pl.BlockSpec((t, vw), lambda i, h: (i, h))],
        out_specs=pl.BlockSpec((t, vw), lambda i, h: (i, h)),
        compiler_params=_cp(2), name=name)(q, kc, v)


def _ones_row_values(vt, c):
    row = _row_iota(vt.shape)
    if c == 0:
        return jnp.where(row < 64, vt, jnp.where(row == 64, 1.0, 0.0)).astype(BF16)
    return jnp.where(row >= 64, vt, jnp.where(row == 0, 1.0, 0.0)).astype(BF16)


def _merge_pair(acc0, acc1):
    lo = _lane_iota(acc0.shape) < 64
    return jnp.where(lo, acc0 / acc0[:, 64:65], acc1 / acc1[:, 0:1])


def _pair_attn_kernel(*refs, mode, tq, lam_init):
    if mode == "fox":
        q_ref, kt_ref, vt_ref, fk_ref, o_ref, ka, va = refs
    else:
        q_ref, kt_ref, v_ref, lam_ref, gsub_ref, o_ref, ka, vb = refs
    t = kt_ref.shape[1]
    if mode == "fox":
        ka[0:LANES, :] = kt_ref[...].astype(BF16)
        f = fk_ref[...] * (-LOG2E)
        terms = [x.astype(F32) for x in _split3(f)]
        nr = 2 * SUBLANES
        row = _row_iota((nr, t))
        aug = jnp.zeros((nr, t), F32)
        for c in range(2):
            for j in range(3):
                aug = jnp.where(row == 3 * c + j, terms[j][c:c + 1, :], aug)
        ka[LANES:LANES + nr, :] = aug.astype(BF16)
        ka[LANES + nr:, :] = jnp.zeros((LANES - nr, t), BF16)
        vt = vt_ref[...]
        va[0] = _ones_row_values(vt, 0)
        va[1] = _ones_row_values(vt, 1)
    else:
        ka[...] = kt_ref[...].astype(BF16)
        nhv = v_ref.shape[0] // t
        vb[...] = v_ref[pl.ds(pl.program_id(1), t, stride=nhv), :].astype(BF16)

    def q_tile(qi, q):
        lane = _lane_iota((tq, LANES))
        zero = jnp.zeros_like(q)
        qa = [jnp.where(lane < 64, q, zero), jnp.where(lane < 64, zero, q)]
        qpos = qi * tq + _row_iota((tq, 1))
        if mode == "fox":
            limit = qpos + 1
            pick = [jnp.where(lane < 3 * c, 0.0, jnp.where(lane < 3 * c + 3, 1.0, 0.0)).astype(BF16)
                    for c in range(2)]
            qa = [jnp.concatenate([qa[c], pick[c]], axis=-1) for c in range(2)]
        else:
            limit = ((qpos >> 6) + 1) << 6

        n0 = qi * tq
        accs, sums = [], []
        for c in range(2):
            s_d = jnp.where(n0 + _lane_iota((1, tq)) < limit, _dot(qa[c], ka[:, n0:n0 + tq]), NEG)
            m = jnp.max(s_d, axis=-1, keepdims=True)
            if qi:
                s_f = _dot(qa[c], ka[:, 0:n0])
                m = jnp.maximum(m, jnp.max(s_f, axis=-1, keepdims=True))
            parts = [(jnp.exp2(s_d - m), n0, n0 + tq)] + ([(jnp.exp2(s_f - m), 0, n0)] if qi else [])
            acc, l = 0.0, 0.0
            for p, lo_k, hi_k in parts:
                pb = p.astype(BF16)
                if mode == "fox":
                    acc = acc + _dot_nt(pb, va[c, :, lo_k:hi_k])
                else:
                    acc = acc + _dot(pb, vb[lo_k:hi_k, :])
                    l = l + jnp.sum(p, axis=-1, keepdims=True)
            accs.append(acc)
            sums.append(l)
        if mode == "fox":
            o = _merge_pair(*accs)
        else:
            lv = lam_ref[...]
            lam = (jnp.exp(jnp.sum(lv[0:1] * lv[1:2], axis=-1, keepdims=True))
                   - jnp.exp(jnp.sum(lv[2:3] * lv[3:4], axis=-1, keepdims=True)) + lam_init)
            o = _rms(accs[0] / sums[0] - lam * (accs[1] / sums[1]), gsub_ref[...]) * (1.0 - lam_init)
        return o.astype(BF16)

    for qi in range(t // tq):
        o_ref[tq * qi:tq * (qi + 1), :] = q_tile(qi, q_ref[tq * qi:tq * (qi + 1), :])


def _pair_attn(q, kt, v, extra, *, mode, b, t, tq, lam_init=0.0, name):
    npair = D_MODEL // LANES
    qspec = pl.BlockSpec((t, LANES), lambda i, p: (i, p))
    ktspec = pl.BlockSpec((None, LANES, t), lambda i, p: (i, p, 0))
    if mode == "fox":
        vspec = ktspec
        especs = [pl.BlockSpec((None, None, 2, t), lambda i, p: (i, p, 0, 0))]
        scratch = [pltpu.VMEM((2 * LANES, t), BF16), pltpu.VMEM((2, LANES, t), BF16)]
    else:
        vspec = pl.BlockSpec((t * npair, LANES), lambda i, p: (i, 0))
        especs = [pl.BlockSpec(x.shape, lambda i, p: (0, 0)) for x in extra]
        scratch = [pltpu.VMEM((LANES, t), BF16), pltpu.VMEM((t, LANES), BF16)]
    return pl.pallas_call(
        functools.partial(_pair_attn_kernel, mode=mode, tq=tq, lam_init=lam_init),
        out_shape=jax.ShapeDtypeStruct((b * t, D_MODEL), BF16),
        grid=(b, npair),
        in_specs=[qspec, ktspec, vspec] + especs,
        out_specs=qspec,
        scratch_shapes=scratch,
        compiler_params=_cp(2), name=name)(q, kt, v, *extra)


def _rel_gather(tab, width, center):
    idx = jnp.clip(center - _lane_iota((REL_PAD, width)), -REL_CLIP, REL_CLIP) + REL_CLIP
    onehot = (_row_iota((REL_PAD, width)) == idx).astype(BF16)
    hi, mid, lo = _split3(tab)
    return (_dot(hi, onehot) + _dot(mid, onehot)) + _dot(lo, onehot)


def _band_attn_kernel(q_ref, kt_ref, vt_ref, tab_ref, o_ref, kb, va, bias_ref, *, tq, win, bw):
    gw = bw + tq

    @pl.when(pl.program_id(1) == 0)
    def _():
        g = _rel_gather(tab_ref[...], gw, BAND_LEFT + tq) * LOG2E
        ii = _row_iota((tq, bw)) >> 6
        jj = _lane_iota((tq, bw)) >> 6
        allowed = (jj >= ii) & (jj <= ii + BAND_LEFT_CHUNKS)
        for c in range(2):
            rows = jnp.broadcast_to(g[c:c + 1, :], (tq, gw))
            skew = pltpu.roll(rows, gw - tq, axis=1, stride=1, stride_axis=0)
            bias_ref[c] = jnp.where(allowed, skew[:, :bw], NEG)

    kb[...] = kt_ref[...].astype(BF16)
    vt = vt_ref[...]
    va[0] = _ones_row_values(vt, 0)
    va[1] = _ones_row_values(vt, 1)

    lo = _lane_iota((tq, LANES)) < 64
    for r in range(q_ref.shape[0] // tq):
        q0 = r * tq
        ws = max(q0 - BAND_LEFT, 0)
        d = BAND_LEFT - q0 + ws
        k = kb[:, pl.ds(ws, win)]
        q = q_ref[tq * r:tq * (r + 1), :]
        zero = jnp.zeros_like(q)
        accs = []
        for c in range(2):
            qc = jnp.where(lo, q, zero) if c == 0 else jnp.where(lo, zero, q)
            s = _dot(qc, k) + bias_ref[c, :, pl.ds(d, win)]
            p = jnp.exp2(s - jnp.max(s, axis=-1, keepdims=True))
            accs.append(_dot_nt(p.astype(BF16), va[c, :, pl.ds(ws, win)]))
        o_ref[tq * r:tq * (r + 1), :] = _merge_pair(*accs).astype(BF16)


def _band_attn(q, kt, vt, tab, *, b, t, tq, name):
    npair = D_MODEL // LANES
    win = BAND_LEFT + tq
    bw = win + BAND_LEFT
    assert t >= win and tq % CHUNK == 0 and BAND_LEFT % tq == 0
    qspec = pl.BlockSpec((t, LANES), lambda p, i: (i, p))
    kvspec = pl.BlockSpec((None, LANES, t), lambda p, i: (i, p, 0))
    return pl.pallas_call(
        functools.partial(_band_attn_kernel, tq=tq, win=win, bw=bw),
        out_shape=jax.ShapeDtypeStruct((b * t, D_MODEL), BF16),
        grid=(npair, b),
        in_specs=[qspec, kvspec, kvspec, pl.BlockSpec((None, SUBLANES, REL_PAD), lambda p, i: (p, 0, 0))],
        out_specs=qspec,
        scratch_shapes=[pltpu.VMEM((LANES, t), BF16), pltpu.VMEM((2, LANES, t), BF16), pltpu.VMEM((2, tq, bw), F32)],
        compiler_params=_cp(2), name=name)(q, kt, vt, tab)


def _block_diag_q(q, nh, width):
    tq = q.shape[0]
    rep = jnp.concatenate([q] * nh, axis=0)
    keep = (_row_iota(rep.shape) >> _log2(tq)) == (_lane_iota(rep.shape) >> _log2(width))
    return jnp.where(keep, rep, jnp.zeros_like(rep))


def _expand_rows(f, tq):
    return jnp.concatenate([jnp.broadcast_to(f[h:h + 1, :], (tq, f.shape[1])) for h in range(f.shape[0])], axis=0)


def _dec_attn_kernel(*refs, mode, n_cache, tq, past, lam_init):
    if mode == "fox":
        (q_ref, kc_ref, vc_ref, kn_ref, vn_ref, fkc_ref, fkn_ref, o_ref, qb, m_ref, l_ref, acc_ref) = refs
    else:
        (q_ref, kc_ref, vc_ref, kn_ref, vn_ref, lam_ref, gsub_ref, o_ref, qb, m_ref, l_ref, acc_ref) = refs
    t = pl.program_id(1)
    nh = D_MODEL // 64
    rows = nh * tq

    @pl.when(t == 0)
    def _():
        qb[...] = _block_diag_q(q_ref[...], nh, 64)
        m_ref[...] = jnp.full_like(m_ref, NEG)
        l_ref[...] = jnp.zeros_like(l_ref)
        acc_ref[...] = jnp.zeros_like(acc_ref)

    @pl.when(t < n_cache)
    def _():
        s = _dot(qb[...], kc_ref[...].astype(BF16))
        if mode == "fox":
            s = s - _expand_rows(fkc_ref[...] * LOG2E, tq)
            v = vc_ref[...].astype(BF16)
            _softmax_update(s, lambda pb: _dot_nt(pb, v), m_ref, l_ref, acc_ref)
        else:
            tk = kc_ref.shape[1]
            nhv = D_MODEL // LANES
            v = jnp.concatenate([vc_ref[pl.ds(h, tk, stride=nhv), :].astype(BF16) for h in range(nhv)], axis=-1)
            _softmax_update(s, lambda pb: _dot(pb, v), m_ref, l_ref, acc_ref)

    @pl.when(t == n_cache)
    def _():
        s = _dot_nt(qb[...], kn_ref[...].astype(BF16))
        qpos = past + (_row_iota((rows, tq)) & (tq - 1))
        kpos = past + _lane_iota((rows, tq))
        if mode == "fox":
            s = s - _expand_rows(fkn_ref[...][:, :tq] * LOG2E, tq)
            allowed = kpos <= qpos
        else:
            allowed = (kpos >> 6) <= (qpos >> 6)
        vn = vn_ref[...].astype(BF16)
        _softmax_update(jnp.where(allowed, s, NEG), lambda pb: _dot(pb, vn), m_ref, l_ref, acc_ref)
        o_all = acc_ref[...] / l_ref[...]
        if mode == "fox":
            o = jnp.zeros((tq, D_MODEL), F32)
            hl = _lane_iota((tq, D_MODEL)) >> 6
            for h in range(nh):
                o = jnp.where(hl == h, o_all[h * tq:(h + 1) * tq, :], o)
        else:
            lv = lam_ref[...]
            lam = (jnp.exp(jnp.sum(lv[0:1] * lv[1:2], axis=-1, keepdims=True))
                   - jnp.exp(jnp.sum(lv[2:3] * lv[3:4], axis=-1, keepdims=True)) + lam_init)
            pieces = []
            for h in range(nh // 2):
                a0 = o_all[(2 * h) * tq:(2 * h + 1) * tq, LANES * h:LANES * (h + 1)]
                a1 = o_all[(2 * h + 1) * tq:(2 * h + 2) * tq, LANES * h:LANES * (h + 1)]
                pieces.append(_rms(a0 - lam * a1, gsub_ref[...]) * (1.0 - lam_init))
            o = jnp.concatenate(pieces, axis=-1)
        o_ref[...] = o.astype(BF16)


def _dec_attn(q, kc, vc, kn, vn, extra, *, mode, b, tq, past, tk, lam_init=0.0, name):
    n_cache = past // tk
    nh = D_MODEL // 64
    last = n_cache - 1
    new = pl.BlockSpec((tq, D_MODEL), lambda i, t: (i, 0))
    cache_t = pl.BlockSpec((None, D_MODEL, tk), lambda i, t: (i, 0, jnp.minimum(t, last)))
    if mode == "fox":
        (fk,) = extra
        vspec = cache_t
        especs = [pl.BlockSpec((None, nh, tk), lambda i, t: (i, 0, jnp.minimum(t, last))),
                  pl.BlockSpec((None, nh, LANES), lambda i, t: (i, 0, past // LANES))]
        ins = [fk, fk]
    else:
        nhv = D_MODEL // LANES
        vspec = pl.BlockSpec((tk * nhv, LANES), lambda i, t: (i * n_cache + jnp.minimum(t, last), 0))
        especs = [pl.BlockSpec(x.shape, lambda i, t: (0, 0)) for x in extra]
        ins = list(extra)
    return pl.pallas_call(
        functools.partial(_dec_attn_kernel, mode=mode, n_cache=n_cache, tq=tq, past=past, lam_init=lam_init),
        out_shape=jax.ShapeDtypeStruct((b * tq, D_MODEL), BF16),
        grid=(b, n_cache + 1),
        in_specs=[new, cache_t, vspec, new, new] + especs,
        out_specs=new,
        scratch_shapes=[pltpu.VMEM((nh * tq, D_MODEL), BF16), pltpu.VMEM((nh * tq, 1), F32),
                        pltpu.VMEM((nh * tq, 1), F32), pltpu.VMEM((nh * tq, D_MODEL), F32)],
        compiler_params=_cp(2), name=name)(q, kc, vc, kn, vn, *ins)


def _mla_dec_kernel(q_ref, cc_ref, rc_ref, cn_ref, rn_ref, wuk_ref, wuv_ref, gkn_ref, o_ref,
                    qn, qr, m_ref, l_ref, acc_ref, *, n_cache, tq, past):
    t = pl.program_id(1)
    nh = MLA_HEADS
    rows = nh * tq

    @pl.when(t == 0)
    def _():
        q = q_ref[...]
        zero = jnp.zeros((tq, LANES), BF16)
        for h in range(nh):
            qn[h * tq:(h + 1) * tq, :] = jnp.concatenate(
                [q[:, 2 * LANES * h:2 * LANES * h + LANES] if c == h else zero for c in range(nh)], axis=-1)
            qr[h * tq:(h + 1) * tq, :] = q[:, 2 * LANES * h + LANES:2 * LANES * (h + 1)]
        m_ref[...] = jnp.full_like(m_ref, NEG)
        l_ref[...] = jnp.zeros_like(l_ref)
        acc_ref[...] = jnp.zeros_like(acc_ref)

    def step(ckv, s_rope, mask_new):
        c = ckv.astype(BF16)
        kn = _head_rms(_dot(c, wuk_ref[...]), gkn_ref[...], MLA_NOPE).astype(BF16)
        s = _dot_nt(qn[...], kn) + s_rope
        if mask_new:
            n = ckv.shape[0]
            qpos = past + (_row_iota((rows, n)) & (tq - 1))
            kpos = past + _lane_iota((rows, n))
            s = jnp.where((kpos >> 6) <= (qpos >> 6), s, NEG)
        _softmax_update(s, lambda pb: _dot(pb, c), m_ref, l_ref, acc_ref)

    @pl.when(t < n_cache)
    def _():
        kr = rc_ref[...].astype(BF16)
        step(cc_ref[...], _dot(qr[...], jnp.concatenate([kr, kr], axis=0)), False)

    @pl.when(t == n_cache)
    def _():
        step(cn_ref[...], _dot_nt(qr[...], rn_ref[...].astype(BF16)), True)
        lat = (acc_ref[...] / l_ref[...]).astype(BF16)
        o_ref[...] = jnp.concatenate(
            [_dot(lat[h * tq:(h + 1) * tq, :], wuv_ref[:, MLA_V * h:MLA_V * (h + 1)]) for h in range(nh)],
            axis=-1).astype(BF16)


def _mla_dec(q, cc, rc, cn, rn, wuk, wuv, gkn, *, b, tq, past, tk, name):
    n_cache = past // tk
    last = n_cache - 1
    nh = MLA_HEADS
    new = lambda n: pl.BlockSpec((tq, n), lambda i, t: (i, 0))
    full = lambda a: pl.BlockSpec(a.shape, lambda i, t: (0,) * a.ndim)
    return pl.pallas_call(
        functools.partial(_mla_dec_kernel, n_cache=n_cache, tq=tq, past=past),
        out_shape=jax.ShapeDtypeStruct((b * tq, nh * MLA_V), BF16),
        grid=(b, n_cache + 1),
        in_specs=[new(nh * 2 * LANES),
                  pl.BlockSpec((tk, MLA_KV_LORA), lambda i, t: (i * n_cache + jnp.minimum(t, last), 0)),
                  pl.BlockSpec((None, MLA_ROPE, tk), lambda i, t: (i, 0, jnp.minimum(t, last))),
                  new(MLA_KV_LORA), new(LANES), full(wuk), full(wuv), full(gkn)],
        out_specs=new(nh * MLA_V),
        scratch_shapes=[pltpu.VMEM((nh * tq, nh * MLA_NOPE), BF16), pltpu.VMEM((nh * tq, LANES), BF16),
                        pltpu.VMEM((nh * tq, 1), F32), pltpu.VMEM((nh * tq, 1), F32),
                        pltpu.VMEM((nh * tq, MLA_KV_LORA), F32)],
        compiler_params=_cp(2), name=name)(q, cc, rc, cn, rn, wuk, wuv, gkn)


def _band_dec_kernel(q_ref, kc_ref, vc_ref, kn_ref, vn_ref, tab_ref, o_ref, *, tq, w, past):
    nh = BAND_HEADS
    rows = nh * tq
    bwid = ((w + tq + LANES - 1) // LANES) * LANES
    qb = _block_diag_q(q_ref[...], nh, BAND_DH)
    g = _rel_gather(tab_ref[...], bwid, w + tq) * LOG2E
    bias = jnp.concatenate(
        [pltpu.roll(jnp.broadcast_to(g[h:h + 1, :], (tq, bwid)), bwid - tq, axis=1, stride=1, stride_axis=0)
         for h in range(nh)], axis=0)
    qc = (past + (_row_iota((rows, bwid)) & (tq - 1))) >> 6
    kc = (past - w + _lane_iota((rows, bwid))) >> 6
    bias = jnp.where((kc <= qc) & (kc >= qc - BAND_LEFT_CHUNKS), bias, NEG)
    s_c = _dot(qb, kc_ref[...].astype(BF16)) + bias[:, :w]
    s_n = _dot_nt(qb, kn_ref[...].astype(BF16)) + bias[:, w:w + tq]
    m = jnp.maximum(jnp.max(s_c, axis=-1, keepdims=True), jnp.max(s_n, axis=-1, keepdims=True))
    p_c = jnp.exp2(s_c - m)
    p_n = jnp.exp2(s_n - m)
    l = jnp.sum(p_c, axis=-1, keepdims=True) + jnp.sum(p_n, axis=-1, keepdims=True)
    o_all = (_dot_nt(p_c.astype(BF16), vc_ref[...].astype(BF16))
             + _dot(p_n.astype(BF16), vn_ref[...].astype(BF16))) / l
    o = jnp.zeros((tq, D_MODEL), F32)
    hl = _lane_iota((tq, D_MODEL)) >> _log2(BAND_DH)
    for h in range(nh):
        o = jnp.where(hl == h, o_all[h * tq:(h + 1) * tq, :], o)
    o_ref[...] = o.astype(BF16)


def _band_dec(q, kc, vc, kn, vn, tab, *, b, tq, w, past, name):
    new = pl.BlockSpec((tq, D_MODEL), lambda i: (i, 0))
    cache = pl.BlockSpec((None, D_MODEL, w), lambda i: (i, 0, 0))
    return pl.pallas_call(
        functools.partial(_band_dec_kernel, tq=tq, w=w, past=past),
        out_shape=jax.ShapeDtypeStruct((b * tq, D_MODEL), BF16),
        grid=(b,),
        in_specs=[new, cache, cache, new, new, pl.BlockSpec(tab.shape, lambda i: (0, 0))],
        out_specs=new,
        compiler_params=_cp(1), name=name)(q, kc, vc, kn, vn, tab)


def _ffn_rows_kernel(*refs, mode, tps, seq, final, tf):
    it = iter(refs)
    h_ref, o_ref, wo_ref, gf_ref, wup_ref, cw_ref, cb_ref, wd_ref = [next(it) for _ in range(8)]
    if mode == "state":
        s1_ref, s2_ref = next(it), next(it)
    if final:
        gfin_ref = next(it)
    out_ref, ug_out, uh_out = next(it), next(it), next(it)
    if mode == "carry":
        carry_ref = next(it)
    tm = h_ref.shape[0]
    h1 = h_ref[...] + _dot(o_ref[...], wo_ref[...])
    xn = _rms(h1, gf_ref[...]).astype(BF16)
    if mode == "carry":
        @pl.when(pl.program_id(0) % tps == 0)
        def _():
            carry_ref[...] = jnp.zeros_like(carry_ref)
    else:
        tpos = _row_iota((tm, tf)) & (seq - 1)
    acts = []
    for j in range(D_FF // tf):
        cs = []
        for part, u_out in enumerate((ug_out, uh_out)):
            tile = slice(tf * j, tf * (j + 1))
            cols = slice(part * D_FF + tf * j, part * D_FF + tf * (j + 1))
            u = _dot(xn, wup_ref[:, cols])
            if mode == "carry":
                tail = u[tm - SUBLANES:]
                u_out[:, tile] = tail
                ext = jnp.concatenate([carry_ref[part, :, tile], u], axis=0)
                carry_ref[part, :, tile] = tail
                um1 = pltpu.roll(ext, 1, axis=0)[SUBLANES:]
                um2 = pltpu.roll(ext, 2, axis=0)[SUBLANES:]
            else:
                u_out[:, tile] = u
                um1 = jnp.where(tpos >= 1, pltpu.roll(u, 1, axis=0), s1_ref[:, cols])
                um2 = jnp.where(tpos >= 2, pltpu.roll(u, 2, axis=0), s2_ref[:, cols])
            cw = cw_ref[:, cols]
            cs.append(((cb_ref[:, cols] + u * cw[2:3]) + um2 * cw[0:1]) + um1 * cw[1:2])
        acts.append((cs[0] * jax.nn.sigmoid(cs[0]) * cs[1]).astype(BF16))
    acc = h1 + _dot(jnp.concatenate(acts, axis=-1), wd_ref[...])
    if final:
        acc = _rms(acc, gfin_ref[...])
    out_ref[...] = acc


def _ffn_rows(h, o, wo, gf, wup, cw, cb, wd, *, tm, tf, seq, state=None, final_g=None, name):
    m = h.shape[0]
    mode = "carry" if state is None else "state"
    tps = seq // tm if mode == "carry" else 1
    row = lambda n: pl.BlockSpec((tm, n), lambda i: (i, 0))
    once = lambda a: pl.BlockSpec(a.shape, lambda i: (0,) * a.ndim, pipeline_mode=pl.Buffered(1))
    ins = [h, o, wo, gf, wup, cw, cb, wd]
    specs = [row(D_MODEL), row(D_MODEL)] + [once(x) for x in ins[2:]]
    scratch = []
    if mode == "state":
        assert m == tm and seq & (seq - 1) == 0
        ins += list(state)
        specs += [row(2 * D_FF), row(2 * D_FF)]
        u_shape = jax.ShapeDtypeStruct((m, D_FF), F32)
        u_spec = row(D_FF)
    else:
        assert seq % tm == 0
        scratch.append(pltpu.VMEM((2, SUBLANES, D_FF), F32))
        u_shape = jax.ShapeDtypeStruct((m // tm, SUBLANES, D_FF), F32)
        u_spec = pl.BlockSpec((None, SUBLANES, D_FF), lambda i: (i, 0, 0))
    if final_g is not None:
        ins.append(final_g)
        specs.append(once(final_g))
    kern = functools.partial(_ffn_rows_kernel, mode=mode, tps=tps, seq=seq, final=final_g is not None, tf=tf)
    return pl.pallas_call(
        kern, out_shape=[jax.ShapeDtypeStruct((m, D_MODEL), F32), u_shape, u_shape],
        grid=(m // tm,), in_specs=specs, out_specs=[row(D_MODEL), u_spec, u_spec], scratch_shapes=scratch,
        compiler_params=_cp(1), name=name)(*ins)


def _rope_tables(pos):
    inv = 1.0 / (ROPE_THETA ** (jnp.arange(0, 64, 2, dtype=F32) / 64))
    ang = pos.astype(F32)[:, None] * inv[None, :]
    c, s = jnp.cos(ang), jnp.sin(ang)
    return jnp.tile(c, (1, 4)), jnp.tile(jnp.concatenate([-s, s], axis=1), (1, 2)), c.T, s.T


def _tile_gain(g, n):
    return jnp.tile(g.astype(F32), n // g.shape[0]).reshape(1, n)


def _gain_t(g, n, w):
    return jnp.broadcast_to(jnp.tile(g.astype(F32), n // g.shape[0])[:, None], (n, w))


def _to_tokens(xt, heads):
    b, n, t = xt.shape
    dh = n // math.prod(heads)
    nd = len(heads)
    return xt.reshape(b, *heads, dh, t).transpose(0, nd + 2, *range(1, nd + 2))


def _to_features(x):
    b, t = x.shape[:2]
    nd = x.ndim
    return x.transpose(0, *range(2, nd), 1).reshape(b, -1, t)


def kernel(x_prompt, x_sample, cache_mla_ckv, cache_mla_krope, cache_fox_k, cache_fox_v, cache_fox_logf,
           cache_diff_k, cache_diff_v, cache_band_k, cache_band_v, state_ffn_conv,
           attn_norm_g, ffn_norm_g, final_norm_g,
           mla_w_dq, mla_g_q, mla_w_uq, mla_w_dkv, mla_g_kv, mla_w_uk, mla_w_uv,
           mla_g_qn, mla_g_qr, mla_g_kn, mla_g_kr, mla_w_o,
           fox_w_qkv, fox_w_f, fox_b_f, fox_g_q, fox_g_k, fox_w_o,
           diff_w_qkv, diff_g_q, diff_g_k, diff_lq1, diff_lk1, diff_lq2, diff_lk2, diff_g_sub, diff_w_o,
           band_w_qkv, band_g_q, band_g_k, band_rel_bias, band_w_o,
           ffn_w_up, ffn_conv_w, ffn_conv_b, ffn_w_down):
    bp, tp, d = x_prompt.shape
    bs, ts, _ = x_sample.shape
    past = cache_mla_ckv.shape[1]
    depth = attn_norm_g.shape[0]
    mp, ms = bp * tp, bs * ts
    tm_p = min(512, tp)
    tm_f = min(512, tp)
    tq = 512
    tq_band = 256
    tkd = min(1024, past)
    tkw = min(2048, past)
    tf = 256
    assert tp % tq == 0 and past % tkd == 0 and past % tkw == 0 and past % CHUNK == 0

    tobf = lambda a: a.astype(BF16)
    rowv = lambda g: g.astype(F32).reshape(1, -1)
    pos_p = jnp.arange(tp, dtype=jnp.int32)
    pos_s = past + jnp.arange(ts, dtype=jnp.int32)
    tabs_p = _rope_tables(pos_p)
    tabs_s = tuple(jnp.tile(a, (bs, 1)) for a in _rope_tables(pos_s)[:2])

    nope_cols = jnp.arange(MLA_HEADS)[:, None] * (MLA_NOPE + MLA_ROPE) + jnp.arange(MLA_NOPE)[None, :]
    rope_cols = jnp.arange(MLA_HEADS)[:, None] * (MLA_NOPE + MLA_ROPE) + MLA_NOPE + jnp.arange(MLA_ROPE)[None, :]
    wuq_perm = jnp.concatenate([mla_w_uq[:, nope_cols.reshape(-1)], mla_w_uq[:, rope_cols.reshape(-1)]], axis=1)
    wr = mla_w_dkv[:, MLA_KV_LORA:]
    mla_w = dict(
        wdq=tobf(mla_w_dq), gq=rowv(mla_g_q), wuq=tobf(wuq_perm), wc=tobf(mla_w_dkv[:, :MLA_KV_LORA]),
        wr=tobf(jnp.concatenate([wr, wr], axis=1)), gkv=rowv(mla_g_kv),
        gqn=_tile_gain(mla_g_qn, MLA_HEADS * MLA_NOPE), gqr=_tile_gain(mla_g_qr, MLA_HEADS * MLA_ROPE),
        gkr=_tile_gain(mla_g_kr, LANES), wuk=tobf(mla_w_uk), wuv=tobf(mla_w_uv),
        gkn=_tile_gain(mla_g_kn, MLA_HEADS * MLA_NOPE), wrt=tobf(wr.T), gkrt=_gain_t(mla_g_kr, MLA_ROPE, tm_p))

    def split_qkv(w):
        wq, wk, wv = w[:, :d], w[:, d:2 * d], w[:, 2 * d:]
        return tobf(wq), tobf(wk.T), tobf(wv), tobf(wv.T)

    fox_wf = tobf(jnp.pad(fox_w_f, ((0, 0), (0, LANES - FOX_HEADS))))
    fox_bf = jnp.pad(fox_b_f.astype(F32), (0, LANES - FOX_HEADS)).reshape(1, LANES)
    fox_wft = tobf(fox_w_f.T)
    fox_bft = jnp.broadcast_to(fox_b_f.astype(F32)[:, None], (FOX_HEADS, tm_p))
    lamv = jnp.stack([diff_lq1, diff_lk1, diff_lq2, diff_lk2]).astype(F32)
    gsub = rowv(diff_g_sub)
    tab_pad = jnp.pad(band_rel_bias.astype(F32), ((0, 0), (0, REL_PAD - band_rel_bias.shape[1])))
    wo = [tobf(mla_w_o), tobf(fox_w_o), tobf(diff_w_o), tobf(band_w_o)]
    wup, wdn = tobf(ffn_w_up), tobf(ffn_w_down)
    cwf, cbf = ffn_conv_w.astype(F32), ffn_conv_b.astype(F32)

    h_p = x_prompt.reshape(mp, d)
    h_s = x_sample.reshape(ms, d)
    outs = {}
    conv_p, conv_s = [], []
    for i in range(depth):
        kind = i % 4
        ga = rowv(attn_norm_g[i])
        if kind == 0:
            scale = (MLA_NOPE + MLA_ROPE) ** -0.5 * LOG2E
            q_p, ckv_p, _, _, kc_p, v_p, krt_p = _mla_proj(h_p, ga, mla_w, tabs_p, b=bp, t=tp, tm=tm_p, with_kv=True,
                                                           scale=scale, name="mla_proj_p")
            o_p = _mla_attn(q_p, kc_p, v_p, b=bp, t=tp, tq=tq, hps=2, name="mla_attn_p")
            q_s, ckv_s, kr_s, krd_s = _mla_proj(h_s, ga, mla_w, tabs_s, b=bs, t=ts, tm=ms, with_kv=False,
                                                scale=scale, name="mla_proj_s")
            o_s = _mla_dec(q_s, cache_mla_ckv.astype(F32).reshape(bs * past, MLA_KV_LORA),
                           _to_features(cache_mla_krope.astype(F32)), ckv_s, krd_s,
                           mla_w["wuk"], mla_w["wuv"], mla_w["gkn"], b=bs, tq=ts, past=past, tk=tkd,
                           name="mla_attn_s")
            outs["mla"] = (ckv_p.reshape(bp, tp, MLA_KV_LORA), jnp.swapaxes(krt_p, 1, 2),
                           ckv_s.reshape(bs, ts, MLA_KV_LORA), kr_s.reshape(bs, ts, MLA_ROPE))
        elif kind == 1:
            scale = FOX_DH ** -0.5 * LOG2E
            gq, gk = _tile_gain(fox_g_q, d), _tile_gain(fox_g_k, d)
            wq, wkt, wv, wvt = split_qkv(fox_w_qkv)
            q_p, kt_p, vt_p, lft_p = _qkv_proj_t(h_p, ga, wq, wkt, wvt, gq, _gain_t(fox_g_k, d, tm_p), b=bp, t=tp,
                                                 tm=tm_p, wft=fox_wft, bft=fox_bft, scale=scale, name="fox_proj_p")
            fk_p = _cumsum_last(lft_p, tk=tq, name="fox_cumsum_p")
            o_p = _pair_attn(q_p, kt_p, vt_p, (fk_p.reshape(bp, FOX_HEADS // 2, 2, tp),), mode="fox",
                             b=bp, t=tp, tq=tq, name="fox_attn_p")
            q_s, k_s, v_s, lf_s = _qkv_proj(h_s, ga, tobf(fox_w_qkv), gq, gk, tm=ms, wf=fox_wf, bf=fox_bf,
                                            scale=scale, name="fox_proj_s")
            lf_all = jnp.concatenate([jnp.swapaxes(cache_fox_logf.astype(F32), 1, 2),
                                      jnp.swapaxes(lf_s.reshape(bs, ts, FOX_HEADS), 1, 2)], axis=2)
            lf_all = jnp.pad(lf_all, ((0, 0), (0, 0), (0, tkw - ts)))
            f_all = _cumsum_last(lf_all, tk=tq, name="fox_cumsum_s")
            o_s = _dec_attn(q_s, _to_features(cache_fox_k.astype(F32)), _to_features(cache_fox_v.astype(F32)),
                            k_s, v_s, (f_all,), mode="fox", b=bs, tq=ts, past=past, tk=tkw, name="fox_attn_s")
            sh = (FOX_HEADS, FOX_DH)
            outs["fox"] = (_to_tokens(kt_p, (FOX_HEADS,)), _to_tokens(vt_p, (FOX_HEADS,)),
                           jnp.swapaxes(lft_p, 1, 2),
                           k_s.reshape(bs, ts, *sh), v_s.reshape(bs, ts, *sh), lf_s.reshape(bs, ts, FOX_HEADS))
        elif kind == 2:
            scale = DIFF_DH ** -0.5 * LOG2E
            lam_init = 0.8 - 0.6 * math.exp(-0.3 * i)
            gq, gk = _tile_gain(diff_g_q, d), _tile_gain(diff_g_k, d)
            wq, wkt, wv, wvt = split_qkv(diff_w_qkv)
            q_p, kt_p, v4_p = _qkv_proj_t(h_p, ga, wq, wkt, wv, gq, _gain_t(diff_g_k, d, tm_p), b=bp, t=tp, tm=tm_p,
                                          rope_tabs=tabs_p, v_tokens=True, scale=scale, name="diff_proj_p")
            v_p = v4_p.reshape(bp, tp, DIFF_HEADS, 2 * DIFF_DH)
            o_p = _pair_attn(q_p, kt_p, v4_p, (lamv, gsub), mode="diff", b=bp, t=tp, tq=tq, lam_init=lam_init,
                             name="diff_attn_p")
            q_s, k_s, v_s = _qkv_proj(h_s, ga, tobf(diff_w_qkv), gq, gk, tm=ms, rope_tabs=tabs_s, scale=scale,
                                      name="diff_proj_s")
            o_s = _dec_attn(q_s, _to_features(cache_diff_k.astype(F32)),
                            cache_diff_v.astype(F32).reshape(bs * past * DIFF_HEADS, 2 * DIFF_DH), k_s, v_s,
                            (lamv, gsub), mode="diff", b=bs, tq=ts, past=past, tk=tkw, lam_init=lam_init,
                            name="diff_attn_s")
            outs["diff"] = (_to_tokens(kt_p, (DIFF_HEADS, 2)), v_p,
                            k_s.reshape(bs, ts, DIFF_HEADS, 2, DIFF_DH), v_s.reshape(bs, ts, DIFF_HEADS, 2 * DIFF_DH))
        else:
            scale = BAND_DH ** -0.5 * LOG2E
            gq, gk = _tile_gain(band_g_q, d), _tile_gain(band_g_k, d)
            wq, wkt, wv, wvt = split_qkv(band_w_qkv)
            q_p, kt_p, vt_p = _qkv_proj_t(h_p, ga, wq, wkt, wvt, gq, _gain_t(band_g_k, d, tm_p), b=bp, t=tp, tm=tm_p,
                                          scale=scale, name="band_proj_p")
            tab_pairs = jnp.pad(tab_pad.reshape(BAND_HEADS // 2, 2, REL_PAD), ((0, 0), (0, SUBLANES - 2), (0, 0)))
            o_p = _band_attn(q_p, kt_p, vt_p, tab_pairs, b=bp, t=tp, tq=tq_band, name="band_attn_p")
            q_s, k_s, v_s = _qkv_proj(h_s, ga, tobf(band_w_qkv), gq, gk, tm=ms, scale=scale, name="band_proj_s")
            w = cache_band_k.shape[1]
            kct = _to_features(cache_band_k.astype(F32))
            vct = _to_features(cache_band_v.astype(F32))
            o_s = _band_dec(q_s, kct, vct, k_s, v_s, tab_pad, b=bs, tq=ts, w=w, past=past, name="band_attn_s")
            keep = min(BAND_LEFT, tp)
            roll_in = lambda ct, new: jnp.concatenate(
                [ct, jnp.swapaxes(new.reshape(bs, ts, d), 1, 2)], axis=2)[:, :, ts:]
            outs["band"] = (_to_tokens(kt_p[:, :, tp - keep:], (BAND_HEADS,)),
                            _to_tokens(vt_p[:, :, tp - keep:], (BAND_HEADS,)),
                            _to_tokens(roll_in(kct, k_s), (BAND_HEADS,)), _to_tokens(roll_in(vct, v_s), (BAND_HEADS,)))

        last = i == depth - 1
        gfin = rowv(final_norm_g) if last else None
        gfn = rowv(ffn_norm_g[i])
        h_p, ug, uh = _ffn_rows(h_p, o_p, wo[kind], gfn, wup[i], cwf[i], cbf[i].reshape(1, -1), wdn[i], tm=tm_f,
                                tf=tf, seq=tp, final_g=gfin, name="ffn_p")
        tps = tp // tm_f
        conv_p.append(jnp.concatenate([ug, uh], axis=-1)[tps - 1::tps, SUBLANES - (CONV_W - 1):])
        st = state_ffn_conv[i].astype(F32)
        zeros = jnp.zeros((bs, ts, 2 * D_FF), F32)
        s1 = zeros.at[:, 0].set(st[:, 1]).reshape(ms, 2 * D_FF)
        s2 = zeros.at[:, 0].set(st[:, 0]).at[:, 1].set(st[:, 1]).reshape(ms, 2 * D_FF)
        h_s, ug, uh = _ffn_rows(h_s, o_s, wo[kind], gfn, wup[i], cwf[i], cbf[i].reshape(1, -1), wdn[i], tm=ms,
                                tf=tf, seq=ts, state=(s1, s2), final_g=gfin, name="ffn_s")
        u_s = jnp.concatenate([ug, uh], axis=-1).reshape(bs, ts, 2 * D_FF)
        conv_s.append(jnp.concatenate([st, u_s], axis=1)[:, ts:])

    y_prompt = h_p.reshape(bp, tp, d)
    y_sample = h_s.reshape(bs, ts, d)
    return (y_prompt, y_sample) + outs["mla"] + outs["fox"] + outs["diff"] + outs["band"] + (
        jnp.stack(conv_p, axis=0), jnp.stack(conv_s, axis=0))
```

```python
import functools
import math

import jax
import jax.numpy as jnp
from jax import lax
from jax.experimental import pallas as pl
from jax.experimental.pallas import tpu as pltpu

F32 = jnp.float32
BF16 = jnp.bfloat16

D_MODEL = 1024
CHUNK = 64
ROPE_THETA = 10000.0
NORM_EPS = 1e-6
MLA_HEADS, MLA_Q_LORA, MLA_KV_LORA, MLA_NOPE, MLA_ROPE, MLA_V = 8, 384, 256, 128, 64, 128
FOX_HEADS, FOX_DH = 16, 64
DIFF_HEADS, DIFF_DH = 8, 64
BAND_HEADS, BAND_DH = 16, 64
BAND_LEFT_CHUNKS = 8
BAND_LEFT = BAND_LEFT_CHUNKS * CHUNK
REL_CLIP = 128
D_FF = 2816
CONV_W = 3

LANES = 128
SUBLANES = 8
NEG = -1e30
REL_PAD = 384
LOG2E = math.log2(math.e)
VMEM_LIMIT = 56 * 1024 * 1024


def _cp(n_axes):
    return pltpu.CompilerParams(dimension_semantics=("arbitrary",) * n_axes, vmem_limit_bytes=VMEM_LIMIT)


def _dot(a, b):
    return jnp.dot(a, b, preferred_element_type=F32)


def _dot_nt(a, b):
    return lax.dot_general(a, b, (((1,), (1,)), ((), ())), preferred_element_type=F32)


def _rms(x, g):
    ms = jnp.mean(x * x, axis=-1, keepdims=True)
    return x * lax.rsqrt(ms + NORM_EPS) * g


def _lane_iota(shape):
    return lax.broadcasted_iota(jnp.int32, shape, len(shape) - 1)


def _row_iota(shape):
    return lax.broadcasted_iota(jnp.int32, shape, len(shape) - 2)


def _log2(n):
    assert n & (n - 1) == 0, n
    return n.bit_length() - 1


def _head_rms(x, g, seg):
    n = x.shape[1]
    outs = []
    for c in range(n // LANES):
        xs = x[:, LANES * c:LANES * (c + 1)]
        sq = xs * xs
        if seg == LANES:
            r = lax.rsqrt(jnp.sum(sq, axis=-1, keepdims=True) * (1.0 / seg) + NORM_EPS)
        else:
            lo = _lane_iota(xs.shape) < seg
            s_lo = jnp.sum(jnp.where(lo, sq, 0.0), axis=-1, keepdims=True)
            s_hi = jnp.sum(jnp.where(lo, 0.0, sq), axis=-1, keepdims=True)
            r = jnp.where(lo, lax.rsqrt(s_lo * (1.0 / seg) + NORM_EPS), lax.rsqrt(s_hi * (1.0 / seg) + NORM_EPS))
        outs.append(xs * r * g[:, LANES * c:LANES * (c + 1)])
    return jnp.concatenate(outs, axis=-1) if len(outs) > 1 else outs[0]


def _head_rms_t(x, g, seg=64):
    outs = []
    for hd in range(x.shape[0] // seg):
        xs = x[seg * hd:seg * (hd + 1)]
        r = lax.rsqrt(jnp.sum(xs * xs, axis=0, keepdims=True) * (1.0 / seg) + NORM_EPS)
        outs.append(xs * r * g[seg * hd:seg * (hd + 1)])
    return jnp.concatenate(outs, axis=0) if len(outs) > 1 else outs[0]


def _rope(x, cos, sinp):
    lo32 = (_lane_iota((x.shape[0], LANES)) & 63) < 32
    outs = []
    for c in range(x.shape[1] // LANES):
        xs = x[:, LANES * c:LANES * (c + 1)]
        fwd = pltpu.roll(xs, 32, axis=1)
        bwd = pltpu.roll(xs, 96, axis=1)
        outs.append(xs * cos + jnp.where(lo32, bwd, fwd) * sinp)
    return jnp.concatenate(outs, axis=-1) if len(outs) > 1 else outs[0]


def _rope_t(x, cos_t, sin_t):
    outs = []
    for hd in range(x.shape[0] // 64):
        x1 = x[64 * hd:64 * hd + 32]
        x2 = x[64 * hd + 32:64 * (hd + 1)]
        outs += [x1 * cos_t - x2 * sin_t, x2 * cos_t + x1 * sin_t]
    return jnp.concatenate(outs, axis=0)


def _split3(x):
    hi = x.astype(BF16)
    r = x - hi.astype(F32)
    mid = r.astype(BF16)
    lo = (r - mid.astype(F32)).astype(BF16)
    return hi, mid, lo


def _softmax_step(s, pv, m, l, acc):
    m_new = jnp.maximum(m, jnp.max(s, axis=-1, keepdims=True))
    alpha = jnp.exp2(m - m_new)
    p = jnp.exp2(s - m_new)
    if l is not None:
        l = alpha * l + jnp.sum(p, axis=-1, keepdims=True)
    return m_new, l, alpha * acc + pv(p.astype(BF16))


def _softmax_update(s, pv, m_ref, l_ref, acc_ref):
    m, l, acc = _softmax_step(s, pv, m_ref[...], l_ref[...], acc_ref[...])
    m_ref[...] = m
    l_ref[...] = l
    acc_ref[...] = acc


def _log_sigmoid(z):
    return jnp.minimum(z, 0.0) - jnp.log1p(jnp.exp(-jnp.abs(z)))


def _qkv_proj_kernel(*refs, rope, logf, scale):
    it = iter(refs)
    h_ref, g_ref, w_ref, gq_ref, gk_ref = next(it), next(it), next(it), next(it), next(it)
    cos_ref = sin_ref = wf_ref = bf_ref = None
    if rope:
        cos_ref, sin_ref = next(it), next(it)
    if logf:
        wf_ref, bf_ref = next(it), next(it)
    q_out, k_out, v_out = next(it), next(it), next(it)
    a = _rms(h_ref[...], g_ref[...]).astype(BF16)
    qkv = _dot(a, w_ref[...])
    q = _head_rms(qkv[:, :D_MODEL], gq_ref[...], 64)
    k = _head_rms(qkv[:, D_MODEL:2 * D_MODEL], gk_ref[...], 64)
    if rope:
        q = _rope(q, cos_ref[...], sin_ref[...])
        k = _rope(k, cos_ref[...], sin_ref[...])
    q_out[...] = (q * scale).astype(BF16)
    k_out[...] = k
    v_out[...] = qkv[:, 2 * D_MODEL:]
    if logf:
        logf_out = next(it)
        lf = _log_sigmoid(_dot(a, wf_ref[...]) + bf_ref[...])
        logf_out[...] = lf[:, :FOX_HEADS]


def _qkv_proj(h, g, w, gq, gk, *, tm, rope_tabs=None, wf=None, bf=None, scale, name):
    m = h.shape[0]
    grid = (m // tm,)
    row = lambda n: pl.BlockSpec((tm, n), lambda i: (i, 0))
    full = lambda a: pl.BlockSpec(a.shape, lambda i: (0,) * a.ndim)
    ins = [h, g, w, gq, gk]
    specs = [row(D_MODEL), full(g), full(w), full(gq), full(gk)]
    if rope_tabs is not None:
        cos, sinp = rope_tabs
        nt = cos.shape[0] // tm
        tab = pl.BlockSpec((tm, LANES), lambda i: (i % nt, 0))
        ins += [cos, sinp]
        specs += [tab, tab]
    if wf is not None:
        ins += [wf, bf]
        specs += [full(wf), full(bf)]
    outs = [jax.ShapeDtypeStruct((m, D_MODEL), BF16), jax.ShapeDtypeStruct((m, D_MODEL), F32),
            jax.ShapeDtypeStruct((m, D_MODEL), F32)]
    ospecs = [row(D_MODEL), row(D_MODEL), row(D_MODEL)]
    if wf is not None:
        outs.append(jax.ShapeDtypeStruct((m, FOX_HEADS), F32))
        ospecs.append(row(FOX_HEADS))
    kern = functools.partial(_qkv_proj_kernel, rope=rope_tabs is not None, logf=wf is not None, scale=scale)
    return pl.pallas_call(kern, out_shape=outs, grid=grid, in_specs=specs, out_specs=ospecs,
                          compiler_params=_cp(1), name=name)(*ins)


def _qkv_proj_t_kernel(*refs, rope, logf, v_tokens, tail, scale):
    it = iter(refs)
    h_ref, g_ref, wq_ref, wkt_ref, wv_ref, gq_ref, gkt_ref = [next(it) for _ in range(7)]
    if rope:
        cos_ref, sin_ref, cost_ref, sint_ref = [next(it) for _ in range(4)]
    if logf:
        wft_ref, bft_ref = next(it), next(it)
    q_out, kt_out, v_out = next(it), next(it), next(it)
    a = _rms(h_ref[...], g_ref[...]).astype(BF16)
    tm = a.shape[0]
    q = _head_rms(_dot(a, wq_ref[...]), gq_ref[...], 64)
    kt = _head_rms_t(_dot_nt(wkt_ref[...], a), gkt_ref[...])
    if rope:
        q = _rope(q, cos_ref[...], sin_ref[...])
        kt = _rope_t(kt, cost_ref[...], sint_ref[...])
    q_out[...] = (q * scale).astype(BF16)
    kt_out[...] = kt
    if v_tokens:
        v = _dot(a, wv_ref[...])
        nh = D_MODEL // LANES
        for hd in range(nh):
            v_out[pl.ds(hd, tm, stride=nh), :] = v[:, LANES * hd:LANES * (hd + 1)]
    else:
        vt = _dot_nt(wv_ref[...], a)
        v_out[...] = vt
    if logf:
        lft_out = next(it)
        lft_out[...] = _log_sigmoid(_dot_nt(wft_ref[...], a) + bft_ref[...])
    if tail:
        kt_tail, vt_tail = next(it), next(it)
        kt_tail[...] = kt
        vt_tail[...] = vt


def _qkv_proj_t(h, g, wq, wkt, wv, gq, gkt, *, b, t, tm, rope_tabs=None, wft=None, bft=None, v_tokens=False,
                tail=False, scale, name):
    m = b * t
    nt = t // tm
    nh = D_MODEL // LANES
    row = lambda n: pl.BlockSpec((tm, n), lambda i: (i, 0))
    full = lambda a: pl.BlockSpec(a.shape, lambda i: (0,) * a.ndim)
    feat = lambda n: pl.BlockSpec((None, n, tm), lambda i: (i // nt, 0, i % nt))
    ins = [h, g, wq, wkt, wv, gq, gkt]
    specs = [row(D_MODEL)] + [full(x) for x in ins[1:]]
    if rope_tabs is not None:
        cos, sinp, cos_t, sin_t = rope_tabs
        ins += [cos, sinp, cos_t, sin_t]
        tab = pl.BlockSpec((tm, LANES), lambda i: (i % nt, 0))
        tab_t = pl.BlockSpec((32, tm), lambda i: (0, i % nt))
        specs += [tab, tab, tab_t, tab_t]
    if wft is not None:
        ins += [wft, bft]
        specs += [full(wft), full(bft)]
    outs = [jax.ShapeDtypeStruct((m, D_MODEL), BF16), jax.ShapeDtypeStruct((b, D_MODEL, t), F32)]
    ospecs = [row(D_MODEL), feat(D_MODEL)]
    if v_tokens:
        outs.append(jax.ShapeDtypeStruct((m * nh, LANES), F32))
        ospecs.append(pl.BlockSpec((tm * nh, LANES), lambda i: (i, 0)))
    else:
        outs.append(jax.ShapeDtypeStruct((b, D_MODEL, t), F32))
        ospecs.append(feat(D_MODEL))
    if wft is not None:
        outs.append(jax.ShapeDtypeStruct((b, FOX_HEADS, t), F32))
        ospecs.append(feat(FOX_HEADS))
    if tail:
        assert not v_tokens
        outs += [jax.ShapeDtypeStruct((b, D_MODEL, tm), F32)] * 2
        ospecs += [pl.BlockSpec((None, D_MODEL, tm), lambda i: (i // nt, 0, 0))] * 2
    kern = functools.partial(_qkv_proj_t_kernel, rope=rope_tabs is not None, logf=wft is not None,
                             v_tokens=v_tokens, tail=tail, scale=scale)
    return pl.pallas_call(kern, out_shape=outs, grid=(m // tm,), in_specs=specs, out_specs=ospecs,
                          compiler_params=_cp(1), name=name)(*ins)


def _mla_proj_kernel(*refs, with_kv, scale):
    it = iter(refs)
    (h_ref, g_ref, wdq_ref, gq_ref, wuq_ref, wc_ref, wr_ref, gkv_ref, gqn_ref, gqr_ref, gkr_ref,
     cos_ref, sin_ref) = [next(it) for _ in range(13)]
    if with_kv:
        wuk_ref, wuv_ref, gkn_ref, wrt_ref, gkrt_ref, cost_ref, sint_ref = [next(it) for _ in range(7)]
    q_out, ckv_out, kr_out, krd_out = next(it), next(it), next(it), next(it)
    a = _rms(h_ref[...], g_ref[...]).astype(BF16)
    cq = _rms(_dot(a, wdq_ref[...]), gq_ref[...]).astype(BF16)
    q = _dot(cq, wuq_ref[...])
    n_nope = MLA_HEADS * MLA_NOPE
    qn = _head_rms(q[:, :n_nope], gqn_ref[...], MLA_NOPE) * scale
    qr = _rope(_head_rms(q[:, n_nope:], gqr_ref[...], MLA_ROPE), cos_ref[...], sin_ref[...]) * scale
    lo = _lane_iota((q.shape[0], LANES)) < MLA_ROPE
    pieces = []
    for hd in range(MLA_HEADS):
        slab = qr[:, LANES * (hd // 2):LANES * (hd // 2 + 1)]
        keep = lo if hd % 2 == 0 else jnp.logical_not(lo)
        pieces += [qn[:, LANES * hd:LANES * (hd + 1)], jnp.where(keep, slab, 0.0)]
    q_out[...] = jnp.concatenate(pieces, axis=-1).astype(BF16)
    ckv = _rms(_dot(a, wc_ref[...]), gkv_ref[...])
    ckv_out[...] = ckv
    kr2 = _rope(_head_rms(_dot(a, wr_ref[...]), gkr_ref[...], MLA_ROPE), cos_ref[...], sin_ref[...])
    kr_out[...] = kr2[:, :MLA_ROPE]
    krd_out[...] = kr2
    if with_kv:
        kc_out, v_out, krt_out = next(it), next(it), next(it)
        c = ckv.astype(BF16)
        kn = _head_rms(_dot(c, wuk_ref[...]), gkn_ref[...], MLA_NOPE)
        pieces = []
        for hd in range(MLA_HEADS):
            pieces += [kn[:, LANES * hd:LANES * (hd + 1)], kr2]
        kc_out[...] = jnp.concatenate(pieces, axis=-1).astype(BF16)
        v_out[...] = _dot(c, wuv_ref[...]).astype(BF16)
        krt_out[...] = _rope_t(_head_rms_t(_dot_nt(wrt_ref[...], a), gkrt_ref[...]), cost_ref[...], sint_ref[...])


def _mla_proj(h, g, w, rope_tabs, *, b, t, tm, with_kv, scale, name):
    m = h.shape[0]
    cos, sinp = rope_tabs[:2]
    nt = cos.shape[0] // tm
    row = lambda n: pl.BlockSpec((tm, n), lambda i: (i, 0))
    full = lambda a: pl.BlockSpec(a.shape, lambda i: (0,) * a.ndim)
    tab = pl.BlockSpec((tm, LANES), lambda i: (i % nt, 0))
    ins = [h, g, w["wdq"], w["gq"], w["wuq"], w["wc"], w["wr"], w["gkv"], w["gqn"], w["gqr"], w["gkr"], cos, sinp]
    specs = [row(D_MODEL)] + [full(x) for x in ins[1:11]] + [tab, tab]
    qw = MLA_HEADS * 2 * LANES
    outs = [jax.ShapeDtypeStruct((m, qw), BF16), jax.ShapeDtypeStruct((m, MLA_KV_LORA), F32),
            jax.ShapeDtypeStruct((m, MLA_ROPE), F32), jax.ShapeDtypeStruct((m, LANES), F32)]
    ospecs = [row(qw), row(MLA_KV_LORA), row(MLA_ROPE), row(LANES)]
    if with_kv:
        cos_t, sin_t = rope_tabs[2:]
        tab_t = pl.BlockSpec((32, tm), lambda i: (0, i % nt))
        extra = [w["wuk"], w["wuv"], w["gkn"], w["wrt"], w["gkrt"]]
        ins += extra + [cos_t, sin_t]
        specs += [full(x) for x in extra] + [tab_t, tab_t]
        outs += [jax.ShapeDtypeStruct((m, qw), BF16), jax.ShapeDtypeStruct((m, MLA_HEADS * MLA_V), BF16),
                 jax.ShapeDtypeStruct((b, MLA_ROPE, t), F32)]
        ospecs += [row(qw), row(MLA_HEADS * MLA_V),
                   pl.BlockSpec((None, MLA_ROPE, tm), lambda i: (i // nt, 0, i % nt))]
    kern = functools.partial(_mla_proj_kernel, with_kv=with_kv, scale=scale)
    return pl.pallas_call(kern, out_shape=outs, grid=(m // tm,), in_specs=specs, out_specs=ospecs,
                          compiler_params=_cp(1), name=name)(*ins)


def _cumsum_kernel(x_ref, o_ref, *, tk):
    nh, t = x_ref.shape
    tri = (_row_iota((tk, tk)) <= _lane_iota((tk, tk))).astype(BF16)
    carry = jnp.zeros((nh, 1), F32)
    for c in range(t // tk):
        hi, mid, lo = _split3(x_ref[:, tk * c:tk * (c + 1)])
        y = _dot(jnp.concatenate([hi, mid, lo], axis=0), tri)
        f = (y[:nh] + y[nh:2 * nh]) + y[2 * nh:] + carry
        o_ref[:, tk * c:tk * (c + 1)] = f
        carry = f[:, tk - 1:tk]


def _cumsum_last(x, *, tk, name):
    b, nh, t = x.shape
    spec = pl.BlockSpec((None, nh, t), lambda i: (i, 0, 0))
    return pl.pallas_call(functools.partial(_cumsum_kernel, tk=tk), out_shape=jax.ShapeDtypeStruct(x.shape, F32),
                          grid=(b,), in_specs=[spec], out_specs=spec, compiler_params=_cp(1), name=name)(x)


def _mla_attn_kernel(q_ref, k_ref, v_ref, o_ref, *, tq, hps):
    kw = 2 * LANES
    t = k_ref.shape[0]
    for qi in range(t // tq):
        rows = slice(tq * qi, tq * (qi + 1))
        limit = (((qi * tq + _row_iota((tq, 1))) >> 6) + 1) << 6
        outs = []
        n0 = qi * tq
        for h in range(hps):
            q = q_ref[rows, kw * h:kw * (h + 1)]
            kcols = slice(kw * h, kw * (h + 1))
            vcols = slice(MLA_V * h, MLA_V * (h + 1))
            s_d = jnp.where(n0 + _lane_iota((1, tq)) < limit, _dot_nt(q, k_ref[n0:n0 + tq, kcols]), NEG)
            m = jnp.max(s_d, axis=-1, keepdims=True)
            if qi:
                s_f = _dot_nt(q, k_ref[0:n0, kcols])
                m = jnp.maximum(m, jnp.max(s_f, axis=-1, keepdims=True))
            p_d = jnp.exp2(s_d - m)
            acc = _dot(p_d.astype(BF16), v_ref[n0:n0 + tq, vcols])
            l = jnp.sum(p_d, axis=-1, keepdims=True)
            if qi:
                p_f = jnp.exp2(s_f - m)
                acc = acc + _dot(p_f.astype(BF16), v_ref[0:n0, vcols])
                l = l + jnp.sum(p_f, axis=-1, keepdims=True)
            outs.append(acc / l)
        o_ref[rows, :] = jnp.concatenate(outs, axis=-1).astype(BF16)


def _mla_attn(q, kc, v, *, b, t, tq, hps, name):
    kw = 2 * LANES * hps
    vw = MLA_V * hps
    return pl.pallas_call(
        functools.partial(_mla_attn_kernel, tq=tq, hps=hps),
        out_shape=jax.ShapeDtypeStruct((b * t, MLA_HEADS * MLA_V), BF16),
        grid=(b, MLA_HEADS // hps),
        in_specs=[pl.BlockSpec((t, kw), lambda i, h: (i, h)),
                  pl.BlockSpec((t, kw), lambda i, h: (i, h)),
                  pl.BlockSpec((t, vw), lambda i, h: (i, h))],
        out_specs=pl.BlockSpec((t, vw), lambda i, h: (i, h)),
        compiler_params=_cp(2), name=name)(q, kc, v)


def _ones_row_values(vt, c):
    row = _row_iota(vt.shape)
    if c == 0:
        return jnp.where(row < 64, vt, jnp.where(row == 64, 1.0, 0.0)).astype(BF16)
    return jnp.where(row >= 64, vt, jnp.where(row == 0, 1.0, 0.0)).astype(BF16)


def _merge_pair(acc0, acc1):
    lo = _lane_iota(acc0.shape) < 64
    return jnp.where(lo, acc0 / acc0[:, 64:65], acc1 / acc1[:, 0:1])


def _pair_attn_kernel(*refs, mode, tq, lam_init):
    if mode == "fox":
        q_ref, kt_ref, vt_ref, fk_ref, o_ref, ka, va = refs
    else:
        q_ref, kt_ref, v_ref, lam_ref, gsub_ref, o_ref, ka, vb = refs
    t = kt_ref.shape[1]
    if mode == "fox":
        ka[0:LANES, :] = kt_ref[...].astype(BF16)
        f = fk_ref[...] * (-LOG2E)
        terms = [x.astype(F32) for x in _split3(f)]
        nr = 2 * SUBLANES
        row = _row_iota((nr, t))
        aug = jnp.zeros((nr, t), F32)
        for c in range(2):
            for j in range(3):
                aug = jnp.where(row == 3 * c + j, terms[j][c:c + 1, :], aug)
        ka[LANES:LANES + nr, :] = aug.astype(BF16)
        ka[LANES + nr:, :] = jnp.zeros((LANES - nr, t), BF16)
        vt = vt_ref[...]
        va[0] = _ones_row_values(vt, 0)
        va[1] = _ones_row_values(vt, 1)
    else:
        ka[...] = kt_ref[...].astype(BF16)
        nhv = v_ref.shape[0] // t
        vb[...] = v_ref[pl.ds(pl.program_id(1), t, stride=nhv), :].astype(BF16)

    def q_tile(qi, q):
        lane = _lane_iota((tq, LANES))
        zero = jnp.zeros_like(q)
        qa = [jnp.where(lane < 64, q, zero), jnp.where(lane < 64, zero, q)]
        qpos = qi * tq + _row_iota((tq, 1))
        if mode == "fox":
            limit = qpos + 1
            pick = [jnp.where(lane < 3 * c, 0.0, jnp.where(lane < 3 * c + 3, 1.0, 0.0)).astype(BF16)
                    for c in range(2)]
            qa = [jnp.concatenate([qa[c], pick[c]], axis=-1) for c in range(2)]
        else:
            limit = ((qpos >> 6) + 1) << 6

        n0 = qi * tq
        accs, sums = [], []
        for c in range(2):
            s_d = jnp.where(n0 + _lane_iota((1, tq)) < limit, _dot(qa[c], ka[:, n0:n0 + tq]), NEG)
            m = jnp.max(s_d, axis=-1, keepdims=True)
            if qi:
                s_f = _dot(qa[c], ka[:, 0:n0])
                m = jnp.maximum(m, jnp.max(s_f, axis=-1, keepdims=True))
            parts = [(jnp.exp2(s_d - m), n0, n0 + tq)] + ([(jnp.exp2(s_f - m), 0, n0)] if qi else [])
            acc, l = 0.0, 0.0
            for p, lo_k, hi_k in parts:
                pb = p.astype(BF16)
                if mode == "fox":
                    acc = acc + _dot_nt(pb, va[c, :, lo_k:hi_k])
                else:
                    acc = acc + _dot(pb, vb[lo_k:hi_k, :])
                    l = l + jnp.sum(p, axis=-1, keepdims=True)
            accs.append(acc)
            sums.append(l)
        if mode == "fox":
            o = _merge_pair(*accs)
        else:
            lv = lam_ref[...]
            lam = (jnp.exp(jnp.sum(lv[0:1] * lv[1:2], axis=-1, keepdims=True))
                   - jnp.exp(jnp.sum(lv[2:3] * lv[3:4], axis=-1, keepdims=True)) + lam_init)
            o = _rms(accs[0] / sums[0] - lam * (accs[1] / sums[1]), gsub_ref[...]) * (1.0 - lam_init)
        return o.astype(BF16)

    for qi in range(t // tq):
        o_ref[tq * qi:tq * (qi + 1), :] = q_tile(qi, q_ref[tq * qi:tq * (qi + 1), :])


def _pair_attn(q, kt, v, extra, *, mode, b, t, tq, lam_init=0.0, name):
    npair = D_MODEL // LANES
    qspec = pl.BlockSpec((t, LANES), lambda i, p: (i, p))
    ktspec = pl.BlockSpec((None, LANES, t), lambda i, p: (i, p, 0))
    if mode == "fox":
        vspec = ktspec
        especs = [pl.BlockSpec((None, None, 2, t), lambda i, p: (i, p, 0, 0))]
        scratch = [pltpu.VMEM((2 * LANES, t), BF16), pltpu.VMEM((2, LANES, t), BF16)]
    else:
        vspec = pl.BlockSpec((t * npair, LANES), lambda i, p: (i, 0))
        especs = [pl.BlockSpec(x.shape, lambda i, p: (0, 0)) for x in extra]
        scratch = [pltpu.VMEM((LANES, t), BF16), pltpu.VMEM((t, LANES), BF16)]
    return pl.pallas_call(
        functools.partial(_pair_attn_kernel, mode=mode, tq=tq, lam_init=lam_init),
        out_shape=jax.ShapeDtypeStruct((b * t, D_MODEL), BF16),
        grid=(b, npair),
        in_specs=[qspec, ktspec, vspec] + especs,
        out_specs=qspec,
        scratch_shapes=scratch,
        compiler_params=_cp(2), name=name)(q, kt, v, *extra)


def _rel_gather(tab, width, center):
    idx = jnp.clip(center - _lane_iota((REL_PAD, width)), -REL_CLIP, REL_CLIP) + REL_CLIP
    onehot = (_row_iota((REL_PAD, width)) == idx).astype(BF16)
    hi, mid, lo = _split3(tab)
    return (_dot(hi, onehot) + _dot(mid, onehot)) + _dot(lo, onehot)


def _band_attn_kernel(q_ref, kt_ref, vt_ref, tab_ref, o_ref, kb, va, bias_ref, *, tq, win, bw):
    gw = bw + tq

    @pl.when(pl.program_id(1) == 0)
    def _():
        g = _rel_gather(tab_ref[...], gw, BAND_LEFT + tq) * LOG2E
        ii = _row_iota((tq, bw)) >> 6
        jj = _lane_iota((tq, bw)) >> 6
        allowed = (jj >= ii) & (jj <= ii + BAND_LEFT_CHUNKS)
        for c in range(2):
            rows = jnp.broadcast_to(g[c:c + 1, :], (tq, gw))
            skew = pltpu.roll(rows, gw - tq, axis=1, stride=1, stride_axis=0)
            bias_ref[c] = jnp.where(allowed, skew[:, :bw], NEG)

    kb[...] = kt_ref[...].astype(BF16)
    vt = vt_ref[...]
    va[0] = _ones_row_values(vt, 0)
    va[1] = _ones_row_values(vt, 1)

    lo = _lane_iota((tq, LANES)) < 64
    for r in range(q_ref.shape[0] // tq):
        q0 = r * tq
        ws = max(q0 - BAND_LEFT, 0)
        d = BAND_LEFT - q0 + ws
        k = kb[:, pl.ds(ws, win)]
        q = q_ref[tq * r:tq * (r + 1), :]
        zero = jnp.zeros_like(q)
        accs = []
        for c in range(2):
            qc = jnp.where(lo, q, zero) if c == 0 else jnp.where(lo, zero, q)
            s = _dot(qc, k) + bias_ref[c, :, pl.ds(d, win)]
            p = jnp.exp2(s - jnp.max(s, axis=-1, keepdims=True))
            accs.append(_dot_nt(p.astype(BF16), va[c, :, pl.ds(ws, win)]))
        o_ref[tq * r:tq * (r + 1), :] = _merge_pair(*accs).astype(BF16)


def _band_attn(q, kt, vt, tab, *, b, t, tq, name):
    npair = D_MODEL // LANES
    win = BAND_LEFT + tq
    bw = win + BAND_LEFT
    assert t >= win and tq % CHUNK == 0 and BAND_LEFT % tq == 0
    qspec = pl.BlockSpec((t, LANES), lambda p, i: (i, p))
    kvspec = pl.BlockSpec((None, LANES, t), lambda p, i: (i, p, 0))
    return pl.pallas_call(
        functools.partial(_band_attn_kernel, tq=tq, win=win, bw=bw),
        out_shape=jax.ShapeDtypeStruct((b * t, D_MODEL), BF16),
        grid=(npair, b),
        in_specs=[qspec, kvspec, kvspec, pl.BlockSpec((None, SUBLANES, REL_PAD), lambda p, i: (p, 0, 0))],
        out_specs=qspec,
        scratch_shapes=[pltpu.VMEM((LANES, t), BF16), pltpu.VMEM((2, LANES, t), BF16), pltpu.VMEM((2, tq, bw), F32)],
        compiler_params=_cp(2), name=name)(q, kt, vt, tab)


def _block_diag_q(q, nh, width):
    tq = q.shape[0]
    rep = jnp.concatenate([q] * nh, axis=0)
    keep = (_row_iota(rep.shape) >> _log2(tq)) == (_lane_iota(rep.shape) >> _log2(width))
    return jnp.where(keep, rep, jnp.zeros_like(rep))


def _expand_rows(f, tq):
    return jnp.concatenate([jnp.broadcast_to(f[h:h + 1, :], (tq, f.shape[1])) for h in range(f.shape[0])], axis=0)


def _dec_attn_kernel(*refs, mode, n_cache, tq, past, lam_init):
    if mode == "fox":
        (q_ref, kc_ref, vc_ref, kn_ref, vn_ref, fkc_ref, fkn_ref, o_ref, qb, m_ref, l_ref, acc_ref) = refs
    else:
        (q_ref, kc_ref, vc_ref, kn_ref, vn_ref, lam_ref, gsub_ref, o_ref, qb, m_ref, l_ref, acc_ref) = refs
    t = pl.program_id(1)
    nh = D_MODEL // 64
    rows = nh * tq

    @pl.when(t == 0)
    def _():
        qb[...] = _block_diag_q(q_ref[...], nh, 64)
        m_ref[...] = jnp.full_like(m_ref, NEG)
        l_ref[...] = jnp.zeros_like(l_ref)
        acc_ref[...] = jnp.zeros_like(acc_ref)

    def cache_step():
        s = _dot(qb[...], kc_ref[...].astype(BF16))
        if mode == "fox":
            s = s - _expand_rows(fkc_ref[...] * LOG2E, tq)
            v = vc_ref[...].astype(BF16)
            _softmax_update(s, lambda pb: _dot_nt(pb, v), m_ref, l_ref, acc_ref)
        else:
            tk = kc_ref.shape[1]
            nhv = D_MODEL // LANES
            v = jnp.concatenate([vc_ref[pl.ds(h, tk, stride=nhv), :].astype(BF16) for h in range(nhv)], axis=-1)
            _softmax_update(s, lambda pb: _dot(pb, v), m_ref, l_ref, acc_ref)

    cache_step()

    @pl.when(t == n_cache - 1)
    def _():
        s = _dot_nt(qb[...], kn_ref[...].astype(BF16))
        qpos = past + (_row_iota((rows, tq)) & (tq - 1))
        kpos = past + _lane_iota((rows, tq))
        if mode == "fox":
            s = s - _expand_rows(fkn_ref[...][:, :tq] * LOG2E, tq)
            allowed = kpos <= qpos
        else:
            allowed = (kpos >> 6) <= (qpos >> 6)
        vn = vn_ref[...].astype(BF16)
        _softmax_update(jnp.where(allowed, s, NEG), lambda pb: _dot(pb, vn), m_ref, l_ref, acc_ref)
        o_all = acc_ref[...] / l_ref[...]
        if mode == "fox":
            o = jnp.zeros((tq, D_MODEL), F32)
            hl = _lane_iota((tq, D_MODEL)) >> 6
            for h in range(nh):
                o = jnp.where(hl == h, o_all[h * tq:(h + 1) * tq, :], o)
        else:
            lv = lam_ref[...]
            lam = (jnp.exp(jnp.sum(lv[0:1] * lv[1:2], axis=-1, keepdims=True))
                   - jnp.exp(jnp.sum(lv[2:3] * lv[3:4], axis=-1, keepdims=True)) + lam_init)
            pieces = []
            for h in range(nh // 2):
                a0 = o_all[(2 * h) * tq:(2 * h + 1) * tq, LANES * h:LANES * (h + 1)]
                a1 = o_all[(2 * h + 1) * tq:(2 * h + 2) * tq, LANES * h:LANES * (h + 1)]
                pieces.append(_rms(a0 - lam * a1, gsub_ref[...]) * (1.0 - lam_init))
            o = jnp.concatenate(pieces, axis=-1)
        o_ref[...] = o.astype(BF16)


def _dec_attn(q, kc, vc, kn, vn, extra, *, mode, b, tq, past, tk, lam_init=0.0, name):
    n_cache = past // tk
    nh = D_MODEL // 64
    new = pl.BlockSpec((tq, D_MODEL), lambda i, t: (i, 0))
    cache_t = pl.BlockSpec((None, D_MODEL, tk), lambda i, t: (i, 0, t))
    if mode == "fox":
        (fk,) = extra
        vspec = cache_t
        especs = [pl.BlockSpec((None, nh, tk), lambda i, t: (i, 0, t)),
                  pl.BlockSpec((None, nh, LANES), lambda i, t: (i, 0, past // LANES))]
        ins = [fk, fk]
    else:
        nhv = D_MODEL // LANES
        vspec = pl.BlockSpec((tk * nhv, LANES), lambda i, t: (i * n_cache + t, 0))
        especs = [pl.BlockSpec(x.shape, lambda i, t: (0, 0)) for x in extra]
        ins = list(extra)
    return pl.pallas_call(
        functools.partial(_dec_attn_kernel, mode=mode, n_cache=n_cache, tq=tq, past=past, lam_init=lam_init),
        out_shape=jax.ShapeDtypeStruct((b * tq, D_MODEL), BF16),
        grid=(b, n_cache),
        in_specs=[new, cache_t, vspec, new, new] + especs,
        out_specs=new,
        scratch_shapes=[pltpu.VMEM((nh * tq, D_MODEL), BF16), pltpu.VMEM((nh * tq, 1), F32),
                        pltpu.VMEM((nh * tq, 1), F32), pltpu.VMEM((nh * tq, D_MODEL), F32)],
        compiler_params=_cp(2), name=name)(q, kc, vc, kn, vn, *ins)


def _mla_dec_kernel(q_ref, cc_ref, rc_ref, cn_ref, rn_ref, wuk_ref, wuv_ref, gkn_ref, o_ref,
                    qn, qr, m_ref, l_ref, acc_ref, *, n_cache, tq, past):
    t = pl.program_id(1)
    nh = MLA_HEADS
    rows = nh * tq

    @pl.when(t == 0)
    def _():
        q = q_ref[...]
        zero = jnp.zeros((tq, LANES), BF16)
        for h in range(nh):
            qn[h * tq:(h + 1) * tq, :] = jnp.concatenate(
                [q[:, 2 * LANES * h:2 * LANES * h + LANES] if c == h else zero for c in range(nh)], axis=-1)
            qr[h * tq:(h + 1) * tq, :] = q[:, 2 * LANES * h + LANES:2 * LANES * (h + 1)]
        m_ref[...] = jnp.full_like(m_ref, NEG)
        l_ref[...] = jnp.zeros_like(l_ref)
        acc_ref[...] = jnp.zeros_like(acc_ref)

    def step(ckv, s_rope, mask_new):
        c = ckv.astype(BF16)
        kn = _head_rms(_dot(c, wuk_ref[...]), gkn_ref[...], MLA_NOPE).astype(BF16)
        s = _dot_nt(qn[...], kn) + s_rope
        if mask_new:
            n = ckv.shape[0]
            qpos = past + (_row_iota((rows, n)) & (tq - 1))
            kpos = past + _lane_iota((rows, n))
            s = jnp.where((kpos >> 6) <= (qpos >> 6), s, NEG)
        _softmax_update(s, lambda pb: _dot(pb, c), m_ref, l_ref, acc_ref)

    def cache_step():
        kr = rc_ref[...].astype(BF16)
        step(cc_ref[...], _dot(qr[...], jnp.concatenate([kr, kr], axis=0)), False)

    cache_step()

    @pl.when(t == n_cache - 1)
    def _():
        step(cn_ref[...], _dot_nt(qr[...], rn_ref[...].astype(BF16)), True)
        lat = (acc_ref[...] / l_ref[...]).astype(BF16)
        o_ref[...] = jnp.concatenate(
            [_dot(lat[h * tq:(h + 1) * tq, :], wuv_ref[:, MLA_V * h:MLA_V * (h + 1)]) for h in range(nh)],
            axis=-1).astype(BF16)


def _mla_dec(q, cc, rc, cn, rn, wuk, wuv, gkn, *, b, tq, past, tk, name):
    n_cache = past // tk
    nh = MLA_HEADS
    new = lambda n: pl.BlockSpec((tq, n), lambda i, t: (i, 0))
    full = lambda a: pl.BlockSpec(a.shape, lambda i, t: (0,) * a.ndim)
    return pl.pallas_call(
        functools.partial(_mla_dec_kernel, n_cache=n_cache, tq=tq, past=past),
        out_shape=jax.ShapeDtypeStruct((b * tq, nh * MLA_V), BF16),
        grid=(b, n_cache),
        in_specs=[new(nh * 2 * LANES),
                  pl.BlockSpec((tk, MLA_KV_LORA), lambda i, t: (i * n_cache + t, 0)),
                  pl.BlockSpec((None, MLA_ROPE, tk), lambda i, t: (i, 0, t)),
                  new(MLA_KV_LORA), new(LANES), full(wuk), full(wuv), full(gkn)],
        out_specs=new(nh * MLA_V),
        scratch_shapes=[pltpu.VMEM((nh * tq, nh * MLA_NOPE), BF16), pltpu.VMEM((nh * tq, LANES), BF16),
                        pltpu.VMEM((nh * tq, 1), F32), pltpu.VMEM((nh * tq, 1), F32),
                        pltpu.VMEM((nh * tq, MLA_KV_LORA), F32)],
        compiler_params=_cp(2), name=name)(q, cc, rc, cn, rn, wuk, wuv, gkn)


def _band_dec_kernel(q_ref, kc_ref, vc_ref, kn_ref, vn_ref, knt_ref, vnt_ref, tab_ref, o_ref, kr_out, vr_out,
                     *, tq, w, past):
    keep_old = _lane_iota((D_MODEL, LANES)) < LANES - tq
    for c_ref, nt_ref, r_out in ((kc_ref, knt_ref, kr_out), (vc_ref, vnt_ref, vr_out)):
        rolled = pltpu.roll(c_ref[...], w - tq, axis=1)
        r_out[:, :w - LANES] = rolled[:, :w - LANES]
        r_out[:, w - LANES:] = jnp.where(keep_old, rolled[:, w - LANES:], nt_ref[...])
    nh = BAND_HEADS
    rows = nh * tq
    bwid = ((w + tq + LANES - 1) // LANES) * LANES
    qb = _block_diag_q(q_ref[...], nh, BAND_DH)
    g = _rel_gather(tab_ref[...], bwid, w + tq) * LOG2E
    bias = jnp.concatenate(
        [pltpu.roll(jnp.broadcast_to(g[h:h + 1, :], (tq, bwid)), bwid - tq, axis=1, stride=1, stride_axis=0)
         for h in range(nh)], axis=0)
    qc = (past + (_row_iota((rows, bwid)) & (tq - 1))) >> 6
    kc = (past - w + _lane_iota((rows, bwid))) >> 6
    bias = jnp.where((kc <= qc) & (kc >= qc - BAND_LEFT_CHUNKS), bias, NEG)
    s_c = _dot(qb, kc_ref[...].astype(BF16)) + bias[:, :w]
    s_n = _dot_nt(qb, kn_ref[...].astype(BF16)) + bias[:, w:w + tq]
    m = jnp.maximum(jnp.max(s_c, axis=-1, keepdims=True), jnp.max(s_n, axis=-1, keepdims=True))
    p_c = jnp.exp2(s_c - m)
    p_n = jnp.exp2(s_n - m)
    l = jnp.sum(p_c, axis=-1, keepdims=True) + jnp.sum(p_n, axis=-1, keepdims=True)
    o_all = (_dot_nt(p_c.astype(BF16), vc_ref[...].astype(BF16))
             + _dot(p_n.astype(BF16), vn_ref[...].astype(BF16))) / l
    o = jnp.zeros((tq, D_MODEL), F32)
    hl = _lane_iota((tq, D_MODEL)) >> _log2(BAND_DH)
    for h in range(nh):
        o = jnp.where(hl == h, o_all[h * tq:(h + 1) * tq, :], o)
    o_ref[...] = o.astype(BF16)


def _band_dec(q, kc, vc, kn, vn, tab, *, b, tq, w, past, name):
    assert w % LANES == 0 and tq <= LANES
    new = pl.BlockSpec((tq, D_MODEL), lambda i: (i, 0))
    cache = pl.BlockSpec((None, D_MODEL, w), lambda i: (i, 0, 0))
    slab = pl.BlockSpec((None, D_MODEL, LANES), lambda i: (i, 0, 0))
    right = lambda x: jnp.pad(jnp.swapaxes(x.reshape(b, tq, D_MODEL), 1, 2), ((0, 0), (0, 0), (LANES - tq, 0)))
    return pl.pallas_call(
        functools.partial(_band_dec_kernel, tq=tq, w=w, past=past),
        out_shape=[jax.ShapeDtypeStruct((b * tq, D_MODEL), BF16), jax.ShapeDtypeStruct(kc.shape, F32),
                   jax.ShapeDtypeStruct(vc.shape, F32)],
        grid=(b,),
        in_specs=[new, cache, cache, new, new, slab, slab, pl.BlockSpec(tab.shape, lambda i: (0, 0))],
        out_specs=[new, cache, cache],
        compiler_params=_cp(1), name=name)(q, kc, vc, kn, vn, right(kn), right(vn), tab)


def _ffn_rows_kernel(*refs, mode, tps, seq, final, tf):
    it = iter(refs)
    h_ref, o_ref, wo_ref, gf_ref, wup_ref, cw_ref, cb_ref, wd_ref = [next(it) for _ in range(8)]
    if mode == "state":
        s1_ref, s2_ref = next(it), next(it)
    if final:
        gfin_ref = next(it)
    out_ref, ug_out, uh_out = next(it), next(it), next(it)
    if mode == "carry":
        carry_ref = next(it)
    tm = h_ref.shape[0]
    h1 = h_ref[...] + _dot(o_ref[...], wo_ref[...])
    xn = _rms(h1, gf_ref[...]).astype(BF16)
    if mode == "carry":
        @pl.when(pl.program_id(0) % tps == 0)
        def _():
            carry_ref[...] = jnp.zeros_like(carry_ref)
    else:
        tpos = _row_iota((tm, tf)) & (seq - 1)
    acts = []
    for j in range(D_FF // tf):
        cs = []
        for part, u_out in enumerate((ug_out, uh_out)):
            tile = slice(tf * j, tf * (j + 1))
            cols = slice(part * D_FF + tf * j, part * D_FF + tf * (j + 1))
            u = _dot(xn, wup_ref[:, cols])
            if mode == "carry":
                tail = u[tm - SUBLANES:]
                u_out[:, tile] = tail
                ext = jnp.concatenate([carry_ref[part, :, tile], u], axis=0)
                carry_ref[part, :, tile] = tail
                um1 = pltpu.roll(ext, 1, axis=0)[SUBLANES:]
                um2 = pltpu.roll(ext, 2, axis=0)[SUBLANES:]
            else:
                u_out[:, tile] = u
                um1 = jnp.where(tpos >= 1, pltpu.roll(u, 1, axis=0), s1_ref[:, cols])
                um2 = jnp.where(tpos >= 2, pltpu.roll(u, 2, axis=0), s2_ref[:, cols])
            cw = cw_ref[:, cols]
            cs.append(((cb_ref[:, cols] + u * cw[2:3]) + um2 * cw[0:1]) + um1 * cw[1:2])
        acts.append((cs[0] * jax.nn.sigmoid(cs[0]) * cs[1]).astype(BF16))
    acc = h1 + _dot(jnp.concatenate(acts, axis=-1), wd_ref[...])
    if final:
        acc = _rms(acc, gfin_ref[...])
    out_ref[...] = acc


def _ffn_rows(h, o, wo, gf, wup, cw, cb, wd, *, tm, tf, seq, state=None, final_g=None, name):
    m = h.shape[0]
    mode = "carry" if state is None else "state"
    tps = seq // tm if mode == "carry" else 1
    row = lambda n: pl.BlockSpec((tm, n), lambda i: (i, 0))
    once = lambda a: pl.BlockSpec(a.shape, lambda i: (0,) * a.ndim, pipeline_mode=pl.Buffered(1))
    ins = [h, o, wo, gf, wup, cw, cb, wd]
    specs = [row(D_MODEL), row(D_MODEL)] + [once(x) for x in ins[2:]]
    scratch = []
    if mode == "state":
        assert m == tm and seq & (seq - 1) == 0
        ins += list(state)
        specs += [row(2 * D_FF), row(2 * D_FF)]
        u_shape = jax.ShapeDtypeStruct((m, D_FF), F32)
        u_spec = row(D_FF)
    else:
        assert seq % tm == 0
        scratch.append(pltpu.VMEM((2, SUBLANES, D_FF), F32))
        u_shape = jax.ShapeDtypeStruct((m // tm, SUBLANES, D_FF), F32)
        u_spec = pl.BlockSpec((None, SUBLANES, D_FF), lambda i: (i, 0, 0))
    if final_g is not None:
        ins.append(final_g)
        specs.append(once(final_g))
    kern = functools.partial(_ffn_rows_kernel, mode=mode, tps=tps, seq=seq, final=final_g is not None, tf=tf)
    return pl.pallas_call(
        kern, out_shape=[jax.ShapeDtypeStruct((m, D_MODEL), F32), u_shape, u_shape],
        grid=(m // tm,), in_specs=specs, out_specs=[row(D_MODEL), u_spec, u_spec], scratch_shapes=scratch,
        compiler_params=_cp(1), name=name)(*ins)


def _rope_tables(pos):
    inv = 1.0 / (ROPE_THETA ** (jnp.arange(0, 64, 2, dtype=F32) / 64))
    ang = pos.astype(F32)[:, None] * inv[None, :]
    c, s = jnp.cos(ang), jnp.sin(ang)
    return jnp.tile(c, (1, 4)), jnp.tile(jnp.concatenate([-s, s], axis=1), (1, 2)), c.T, s.T


def _tile_gain(g, n):
    return jnp.tile(g.astype(F32), n // g.shape[0]).reshape(1, n)


def _gain_t(g, n, w):
    return jnp.broadcast_to(jnp.tile(g.astype(F32), n // g.shape[0])[:, None], (n, w))


def _to_tokens(xt, heads):
    b, n, t = xt.shape
    dh = n // math.prod(heads)
    nd = len(heads)
    return xt.reshape(b, *heads, dh, t).transpose(0, nd + 2, *range(1, nd + 2))


def _to_features(x):
    b, t = x.shape[:2]
    nd = x.ndim
    return x.transpose(0, *range(2, nd), 1).reshape(b, -1, t)


def kernel(x_prompt, x_sample, cache_mla_ckv, cache_mla_krope, cache_fox_k, cache_fox_v, cache_fox_logf,
           cache_diff_k, cache_diff_v, cache_band_k, cache_band_v, state_ffn_conv,
           attn_norm_g, ffn_norm_g, final_norm_g,
           mla_w_dq, mla_g_q, mla_w_uq, mla_w_dkv, mla_g_kv, mla_w_uk, mla_w_uv,
           mla_g_qn, mla_g_qr, mla_g_kn, mla_g_kr, mla_w_o,
           fox_w_qkv, fox_w_f, fox_b_f, fox_g_q, fox_g_k, fox_w_o,
           diff_w_qkv, diff_g_q, diff_g_k, diff_lq1, diff_lk1, diff_lq2, diff_lk2, diff_g_sub, diff_w_o,
           band_w_qkv, band_g_q, band_g_k, band_rel_bias, band_w_o,
           ffn_w_up, ffn_conv_w, ffn_conv_b, ffn_w_down):
    bp, tp, d = x_prompt.shape
    bs, ts, _ = x_sample.shape
    past = cache_mla_ckv.shape[1]
    depth = attn_norm_g.shape[0]
    mp, ms = bp * tp, bs * ts
    tm_p = min(512, tp)
    tm_f = min(512, tp)
    tq = 512
    tq_band = 256
    tkd = min(1024, past)
    tkw = min(2048, past)
    tf = 256
    assert tp % tq == 0 and past % tkd == 0 and past % tkw == 0 and past % CHUNK == 0

    tobf = lambda a: a.astype(BF16)
    rowv = lambda g: g.astype(F32).reshape(1, -1)
    pos_p = jnp.arange(tp, dtype=jnp.int32)
    pos_s = past + jnp.arange(ts, dtype=jnp.int32)
    tabs_p = _rope_tables(pos_p)
    tabs_s = tuple(jnp.tile(a, (bs, 1)) for a in _rope_tables(pos_s)[:2])

    nope_cols = jnp.arange(MLA_HEADS)[:, None] * (MLA_NOPE + MLA_ROPE) + jnp.arange(MLA_NOPE)[None, :]
    rope_cols = jnp.arange(MLA_HEADS)[:, None] * (MLA_NOPE + MLA_ROPE) + MLA_NOPE + jnp.arange(MLA_ROPE)[None, :]
    wuq_perm = jnp.concatenate([mla_w_uq[:, nope_cols.reshape(-1)], mla_w_uq[:, rope_cols.reshape(-1)]], axis=1)
    wr = mla_w_dkv[:, MLA_KV_LORA:]
    mla_w = dict(
        wdq=tobf(mla_w_dq), gq=rowv(mla_g_q), wuq=tobf(wuq_perm), wc=tobf(mla_w_dkv[:, :MLA_KV_LORA]),
        wr=tobf(jnp.concatenate([wr, wr], axis=1)), gkv=rowv(mla_g_kv),
        gqn=_tile_gain(mla_g_qn, MLA_HEADS * MLA_NOPE), gqr=_tile_gain(mla_g_qr, MLA_HEADS * MLA_ROPE),
        gkr=_tile_gain(mla_g_kr, LANES), wuk=tobf(mla_w_uk), wuv=tobf(mla_w_uv),
        gkn=_tile_gain(mla_g_kn, MLA_HEADS * MLA_NOPE), wrt=tobf(wr.T), gkrt=_gain_t(mla_g_kr, MLA_ROPE, tm_p))

    def split_qkv(w):
        wq, wk, wv = w[:, :d], w[:, d:2 * d], w[:, 2 * d:]
        return tobf(wq), tobf(wk.T), tobf(wv), tobf(wv.T)

    fox_wf = tobf(jnp.pad(fox_w_f, ((0, 0), (0, LANES - FOX_HEADS))))
    fox_bf = jnp.pad(fox_b_f.astype(F32), (0, LANES - FOX_HEADS)).reshape(1, LANES)
    fox_wft = tobf(fox_w_f.T)
    fox_bft = jnp.broadcast_to(fox_b_f.astype(F32)[:, None], (FOX_HEADS, tm_p))
    lamv = jnp.stack([diff_lq1, diff_lk1, diff_lq2, diff_lk2]).astype(F32)
    gsub = rowv(diff_g_sub)
    tab_pad = jnp.pad(band_rel_bias.astype(F32), ((0, 0), (0, REL_PAD - band_rel_bias.shape[1])))
    wo = [tobf(mla_w_o), tobf(fox_w_o), tobf(diff_w_o), tobf(band_w_o)]
    wup, wdn = tobf(ffn_w_up), tobf(ffn_w_down)
    cwf, cbf = ffn_conv_w.astype(F32), ffn_conv_b.astype(F32)

    h_p = x_prompt.reshape(mp, d)
    h_s = x_sample.reshape(ms, d)
    outs = {}
    conv_p, conv_s = [], []
    for i in range(depth):
        kind = i % 4
        ga = rowv(attn_norm_g[i])
        if kind == 0:
            scale = (MLA_NOPE + MLA_ROPE) ** -0.5 * LOG2E
            q_p, ckv_p, _, _, kc_p, v_p, krt_p = _mla_proj(h_p, ga, mla_w, tabs_p, b=bp, t=tp, tm=tm_p, with_kv=True,
                                                           scale=scale, name="mla_proj_p")
            o_p = _mla_attn(q_p, kc_p, v_p, b=bp, t=tp, tq=tq, hps=2, name="mla_attn_p")
            q_s, ckv_s, kr_s, krd_s = _mla_proj(h_s, ga, mla_w, tabs_s, b=bs, t=ts, tm=ms, with_kv=False,
                                                scale=scale, name="mla_proj_s")
            o_s = _mla_dec(q_s, cache_mla_ckv.astype(F32).reshape(bs * past, MLA_KV_LORA),
                           _to_features(cache_mla_krope.astype(F32)), ckv_s, krd_s,
                           mla_w["wuk"], mla_w["wuv"], mla_w["gkn"], b=bs, tq=ts, past=past, tk=tkd,
                           name="mla_attn_s")
            outs["mla"] = (ckv_p.reshape(bp, tp, MLA_KV_LORA), jnp.swapaxes(krt_p, 1, 2),
                           ckv_s.reshape(bs, ts, MLA_KV_LORA), kr_s.reshape(bs, ts, MLA_ROPE))
        elif kind == 1:
            scale = FOX_DH ** -0.5 * LOG2E
            gq, gk = _tile_gain(fox_g_q, d), _tile_gain(fox_g_k, d)
            wq, wkt, wv, wvt = split_qkv(fox_w_qkv)
            q_p, kt_p, vt_p, lft_p = _qkv_proj_t(h_p, ga, wq, wkt, wvt, gq, _gain_t(fox_g_k, d, tm_p), b=bp, t=tp,
                                                 tm=tm_p, wft=fox_wft, bft=fox_bft, scale=scale, name="fox_proj_p")
            fk_p = _cumsum_last(lft_p, tk=tq, name="fox_cumsum_p")
            o_p = _pair_attn(q_p, kt_p, vt_p, (fk_p.reshape(bp, FOX_HEADS // 2, 2, tp),), mode="fox",
                             b=bp, t=tp, tq=tq, name="fox_attn_p")
            q_s, k_s, v_s, lf_s = _qkv_proj(h_s, ga, tobf(fox_w_qkv), gq, gk, tm=ms, wf=fox_wf, bf=fox_bf,
                                            scale=scale, name="fox_proj_s")
            lf_all = jnp.concatenate([jnp.swapaxes(cache_fox_logf.astype(F32), 1, 2),
                                      jnp.swapaxes(lf_s.reshape(bs, ts, FOX_HEADS), 1, 2)], axis=2)
            lf_all = jnp.pad(lf_all, ((0, 0), (0, 0), (0, tkw - ts)))
            f_all = _cumsum_last(lf_all, tk=tq, name="fox_cumsum_s")
            o_s = _dec_attn(q_s, _to_features(cache_fox_k.astype(F32)), _to_features(cache_fox_v.astype(F32)),
                            k_s, v_s, (f_all,), mode="fox", b=bs, tq=ts, past=past, tk=tkw, name="fox_attn_s")
            sh = (FOX_HEADS, FOX_DH)
            outs["fox"] = (_to_tokens(kt_p, (FOX_HEADS,)), _to_tokens(vt_p, (FOX_HEADS,)),
                           jnp.swapaxes(lft_p, 1, 2),
                           k_s.reshape(bs, ts, *sh), v_s.reshape(bs, ts, *sh), lf_s.reshape(bs, ts, FOX_HEADS))
        elif kind == 2:
            scale = DIFF_DH ** -0.5 * LOG2E
            lam_init = 0.8 - 0.6 * math.exp(-0.3 * i)
            gq, gk = _tile_gain(diff_g_q, d), _tile_gain(diff_g_k, d)
            wq, wkt, wv, wvt = split_qkv(diff_w_qkv)
            q_p, kt_p, v4_p = _qkv_proj_t(h_p, ga, wq, wkt, wv, gq, _gain_t(diff_g_k, d, tm_p), b=bp, t=tp, tm=tm_p,
                                          rope_tabs=tabs_p, v_tokens=True, scale=scale, name="diff_proj_p")
            v_p = v4_p.reshape(bp, tp, DIFF_HEADS, 2 * DIFF_DH)
            o_p = _pair_attn(q_p, kt_p, v4_p, (lamv, gsub), mode="diff", b=bp, t=tp, tq=tq, lam_init=lam_init,
                             name="diff_attn_p")
            q_s, k_s, v_s = _qkv_proj(h_s, ga, tobf(diff_w_qkv), gq, gk, tm=ms, rope_tabs=tabs_s, scale=scale,
                                      name="diff_proj_s")
            o_s = _dec_attn(q_s, _to_features(cache_diff_k.astype(F32)),
                            cache_diff_v.astype(F32).reshape(bs * past * DIFF_HEADS, 2 * DIFF_DH), k_s, v_s,
                            (lamv, gsub), mode="diff", b=bs, tq=ts, past=past, tk=tkw, lam_init=lam_init,
                            name="diff_attn_s")
            outs["diff"] = (_to_tokens(kt_p, (DIFF_HEADS, 2)), v_p,
                            k_s.reshape(bs, ts, DIFF_HEADS, 2, DIFF_DH), v_s.reshape(bs, ts, DIFF_HEADS, 2 * DIFF_DH))
        else:
            scale = BAND_DH ** -0.5 * LOG2E
            gq, gk = _tile_gain(band_g_q, d), _tile_gain(band_g_k, d)
            wq, wkt, wv, wvt = split_qkv(band_w_qkv)
            keep = min(BAND_LEFT, tp)
            assert keep == tm_p
            q_p, kt_p, vt_p, kt_keep, vt_keep = _qkv_proj_t(
                h_p, ga, wq, wkt, wvt, gq, _gain_t(band_g_k, d, tm_p), b=bp, t=tp, tm=tm_p, tail=True, scale=scale,
                name="band_proj_p")
            tab_pairs = jnp.pad(tab_pad.reshape(BAND_HEADS // 2, 2, REL_PAD), ((0, 0), (0, SUBLANES - 2), (0, 0)))
            o_p = _band_attn(q_p, kt_p, vt_p, tab_pairs, b=bp, t=tp, tq=tq_band, name="band_attn_p")
            q_s, k_s, v_s = _qkv_proj(h_s, ga, tobf(band_w_qkv), gq, gk, tm=ms, scale=scale, name="band_proj_s")
            w = cache_band_k.shape[1]
            kct = _to_features(cache_band_k.astype(F32))
            vct = _to_features(cache_band_v.astype(F32))
            o_s, k_roll, v_roll = _band_dec(q_s, kct, vct, k_s, v_s, tab_pad, b=bs, tq=ts, w=w, past=past,
                                            name="band_attn_s")
            outs["band"] = (_to_tokens(kt_keep, (BAND_HEADS,)), _to_tokens(vt_keep, (BAND_HEADS,)),
                            _to_tokens(k_roll, (BAND_HEADS,)), _to_tokens(v_roll, (BAND_HEADS,)))

        last = i == depth - 1
        gfin = rowv(final_norm_g) if last else None
        gfn = rowv(ffn_norm_g[i])
        h_p, ug, uh = _ffn_rows(h_p, o_p, wo[kind], gfn, wup[i], cwf[i], cbf[i].reshape(1, -1), wdn[i], tm=tm_f,
                                tf=tf, seq=tp, final_g=gfin, name="ffn_p")
        tps = tp // tm_f
        conv_p.append(jnp.concatenate([ug, uh], axis=-1)[tps - 1::tps, SUBLANES - (CONV_W - 1):])
        st = state_ffn_conv[i].astype(F32)
        zeros = jnp.zeros((bs, ts, 2 * D_FF), F32)
        s1 = zeros.at[:, 0].set(st[:, 1]).reshape(ms, 2 * D_FF)
        s2 = zeros.at[:, 0].set(st[:, 0]).at[:, 1].set(st[:, 1]).reshape(ms, 2 * D_FF)
        h_s, ug, uh = _ffn_rows(h_s, o_s, wo[kind], gfn, wup[i], cwf[i], cbf[i].reshape(1, -1), wdn[i], tm=ms,
                                tf=tf, seq=ts, state=(s1, s2), final_g=gfin, name="ffn_s")
        u_s = jnp.concatenate([ug, uh], axis=-1).reshape(bs, ts, 2 * D_FF)
        conv_s.append(jnp.concatenate([st, u_s], axis=1)[:, ts:])

    y_prompt = h_p.reshape(bp, tp, d)
    y_sample = h_s.reshape(bs, ts, d)
    return (y_prompt, y_sample) + outs["mla"] + outs["fox"] + outs["diff"] + outs["band"] + (
        jnp.stack(conv_p, axis=0), jnp.stack(conv_s, axis=0))
```

```python
import functools
import math

import jax
import jax.numpy as jnp
from jax import lax
from jax.experimental import pallas as pl
from jax.experimental.pallas import tpu as pltpu

F32 = jnp.float32
BF16 = jnp.bfloat16

D_MODEL = 1024
CHUNK = 64
CHUNK_SHIFT = CHUNK.bit_length() - 1
ROPE_DIM = 64
ROPE_THETA = 10000.0
NORM_EPS = 1e-6
MLA_HEADS, MLA_Q_LORA, MLA_KV_LORA, MLA_NOPE, MLA_ROPE, MLA_V = 8, 384, 256, 128, 64, 128
FOX_HEADS, FOX_DH = 16, 64
DIFF_HEADS, DIFF_DH = 8, 64
BAND_HEADS, BAND_DH = 16, 64
BAND_LEFT_CHUNKS = 8
BAND_LEFT = BAND_LEFT_CHUNKS * CHUNK
REL_CLIP = 128
D_FF = 2816
CONV_W = 3

LANES = 128
SUBLANES = 8
NEG = -1e30
REL_PAD = 384
LOG2E = math.log2(math.e)
VMEM_LIMIT = 56 * 1024 * 1024


def _cp(n_axes):
    return pltpu.CompilerParams(dimension_semantics=("arbitrary",) * n_axes, vmem_limit_bytes=VMEM_LIMIT)


def _dot(a, b):
    return jnp.dot(a, b, preferred_element_type=F32)


def _dot_nt(a, b):
    return lax.dot_general(a, b, (((1,), (1,)), ((), ())), preferred_element_type=F32)


def _rms(x, g):
    ms = jnp.mean(x * x, axis=-1, keepdims=True)
    return x * lax.rsqrt(ms + NORM_EPS) * g


def _lane_iota(shape):
    return lax.broadcasted_iota(jnp.int32, shape, len(shape) - 1)


def _row_iota(shape):
    return lax.broadcasted_iota(jnp.int32, shape, len(shape) - 2)


def _log2(n):
    assert n & (n - 1) == 0, n
    return n.bit_length() - 1


def _head_rms(x, g, seg):
    n = x.shape[1]
    outs = []
    for c in range(n // LANES):
        xs = x[:, LANES * c:LANES * (c + 1)]
        sq = xs * xs
        if seg == LANES:
            r = lax.rsqrt(jnp.sum(sq, axis=-1, keepdims=True) * (1.0 / seg) + NORM_EPS)
        else:
            lo = _lane_iota(xs.shape) < seg
            s_lo = jnp.sum(jnp.where(lo, sq, 0.0), axis=-1, keepdims=True)
            s_hi = jnp.sum(jnp.where(lo, 0.0, sq), axis=-1, keepdims=True)
            r = jnp.where(lo, lax.rsqrt(s_lo * (1.0 / seg) + NORM_EPS), lax.rsqrt(s_hi * (1.0 / seg) + NORM_EPS))
        outs.append(xs * r * g[:, LANES * c:LANES * (c + 1)])
    return jnp.concatenate(outs, axis=-1) if len(outs) > 1 else outs[0]


def _head_rms_t(x, g, seg=64):
    outs = []
    for hd in range(x.shape[0] // seg):
        xs = x[seg * hd:seg * (hd + 1)]
        r = lax.rsqrt(jnp.sum(xs * xs, axis=0, keepdims=True) * (1.0 / seg) + NORM_EPS)
        outs.append(xs * r * g[seg * hd:seg * (hd + 1)])
    return jnp.concatenate(outs, axis=0) if len(outs) > 1 else outs[0]


def _rope(x, cos, sinp):
    half = ROPE_DIM // 2
    first_half = (_lane_iota((x.shape[0], LANES)) & (ROPE_DIM - 1)) < half
    outs = []
    for c in range(x.shape[1] // LANES):
        xs = x[:, LANES * c:LANES * (c + 1)]
        fwd = pltpu.roll(xs, half, axis=1)
        bwd = pltpu.roll(xs, LANES - half, axis=1)
        outs.append(xs * cos + jnp.where(first_half, bwd, fwd) * sinp)
    return jnp.concatenate(outs, axis=-1) if len(outs) > 1 else outs[0]


def _rope_t(x, cos_t, sin_t):
    outs = []
    for hd in range(x.shape[0] // 64):
        x1 = x[64 * hd:64 * hd + 32]
        x2 = x[64 * hd + 32:64 * (hd + 1)]
        outs += [x1 * cos_t - x2 * sin_t, x2 * cos_t + x1 * sin_t]
    return jnp.concatenate(outs, axis=0)


def _split3(x):
    hi = x.astype(BF16)
    r = x - hi.astype(F32)
    mid = r.astype(BF16)
    lo = (r - mid.astype(F32)).astype(BF16)
    return hi, mid, lo


def _softmax_step(s, pv, m, l, acc):
    m_new = jnp.maximum(m, jnp.max(s, axis=-1, keepdims=True))
    alpha = jnp.exp2(m - m_new)
    p = jnp.exp2(s - m_new)
    if l is not None:
        l = alpha * l + jnp.sum(p, axis=-1, keepdims=True)
    return m_new, l, alpha * acc + pv(p.astype(BF16))


def _softmax_update(s, pv, m_ref, l_ref, acc_ref):
    m, l, acc = _softmax_step(s, pv, m_ref[...], l_ref[...], acc_ref[...])
    m_ref[...] = m
    l_ref[...] = l
    acc_ref[...] = acc


def _log_sigmoid(z):
    return jnp.minimum(z, 0.0) - jnp.log1p(jnp.exp(-jnp.abs(z)))


def _qkv_proj_kernel(*refs, rope, logf, scale):
    it = iter(refs)
    h_ref, g_ref, w_ref, gq_ref, gk_ref = next(it), next(it), next(it), next(it), next(it)
    cos_ref = sin_ref = wf_ref = bf_ref = None
    if rope:
        cos_ref, sin_ref = next(it), next(it)
    if logf:
        wf_ref, bf_ref = next(it), next(it)
    q_out, k_out, v_out = next(it), next(it), next(it)
    a = _rms(h_ref[...], g_ref[...]).astype(BF16)
    qkv = _dot(a, w_ref[...])
    q = _head_rms(qkv[:, :D_MODEL], gq_ref[...], 64)
    k = _head_rms(qkv[:, D_MODEL:2 * D_MODEL], gk_ref[...], 64)
    if rope:
        q = _rope(q, cos_ref[...], sin_ref[...])
        k = _rope(k, cos_ref[...], sin_ref[...])
    q_out[...] = (q * scale).astype(BF16)
    k_out[...] = k
    v_out[...] = qkv[:, 2 * D_MODEL:]
    if logf:
        logf_out = next(it)
        lf = _log_sigmoid(_dot(a, wf_ref[...]) + bf_ref[...])
        logf_out[...] = lf[:, :FOX_HEADS]


def _qkv_proj(h, g, w, gq, gk, *, tm, rope_tabs=None, wf=None, bf=None, scale, name):
    m = h.shape[0]
    grid = (m // tm,)
    row = lambda n: pl.BlockSpec((tm, n), lambda i: (i, 0))
    full = lambda a: pl.BlockSpec(a.shape, lambda i: (0,) * a.ndim)
    ins = [h, g, w, gq, gk]
    specs = [row(D_MODEL), full(g), full(w), full(gq), full(gk)]
    if rope_tabs is not None:
        cos, sinp = rope_tabs
        nt = cos.shape[0] // tm
        tab = pl.BlockSpec((tm, LANES), lambda i: (i % nt, 0))
        ins += [cos, sinp]
        specs += [tab, tab]
    if wf is not None:
        ins += [wf, bf]
        specs += [full(wf), full(bf)]
    outs = [jax.ShapeDtypeStruct((m, D_MODEL), BF16), jax.ShapeDtypeStruct((m, D_MODEL), F32),
            jax.ShapeDtypeStruct((m, D_MODEL), F32)]
    ospecs = [row(D_MODEL), row(D_MODEL), row(D_MODEL)]
    if wf is not None:
        outs.append(jax.ShapeDtypeStruct((m, FOX_HEADS), F32))
        ospecs.append(row(FOX_HEADS))
    kern = functools.partial(_qkv_proj_kernel, rope=rope_tabs is not None, logf=wf is not None, scale=scale)
    return pl.pallas_call(kern, out_shape=outs, grid=grid, in_specs=specs, out_specs=ospecs,
                          compiler_params=_cp(1), name=name)(*ins)


def _qkv_proj_t_kernel(*refs, rope, logf, v_tokens, tail, scale):
    it = iter(refs)
    h_ref, g_ref, wq_ref, wkt_ref, wv_ref, gq_ref, gkt_ref = [next(it) for _ in range(7)]
    if rope:
        cos_ref, sin_ref, cost_ref, sint_ref = [next(it) for _ in range(4)]
    if logf:
        wft_ref, bft_ref = next(it), next(it)
    q_out, kt_out, v_out = next(it), next(it), next(it)
    a = _rms(h_ref[...], g_ref[...]).astype(BF16)
    tm = a.shape[0]
    q = _head_rms(_dot(a, wq_ref[...]), gq_ref[...], 64)
    kt = _head_rms_t(_dot_nt(wkt_ref[...], a), gkt_ref[...])
    if rope:
        q = _rope(q, cos_ref[...], sin_ref[...])
        kt = _rope_t(kt, cost_ref[...], sint_ref[...])
    q_out[...] = (q * scale).astype(BF16)
    kt_out[...] = kt
    if v_tokens:
        v = _dot(a, wv_ref[...])
        nh = D_MODEL // LANES
        for hd in range(nh):
            v_out[pl.ds(hd, tm, stride=nh), :] = v[:, LANES * hd:LANES * (hd + 1)]
    else:
        vt = _dot_nt(wv_ref[...], a)
        v_out[...] = vt
    if logf:
        lft_out = next(it)
        lft_out[...] = _log_sigmoid(_dot_nt(wft_ref[...], a) + bft_ref[...])
    if tail:
        kt_tail, vt_tail = next(it), next(it)
        kt_tail[...] = kt
        vt_tail[...] = vt


def _qkv_proj_t(h, g, wq, wkt, wv, gq, gkt, *, b, t, tm, rope_tabs=None, wft=None, bft=None, v_tokens=False,
                tail=False, scale, name):
    m = b * t
    nt = t // tm
    nh = D_MODEL // LANES
    row = lambda n: pl.BlockSpec((tm, n), lambda i: (i, 0))
    full = lambda a: pl.BlockSpec(a.shape, lambda i: (0,) * a.ndim)
    feat = lambda n: pl.BlockSpec((None, n, tm), lambda i: (i // nt, 0, i % nt))
    ins = [h, g, wq, wkt, wv, gq, gkt]
    specs = [row(D_MODEL)] + [full(x) for x in ins[1:]]
    if rope_tabs is not None:
        cos, sinp, cos_t, sin_t = rope_tabs
        ins += [cos, sinp, cos_t, sin_t]
        tab = pl.BlockSpec((tm, LANES), lambda i: (i % nt, 0))
        tab_t = pl.BlockSpec((32, tm), lambda i: (0, i % nt))
        specs += [tab, tab, tab_t, tab_t]
    if wft is not None:
        ins += [wft, bft]
        specs += [full(wft), full(bft)]
    outs = [jax.ShapeDtypeStruct((m, D_MODEL), BF16), jax.ShapeDtypeStruct((b, D_MODEL, t), F32)]
    ospecs = [row(D_MODEL), feat(D_MODEL)]
    if v_tokens:
        outs.append(jax.ShapeDtypeStruct((m * nh, LANES), F32))
        ospecs.append(pl.BlockSpec((tm * nh, LANES), lambda i: (i, 0)))
    else:
        outs.append(jax.ShapeDtypeStruct((b, D_MODEL, t), F32))
        ospecs.append(feat(D_MODEL))
    if wft is not None:
        outs.append(jax.ShapeDtypeStruct((b, FOX_HEADS, t), F32))
        ospecs.append(feat(FOX_HEADS))
    if tail:
        assert not v_tokens
        outs += [jax.ShapeDtypeStruct((b, D_MODEL, tm), F32)] * 2
        ospecs += [pl.BlockSpec((None, D_MODEL, tm), lambda i: (i // nt, 0, 0))] * 2
    kern = functools.partial(_qkv_proj_t_kernel, rope=rope_tabs is not None, logf=wft is not None,
                             v_tokens=v_tokens, tail=tail, scale=scale)
    return pl.pallas_call(kern, out_shape=outs, grid=(m // tm,), in_specs=specs, out_specs=ospecs,
                          compiler_params=_cp(1), name=name)(*ins)


def _mla_proj_kernel(*refs, with_kv, scale):
    it = iter(refs)
    (h_ref, g_ref, wdq_ref, gq_ref, wuq_ref, wc_ref, wr_ref, gkv_ref, gqn_ref, gqr_ref, gkr_ref,
     cos_ref, sin_ref) = [next(it) for _ in range(13)]
    if with_kv:
        wuk_ref, wuv_ref, gkn_ref, wrt_ref, gkrt_ref, cost_ref, sint_ref = [next(it) for _ in range(7)]
    q_out, ckv_out, kr_out, krd_out = next(it), next(it), next(it), next(it)
    a = _rms(h_ref[...], g_ref[...]).astype(BF16)
    cq = _rms(_dot(a, wdq_ref[...]), gq_ref[...]).astype(BF16)
    q = _dot(cq, wuq_ref[...])
    n_nope = MLA_HEADS * MLA_NOPE
    qn = _head_rms(q[:, :n_nope], gqn_ref[...], MLA_NOPE) * scale
    qr = _rope(_head_rms(q[:, n_nope:], gqr_ref[...], MLA_ROPE), cos_ref[...], sin_ref[...]) * scale
    lo = _lane_iota((q.shape[0], LANES)) < MLA_ROPE
    pieces = []
    for hd in range(MLA_HEADS):
        slab = qr[:, LANES * (hd // 2):LANES * (hd // 2 + 1)]
        keep = lo if hd % 2 == 0 else jnp.logical_not(lo)
        pieces += [qn[:, LANES * hd:LANES * (hd + 1)], jnp.where(keep, slab, 0.0)]
    q_out[...] = jnp.concatenate(pieces, axis=-1).astype(BF16)
    ckv = _rms(_dot(a, wc_ref[...]), gkv_ref[...])
    ckv_out[...] = ckv
    kr2 = _rope(_head_rms(_dot(a, wr_ref[...]), gkr_ref[...], MLA_ROPE), cos_ref[...], sin_ref[...])
    kr_out[...] = kr2[:, :MLA_ROPE]
    krd_out[...] = kr2
    if with_kv:
        kc_out, v_out, krt_out = next(it), next(it), next(it)
        c = ckv.astype(BF16)
        kn = _head_rms(_dot(c, wuk_ref[...]), gkn_ref[...], MLA_NOPE)
        pieces = []
        for hd in range(MLA_HEADS):
            pieces += [kn[:, LANES * hd:LANES * (hd + 1)], kr2]
        kc_out[...] = jnp.concatenate(pieces, axis=-1).astype(BF16)
        v_out[...] = _dot(c, wuv_ref[...]).astype(BF16)
        krt_out[...] = _rope_t(_head_rms_t(_dot_nt(wrt_ref[...], a), gkrt_ref[...]), cost_ref[...], sint_ref[...])


def _mla_proj(h, g, w, rope_tabs, *, b, t, tm, with_kv, scale, name):
    m = h.shape[0]
    cos, sinp = rope_tabs[:2]
    nt = cos.shape[0] // tm
    row = lambda n: pl.BlockSpec((tm, n), lambda i: (i, 0))
    full = lambda a: pl.BlockSpec(a.shape, lambda i: (0,) * a.ndim)
    tab = pl.BlockSpec((tm, LANES), lambda i: (i % nt, 0))
    ins = [h, g, w["wdq"], w["gq"], w["wuq"], w["wc"], w["wr"], w["gkv"], w["gqn"], w["gqr"], w["gkr"], cos, sinp]
    specs = [row(D_MODEL)] + [full(x) for x in ins[1:11]] + [tab, tab]
    qw = MLA_HEADS * 2 * LANES
    outs = [jax.ShapeDtypeStruct((m, qw), BF16), jax.ShapeDtypeStruct((m, MLA_KV_LORA), F32),
            jax.ShapeDtypeStruct((m, MLA_ROPE), F32), jax.ShapeDtypeStruct((m, LANES), F32)]
    ospecs = [row(qw), row(MLA_KV_LORA), row(MLA_ROPE), row(LANES)]
    if with_kv:
        cos_t, sin_t = rope_tabs[2:]
        tab_t = pl.BlockSpec((32, tm), lambda i: (0, i % nt))
        extra = [w["wuk"], w["wuv"], w["gkn"], w["wrt"], w["gkrt"]]
        ins += extra + [cos_t, sin_t]
        specs += [full(x) for x in extra] + [tab_t, tab_t]
        outs += [jax.ShapeDtypeStruct((m, qw), BF16), jax.ShapeDtypeStruct((m, MLA_HEADS * MLA_V), BF16),
                 jax.ShapeDtypeStruct((b, MLA_ROPE, t), F32)]
        ospecs += [row(qw), row(MLA_HEADS * MLA_V),
                   pl.BlockSpec((None, MLA_ROPE, tm), lambda i: (i // nt, 0, i % nt))]
    kern = functools.partial(_mla_proj_kernel, with_kv=with_kv, scale=scale)
    return pl.pallas_call(kern, out_shape=outs, grid=(m // tm,), in_specs=specs, out_specs=ospecs,
                          compiler_params=_cp(1), name=name)(*ins)


def _cumsum_kernel(x_ref, o_ref, *, tk):
    nh, t = x_ref.shape
    tri = (_row_iota((tk, tk)) <= _lane_iota((tk, tk))).astype(BF16)
    carry = jnp.zeros((nh, 1), F32)
    for c in range(t // tk):
        hi, mid, lo = _split3(x_ref[:, tk * c:tk * (c + 1)])
        y = _dot(jnp.concatenate([hi, mid, lo], axis=0), tri)
        f = (y[:nh] + y[nh:2 * nh]) + y[2 * nh:] + carry
        o_ref[:, tk * c:tk * (c + 1)] = f
        carry = f[:, tk - 1:tk]


def _cumsum_last(x, *, tk, name):
    b, nh, t = x.shape
    spec = pl.BlockSpec((None, nh, t), lambda i: (i, 0, 0))
    return pl.pallas_call(functools.partial(_cumsum_kernel, tk=tk), out_shape=jax.ShapeDtypeStruct(x.shape, F32),
                          grid=(b,), in_specs=[spec], out_specs=spec, compiler_params=_cp(1), name=name)(x)


def _mla_attn_kernel(q_ref, k_ref, v_ref, o_ref, *, tq, hps):
    kw = 2 * LANES
    t = k_ref.shape[0]
    for qi in range(t // tq):
        rows = slice(tq * qi, tq * (qi + 1))
        limit = (((qi * tq + _row_iota((tq, 1))) >> CHUNK_SHIFT) + 1) << CHUNK_SHIFT
        outs = []
        n0 = qi * tq
        for h in range(hps):
            q = q_ref[rows, kw * h:kw * (h + 1)]
            kcols = slice(kw * h, kw * (h + 1))
            vcols = slice(MLA_V * h, MLA_V * (h + 1))
            s_d = jnp.where(n0 + _lane_iota((1, tq)) < limit, _dot_nt(q, k_ref[n0:n0 + tq, kcols]), NEG)
            m = jnp.max(s_d, axis=-1, keepdims=True)
            if qi:
                s_f = _dot_nt(q, k_ref[0:n0, kcols])
                m = jnp.maximum(m, jnp.max(s_f, axis=-1, keepdims=True))
            p_d = jnp.exp2(s_d - m)
            acc = _dot(p_d.astype(BF16), v_ref[n0:n0 + tq, vcols])
            l = jnp.sum(p_d, axis=-1, keepdims=True)
            if qi:
                p_f = jnp.exp2(s_f - m)
                acc = acc + _dot(p_f.astype(BF16), v_ref[0:n0, vcols])
                l = l + jnp.sum(p_f, axis=-1, keepdims=True)
            outs.append(acc / l)
        o_ref[rows, :] = jnp.concatenate(outs, axis=-1).astype(BF16)


def _mla_attn(q, kc, v, *, b, t, tq, hps, name):
    kw = 2 * LANES * hps
    vw = MLA_V * hps
    return pl.pallas_call(
        functools.partial(_mla_attn_kernel, tq=tq, hps=hps),
        out_shape=jax.ShapeDtypeStruct((b * t, MLA_HEADS * MLA_V), BF16),
        grid=(b, MLA_HEADS // hps),
        in_specs=[pl.BlockSpec((t, kw), lambda i, h: (i, h)),
                  pl.BlockSpec((t, kw), lambda i, h: (i, h)),
                  pl.BlockSpec((t, vw), lambda i, h: (i, h))],
        out_specs=pl.BlockSpec((t, vw), lambda i, h: (i, h)),
        compiler_params=_cp(2), name=name)(q, kc, v)


def _ones_row_values(vt, c):
    row = _row_iota(vt.shape)
    if c == 0:
        return jnp.where(row < 64, vt, jnp.where(row == 64, 1.0, 0.0)).astype(BF16)
    return jnp.where(row >= 64, vt, jnp.where(row == 0, 1.0, 0.0)).astype(BF16)


def _merge_pair(acc0, acc1):
    lo = _lane_iota(acc0.shape) < 64
    return jnp.where(lo, acc0 / acc0[:, 64:65], acc1 / acc1[:, 0:1])


def _pair_attn_kernel(*refs, mode, tq, lam_init):
    if mode == "fox":
        q_ref, kt_ref, vt_ref, fk_ref, o_ref, ka, va = refs
    else:
        q_ref, kt_ref, v_ref, lam_ref, gsub_ref, o_ref, ka, vb = refs
    t = kt_ref.shape[1]
    if mode == "fox":
        ka[0:LANES, :] = kt_ref[...].astype(BF16)
        f = fk_ref[...] * (-LOG2E)
        terms = [x.astype(F32) for x in _split3(f)]
        nr = 2 * SUBLANES
        row = _row_iota((nr, t))
        aug = jnp.zeros((nr, t), F32)
        for c in range(2):
            for j in range(3):
                aug = jnp.where(row == 3 * c + j, terms[j][c:c + 1, :], aug)
        ka[LANES:LANES + nr, :] = aug.astype(BF16)
        ka[LANES + nr:, :] = jnp.zeros((LANES - nr, t), BF16)
        vt = vt_ref[...]
        va[0] = _ones_row_values(vt, 0)
        va[1] = _ones_row_values(vt, 1)
    else:
        ka[...] = kt_ref[...].astype(BF16)
        nhv = v_ref.shape[0] // t
        vb[...] = v_ref[pl.ds(pl.program_id(1), t, stride=nhv), :].astype(BF16)

    def q_tile(qi, q):
        lane = _lane_iota((tq, LANES))
        zero = jnp.zeros_like(q)
        qa = [jnp.where(lane < 64, q, zero), jnp.where(lane < 64, zero, q)]
        qpos = qi * tq + _row_iota((tq, 1))
        if mode == "fox":
            limit = qpos + 1
            pick = [jnp.where(lane < 3 * c, 0.0, jnp.where(lane < 3 * c + 3, 1.0, 0.0)).astype(BF16)
                    for c in range(2)]
            qa = [jnp.concatenate([qa[c], pick[c]], axis=-1) for c in range(2)]
        else:
            limit = ((qpos >> CHUNK_SHIFT) + 1) << CHUNK_SHIFT

        n0 = qi * tq
        accs, sums = [], []
        for c in range(2):
            s_d = jnp.where(n0 + _lane_iota((1, tq)) < limit, _dot(qa[c], ka[:, n0:n0 + tq]), NEG)
            m = jnp.max(s_d, axis=-1, keepdims=True)
            if qi:
                s_f = _dot(qa[c], ka[:, 0:n0])
                m = jnp.maximum(m, jnp.max(s_f, axis=-1, keepdims=True))
            parts = [(jnp.exp2(s_d - m), n0, n0 + tq)] + ([(jnp.exp2(s_f - m), 0, n0)] if qi else [])
            acc, l = 0.0, 0.0
            for p, lo_k, hi_k in parts:
                pb = p.astype(BF16)
                if mode == "fox":
                    acc = acc + _dot_nt(pb, va[c, :, lo_k:hi_k])
                else:
                    acc = acc + _dot(pb, vb[lo_k:hi_k, :])
                    l = l + jnp.sum(p, axis=-1, keepdims=True)
            accs.append(acc)
            sums.append(l)
        if mode == "fox":
            o = _merge_pair(*accs)
        else:
            lv = lam_ref[...]
            lam = (jnp.exp(jnp.sum(lv[0:1] * lv[1:2], axis=-1, keepdims=True))
                   - jnp.exp(jnp.sum(lv[2:3] * lv[3:4], axis=-1, keepdims=True)) + lam_init)
            o = _rms(accs[0] / sums[0] - lam * (accs[1] / sums[1]), gsub_ref[...]) * (1.0 - lam_init)
        return o.astype(BF16)

    for qi in range(t // tq):
        o_ref[tq * qi:tq * (qi + 1), :] = q_tile(qi, q_ref[tq * qi:tq * (qi + 1), :])


def _pair_attn(q, kt, v, extra, *, mode, b, t, tq, lam_init=0.0, name):
    npair = D_MODEL // LANES
    qspec = pl.BlockSpec((t, LANES), lambda i, p: (i, p))
    ktspec = pl.BlockSpec((None, LANES, t), lambda i, p: (i, p, 0))
    if mode == "fox":
        vspec = ktspec
        especs = [pl.BlockSpec((None, None, 2, t), lambda i, p: (i, p, 0, 0))]
        scratch = [pltpu.VMEM((2 * LANES, t), BF16), pltpu.VMEM((2, LANES, t), BF16)]
    else:
        vspec = pl.BlockSpec((t * npair, LANES), lambda i, p: (i, 0))
        especs = [pl.BlockSpec(x.shape, lambda i, p: (0, 0)) for x in extra]
        scratch = [pltpu.VMEM((LANES, t), BF16), pltpu.VMEM((t, LANES), BF16)]
    return pl.pallas_call(
        functools.partial(_pair_attn_kernel, mode=mode, tq=tq, lam_init=lam_init),
        out_shape=jax.ShapeDtypeStruct((b * t, D_MODEL), BF16),
        grid=(b, npair),
        in_specs=[qspec, ktspec, vspec] + especs,
        out_specs=qspec,
        scratch_shapes=scratch,
        compiler_params=_cp(2), name=name)(q, kt, v, *extra)


def _rel_gather(tab, width, center):
    idx = jnp.clip(center - _lane_iota((REL_PAD, width)), -REL_CLIP, REL_CLIP) + REL_CLIP
    onehot = (_row_iota((REL_PAD, width)) == idx).astype(BF16)
    hi, mid, lo = _split3(tab)
    return (_dot(hi, onehot) + _dot(mid, onehot)) + _dot(lo, onehot)


def _band_attn_kernel(q_ref, kt_ref, vt_ref, tab_ref, o_ref, kb, va, bias_ref, *, tq, win, bw):
    gw = bw + tq

    @pl.when(pl.program_id(1) == 0)
    def _():
        g = _rel_gather(tab_ref[...], gw, BAND_LEFT + tq) * LOG2E
        ii = _row_iota((tq, bw)) >> CHUNK_SHIFT
        jj = _lane_iota((tq, bw)) >> CHUNK_SHIFT
        allowed = (jj >= ii) & (jj <= ii + BAND_LEFT_CHUNKS)
        for c in range(2):
            rows = jnp.broadcast_to(g[c:c + 1, :], (tq, gw))
            skew = pltpu.roll(rows, gw - tq, axis=1, stride=1, stride_axis=0)
            bias_ref[c] = jnp.where(allowed, skew[:, :bw], NEG)

    kb[...] = kt_ref[...].astype(BF16)
    vt = vt_ref[...]
    va[0] = _ones_row_values(vt, 0)
    va[1] = _ones_row_values(vt, 1)

    lo = _lane_iota((tq, LANES)) < 64
    for r in range(q_ref.shape[0] // tq):
        q0 = r * tq
        ws = max(q0 - BAND_LEFT, 0)
        d = BAND_LEFT - q0 + ws
        k = kb[:, pl.ds(ws, win)]
        q = q_ref[tq * r:tq * (r + 1), :]
        zero = jnp.zeros_like(q)
        accs = []
        for c in range(2):
            qc = jnp.where(lo, q, zero) if c == 0 else jnp.where(lo, zero, q)
            s = _dot(qc, k) + bias_ref[c, :, pl.ds(d, win)]
            p = jnp.exp2(s - jnp.max(s, axis=-1, keepdims=True))
            accs.append(_dot_nt(p.astype(BF16), va[c, :, pl.ds(ws, win)]))
        o_ref[tq * r:tq * (r + 1), :] = _merge_pair(*accs).astype(BF16)


def _band_attn(q, kt, vt, tab, *, b, t, tq, name):
    npair = D_MODEL // LANES
    win = BAND_LEFT + tq
    bw = win + BAND_LEFT
    assert t >= win and tq % CHUNK == 0 and BAND_LEFT % tq == 0
    qspec = pl.BlockSpec((t, LANES), lambda p, i: (i, p))
    kvspec = pl.BlockSpec((None, LANES, t), lambda p, i: (i, p, 0))
    return pl.pallas_call(
        functools.partial(_band_attn_kernel, tq=tq, win=win, bw=bw),
        out_shape=jax.ShapeDtypeStruct((b * t, D_MODEL), BF16),
        grid=(npair, b),
        in_specs=[qspec, kvspec, kvspec, pl.BlockSpec((None, SUBLANES, REL_PAD), lambda p, i: (p, 0, 0))],
        out_specs=qspec,
        scratch_shapes=[pltpu.VMEM((LANES, t), BF16), pltpu.VMEM((2, LANES, t), BF16), pltpu.VMEM((2, tq, bw), F32)],
        compiler_params=_cp(2), name=name)(q, kt, vt, tab)


def _block_diag_q(q, nh, width):
    tq = q.shape[0]
    rep = jnp.concatenate([q] * nh, axis=0)
    keep = (_row_iota(rep.shape) >> _log2(tq)) == (_lane_iota(rep.shape) >> _log2(width))
    return jnp.where(keep, rep, jnp.zeros_like(rep))


def _expand_rows(f, tq):
    return jnp.concatenate([jnp.broadcast_to(f[h:h + 1, :], (tq, f.shape[1])) for h in range(f.shape[0])], axis=0)


def _dec_attn_kernel(*refs, mode, n_cache, tq, past, lam_init):
    if mode == "fox":
        (q_ref, kc_ref, vc_ref, kn_ref, vn_ref, fkc_ref, fkn_ref, o_ref, qb, m_ref, l_ref, acc_ref) = refs
    else:
        (q_ref, kc_ref, vc_ref, kn_ref, vn_ref, lam_ref, gsub_ref, o_ref, qb, m_ref, l_ref, acc_ref) = refs
    t = pl.program_id(1)
    nh = D_MODEL // 64
    rows = nh * tq

    @pl.when(t == 0)
    def _():
        qb[...] = _block_diag_q(q_ref[...], nh, 64)
        m_ref[...] = jnp.full_like(m_ref, NEG)
        l_ref[...] = jnp.zeros_like(l_ref)
        acc_ref[...] = jnp.zeros_like(acc_ref)

    def cache_step():
        s = _dot(qb[...], kc_ref[...].astype(BF16))
        if mode == "fox":
            s = s - _expand_rows(fkc_ref[...] * LOG2E, tq)
            v = vc_ref[...].astype(BF16)
            _softmax_update(s, lambda pb: _dot_nt(pb, v), m_ref, l_ref, acc_ref)
        else:
            tk = kc_ref.shape[1]
            nhv = D_MODEL // LANES
            v = jnp.concatenate([vc_ref[pl.ds(h, tk, stride=nhv), :].astype(BF16) for h in range(nhv)], axis=-1)
            _softmax_update(s, lambda pb: _dot(pb, v), m_ref, l_ref, acc_ref)

    cache_step()

    @pl.when(t == n_cache - 1)
    def _():
        s = _dot_nt(qb[...], kn_ref[...].astype(BF16))
        qpos = past + (_row_iota((rows, tq)) & (tq - 1))
        kpos = past + _lane_iota((rows, tq))
        if mode == "fox":
            s = s - _expand_rows(fkn_ref[...][:, :tq] * LOG2E, tq)
            allowed = kpos <= qpos
        else:
            allowed = (kpos >> CHUNK_SHIFT) <= (qpos >> CHUNK_SHIFT)
        vn = vn_ref[...].astype(BF16)
        _softmax_update(jnp.where(allowed, s, NEG), lambda pb: _dot(pb, vn), m_ref, l_ref, acc_ref)
        o_all = acc_ref[...] / l_ref[...]
        if mode == "fox":
            o = jnp.zeros((tq, D_MODEL), F32)
            hl = _lane_iota((tq, D_MODEL)) >> _log2(FOX_DH)
            for h in range(nh):
                o = jnp.where(hl == h, o_all[h * tq:(h + 1) * tq, :], o)
        else:
            lv = lam_ref[...]
            lam = (jnp.exp(jnp.sum(lv[0:1] * lv[1:2], axis=-1, keepdims=True))
                   - jnp.exp(jnp.sum(lv[2:3] * lv[3:4], axis=-1, keepdims=True)) + lam_init)
            pieces = []
            for h in range(nh // 2):
                a0 = o_all[(2 * h) * tq:(2 * h + 1) * tq, LANES * h:LANES * (h + 1)]
                a1 = o_all[(2 * h + 1) * tq:(2 * h + 2) * tq, LANES * h:LANES * (h + 1)]
                pieces.append(_rms(a0 - lam * a1, gsub_ref[...]) * (1.0 - lam_init))
            o = jnp.concatenate(pieces, axis=-1)
        o_ref[...] = o.astype(BF16)


def _dec_attn(q, kc, vc, kn, vn, extra, *, mode, b, tq, past, tk, lam_init=0.0, name):
    n_cache = past // tk
    nh = D_MODEL // 64
    new = pl.BlockSpec((tq, D_MODEL), lambda i, t: (i, 0))
    cache_t = pl.BlockSpec((None, D_MODEL, tk), lambda i, t: (i, 0, t))
    if mode == "fox":
        (fk,) = extra
        vspec = cache_t
        especs = [pl.BlockSpec((None, nh, tk), lambda i, t: (i, 0, t)),
                  pl.BlockSpec((None, nh, LANES), lambda i, t: (i, 0, past // LANES))]
        ins = [fk, fk]
    else:
        nhv = D_MODEL // LANES
        vspec = pl.BlockSpec((tk * nhv, LANES), lambda i, t: (i * n_cache + t, 0))
        especs = [pl.BlockSpec(x.shape, lambda i, t: (0, 0)) for x in extra]
        ins = list(extra)
    return pl.pallas_call(
        functools.partial(_dec_attn_kernel, mode=mode, n_cache=n_cache, tq=tq, past=past, lam_init=lam_init),
        out_shape=jax.ShapeDtypeStruct((b * tq, D_MODEL), BF16),
        grid=(b, n_cache),
        in_specs=[new, cache_t, vspec, new, new] + especs,
        out_specs=new,
        scratch_shapes=[pltpu.VMEM((nh * tq, D_MODEL), BF16), pltpu.VMEM((nh * tq, 1), F32),
                        pltpu.VMEM((nh * tq, 1), F32), pltpu.VMEM((nh * tq, D_MODEL), F32)],
        compiler_params=_cp(2), name=name)(q, kc, vc, kn, vn, *ins)


def _mla_dec_kernel(q_ref, cc_ref, rc_ref, cn_ref, rn_ref, wuk_ref, wuv_ref, gkn_ref, o_ref,
                    qn, qr, m_ref, l_ref, acc_ref, *, n_cache, tq, past):
    t = pl.program_id(1)
    nh = MLA_HEADS
    rows = nh * tq

    @pl.when(t == 0)
    def _():
        q = q_ref[...]
        zero = jnp.zeros((tq, LANES), BF16)
        for h in range(nh):
            qn[h * tq:(h + 1) * tq, :] = jnp.concatenate(
                [q[:, 2 * LANES * h:2 * LANES * h + LANES] if c == h else zero for c in range(nh)], axis=-1)
            qr[h * tq:(h + 1) * tq, :] = q[:, 2 * LANES * h + LANES:2 * LANES * (h + 1)]
        m_ref[...] = jnp.full_like(m_ref, NEG)
        l_ref[...] = jnp.zeros_like(l_ref)
        acc_ref[...] = jnp.zeros_like(acc_ref)

    def step(ckv, s_rope, mask_new):
        c = ckv.astype(BF16)
        kn = _head_rms(_dot(c, wuk_ref[...]), gkn_ref[...], MLA_NOPE).astype(BF16)
        s = _dot_nt(qn[...], kn) + s_rope
        if mask_new:
            n = ckv.shape[0]
            qpos = past + (_row_iota((rows, n)) & (tq - 1))
            kpos = past + _lane_iota((rows, n))
            s = jnp.where((kpos >> CHUNK_SHIFT) <= (qpos >> CHUNK_SHIFT), s, NEG)
        _softmax_update(s, lambda pb: _dot(pb, c), m_ref, l_ref, acc_ref)

    def cache_step():
        kr = rc_ref[...].astype(BF16)
        step(cc_ref[...], _dot(qr[...], jnp.concatenate([kr, kr], axis=0)), False)

    cache_step()

    @pl.when(t == n_cache - 1)
    def _():
        step(cn_ref[...], _dot_nt(qr[...], rn_ref[...].astype(BF16)), True)
        lat = (acc_ref[...] / l_ref[...]).astype(BF16)
        o_ref[...] = jnp.concatenate(
            [_dot(lat[h * tq:(h + 1) * tq, :], wuv_ref[:, MLA_V * h:MLA_V * (h + 1)]) for h in range(nh)],
            axis=-1).astype(BF16)


def _mla_dec(q, cc, rc, cn, rn, wuk, wuv, gkn, *, b, tq, past, tk, name):
    n_cache = past // tk
    nh = MLA_HEADS
    new = lambda n: pl.BlockSpec((tq, n), lambda i, t: (i, 0))
    full = lambda a: pl.BlockSpec(a.shape, lambda i, t: (0,) * a.ndim)
    return pl.pallas_call(
        functools.partial(_mla_dec_kernel, n_cache=n_cache, tq=tq, past=past),
        out_shape=jax.ShapeDtypeStruct((b * tq, nh * MLA_V), BF16),
        grid=(b, n_cache),
        in_specs=[new(nh * 2 * LANES),
                  pl.BlockSpec((tk, MLA_KV_LORA), lambda i, t: (i * n_cache + t, 0)),
                  pl.BlockSpec((None, MLA_ROPE, tk), lambda i, t: (i, 0, t)),
                  new(MLA_KV_LORA), new(LANES), full(wuk), full(wuv), full(gkn)],
        out_specs=new(nh * MLA_V),
        scratch_shapes=[pltpu.VMEM((nh * tq, nh * MLA_NOPE), BF16), pltpu.VMEM((nh * tq, LANES), BF16),
                        pltpu.VMEM((nh * tq, 1), F32), pltpu.VMEM((nh * tq, 1), F32),
                        pltpu.VMEM((nh * tq, MLA_KV_LORA), F32)],
        compiler_params=_cp(2), name=name)(q, cc, rc, cn, rn, wuk, wuv, gkn)


def _band_dec_kernel(q_ref, kc_ref, vc_ref, kn_ref, vn_ref, knt_ref, vnt_ref, tab_ref, o_ref, kr_out, vr_out,
                     *, tq, w, past):
    keep_old = _lane_iota((D_MODEL, LANES)) < LANES - tq
    for c_ref, nt_ref, r_out in ((kc_ref, knt_ref, kr_out), (vc_ref, vnt_ref, vr_out)):
        rolled = pltpu.roll(c_ref[...], w - tq, axis=1)
        r_out[:, :w - LANES] = rolled[:, :w - LANES]
        r_out[:, w - LANES:] = jnp.where(keep_old, rolled[:, w - LANES:], nt_ref[...])
    nh = BAND_HEADS
    rows = nh * tq
    bwid = ((w + tq + LANES - 1) // LANES) * LANES
    qb = _block_diag_q(q_ref[...], nh, BAND_DH)
    g = _rel_gather(tab_ref[...], bwid, w + tq) * LOG2E
    bias = jnp.concatenate(
        [pltpu.roll(jnp.broadcast_to(g[h:h + 1, :], (tq, bwid)), bwid - tq, axis=1, stride=1, stride_axis=0)
         for h in range(nh)], axis=0)
    qc = (past + (_row_iota((rows, bwid)) & (tq - 1))) >> CHUNK_SHIFT
    kc = (past - w + _lane_iota((rows, bwid))) >> CHUNK_SHIFT
    bias = jnp.where((kc <= qc) & (kc >= qc - BAND_LEFT_CHUNKS), bias, NEG)
    s_c = _dot(qb, kc_ref[...].astype(BF16)) + bias[:, :w]
    s_n = _dot_nt(qb, kn_ref[...].astype(BF16)) + bias[:, w:w + tq]
    m = jnp.maximum(jnp.max(s_c, axis=-1, keepdims=True), jnp.max(s_n, axis=-1, keepdims=True))
    p_c = jnp.exp2(s_c - m)
    p_n = jnp.exp2(s_n - m)
    l = jnp.sum(p_c, axis=-1, keepdims=True) + jnp.sum(p_n, axis=-1, keepdims=True)
    o_all = (_dot_nt(p_c.astype(BF16), vc_ref[...].astype(BF16))
             + _dot(p_n.astype(BF16), vn_ref[...].astype(BF16))) / l
    o = jnp.zeros((tq, D_MODEL), F32)
    hl = _lane_iota((tq, D_MODEL)) >> _log2(BAND_DH)
    for h in range(nh):
        o = jnp.where(hl == h, o_all[h * tq:(h + 1) * tq, :], o)
    o_ref[...] = o.astype(BF16)


def _band_dec(q, kc, vc, kn, vn, tab, *, b, tq, w, past, name):
    assert w % LANES == 0 and tq <= LANES
    new = pl.BlockSpec((tq, D_MODEL), lambda i: (i, 0))
    cache = pl.BlockSpec((None, D_MODEL, w), lambda i: (i, 0, 0))
    slab = pl.BlockSpec((None, D_MODEL, LANES), lambda i: (i, 0, 0))
    right = lambda x: jnp.pad(jnp.swapaxes(x.reshape(b, tq, D_MODEL), 1, 2), ((0, 0), (0, 0), (LANES - tq, 0)))
    return pl.pallas_call(
        functools.partial(_band_dec_kernel, tq=tq, w=w, past=past),
        out_shape=[jax.ShapeDtypeStruct((b * tq, D_MODEL), BF16), jax.ShapeDtypeStruct(kc.shape, F32),
                   jax.ShapeDtypeStruct(vc.shape, F32)],
        grid=(b,),
        in_specs=[new, cache, cache, new, new, slab, slab, pl.BlockSpec(tab.shape, lambda i: (0, 0))],
        out_specs=[new, cache, cache],
        compiler_params=_cp(1), name=name)(q, kc, vc, kn, vn, right(kn), right(vn), tab)


def _ffn_rows_kernel(*refs, mode, tps, seq, final, tf):
    it = iter(refs)
    h_ref, o_ref, wo_ref, gf_ref, wup_ref, cw_ref, cb_ref, wd_ref = [next(it) for _ in range(8)]
    if mode == "state":
        s1_ref, s2_ref = next(it), next(it)
    if final:
        gfin_ref = next(it)
    out_ref, ug_out, uh_out = next(it), next(it), next(it)
    if mode == "carry":
        carry_ref = next(it)
    tm = h_ref.shape[0]
    h1 = h_ref[...] + _dot(o_ref[...], wo_ref[...])
    xn = _rms(h1, gf_ref[...]).astype(BF16)
    if mode == "carry":
        @pl.when(pl.program_id(0) % tps == 0)
        def _():
            carry_ref[...] = jnp.zeros_like(carry_ref)
    else:
        tpos = _row_iota((tm, tf)) & (seq - 1)
    acts = []
    for j in range(D_FF // tf):
        cs = []
        for part, u_out in enumerate((ug_out, uh_out)):
            tile = slice(tf * j, tf * (j + 1))
            cols = slice(part * D_FF + tf * j, part * D_FF + tf * (j + 1))
            u = _dot(xn, wup_ref[:, cols])
            if mode == "carry":
                tail = u[tm - SUBLANES:]
                u_out[:, tile] = tail
                ext = jnp.concatenate([carry_ref[part, :, tile], u], axis=0)
                carry_ref[part, :, tile] = tail
                um1 = pltpu.roll(ext, 1, axis=0)[SUBLANES:]
                um2 = pltpu.roll(ext, 2, axis=0)[SUBLANES:]
            else:
                u_out[:, tile] = u
                um1 = jnp.where(tpos >= 1, pltpu.roll(u, 1, axis=0), s1_ref[:, cols])
                um2 = jnp.where(tpos >= 2, pltpu.roll(u, 2, axis=0), s2_ref[:, cols])
            cw = cw_ref[:, cols]
            cs.append(((cb_ref[:, cols] + u * cw[2:3]) + um2 * cw[0:1]) + um1 * cw[1:2])
        acts.append((cs[0] * jax.nn.sigmoid(cs[0]) * cs[1]).astype(BF16))
    acc = h1 + _dot(jnp.concatenate(acts, axis=-1), wd_ref[...])
    if final:
        acc = _rms(acc, gfin_ref[...])
    out_ref[...] = acc


def _ffn_rows(h, o, wo, gf, wup, cw, cb, wd, *, tm, tf, seq, state=None, final_g=None, name):
    m = h.shape[0]
    mode = "carry" if state is None else "state"
    tps = seq // tm if mode == "carry" else 1
    row = lambda n: pl.BlockSpec((tm, n), lambda i: (i, 0))
    once = lambda a: pl.BlockSpec(a.shape, lambda i: (0,) * a.ndim, pipeline_mode=pl.Buffered(1))
    ins = [h, o, wo, gf, wup, cw, cb, wd]
    specs = [row(D_MODEL), row(D_MODEL)] + [once(x) for x in ins[2:]]
    scratch = []
    if mode == "state":
        assert m == tm and seq & (seq - 1) == 0
        ins += list(state)
        specs += [row(2 * D_FF), row(2 * D_FF)]
        u_shape = jax.ShapeDtypeStruct((m, D_FF), F32)
        u_spec = row(D_FF)
    else:
        assert seq % tm == 0
        scratch.append(pltpu.VMEM((2, SUBLANES, D_FF), F32))
        u_shape = jax.ShapeDtypeStruct((m // tm, SUBLANES, D_FF), F32)
        u_spec = pl.BlockSpec((None, SUBLANES, D_FF), lambda i: (i, 0, 0))
    if final_g is not None:
        ins.append(final_g)
        specs.append(once(final_g))
    kern = functools.partial(_ffn_rows_kernel, mode=mode, tps=tps, seq=seq, final=final_g is not None, tf=tf)
    return pl.pallas_call(
        kern, out_shape=[jax.ShapeDtypeStruct((m, D_MODEL), F32), u_shape, u_shape],
        grid=(m // tm,), in_specs=specs, out_specs=[row(D_MODEL), u_spec, u_spec], scratch_shapes=scratch,
        compiler_params=_cp(1), name=name)(*ins)


def _rope_tables(pos):
    inv = 1.0 / (ROPE_THETA ** (jnp.arange(0, 64, 2, dtype=F32) / 64))
    ang = pos.astype(F32)[:, None] * inv[None, :]
    c, s = jnp.cos(ang), jnp.sin(ang)
    return jnp.tile(c, (1, 4)), jnp.tile(jnp.concatenate([-s, s], axis=1), (1, 2)), c.T, s.T


def _tile_gain(g, n):
    return jnp.tile(g.astype(F32), n // g.shape[0]).reshape(1, n)


def _gain_t(g, n, w):
    return jnp.broadcast_to(jnp.tile(g.astype(F32), n // g.shape[0])[:, None], (n, w))


def _to_tokens(xt, heads):
    b, n, t = xt.shape
    dh = n // math.prod(heads)
    nd = len(heads)
    return xt.reshape(b, *heads, dh, t).transpose(0, nd + 2, *range(1, nd + 2))


def _to_features(x):
    b, t = x.shape[:2]
    nd = x.ndim
    return x.transpose(0, *range(2, nd), 1).reshape(b, -1, t)


def kernel(x_prompt, x_sample, cache_mla_ckv, cache_mla_krope, cache_fox_k, cache_fox_v, cache_fox_logf,
           cache_diff_k, cache_diff_v, cache_band_k, cache_band_v, state_ffn_conv,
           attn_norm_g, ffn_norm_g, final_norm_g,
           mla_w_dq, mla_g_q, mla_w_uq, mla_w_dkv, mla_g_kv, mla_w_uk, mla_w_uv,
           mla_g_qn, mla_g_qr, mla_g_kn, mla_g_kr, mla_w_o,
           fox_w_qkv, fox_w_f, fox_b_f, fox_g_q, fox_g_k, fox_w_o,
           diff_w_qkv, diff_g_q, diff_g_k, diff_lq1, diff_lk1, diff_lq2, diff_lk2, diff_g_sub, diff_w_o,
           band_w_qkv, band_g_q, band_g_k, band_rel_bias, band_w_o,
           ffn_w_up, ffn_conv_w, ffn_conv_b, ffn_w_down):
    bp, tp, d = x_prompt.shape
    bs, ts, _ = x_sample.shape
    past = cache_mla_ckv.shape[1]
    depth = attn_norm_g.shape[0]
    mp, ms = bp * tp, bs * ts
    tm_p = min(512, tp)
    tm_f = min(512, tp)
    tq = 512
    tq_band = 256
    tkd = min(1024, past)
    tkw = min(2048, past)
    tf = 256
    assert tp % tq == 0 and past % tkd == 0 and past % tkw == 0 and past % CHUNK == 0

    tobf = lambda a: a.astype(BF16)
    rowv = lambda g: g.astype(F32).reshape(1, -1)
    pos_p = jnp.arange(tp, dtype=jnp.int32)
    pos_s = past + jnp.arange(ts, dtype=jnp.int32)
    tabs_p = _rope_tables(pos_p)
    tabs_s = tuple(jnp.tile(a, (bs, 1)) for a in _rope_tables(pos_s)[:2])

    nope_cols = jnp.arange(MLA_HEADS)[:, None] * (MLA_NOPE + MLA_ROPE) + jnp.arange(MLA_NOPE)[None, :]
    rope_cols = jnp.arange(MLA_HEADS)[:, None] * (MLA_NOPE + MLA_ROPE) + MLA_NOPE + jnp.arange(MLA_ROPE)[None, :]
    wuq_perm = jnp.concatenate([mla_w_uq[:, nope_cols.reshape(-1)], mla_w_uq[:, rope_cols.reshape(-1)]], axis=1)
    wr = mla_w_dkv[:, MLA_KV_LORA:]
    mla_w = dict(
        wdq=tobf(mla_w_dq), gq=rowv(mla_g_q), wuq=tobf(wuq_perm), wc=tobf(mla_w_dkv[:, :MLA_KV_LORA]),
        wr=tobf(jnp.concatenate([wr, wr], axis=1)), gkv=rowv(mla_g_kv),
        gqn=_tile_gain(mla_g_qn, MLA_HEADS * MLA_NOPE), gqr=_tile_gain(mla_g_qr, MLA_HEADS * MLA_ROPE),
        gkr=_tile_gain(mla_g_kr, LANES), wuk=tobf(mla_w_uk), wuv=tobf(mla_w_uv),
        gkn=_tile_gain(mla_g_kn, MLA_HEADS * MLA_NOPE), wrt=tobf(wr.T), gkrt=_gain_t(mla_g_kr, MLA_ROPE, tm_p))

    def split_qkv(w):
        wq, wk, wv = w[:, :d], w[:, d:2 * d], w[:, 2 * d:]
        return tobf(wq), tobf(wk.T), tobf(wv), tobf(wv.T)

    fox_wf = tobf(jnp.pad(fox_w_f, ((0, 0), (0, LANES - FOX_HEADS))))
    fox_bf = jnp.pad(fox_b_f.astype(F32), (0, LANES - FOX_HEADS)).reshape(1, LANES)
    fox_wft = tobf(fox_w_f.T)
    fox_bft = jnp.broadcast_to(fox_b_f.astype(F32)[:, None], (FOX_HEADS, tm_p))
    lamv = jnp.stack([diff_lq1, diff_lk1, diff_lq2, diff_lk2]).astype(F32)
    gsub = rowv(diff_g_sub)
    tab_pad = jnp.pad(band_rel_bias.astype(F32), ((0, 0), (0, REL_PAD - band_rel_bias.shape[1])))
    wo = [tobf(mla_w_o), tobf(fox_w_o), tobf(diff_w_o), tobf(band_w_o)]
    cwf, cbf = ffn_conv_w.astype(F32), ffn_conv_b.astype(F32)

    h_p = x_prompt.reshape(mp, d)
    h_s = x_sample.reshape(ms, d)
    outs = {}
    conv_p, conv_s = [], []
    for i in range(depth):
        kind = i % 4
        ga = rowv(attn_norm_g[i])
        if kind == 0:
            scale = (MLA_NOPE + MLA_ROPE) ** -0.5 * LOG2E
            q_p, ckv_p, _, _, kc_p, v_p, krt_p = _mla_proj(h_p, ga, mla_w, tabs_p, b=bp, t=tp, tm=tm_p, with_kv=True,
                                                           scale=scale, name="mla_proj_p")
            o_p = _mla_attn(q_p, kc_p, v_p, b=bp, t=tp, tq=tq, hps=2, name="mla_attn_p")
            q_s, ckv_s, kr_s, krd_s = _mla_proj(h_s, ga, mla_w, tabs_s, b=bs, t=ts, tm=ms, with_kv=False,
                                                scale=scale, name="mla_proj_s")
            o_s = _mla_dec(q_s, cache_mla_ckv.astype(F32).reshape(bs * past, MLA_KV_LORA),
                           _to_features(cache_mla_krope.astype(F32)), ckv_s, krd_s,
                           mla_w["wuk"], mla_w["wuv"], mla_w["gkn"], b=bs, tq=ts, past=past, tk=tkd,
                           name="mla_attn_s")
            outs["mla"] = (ckv_p.reshape(bp, tp, MLA_KV_LORA), jnp.swapaxes(krt_p, 1, 2),
                           ckv_s.reshape(bs, ts, MLA_KV_LORA), kr_s.reshape(bs, ts, MLA_ROPE))
        elif kind == 1:
            scale = FOX_DH ** -0.5 * LOG2E
            gq, gk = _tile_gain(fox_g_q, d), _tile_gain(fox_g_k, d)
            wq, wkt, wv, wvt = split_qkv(fox_w_qkv)
            q_p, kt_p, vt_p, lft_p = _qkv_proj_t(h_p, ga, wq, wkt, wvt, gq, _gain_t(fox_g_k, d, tm_p), b=bp, t=tp,
                                                 tm=tm_p, wft=fox_wft, bft=fox_bft, scale=scale, name="fox_proj_p")
            fk_p = _cumsum_last(lft_p, tk=tq, name="fox_cumsum_p")
            o_p = _pair_attn(q_p, kt_p, vt_p, (fk_p.reshape(bp, FOX_HEADS // 2, 2, tp),), mode="fox",
                             b=bp, t=tp, tq=tq, name="fox_attn_p")
            q_s, k_s, v_s, lf_s = _qkv_proj(h_s, ga, tobf(fox_w_qkv), gq, gk, tm=ms, wf=fox_wf, bf=fox_bf,
                                            scale=scale, name="fox_proj_s")
            lf_all = jnp.concatenate([jnp.swapaxes(cache_fox_logf.astype(F32), 1, 2),
                                      jnp.swapaxes(lf_s.reshape(bs, ts, FOX_HEADS), 1, 2)], axis=2)
            lf_all = jnp.pad(lf_all, ((0, 0), (0, 0), (0, tkw - ts)))
            f_all = _cumsum_last(lf_all, tk=tq, name="fox_cumsum_s")
            o_s = _dec_attn(q_s, _to_features(cache_fox_k.astype(F32)), _to_features(cache_fox_v.astype(F32)),
                            k_s, v_s, (f_all,), mode="fox", b=bs, tq=ts, past=past, tk=tkw, name="fox_attn_s")
            sh = (FOX_HEADS, FOX_DH)
            outs["fox"] = (_to_tokens(kt_p, (FOX_HEADS,)), _to_tokens(vt_p, (FOX_HEADS,)),
                           jnp.swapaxes(lft_p, 1, 2),
                           k_s.reshape(bs, ts, *sh), v_s.reshape(bs, ts, *sh), lf_s.reshape(bs, ts, FOX_HEADS))
        elif kind == 2:
            scale = DIFF_DH ** -0.5 * LOG2E
            lam_init = 0.8 - 0.6 * math.exp(-0.3 * i)
            gq, gk = _tile_gain(diff_g_q, d), _tile_gain(diff_g_k, d)
            wq, wkt, wv, wvt = split_qkv(diff_w_qkv)
            q_p, kt_p, v4_p = _qkv_proj_t(h_p, ga, wq, wkt, wv, gq, _gain_t(diff_g_k, d, tm_p), b=bp, t=tp, tm=tm_p,
                                          rope_tabs=tabs_p, v_tokens=True, scale=scale, name="diff_proj_p")
            v_p = v4_p.reshape(bp, tp, DIFF_HEADS, 2 * DIFF_DH)
            o_p = _pair_attn(q_p, kt_p, v4_p, (lamv, gsub), mode="diff", b=bp, t=tp, tq=tq, lam_init=lam_init,
                             name="diff_attn_p")
            q_s, k_s, v_s = _qkv_proj(h_s, ga, tobf(diff_w_qkv), gq, gk, tm=ms, rope_tabs=tabs_s, scale=scale,
                                      name="diff_proj_s")
            o_s = _dec_attn(q_s, _to_features(cache_diff_k.astype(F32)),
                            cache_diff_v.astype(F32).reshape(bs * past * DIFF_HEADS, 2 * DIFF_DH), k_s, v_s,
                            (lamv, gsub), mode="diff", b=bs, tq=ts, past=past, tk=tkw, lam_init=lam_init,
                            name="diff_attn_s")
            outs["diff"] = (_to_tokens(kt_p, (DIFF_HEADS, 2)), v_p,
                            k_s.reshape(bs, ts, DIFF_HEADS, 2, DIFF_DH), v_s.reshape(bs, ts, DIFF_HEADS, 2 * DIFF_DH))
        else:
            scale = BAND_DH ** -0.5 * LOG2E
            gq, gk = _tile_gain(band_g_q, d), _tile_gain(band_g_k, d)
            wq, wkt, wv, wvt = split_qkv(band_w_qkv)
            keep = min(BAND_LEFT, tp)
            assert keep == tm_p
            q_p, kt_p, vt_p, kt_keep, vt_keep = _qkv_proj_t(
                h_p, ga, wq, wkt, wvt, gq, _gain_t(band_g_k, d, tm_p), b=bp, t=tp, tm=tm_p, tail=True, scale=scale,
                name="band_proj_p")
            tab_pairs = jnp.pad(tab_pad.reshape(BAND_HEADS // 2, 2, REL_PAD), ((0, 0), (0, SUBLANES - 2), (0, 0)))
            o_p = _band_attn(q_p, kt_p, vt_p, tab_pairs, b=bp, t=tp, tq=tq_band, name="band_attn_p")
            q_s, k_s, v_s = _qkv_proj(h_s, ga, tobf(band_w_qkv), gq, gk, tm=ms, scale=scale, name="band_proj_s")
            w = cache_band_k.shape[1]
            kct = _to_features(cache_band_k.astype(F32))
            vct = _to_features(cache_band_v.astype(F32))
            o_s, k_roll, v_roll = _band_dec(q_s, kct, vct, k_s, v_s, tab_pad, b=bs, tq=ts, w=w, past=past,
                                            name="band_attn_s")
            outs["band"] = (_to_tokens(kt_keep, (BAND_HEADS,)), _to_tokens(vt_keep, (BAND_HEADS,)),
                            _to_tokens(k_roll, (BAND_HEADS,)), _to_tokens(v_roll, (BAND_HEADS,)))

        last = i == depth - 1
        gfin = rowv(final_norm_g) if last else None
        gfn = rowv(ffn_norm_g[i])
        wup_i, wdn_i = tobf(ffn_w_up[i]), tobf(ffn_w_down[i])
        h_p, ug, uh = _ffn_rows(h_p, o_p, wo[kind], gfn, wup_i, cwf[i], cbf[i].reshape(1, -1), wdn_i, tm=tm_f,
                                tf=tf, seq=tp, final_g=gfin, name="ffn_p")
        tps = tp // tm_f
        conv_p.append(jnp.concatenate([ug, uh], axis=-1)[tps - 1::tps, SUBLANES - (CONV_W - 1):])
        st = state_ffn_conv[i].astype(F32)
        s1 = jnp.pad(st[:, 1:], ((0, 0), (0, ts - 1), (0, 0))).reshape(ms, 2 * D_FF)
        s2 = jnp.pad(st, ((0, 0), (0, ts - 2), (0, 0))).reshape(ms, 2 * D_FF)
        h_s, ug, uh = _ffn_rows(h_s, o_s, wo[kind], gfn, wup_i, cwf[i], cbf[i].reshape(1, -1), wdn_i, tm=ms,
                                tf=tf, seq=ts, state=(s1, s2), final_g=gfin, name="ffn_s")
        u_s = jnp.concatenate([ug, uh], axis=-1).reshape(bs, ts, 2 * D_FF)
        conv_s.append(jnp.concatenate([st, u_s], axis=1)[:, ts:])

    y_prompt = h_p.reshape(bp, tp, d)
    y_sample = h_s.reshape(bs, ts, d)
    return (y_prompt, y_sample) + outs["mla"] + outs["fox"] + outs["diff"] + outs["band"] + (
        jnp.stack(conv_p, axis=0), jnp.stack(conv_s, axis=0))
```

```python
import functools
import math

import jax
import jax.numpy as jnp
from jax import lax
from jax.experimental import pallas as pl
from jax.experimental.pallas import tpu as pltpu

F32 = jnp.float32
BF16 = jnp.bfloat16

D_MODEL = 1024
CHUNK = 64
CHUNK_SHIFT = CHUNK.bit_length() - 1
ROPE_DIM = 64
ROPE_THETA = 10000.0
NORM_EPS = 1e-6
MLA_HEADS, MLA_Q_LORA, MLA_KV_LORA, MLA_NOPE, MLA_ROPE, MLA_V = 8, 384, 256, 128, 64, 128
FOX_HEADS, FOX_DH = 16, 64
DIFF_HEADS, DIFF_DH = 8, 64
BAND_HEADS, BAND_DH = 16, 64
BAND_LEFT_CHUNKS = 8
BAND_LEFT = BAND_LEFT_CHUNKS * CHUNK
REL_CLIP = 128
D_FF = 2816
CONV_W = 3

LANES = 128
SUBLANES = 8
NEG = -1e30
REL_PAD = 384
LOG2E = math.log2(math.e)
VMEM_LIMIT = 56 * 1024 * 1024


def _cp(n_axes):
    return pltpu.CompilerParams(dimension_semantics=("arbitrary",) * n_axes, vmem_limit_bytes=VMEM_LIMIT)


def _dot(a, b):
    return jnp.dot(a, b, preferred_element_type=F32)


def _dot_nt(a, b):
    return lax.dot_general(a, b, (((1,), (1,)), ((), ())), preferred_element_type=F32)


def _rms(x, g):
    ms = jnp.mean(x * x, axis=-1, keepdims=True)
    return x * lax.rsqrt(ms + NORM_EPS) * g


def _lane_iota(shape):
    return lax.broadcasted_iota(jnp.int32, shape, len(shape) - 1)


def _row_iota(shape):
    return lax.broadcasted_iota(jnp.int32, shape, len(shape) - 2)


def _log2(n):
    assert n & (n - 1) == 0, n
    return n.bit_length() - 1


def _head_rms(x, g, seg):
    n = x.shape[1]
    outs = []
    for c in range(n // LANES):
        xs = x[:, LANES * c:LANES * (c + 1)]
        sq = xs * xs
        if seg == LANES:
            r = lax.rsqrt(jnp.sum(sq, axis=-1, keepdims=True) * (1.0 / seg) + NORM_EPS)
        else:
            lo = _lane_iota(xs.shape) < seg
            s_lo = jnp.sum(jnp.where(lo, sq, 0.0), axis=-1, keepdims=True)
            s_hi = jnp.sum(jnp.where(lo, 0.0, sq), axis=-1, keepdims=True)
            r = jnp.where(lo, lax.rsqrt(s_lo * (1.0 / seg) + NORM_EPS), lax.rsqrt(s_hi * (1.0 / seg) + NORM_EPS))
        outs.append(xs * r * g[:, LANES * c:LANES * (c + 1)])
    return jnp.concatenate(outs, axis=-1) if len(outs) > 1 else outs[0]


def _head_rms_t(x, g, seg=64):
    outs = []
    for hd in range(x.shape[0] // seg):
        xs = x[seg * hd:seg * (hd + 1)]
        r = lax.rsqrt(jnp.sum(xs * xs, axis=0, keepdims=True) * (1.0 / seg) + NORM_EPS)
        outs.append(xs * r * g[seg * hd:seg * (hd + 1)])
    return jnp.concatenate(outs, axis=0) if len(outs) > 1 else outs[0]


def _rope(x, cos, sinp):
    half = ROPE_DIM // 2
    first_half = (_lane_iota((x.shape[0], LANES)) & (ROPE_DIM - 1)) < half
    outs = []
    for c in range(x.shape[1] // LANES):
        xs = x[:, LANES * c:LANES * (c + 1)]
        fwd = pltpu.roll(xs, half, axis=1)
        bwd = pltpu.roll(xs, LANES - half, axis=1)
        outs.append(xs * cos + jnp.where(first_half, bwd, fwd) * sinp)
    return jnp.concatenate(outs, axis=-1) if len(outs) > 1 else outs[0]


def _rope_t(x, cos_t, sin_t):
    outs = []
    for hd in range(x.shape[0] // 64):
        x1 = x[64 * hd:64 * hd + 32]
        x2 = x[64 * hd + 32:64 * (hd + 1)]
        outs += [x1 * cos_t - x2 * sin_t, x2 * cos_t + x1 * sin_t]
    return jnp.concatenate(outs, axis=0)


def _split3(x):
    hi = x.astype(BF16)
    r = x - hi.astype(F32)
    mid = r.astype(BF16)
    lo = (r - mid.astype(F32)).astype(BF16)
    return hi, mid, lo


def _softmax_step(s, pv, m, l, acc):
    m_new = jnp.maximum(m, jnp.max(s, axis=-1, keepdims=True))
    alpha = jnp.exp2(m - m_new)
    p = jnp.exp2(s - m_new)
    if l is not None:
        l = alpha * l + jnp.sum(p, axis=-1, keepdims=True)
    return m_new, l, alpha * acc + pv(p.astype(BF16))


def _softmax_update(s, pv, m_ref, l_ref, acc_ref):
    m, l, acc = _softmax_step(s, pv, m_ref[...], l_ref[...], acc_ref[...])
    m_ref[...] = m
    l_ref[...] = l
    acc_ref[...] = acc


def _log_sigmoid(z):
    return jnp.minimum(z, 0.0) - jnp.log1p(jnp.exp(-jnp.abs(z)))


def _qkv_proj_kernel(*refs, rope, logf, scale):
    it = iter(refs)
    h_ref, g_ref, w_ref, gq_ref, gk_ref = next(it), next(it), next(it), next(it), next(it)
    cos_ref = sin_ref = wf_ref = bf_ref = None
    if rope:
        cos_ref, sin_ref = next(it), next(it)
    if logf:
        wf_ref, bf_ref = next(it), next(it)
    q_out, k_out, v_out = next(it), next(it), next(it)
    a = _rms(h_ref[...], g_ref[...]).astype(BF16)
    qkv = _dot(a, w_ref[...])
    q = _head_rms(qkv[:, :D_MODEL], gq_ref[...], 64)
    k = _head_rms(qkv[:, D_MODEL:2 * D_MODEL], gk_ref[...], 64)
    if rope:
        q = _rope(q, cos_ref[...], sin_ref[...])
        k = _rope(k, cos_ref[...], sin_ref[...])
    q_out[...] = (q * scale).astype(BF16)
    k_out[...] = k
    v_out[...] = qkv[:, 2 * D_MODEL:]
    if logf:
        logf_out = next(it)
        lf = _log_sigmoid(_dot(a, wf_ref[...]) + bf_ref[...])
        logf_out[...] = lf[:, :FOX_HEADS]


def _qkv_proj(h, g, w, gq, gk, *, tm, rope_tabs=None, wf=None, bf=None, scale, name):
    m = h.shape[0]
    grid = (m // tm,)
    row = lambda n: pl.BlockSpec((tm, n), lambda i: (i, 0))
    full = lambda a: pl.BlockSpec(a.shape, lambda i: (0,) * a.ndim)
    ins = [h, g, w, gq, gk]
    specs = [row(D_MODEL), full(g), full(w), full(gq), full(gk)]
    if rope_tabs is not None:
        cos, sinp = rope_tabs
        nt = cos.shape[0] // tm
        tab = pl.BlockSpec((tm, LANES), lambda i: (i % nt, 0))
        ins += [cos, sinp]
        specs += [tab, tab]
    if wf is not None:
        ins += [wf, bf]
        specs += [full(wf), full(bf)]
    outs = [jax.ShapeDtypeStruct((m, D_MODEL), BF16), jax.ShapeDtypeStruct((m, D_MODEL), F32),
            jax.ShapeDtypeStruct((m, D_MODEL), F32)]
    ospecs = [row(D_MODEL), row(D_MODEL), row(D_MODEL)]
    if wf is not None:
        outs.append(jax.ShapeDtypeStruct((m, FOX_HEADS), F32))
        ospecs.append(row(FOX_HEADS))
    kern = functools.partial(_qkv_proj_kernel, rope=rope_tabs is not None, logf=wf is not None, scale=scale)
    return pl.pallas_call(kern, out_shape=outs, grid=grid, in_specs=specs, out_specs=ospecs,
                          compiler_params=_cp(1), name=name)(*ins)


def _qkv_proj_t_kernel(*refs, rope, logf, v_tokens, tail, scale):
    it = iter(refs)
    h_ref, g_ref, wq_ref, wkt_ref, wv_ref, gq_ref, gkt_ref = [next(it) for _ in range(7)]
    if rope:
        cos_ref, sin_ref, cost_ref, sint_ref = [next(it) for _ in range(4)]
    if logf:
        wft_ref, bft_ref = next(it), next(it)
    q_out, kt_out, v_out = next(it), next(it), next(it)
    a = _rms(h_ref[...], g_ref[...]).astype(BF16)
    tm = a.shape[0]
    q = _head_rms(_dot(a, wq_ref[...]), gq_ref[...], 64)
    kt = _head_rms_t(_dot_nt(wkt_ref[...], a), gkt_ref[...])
    if rope:
        q = _rope(q, cos_ref[...], sin_ref[...])
        kt = _rope_t(kt, cost_ref[...], sint_ref[...])
    q_out[...] = (q * scale).astype(BF16)
    kt_out[...] = kt
    if v_tokens:
        v = _dot(a, wv_ref[...])
        nh = D_MODEL // LANES
        for hd in range(nh):
            v_out[pl.ds(hd, tm, stride=nh), :] = v[:, LANES * hd:LANES * (hd + 1)]
    else:
        vt = _dot_nt(wv_ref[...], a)
        v_out[...] = vt
    if logf:
        lft_out = next(it)
        lft_out[...] = _log_sigmoid(_dot_nt(wft_ref[...], a) + bft_ref[...])
    if tail:
        kt_tail, vt_tail = next(it), next(it)
        kt_tail[...] = kt
        vt_tail[...] = vt


def _qkv_proj_t(h, g, wq, wkt, wv, gq, gkt, *, b, t, tm, rope_tabs=None, wft=None, bft=None, v_tokens=False,
                tail=False, scale, name):
    m = b * t
    nt = t // tm
    nh = D_MODEL // LANES
    row = lambda n: pl.BlockSpec((tm, n), lambda i: (i, 0))
    full = lambda a: pl.BlockSpec(a.shape, lambda i: (0,) * a.ndim)
    feat = lambda n: pl.BlockSpec((None, n, tm), lambda i: (i // nt, 0, i % nt))
    ins = [h, g, wq, wkt, wv, gq, gkt]
    specs = [row(D_MODEL)] + [full(x) for x in ins[1:]]
    if rope_tabs is not None:
        cos, sinp, cos_t, sin_t = rope_tabs
        ins += [cos, sinp, cos_t, sin_t]
        tab = pl.BlockSpec((tm, LANES), lambda i: (i % nt, 0))
        tab_t = pl.BlockSpec((32, tm), lambda i: (0, i % nt))
        specs += [tab, tab, tab_t, tab_t]
    if wft is not None:
        ins += [wft, bft]
        specs += [full(wft), full(bft)]
    outs = [jax.ShapeDtypeStruct((m, D_MODEL), BF16), jax.ShapeDtypeStruct((b, D_MODEL, t), F32)]
    ospecs = [row(D_MODEL), feat(D_MODEL)]
    if v_tokens:
        outs.append(jax.ShapeDtypeStruct((m * nh, LANES), F32))
        ospecs.append(pl.BlockSpec((tm * nh, LANES), lambda i: (i, 0)))
    else:
        outs.append(jax.ShapeDtypeStruct((b, D_MODEL, t), F32))
        ospecs.append(feat(D_MODEL))
    if wft is not None:
        outs.append(jax.ShapeDtypeStruct((b, FOX_HEADS, t), F32))
        ospecs.append(feat(FOX_HEADS))
    if tail:
        assert not v_tokens
        outs += [jax.ShapeDtypeStruct((b, D_MODEL, tm), F32)] * 2
        ospecs += [pl.BlockSpec((None, D_MODEL, tm), lambda i: (i // nt, 0, 0))] * 2
    kern = functools.partial(_qkv_proj_t_kernel, rope=rope_tabs is not None, logf=wft is not None,
                             v_tokens=v_tokens, tail=tail, scale=scale)
    return pl.pallas_call(kern, out_shape=outs, grid=(m // tm,), in_specs=specs, out_specs=ospecs,
                          compiler_params=_cp(1), name=name)(*ins)


def _mla_proj_kernel(*refs, with_kv, scale):
    it = iter(refs)
    (h_ref, g_ref, wdq_ref, gq_ref, wuq_ref, wc_ref, wr_ref, gkv_ref, gqn_ref, gqr_ref, gkr_ref,
     cos_ref, sin_ref) = [next(it) for _ in range(13)]
    if with_kv:
        wuk_ref, wuv_ref, gkn_ref, wrt_ref, gkrt_ref, cost_ref, sint_ref = [next(it) for _ in range(7)]
    q_out, ckv_out, kr_out, krd_out = next(it), next(it), next(it), next(it)
    a = _rms(h_ref[...], g_ref[...]).astype(BF16)
    cq = _rms(_dot(a, wdq_ref[...]), gq_ref[...]).astype(BF16)
    q = _dot(cq, wuq_ref[...])
    n_nope = MLA_HEADS * MLA_NOPE
    qn = _head_rms(q[:, :n_nope], gqn_ref[...], MLA_NOPE) * scale
    qr = _rope(_head_rms(q[:, n_nope:], gqr_ref[...], MLA_ROPE), cos_ref[...], sin_ref[...]) * scale
    lo = _lane_iota((q.shape[0], LANES)) < MLA_ROPE
    pieces = []
    for hd in range(MLA_HEADS):
        slab = qr[:, LANES * (hd // 2):LANES * (hd // 2 + 1)]
        keep = lo if hd % 2 == 0 else jnp.logical_not(lo)
        pieces += [qn[:, LANES * hd:LANES * (hd + 1)], jnp.where(keep, slab, 0.0)]
    q_out[...] = jnp.concatenate(pieces, axis=-1).astype(BF16)
    ckv = _rms(_dot(a, wc_ref[...]), gkv_ref[...])
    ckv_out[...] = ckv
    kr2 = _rope(_head_rms(_dot(a, wr_ref[...]), gkr_ref[...], MLA_ROPE), cos_ref[...], sin_ref[...])
    kr_out[...] = kr2[:, :MLA_ROPE]
    krd_out[...] = kr2
    if with_kv:
        kc_out, v_out, krt_out = next(it), next(it), next(it)
        c = ckv.astype(BF16)
        kn = _head_rms(_dot(c, wuk_ref[...]), gkn_ref[...], MLA_NOPE)
        pieces = []
        for hd in range(MLA_HEADS):
            pieces += [kn[:, LANES * hd:LANES * (hd + 1)], kr2]
        kc_out[...] = jnp.concatenate(pieces, axis=-1).astype(BF16)
        v_out[...] = _dot(c, wuv_ref[...]).astype(BF16)
        krt_out[...] = _rope_t(_head_rms_t(_dot_nt(wrt_ref[...], a), gkrt_ref[...]), cost_ref[...], sint_ref[...])


def _mla_proj(h, g, w, rope_tabs, *, b, t, tm, with_kv, scale, name):
    m = h.shape[0]
    cos, sinp = rope_tabs[:2]
    nt = cos.shape[0] // tm
    row = lambda n: pl.BlockSpec((tm, n), lambda i: (i, 0))
    full = lambda a: pl.BlockSpec(a.shape, lambda i: (0,) * a.ndim)
    tab = pl.BlockSpec((tm, LANES), lambda i: (i % nt, 0))
    ins = [h, g, w["wdq"], w["gq"], w["wuq"], w["wc"], w["wr"], w["gkv"], w["gqn"], w["gqr"], w["gkr"], cos, sinp]
    specs = [row(D_MODEL)] + [full(x) for x in ins[1:11]] + [tab, tab]
    qw = MLA_HEADS * 2 * LANES
    outs = [jax.ShapeDtypeStruct((m, qw), BF16), jax.ShapeDtypeStruct((m, MLA_KV_LORA), F32),
            jax.ShapeDtypeStruct((m, MLA_ROPE), F32), jax.ShapeDtypeStruct((m, LANES), F32)]
    ospecs = [row(qw), row(MLA_KV_LORA), row(MLA_ROPE), row(LANES)]
    if with_kv:
        cos_t, sin_t = rope_tabs[2:]
        tab_t = pl.BlockSpec((32, tm), lambda i: (0, i % nt))
        extra = [w["wuk"], w["wuv"], w["gkn"], w["wrt"], w["gkrt"]]
        ins += extra + [cos_t, sin_t]
        specs += [full(x) for x in extra] + [tab_t, tab_t]
        outs += [jax.ShapeDtypeStruct((m, qw), BF16), jax.ShapeDtypeStruct((m, MLA_HEADS * MLA_V), BF16),
                 jax.ShapeDtypeStruct((b, MLA_ROPE, t), F32)]
        ospecs += [row(qw), row(MLA_HEADS * MLA_V),
                   pl.BlockSpec((None, MLA_ROPE, tm), lambda i: (i // nt, 0, i % nt))]
    kern = functools.partial(_mla_proj_kernel, with_kv=with_kv, scale=scale)
    return pl.pallas_call(kern, out_shape=outs, grid=(m // tm,), in_specs=specs, out_specs=ospecs,
                          compiler_params=_cp(1), name=name)(*ins)


def _cumsum_kernel(x_ref, o_ref, *, tk):
    nh, t = x_ref.shape
    tri = (_row_iota((tk, tk)) <= _lane_iota((tk, tk))).astype(BF16)
    carry = jnp.zeros((nh, 1), F32)
    for c in range(t // tk):
        hi, mid, lo = _split3(x_ref[:, tk * c:tk * (c + 1)])
        y = _dot(jnp.concatenate([hi, mid, lo], axis=0), tri)
        f = (y[:nh] + y[nh:2 * nh]) + y[2 * nh:] + carry
        o_ref[:, tk * c:tk * (c + 1)] = f
        carry = f[:, tk - 1:tk]


def _cumsum_last(x, *, tk, name):
    b, nh, t = x.shape
    spec = pl.BlockSpec((None, nh, t), lambda i: (i, 0, 0))
    return pl.pallas_call(functools.partial(_cumsum_kernel, tk=tk), out_shape=jax.ShapeDtypeStruct(x.shape, F32),
                          grid=(b,), in_specs=[spec], out_specs=spec, compiler_params=_cp(1), name=name)(x)


def _mla_attn_kernel(q_ref, k_ref, v_ref, o_ref, *, tq, hps):
    kw = 2 * LANES
    t = k_ref.shape[0]
    for qi in range(t // tq):
        rows = slice(tq * qi, tq * (qi + 1))
        limit = (((qi * tq + _row_iota((tq, 1))) >> CHUNK_SHIFT) + 1) << CHUNK_SHIFT
        outs = []
        n0 = qi * tq
        for h in range(hps):
            q = q_ref[rows, kw * h:kw * (h + 1)]
            kcols = slice(kw * h, kw * (h + 1))
            vcols = slice(MLA_V * h, MLA_V * (h + 1))
            s_d = jnp.where(n0 + _lane_iota((1, tq)) < limit, _dot_nt(q, k_ref[n0:n0 + tq, kcols]), NEG)
            m = jnp.max(s_d, axis=-1, keepdims=True)
            if qi:
                s_f = _dot_nt(q, k_ref[0:n0, kcols])
                m = jnp.maximum(m, jnp.max(s_f, axis=-1, keepdims=True))
            p_d = jnp.exp2(s_d - m)
            acc = _dot(p_d.astype(BF16), v_ref[n0:n0 + tq, vcols])
            l = jnp.sum(p_d, axis=-1, keepdims=True)
            if qi:
                p_f = jnp.exp2(s_f - m)
                acc = acc + _dot(p_f.astype(BF16), v_ref[0:n0, vcols])
                l = l + jnp.sum(p_f, axis=-1, keepdims=True)
            outs.append(acc / l)
        o_ref[rows, :] = jnp.concatenate(outs, axis=-1).astype(BF16)


def _mla_attn(q, kc, v, *, b, t, tq, hps, name):
    kw = 2 * LANES * hps
    vw = MLA_V * hps
    return pl.pallas_call(
        functools.partial(_mla_attn_kernel, tq=tq, hps=hps),
        out_shape=jax.ShapeDtypeStruct((b * t, MLA_HEADS * MLA_V), BF16),
        grid=(b, MLA_HEADS // hps),
        in_specs=[pl.BlockSpec((t, kw), lambda i, h: (i, h)),
                  pl.BlockSpec((t, kw), lambda i, h: (i, h)),
                  pl.BlockSpec((t, vw), lambda i, h: (i, h))],
        out_specs=pl.BlockSpec((t, vw), lambda i, h: (i, h)),
        compiler_params=_cp(2), name=name)(q, kc, v)


def _ones_row_values(vt, c):
    row = _row_iota(vt.shape)
    if c == 0:
        return jnp.where(row < 64, vt, jnp.where(row == 64, 1.0, 0.0)).astype(BF16)
    return jnp.where(row >= 64, vt, jnp.where(row == 0, 1.0, 0.0)).astype(BF16)


def _merge_pair(acc0, acc1):
    lo = _lane_iota(acc0.shape) < 64
    return jnp.where(lo, acc0 / acc0[:, 64:65], acc1 / acc1[:, 0:1])


def _pair_attn_kernel(*refs, mode, tq, lam_init):
    if mode == "fox":
        q_ref, kt_ref, vt_ref, fk_ref, o_ref, ka, va = refs
    else:
        q_ref, kt_ref, v_ref, lam_ref, gsub_ref, o_ref, ka, vb = refs
    t = kt_ref.shape[1]
    if mode == "fox":
        ka[0:LANES, :] = kt_ref[...].astype(BF16)
        f = fk_ref[...] * (-LOG2E)
        terms = [x.astype(F32) for x in _split3(f)]
        nr = 2 * SUBLANES
        row = _row_iota((nr, t))
        aug = jnp.zeros((nr, t), F32)
        for c in range(2):
            for j in range(3):
                aug = jnp.where(row == 3 * c + j, terms[j][c:c + 1, :], aug)
        ka[LANES:LANES + nr, :] = aug.astype(BF16)
        ka[LANES + nr:, :] = jnp.zeros((LANES - nr, t), BF16)
        vt = vt_ref[...]
        va[0] = _ones_row_values(vt, 0)
        va[1] = _ones_row_values(vt, 1)
    else:
        ka[...] = kt_ref[...].astype(BF16)
        nhv = v_ref.shape[0] // t
        vb[0:LANES, :] = v_ref[pl.ds(pl.program_id(1), t, stride=nhv), :].T.astype(BF16)
        vb[LANES:, :] = jnp.where(_row_iota((LANES, t)) == 0, 1.0, 0.0).astype(BF16)

    def q_tile(qi, q):
        lane = _lane_iota((tq, LANES))
        zero = jnp.zeros_like(q)
        qa = [jnp.where(lane < 64, q, zero), jnp.where(lane < 64, zero, q)]
        qpos = qi * tq + _row_iota((tq, 1))
        if mode == "fox":
            limit = qpos + 1
            pick = [jnp.where(lane < 3 * c, 0.0, jnp.where(lane < 3 * c + 3, 1.0, 0.0)).astype(BF16)
                    for c in range(2)]
            qa = [jnp.concatenate([qa[c], pick[c]], axis=-1) for c in range(2)]
        else:
            limit = ((qpos >> CHUNK_SHIFT) + 1) << CHUNK_SHIFT

        n0 = qi * tq
        accs = []
        for c in range(2):
            s_d = jnp.where(n0 + _lane_iota((1, tq)) < limit, _dot(qa[c], ka[:, n0:n0 + tq]), NEG)
            m = jnp.max(s_d, axis=-1, keepdims=True)
            if qi:
                s_f = _dot(qa[c], ka[:, 0:n0])
                m = jnp.maximum(m, jnp.max(s_f, axis=-1, keepdims=True))
            parts = [(jnp.exp2(s_d - m), n0, n0 + tq)] + ([(jnp.exp2(s_f - m), 0, n0)] if qi else [])
            acc = 0.0
            for p, lo_k, hi_k in parts:
                vals = va[c, :, lo_k:hi_k] if mode == "fox" else vb[:, lo_k:hi_k]
                acc = acc + _dot_nt(p.astype(BF16), vals)
            accs.append(acc)
        if mode == "fox":
            o = _merge_pair(*accs)
        else:
            lv = lam_ref[...]
            lam = (jnp.exp(jnp.sum(lv[0:1] * lv[1:2], axis=-1, keepdims=True))
                   - jnp.exp(jnp.sum(lv[2:3] * lv[3:4], axis=-1, keepdims=True)) + lam_init)
            o0, o1 = [a[:, :LANES] / a[:, LANES:LANES + 1] for a in accs]
            o = _rms(o0 - lam * o1, gsub_ref[...]) * (1.0 - lam_init)
        return o.astype(BF16)

    for qi in range(t // tq):
        o_ref[tq * qi:tq * (qi + 1), :] = q_tile(qi, q_ref[tq * qi:tq * (qi + 1), :])


def _pair_attn(q, kt, v, extra, *, mode, b, t, tq, lam_init=0.0, name):
    npair = D_MODEL // LANES
    qspec = pl.BlockSpec((t, LANES), lambda i, p: (i, p))
    ktspec = pl.BlockSpec((None, LANES, t), lambda i, p: (i, p, 0))
    if mode == "fox":
        vspec = ktspec
        especs = [pl.BlockSpec((None, None, 2, t), lambda i, p: (i, p, 0, 0))]
        scratch = [pltpu.VMEM((2 * LANES, t), BF16), pltpu.VMEM((2, LANES, t), BF16)]
    else:
        vspec = pl.BlockSpec((t * npair, LANES), lambda i, p: (i, 0))
        especs = [pl.BlockSpec(x.shape, lambda i, p: (0, 0)) for x in extra]
        scratch = [pltpu.VMEM((LANES, t), BF16), pltpu.VMEM((2 * LANES, t), BF16)]
    return pl.pallas_call(
        functools.partial(_pair_attn_kernel, mode=mode, tq=tq, lam_init=lam_init),
        out_shape=jax.ShapeDtypeStruct((b * t, D_MODEL), BF16),
        grid=(b, npair),
        in_specs=[qspec, ktspec, vspec] + especs,
        out_specs=qspec,
        scratch_shapes=scratch,
        compiler_params=_cp(2), name=name)(q, kt, v, *extra)


def _rel_gather(tab, width, center):
    idx = jnp.clip(center - _lane_iota((REL_PAD, width)), -REL_CLIP, REL_CLIP) + REL_CLIP
    onehot = (_row_iota((REL_PAD, width)) == idx).astype(BF16)
    hi, mid, lo = _split3(tab)
    return (_dot(hi, onehot) + _dot(mid, onehot)) + _dot(lo, onehot)


def _band_attn_kernel(q_ref, kt_ref, vt_ref, tab_ref, o_ref, kb, va, bias_ref, *, tq, win, bw):
    gw = bw + tq

    @pl.when(pl.program_id(1) == 0)
    def _():
        g = _rel_gather(tab_ref[...], gw, BAND_LEFT + tq) * LOG2E
        ii = _row_iota((tq, bw)) >> CHUNK_SHIFT
        jj = _lane_iota((tq, bw)) >> CHUNK_SHIFT
        allowed = (jj >= ii) & (jj <= ii + BAND_LEFT_CHUNKS)
        for c in range(2):
            rows = jnp.broadcast_to(g[c:c + 1, :], (tq, gw))
            skew = pltpu.roll(rows, gw - tq, axis=1, stride=1, stride_axis=0)
            bias_ref[c] = jnp.where(allowed, skew[:, :bw], NEG)

    kb[...] = kt_ref[...].astype(BF16)
    vt = vt_ref[...]
    va[0] = _ones_row_values(vt, 0)
    va[1] = _ones_row_values(vt, 1)

    lo = _lane_iota((tq, LANES)) < 64
    for r in range(q_ref.shape[0] // tq):
        q0 = r * tq
        ws = max(q0 - BAND_LEFT, 0)
        d = BAND_LEFT - q0 + ws
        k = kb[:, pl.ds(ws, win)]
        q = q_ref[tq * r:tq * (r + 1), :]
        zero = jnp.zeros_like(q)
        accs = []
        for c in range(2):
            qc = jnp.where(lo, q, zero) if c == 0 else jnp.where(lo, zero, q)
            s = _dot(qc, k) + bias_ref[c, :, pl.ds(d, win)]
            p = jnp.exp2(s - jnp.max(s, axis=-1, keepdims=True))
            accs.append(_dot_nt(p.astype(BF16), va[c, :, pl.ds(ws, win)]))
        o_ref[tq * r:tq * (r + 1), :] = _merge_pair(*accs).astype(BF16)


def _band_attn(q, kt, vt, tab, *, b, t, tq, name):
    npair = D_MODEL // LANES
    win = BAND_LEFT + tq
    bw = win + BAND_LEFT
    assert t >= win and tq % CHUNK == 0 and BAND_LEFT % tq == 0
    qspec = pl.BlockSpec((t, LANES), lambda p, i: (i, p))
    kvspec = pl.BlockSpec((None, LANES, t), lambda p, i: (i, p, 0))
    return pl.pallas_call(
        functools.partial(_band_attn_kernel, tq=tq, win=win, bw=bw),
        out_shape=jax.ShapeDtypeStruct((b * t, D_MODEL), BF16),
        grid=(npair, b),
        in_specs=[qspec, kvspec, kvspec, pl.BlockSpec((None, SUBLANES, REL_PAD), lambda p, i: (p, 0, 0))],
        out_specs=qspec,
        scratch_shapes=[pltpu.VMEM((LANES, t), BF16), pltpu.VMEM((2, LANES, t), BF16), pltpu.VMEM((2, tq, bw), F32)],
        compiler_params=_cp(2), name=name)(q, kt, vt, tab)


def _block_diag_q(q, nh, width):
    tq = q.shape[0]
    rep = jnp.concatenate([q] * nh, axis=0)
    keep = (_row_iota(rep.shape) >> _log2(tq)) == (_lane_iota(rep.shape) >> _log2(width))
    return jnp.where(keep, rep, jnp.zeros_like(rep))


def _expand_rows(f, tq):
    return jnp.concatenate([jnp.broadcast_to(f[h:h + 1, :], (tq, f.shape[1])) for h in range(f.shape[0])], axis=0)


def _dec_attn_kernel(*refs, mode, n_cache, tq, past, lam_init):
    if mode == "fox":
        (q_ref, kc_ref, vc_ref, kn_ref, vn_ref, fkc_ref, fkn_ref, o_ref, qb, m_ref, l_ref, acc_ref) = refs
    else:
        (q_ref, kc_ref, vc_ref, kn_ref, vn_ref, lam_ref, gsub_ref, o_ref, qb, m_ref, l_ref, acc_ref) = refs
    t = pl.program_id(1)
    nh = D_MODEL // 64
    rows = nh * tq

    @pl.when(t == 0)
    def _():
        qb[...] = _block_diag_q(q_ref[...], nh, 64)
        m_ref[...] = jnp.full_like(m_ref, NEG)
        l_ref[...] = jnp.zeros_like(l_ref)
        acc_ref[...] = jnp.zeros_like(acc_ref)

    def cache_step():
        s = _dot(qb[...], kc_ref[...].astype(BF16))
        if mode == "fox":
            s = s - _expand_rows(fkc_ref[...] * LOG2E, tq)
            v = vc_ref[...].astype(BF16)
            _softmax_update(s, lambda pb: _dot_nt(pb, v), m_ref, l_ref, acc_ref)
        else:
            tk = kc_ref.shape[1]
            nhv = D_MODEL // LANES
            v = jnp.concatenate([vc_ref[pl.ds(h, tk, stride=nhv), :].astype(BF16) for h in range(nhv)], axis=-1)
            _softmax_update(s, lambda pb: _dot(pb, v), m_ref, l_ref, acc_ref)

    cache_step()

    @pl.when(t == n_cache - 1)
    def _():
        s = _dot_nt(qb[...], kn_ref[...].astype(BF16))
        qpos = past + (_row_iota((rows, tq)) & (tq - 1))
        kpos = past + _lane_iota((rows, tq))
        if mode == "fox":
            s = s - _expand_rows(fkn_ref[...][:, :tq] * LOG2E, tq)
            allowed = kpos <= qpos
        else:
            allowed = (kpos >> CHUNK_SHIFT) <= (qpos >> CHUNK_SHIFT)
        vn = vn_ref[...].astype(BF16)
        _softmax_update(jnp.where(allowed, s, NEG), lambda pb: _dot(pb, vn), m_ref, l_ref, acc_ref)
        o_all = acc_ref[...] / l_ref[...]
        if mode == "fox":
            o = jnp.zeros((tq, D_MODEL), F32)
            hl = _lane_iota((tq, D_MODEL)) >> _log2(FOX_DH)
            for h in range(nh):
                o = jnp.where(hl == h, o_all[h * tq:(h + 1) * tq, :], o)
        else:
            lv = lam_ref[...]
            lam = (jnp.exp(jnp.sum(lv[0:1] * lv[1:2], axis=-1, keepdims=True))
                   - jnp.exp(jnp.sum(lv[2:3] * lv[3:4], axis=-1, keepdims=True)) + lam_init)
            pieces = []
            for h in range(nh // 2):
                a0 = o_all[(2 * h) * tq:(2 * h + 1) * tq, LANES * h:LANES * (h + 1)]
                a1 = o_all[(2 * h + 1) * tq:(2 * h + 2) * tq, LANES * h:LANES * (h + 1)]
                pieces.append(_rms(a0 - lam * a1, gsub_ref[...]) * (1.0 - lam_init))
            o = jnp.concatenate(pieces, axis=-1)
        o_ref[...] = o.astype(BF16)


def _dec_attn(q, kc, vc, kn, vn, extra, *, mode, b, tq, past, tk, lam_init=0.0, name):
    n_cache = past // tk
    nh = D_MODEL // 64
    new = pl.BlockSpec((tq, D_MODEL), lambda i, t: (i, 0))
    cache_t = pl.BlockSpec((None, D_MODEL, tk), lambda i, t: (i, 0, t))
    if mode == "fox":
        (fk,) = extra
        vspec = cache_t
        especs = [pl.BlockSpec((None, nh, tk), lambda i, t: (i, 0, t)),
                  pl.BlockSpec((None, nh, LANES), lambda i, t: (i, 0, past // LANES))]
        ins = [fk, fk]
    else:
        nhv = D_MODEL // LANES
        vspec = pl.BlockSpec((tk * nhv, LANES), lambda i, t: (i * n_cache + t, 0))
        especs = [pl.BlockSpec(x.shape, lambda i, t: (0, 0)) for x in extra]
        ins = list(extra)
    return pl.pallas_call(
        functools.partial(_dec_attn_kernel, mode=mode, n_cache=n_cache, tq=tq, past=past, lam_init=lam_init),
        out_shape=jax.ShapeDtypeStruct((b * tq, D_MODEL), BF16),
        grid=(b, n_cache),
        in_specs=[new, cache_t, vspec, new, new] + especs,
        out_specs=new,
        scratch_shapes=[pltpu.VMEM((nh * tq, D_MODEL), BF16), pltpu.VMEM((nh * tq, 1), F32),
                        pltpu.VMEM((nh * tq, 1), F32), pltpu.VMEM((nh * tq, D_MODEL), F32)],
        compiler_params=_cp(2), name=name)(q, kc, vc, kn, vn, *ins)


def _mla_dec_kernel(q_ref, cc_ref, rc_ref, cn_ref, rn_ref, wuk_ref, wuv_ref, gkn_ref, o_ref,
                    qn, qr, m_ref, l_ref, acc_ref, *, n_cache, tq, past):
    t = pl.program_id(1)
    nh = MLA_HEADS
    rows = nh * tq

    @pl.when(t == 0)
    def _():
        q = q_ref[...]
        zero = jnp.zeros((tq, LANES), BF16)
        for h in range(nh):
            qn[h * tq:(h + 1) * tq, :] = jnp.concatenate(
                [q[:, 2 * LANES * h:2 * LANES * h + LANES] if c == h else zero for c in range(nh)], axis=-1)
            qr[h * tq:(h + 1) * tq, :] = q[:, 2 * LANES * h + LANES:2 * LANES * (h + 1)]
        m_ref[...] = jnp.full_like(m_ref, NEG)
        l_ref[...] = jnp.zeros_like(l_ref)
        acc_ref[...] = jnp.zeros_like(acc_ref)

    def step(ckv, s_rope, mask_new):
        c = ckv.astype(BF16)
        kn = _head_rms(_dot(c, wuk_ref[...]), gkn_ref[...], MLA_NOPE).astype(BF16)
        s = _dot_nt(qn[...], kn) + s_rope
        if mask_new:
            n = ckv.shape[0]
            qpos = past + (_row_iota((rows, n)) & (tq - 1))
            kpos = past + _lane_iota((rows, n))
            s = jnp.where((kpos >> CHUNK_SHIFT) <= (qpos >> CHUNK_SHIFT), s, NEG)
        _softmax_update(s, lambda pb: _dot(pb, c), m_ref, l_ref, acc_ref)

    def cache_step():
        kr = rc_ref[...].astype(BF16)
        step(cc_ref[...], _dot(qr[...], jnp.concatenate([kr, kr], axis=0)), False)

    cache_step()

    @pl.when(t == n_cache - 1)
    def _():
        step(cn_ref[...], _dot_nt(qr[...], rn_ref[...].astype(BF16)), True)
        lat = (acc_ref[...] / l_ref[...]).astype(BF16)
        o_ref[...] = jnp.concatenate(
            [_dot(lat[h * tq:(h + 1) * tq, :], wuv_ref[:, MLA_V * h:MLA_V * (h + 1)]) for h in range(nh)],
            axis=-1).astype(BF16)


def _mla_dec(q, cc, rc, cn, rn, wuk, wuv, gkn, *, b, tq, past, tk, name):
    n_cache = past // tk
    nh = MLA_HEADS
    new = lambda n: pl.BlockSpec((tq, n), lambda i, t: (i, 0))
    full = lambda a: pl.BlockSpec(a.shape, lambda i, t: (0,) * a.ndim)
    return pl.pallas_call(
        functools.partial(_mla_dec_kernel, n_cache=n_cache, tq=tq, past=past),
        out_shape=jax.ShapeDtypeStruct((b * tq, nh * MLA_V), BF16),
        grid=(b, n_cache),
        in_specs=[new(nh * 2 * LANES),
                  pl.BlockSpec((tk, MLA_KV_LORA), lambda i, t: (i * n_cache + t, 0)),
                  pl.BlockSpec((None, MLA_ROPE, tk), lambda i, t: (i, 0, t)),
                  new(MLA_KV_LORA), new(LANES), full(wuk), full(wuv), full(gkn)],
        out_specs=new(nh * MLA_V),
        scratch_shapes=[pltpu.VMEM((nh * tq, nh * MLA_NOPE), BF16), pltpu.VMEM((nh * tq, LANES), BF16),
                        pltpu.VMEM((nh * tq, 1), F32), pltpu.VMEM((nh * tq, 1), F32),
                        pltpu.VMEM((nh * tq, MLA_KV_LORA), F32)],
        compiler_params=_cp(2), name=name)(q, cc, rc, cn, rn, wuk, wuv, gkn)


def _band_dec_kernel(q_ref, kc_ref, vc_ref, kn_ref, vn_ref, knt_ref, vnt_ref, tab_ref, o_ref, kr_out, vr_out,
                     bias_ref, *, tq, w, past):
    keep_old = _lane_iota((D_MODEL, LANES)) < LANES - tq
    for c_ref, nt_ref, r_out in ((kc_ref, knt_ref, kr_out), (vc_ref, vnt_ref, vr_out)):
        rolled = pltpu.roll(c_ref[...], w - tq, axis=1)
        r_out[:, :w - LANES] = rolled[:, :w - LANES]
        r_out[:, w - LANES:] = jnp.where(keep_old, rolled[:, w - LANES:], nt_ref[...])
    nh = BAND_HEADS
    rows = nh * tq
    bwid = ((w + tq + LANES - 1) // LANES) * LANES
    qb = _block_diag_q(q_ref[...], nh, BAND_DH)

    @pl.when(pl.program_id(0) == 0)
    def _():
        g = _rel_gather(tab_ref[...], bwid, w + tq) * LOG2E
        bias = jnp.concatenate(
            [pltpu.roll(jnp.broadcast_to(g[h:h + 1, :], (tq, bwid)), bwid - tq, axis=1, stride=1, stride_axis=0)
             for h in range(nh)], axis=0)
        qc = (past + (_row_iota((rows, bwid)) & (tq - 1))) >> CHUNK_SHIFT
        kc = (past - w + _lane_iota((rows, bwid))) >> CHUNK_SHIFT
        bias_ref[...] = jnp.where((kc <= qc) & (kc >= qc - BAND_LEFT_CHUNKS), bias, NEG)

    s_c = _dot(qb, kc_ref[...].astype(BF16)) + bias_ref[:, :w]
    s_n = _dot_nt(qb, kn_ref[...].astype(BF16)) + bias_ref[:, w:w + tq]
    m = jnp.maximum(jnp.max(s_c, axis=-1, keepdims=True), jnp.max(s_n, axis=-1, keepdims=True))
    p_c = jnp.exp2(s_c - m)
    p_n = jnp.exp2(s_n - m)
    l = jnp.sum(p_c, axis=-1, keepdims=True) + jnp.sum(p_n, axis=-1, keepdims=True)
    o_all = (_dot_nt(p_c.astype(BF16), vc_ref[...].astype(BF16))
             + _dot(p_n.astype(BF16), vn_ref[...].astype(BF16))) / l
    o = jnp.zeros((tq, D_MODEL), F32)
    hl = _lane_iota((tq, D_MODEL)) >> _log2(BAND_DH)
    for h in range(nh):
        o = jnp.where(hl == h, o_all[h * tq:(h + 1) * tq, :], o)
    o_ref[...] = o.astype(BF16)


def _band_dec(q, kc, vc, kn, vn, tab, *, b, tq, w, past, name):
    assert w % LANES == 0 and tq <= LANES
    new = pl.BlockSpec((tq, D_MODEL), lambda i: (i, 0))
    cache = pl.BlockSpec((None, D_MODEL, w), lambda i: (i, 0, 0))
    slab = pl.BlockSpec((None, D_MODEL, LANES), lambda i: (i, 0, 0))
    right = lambda x: jnp.pad(jnp.swapaxes(x.reshape(b, tq, D_MODEL), 1, 2), ((0, 0), (0, 0), (LANES - tq, 0)))
    return pl.pallas_call(
        functools.partial(_band_dec_kernel, tq=tq, w=w, past=past),
        out_shape=[jax.ShapeDtypeStruct((b * tq, D_MODEL), BF16), jax.ShapeDtypeStruct(kc.shape, F32),
                   jax.ShapeDtypeStruct(vc.shape, F32)],
        grid=(b,),
        in_specs=[new, cache, cache, new, new, slab, slab, pl.BlockSpec(tab.shape, lambda i: (0, 0))],
        out_specs=[new, cache, cache],
        scratch_shapes=[pltpu.VMEM((BAND_HEADS * tq, pl.cdiv(w + tq, LANES) * LANES), F32)],
        compiler_params=_cp(1), name=name)(q, kc, vc, kn, vn, right(kn), right(vn), tab)


def _ffn_rows_kernel(*refs, mode, tps, seq, final, tf):
    it = iter(refs)
    h_ref, o_ref, wo_ref, gf_ref, wup_ref, cw_ref, cb_ref, wd_ref = [next(it) for _ in range(8)]
    if mode == "state":
        s1_ref, s2_ref = next(it), next(it)
    if final:
        gfin_ref = next(it)
    out_ref, ug_out, uh_out = next(it), next(it), next(it)
    if mode == "carry":
        carry_ref = next(it)
    tm = h_ref.shape[0]
    h1 = h_ref[...] + _dot(o_ref[...], wo_ref[...])
    xn = _rms(h1, gf_ref[...]).astype(BF16)
    if mode == "carry":
        @pl.when(pl.program_id(0) % tps == 0)
        def _():
            carry_ref[...] = jnp.zeros_like(carry_ref)
    else:
        tpos = _row_iota((tm, tf)) & (seq - 1)
    acts = []
    for j in range(D_FF // tf):
        cs = []
        for part, u_out in enumerate((ug_out, uh_out)):
            tile = slice(tf * j, tf * (j + 1))
            cols = slice(part * D_FF + tf * j, part * D_FF + tf * (j + 1))
            u = _dot(xn, wup_ref[:, cols])
            if mode == "carry":
                tail = u[tm - SUBLANES:]
                u_out[:, tile] = tail
                ext = jnp.concatenate([carry_ref[part, :, tile], u], axis=0)
                carry_ref[part, :, tile] = tail
                um1 = pltpu.roll(ext, 1, axis=0)[SUBLANES:]
                um2 = pltpu.roll(ext, 2, axis=0)[SUBLANES:]
            else:
                u_out[:, tile] = u
                um1 = jnp.where(tpos >= 1, pltpu.roll(u, 1, axis=0), s1_ref[:, cols])
                um2 = jnp.where(tpos >= 2, pltpu.roll(u, 2, axis=0), s2_ref[:, cols])
            cw = cw_ref[:, cols]
            cs.append(((cb_ref[:, cols] + u * cw[2:3]) + um2 * cw[0:1]) + um1 * cw[1:2])
        acts.append((cs[0] * jax.nn.sigmoid(cs[0]) * cs[1]).astype(BF16))
    acc = h1 + _dot(jnp.concatenate(acts, axis=-1), wd_ref[...])
    if final:
        acc = _rms(acc, gfin_ref[...])
    out_ref[...] = acc


def _ffn_rows(h, o, wo, gf, wup, cw, cb, wd, *, layer, tm, tf, seq, state=None, final_g=None, name):
    m = h.shape[0]
    mode = "carry" if state is None else "state"
    tps = seq // tm if mode == "carry" else 1
    row = lambda n: pl.BlockSpec((tm, n), lambda i: (i, 0))
    once = lambda a: pl.BlockSpec(a.shape, lambda i: (0,) * a.ndim, pipeline_mode=pl.Buffered(1))
    of_layer = lambda a: pl.BlockSpec((None,) + a.shape[1:], lambda i: (layer,) + (0,) * (a.ndim - 1),
                                      pipeline_mode=pl.Buffered(1))
    ins = [h, o, wo, gf, wup, cw, cb, wd]
    specs = [row(D_MODEL), row(D_MODEL), once(wo), once(gf), of_layer(wup), once(cw), once(cb), of_layer(wd)]
    scratch = []
    if mode == "state":
        assert m == tm and seq & (seq - 1) == 0
        ins += list(state)
        specs += [row(2 * D_FF), row(2 * D_FF)]
        u_shape = jax.ShapeDtypeStruct((m, D_FF), F32)
        u_spec = row(D_FF)
    else:
        assert seq % tm == 0
        scratch.append(pltpu.VMEM((2, SUBLANES, D_FF), F32))
        u_shape = jax.ShapeDtypeStruct((m // tm, SUBLANES, D_FF), F32)
        u_spec = pl.BlockSpec((None, SUBLANES, D_FF), lambda i: (i, 0, 0))
    if final_g is not None:
        ins.append(final_g)
        specs.append(once(final_g))
    kern = functools.partial(_ffn_rows_kernel, mode=mode, tps=tps, seq=seq, final=final_g is not None, tf=tf)
    return pl.pallas_call(
        kern, out_shape=[jax.ShapeDtypeStruct((m, D_MODEL), F32), u_shape, u_shape],
        grid=(m // tm,), in_specs=specs, out_specs=[row(D_MODEL), u_spec, u_spec], scratch_shapes=scratch,
        compiler_params=_cp(1), name=name)(*ins)


def _rope_tables(pos):
    inv = 1.0 / (ROPE_THETA ** (jnp.arange(0, 64, 2, dtype=F32) / 64))
    ang = pos.astype(F32)[:, None] * inv[None, :]
    c, s = jnp.cos(ang), jnp.sin(ang)
    return jnp.tile(c, (1, 4)), jnp.tile(jnp.concatenate([-s, s], axis=1), (1, 2)), c.T, s.T


def _tile_gain(g, n):
    return jnp.tile(g.astype(F32), n // g.shape[0]).reshape(1, n)


def _gain_t(g, n, w):
    return jnp.broadcast_to(jnp.tile(g.astype(F32), n // g.shape[0])[:, None], (n, w))


def _to_tokens(xt, heads):
    b, n, t = xt.shape
    dh = n // math.prod(heads)
    nd = len(heads)
    return xt.reshape(b, *heads, dh, t).transpose(0, nd + 2, *range(1, nd + 2))


def _to_features(x):
    b, t = x.shape[:2]
    nd = x.ndim
    return x.transpose(0, *range(2, nd), 1).reshape(b, -1, t)


def kernel(x_prompt, x_sample, cache_mla_ckv, cache_mla_krope, cache_fox_k, cache_fox_v, cache_fox_logf,
           cache_diff_k, cache_diff_v, cache_band_k, cache_band_v, state_ffn_conv,
           attn_norm_g, ffn_norm_g, final_norm_g,
           mla_w_dq, mla_g_q, mla_w_uq, mla_w_dkv, mla_g_kv, mla_w_uk, mla_w_uv,
           mla_g_qn, mla_g_qr, mla_g_kn, mla_g_kr, mla_w_o,
           fox_w_qkv, fox_w_f, fox_b_f, fox_g_q, fox_g_k, fox_w_o,
           diff_w_qkv, diff_g_q, diff_g_k, diff_lq1, diff_lk1, diff_lq2, diff_lk2, diff_g_sub, diff_w_o,
           band_w_qkv, band_g_q, band_g_k, band_rel_bias, band_w_o,
           ffn_w_up, ffn_conv_w, ffn_conv_b, ffn_w_down):
    bp, tp, d = x_prompt.shape
    bs, ts, _ = x_sample.shape
    past = cache_mla_ckv.shape[1]
    depth = attn_norm_g.shape[0]
    mp, ms = bp * tp, bs * ts
    tm_p = min(512, tp)
    tm_f = min(512, tp)
    tq = 512
    tq_band = 256
    tkd = min(1024, past)
    tkw = min(2048, past)
    tf = 256
    assert tp % tq == 0 and past % tkd == 0 and past % tkw == 0 and past % CHUNK == 0

    tobf = lambda a: a.astype(BF16)
    rowv = lambda g: g.astype(F32).reshape(1, -1)
    pos_p = jnp.arange(tp, dtype=jnp.int32)
    pos_s = past + jnp.arange(ts, dtype=jnp.int32)
    tabs_p = _rope_tables(pos_p)
    tabs_s = tuple(jnp.tile(a, (bs, 1)) for a in _rope_tables(pos_s)[:2])

    nope_cols = jnp.arange(MLA_HEADS)[:, None] * (MLA_NOPE + MLA_ROPE) + jnp.arange(MLA_NOPE)[None, :]
    rope_cols = jnp.arange(MLA_HEADS)[:, None] * (MLA_NOPE + MLA_ROPE) + MLA_NOPE + jnp.arange(MLA_ROPE)[None, :]
    wuq_perm = jnp.concatenate([mla_w_uq[:, nope_cols.reshape(-1)], mla_w_uq[:, rope_cols.reshape(-1)]], axis=1)
    wr = mla_w_dkv[:, MLA_KV_LORA:]
    mla_w = dict(
        wdq=tobf(mla_w_dq), gq=rowv(mla_g_q), wuq=tobf(wuq_perm), wc=tobf(mla_w_dkv[:, :MLA_KV_LORA]),
        wr=tobf(jnp.concatenate([wr, wr], axis=1)), gkv=rowv(mla_g_kv),
        gqn=_tile_gain(mla_g_qn, MLA_HEADS * MLA_NOPE), gqr=_tile_gain(mla_g_qr, MLA_HEADS * MLA_ROPE),
        gkr=_tile_gain(mla_g_kr, LANES), wuk=tobf(mla_w_uk), wuv=tobf(mla_w_uv),
        gkn=_tile_gain(mla_g_kn, MLA_HEADS * MLA_NOPE), wrt=tobf(wr.T), gkrt=_gain_t(mla_g_kr, MLA_ROPE, tm_p))

    def split_qkv(w):
        wq, wk, wv = w[:, :d], w[:, d:2 * d], w[:, 2 * d:]
        return tobf(wq), tobf(wk.T), tobf(wv), tobf(wv.T)

    fox_wf = tobf(jnp.pad(fox_w_f, ((0, 0), (0, LANES - FOX_HEADS))))
    fox_bf = jnp.pad(fox_b_f.astype(F32), (0, LANES - FOX_HEADS)).reshape(1, LANES)
    fox_wft = tobf(fox_w_f.T)
    fox_bft = jnp.broadcast_to(fox_b_f.astype(F32)[:, None], (FOX_HEADS, tm_p))
    lamv = jnp.stack([diff_lq1, diff_lk1, diff_lq2, diff_lk2]).astype(F32)
    gsub = rowv(diff_g_sub)
    tab_pad = jnp.pad(band_rel_bias.astype(F32), ((0, 0), (0, REL_PAD - band_rel_bias.shape[1])))
    wo = [tobf(mla_w_o), tobf(fox_w_o), tobf(diff_w_o), tobf(band_w_o)]
    wup, wdn = tobf(ffn_w_up), tobf(ffn_w_down)
    cwf, cbf = ffn_conv_w.astype(F32), ffn_conv_b.astype(F32)

    h_p = x_prompt.reshape(mp, d)
    h_s = x_sample.reshape(ms, d)
    outs = {}
    conv_p, conv_s = [], []
    for i in range(depth):
        kind = i % 4
        ga = rowv(attn_norm_g[i])
        if kind == 0:
            scale = (MLA_NOPE + MLA_ROPE) ** -0.5 * LOG2E
            q_p, ckv_p, _, _, kc_p, v_p, krt_p = _mla_proj(h_p, ga, mla_w, tabs_p, b=bp, t=tp, tm=tm_p, with_kv=True,
                                                           scale=scale, name="mla_proj_p")
            o_p = _mla_attn(q_p, kc_p, v_p, b=bp, t=tp, tq=tq, hps=2, name="mla_attn_p")
            q_s, ckv_s, kr_s, krd_s = _mla_proj(h_s, ga, mla_w, tabs_s, b=bs, t=ts, tm=ms, with_kv=False,
                                                scale=scale, name="mla_proj_s")
            o_s = _mla_dec(q_s, cache_mla_ckv.astype(F32).reshape(bs * past, MLA_KV_LORA),
                           _to_features(cache_mla_krope.astype(F32)), ckv_s, krd_s,
                           mla_w["wuk"], mla_w["wuv"], mla_w["gkn"], b=bs, tq=ts, past=past, tk=tkd,
                           name="mla_attn_s")
            outs["mla"] = (ckv_p.reshape(bp, tp, MLA_KV_LORA), jnp.swapaxes(krt_p, 1, 2),
                           ckv_s.reshape(bs, ts, MLA_KV_LORA), kr_s.reshape(bs, ts, MLA_ROPE))
        elif kind == 1:
            scale = FOX_DH ** -0.5 * LOG2E
            gq, gk = _tile_gain(fox_g_q, d), _tile_gain(fox_g_k, d)
            wq, wkt, wv, wvt = split_qkv(fox_w_qkv)
            q_p, kt_p, vt_p, lft_p = _qkv_proj_t(h_p, ga, wq, wkt, wvt, gq, _gain_t(fox_g_k, d, tm_p), b=bp, t=tp,
                                                 tm=tm_p, wft=fox_wft, bft=fox_bft, scale=scale, name="fox_proj_p")
            fk_p = _cumsum_last(lft_p, tk=tq, name="fox_cumsum_p")
            o_p = _pair_attn(q_p, kt_p, vt_p, (fk_p.reshape(bp, FOX_HEADS // 2, 2, tp),), mode="fox",
                             b=bp, t=tp, tq=tq, name="fox_attn_p")
            q_s, k_s, v_s, lf_s = _qkv_proj(h_s, ga, tobf(fox_w_qkv), gq, gk, tm=ms, wf=fox_wf, bf=fox_bf,
                                            scale=scale, name="fox_proj_s")
            lf_all = jnp.concatenate([jnp.swapaxes(cache_fox_logf.astype(F32), 1, 2),
                                      jnp.swapaxes(lf_s.reshape(bs, ts, FOX_HEADS), 1, 2)], axis=2)
            lf_all = jnp.pad(lf_all, ((0, 0), (0, 0), (0, tkw - ts)))
            f_all = _cumsum_last(lf_all, tk=tq, name="fox_cumsum_s")
            o_s = _dec_attn(q_s, _to_features(cache_fox_k.astype(F32)), _to_features(cache_fox_v.astype(F32)),
                            k_s, v_s, (f_all,), mode="fox", b=bs, tq=ts, past=past, tk=tkw, name="fox_attn_s")
            sh = (FOX_HEADS, FOX_DH)
            outs["fox"] = (_to_tokens(kt_p, (FOX_HEADS,)), _to_tokens(vt_p, (FOX_HEADS,)),
                           jnp.swapaxes(lft_p, 1, 2),
                           k_s.reshape(bs, ts, *sh), v_s.reshape(bs, ts, *sh), lf_s.reshape(bs, ts, FOX_HEADS))
        elif kind == 2:
            scale = DIFF_DH ** -0.5 * LOG2E
            lam_init = 0.8 - 0.6 * math.exp(-0.3 * i)
            gq, gk = _tile_gain(diff_g_q, d), _tile_gain(diff_g_k, d)
            wq, wkt, wv, wvt = split_qkv(diff_w_qkv)
            q_p, kt_p, v4_p = _qkv_proj_t(h_p, ga, wq, wkt, wv, gq, _gain_t(diff_g_k, d, tm_p), b=bp, t=tp, tm=tm_p,
                                          rope_tabs=tabs_p, v_tokens=True, scale=scale, name="diff_proj_p")
            v_p = v4_p.reshape(bp, tp, DIFF_HEADS, 2 * DIFF_DH)
            o_p = _pair_attn(q_p, kt_p, v4_p, (lamv, gsub), mode="diff", b=bp, t=tp, tq=tq, lam_init=lam_init,
                             name="diff_attn_p")
            q_s, k_s, v_s = _qkv_proj(h_s, ga, tobf(diff_w_qkv), gq, gk, tm=ms, rope_tabs=tabs_s, scale=scale,
                                      name="diff_proj_s")
            o_s = _dec_attn(q_s, _to_features(cache_diff_k.astype(F32)),
                            cache_diff_v.astype(F32).reshape(bs * past * DIFF_HEADS, 2 * DIFF_DH), k_s, v_s,
                            (lamv, gsub), mode="diff", b=bs, tq=ts, past=past, tk=tkw, lam_init=lam_init,
                            name="diff_attn_s")
            outs["diff"] = (_to_tokens(kt_p, (DIFF_HEADS, 2)), v_p,
                            k_s.reshape(bs, ts, DIFF_HEADS, 2, DIFF_DH), v_s.reshape(bs, ts, DIFF_HEADS, 2 * DIFF_DH))
        else:
            scale = BAND_DH ** -0.5 * LOG2E
            gq, gk = _tile_gain(band_g_q, d), _tile_gain(band_g_k, d)
            wq, wkt, wv, wvt = split_qkv(band_w_qkv)
            keep = min(BAND_LEFT, tp)
            assert keep == tm_p
            q_p, kt_p, vt_p, kt_keep, vt_keep = _qkv_proj_t(
                h_p, ga, wq, wkt, wvt, gq, _gain_t(band_g_k, d, tm_p), b=bp, t=tp, tm=tm_p, tail=True, scale=scale,
                name="band_proj_p")
            tab_pairs = jnp.pad(tab_pad.reshape(BAND_HEADS // 2, 2, REL_PAD), ((0, 0), (0, SUBLANES - 2), (0, 0)))
            o_p = _band_attn(q_p, kt_p, vt_p, tab_pairs, b=bp, t=tp, tq=tq_band, name="band_attn_p")
            q_s, k_s, v_s = _qkv_proj(h_s, ga, tobf(band_w_qkv), gq, gk, tm=ms, scale=scale, name="band_proj_s")
            w = cache_band_k.shape[1]
            kct = _to_features(cache_band_k.astype(F32))
            vct = _to_features(cache_band_v.astype(F32))
            o_s, k_roll, v_roll = _band_dec(q_s, kct, vct, k_s, v_s, tab_pad, b=bs, tq=ts, w=w, past=past,
                                            name="band_attn_s")
            outs["band"] = (_to_tokens(kt_keep, (BAND_HEADS,)), _to_tokens(vt_keep, (BAND_HEADS,)),
                            _to_tokens(k_roll, (BAND_HEADS,)), _to_tokens(v_roll, (BAND_HEADS,)))

        last = i == depth - 1
        gfin = rowv(final_norm_g) if last else None
        gfn = rowv(ffn_norm_g[i])
        h_p, ug, uh = _ffn_rows(h_p, o_p, wo[kind], gfn, wup, cwf[i], cbf[i].reshape(1, -1), wdn, layer=i, tm=tm_f,
                                tf=tf, seq=tp, final_g=gfin, name="ffn_p")
        tps = tp // tm_f
        conv_p.append(jnp.concatenate([ug, uh], axis=-1)[tps - 1::tps, SUBLANES - (CONV_W - 1):])
        st = state_ffn_conv[i].astype(F32)
        s1 = jnp.pad(st[:, 1:], ((0, 0), (0, ts - 1), (0, 0))).reshape(ms, 2 * D_FF)
        s2 = jnp.pad(st, ((0, 0), (0, ts - 2), (0, 0))).reshape(ms, 2 * D_FF)
        h_s, ug, uh = _ffn_rows(h_s, o_s, wo[kind], gfn, wup, cwf[i], cbf[i].reshape(1, -1), wdn, layer=i, tm=ms,
                                tf=tf, seq=ts, state=(s1, s2), final_g=gfin, name="ffn_s")
        u_s = jnp.concatenate([ug, uh], axis=-1).reshape(bs, ts, 2 * D_FF)
        conv_s.append(jnp.concatenate([st, u_s], axis=1)[:, ts:])

    y_prompt = h_p.reshape(bp, tp, d)
    y_sample = h_s.reshape(bs, ts, d)
    return (y_prompt, y_sample) + outs["mla"] + outs["fox"] + outs["diff"] + outs["band"] + (
        jnp.stack(conv_p, axis=0), jnp.stack(conv_s, axis=0))
```

```python
import functools
import math

import jax
import jax.numpy as jnp
from jax import lax
from jax.experimental import pallas as pl
from jax.experimental.pallas import tpu as pltpu

F32 = jnp.float32
BF16 = jnp.bfloat16

D_MODEL = 1024
CHUNK = 64
CHUNK_SHIFT = CHUNK.bit_length() - 1
ROPE_DIM = 64
ROPE_THETA = 10000.0
NORM_EPS = 1e-6
MLA_HEADS, MLA_Q_LORA, MLA_KV_LORA, MLA_NOPE, MLA_ROPE, MLA_V = 8, 384, 256, 128, 64, 128
FOX_HEADS, FOX_DH = 16, 64
DIFF_HEADS, DIFF_DH = 8, 64
BAND_HEADS, BAND_DH = 16, 64
BAND_LEFT_CHUNKS = 8
BAND_LEFT = BAND_LEFT_CHUNKS * CHUNK
REL_CLIP = 128
D_FF = 2816
CONV_W = 3

LANES = 128
SUBLANES = 8
NEG = -1e30
REL_PAD = 384
LOG2E = math.log2(math.e)
VMEM_LIMIT = 56 * 1024 * 1024


def _cp(n_axes):
    return pltpu.CompilerParams(dimension_semantics=("arbitrary",) * n_axes, vmem_limit_bytes=VMEM_LIMIT)


def _dot(a, b):
    return jnp.dot(a, b, preferred_element_type=F32)


def _dot_nt(a, b):
    return lax.dot_general(a, b, (((1,), (1,)), ((), ())), preferred_element_type=F32)


def _rms(x, g):
    ms = jnp.mean(x * x, axis=-1, keepdims=True)
    return x * lax.rsqrt(ms + NORM_EPS) * g


def _lane_iota(shape):
    return lax.broadcasted_iota(jnp.int32, shape, len(shape) - 1)


def _row_iota(shape):
    return lax.broadcasted_iota(jnp.int32, shape, len(shape) - 2)


def _log2(n):
    assert n & (n - 1) == 0, n
    return n.bit_length() - 1


def _head_rms(x, g, seg):
    n = x.shape[1]
    outs = []
    for c in range(n // LANES):
        xs = x[:, LANES * c:LANES * (c + 1)]
        sq = xs * xs
        if seg == LANES:
            r = lax.rsqrt(jnp.sum(sq, axis=-1, keepdims=True) * (1.0 / seg) + NORM_EPS)
        else:
            lo = _lane_iota(xs.shape) < seg
            s_lo = jnp.sum(jnp.where(lo, sq, 0.0), axis=-1, keepdims=True)
            s_hi = jnp.sum(jnp.where(lo, 0.0, sq), axis=-1, keepdims=True)
            r = jnp.where(lo, lax.rsqrt(s_lo * (1.0 / seg) + NORM_EPS), lax.rsqrt(s_hi * (1.0 / seg) + NORM_EPS))
        outs.append(xs * r * g[:, LANES * c:LANES * (c + 1)])
    return jnp.concatenate(outs, axis=-1) if len(outs) > 1 else outs[0]


def _head_rms_t(x, g, seg=64):
    outs = []
    for hd in range(x.shape[0] // seg):
        xs = x[seg * hd:seg * (hd + 1)]
        r = lax.rsqrt(jnp.sum(xs * xs, axis=0, keepdims=True) * (1.0 / seg) + NORM_EPS)
        outs.append(xs * r * g[seg * hd:seg * (hd + 1)])
    return jnp.concatenate(outs, axis=0) if len(outs) > 1 else outs[0]


def _rope(x, cos, sinp):
    half = ROPE_DIM // 2
    first_half = (_lane_iota((x.shape[0], LANES)) & (ROPE_DIM - 1)) < half
    outs = []
    for c in range(x.shape[1] // LANES):
        xs = x[:, LANES * c:LANES * (c + 1)]
        fwd = pltpu.roll(xs, half, axis=1)
        bwd = pltpu.roll(xs, LANES - half, axis=1)
        outs.append(xs * cos + jnp.where(first_half, bwd, fwd) * sinp)
    return jnp.concatenate(outs, axis=-1) if len(outs) > 1 else outs[0]


def _rope_t(x, cos_t, sin_t):
    outs = []
    for hd in range(x.shape[0] // 64):
        x1 = x[64 * hd:64 * hd + 32]
        x2 = x[64 * hd + 32:64 * (hd + 1)]
        outs += [x1 * cos_t - x2 * sin_t, x2 * cos_t + x1 * sin_t]
    return jnp.concatenate(outs, axis=0)


def _split3(x):
    hi = x.astype(BF16)
    r = x - hi.astype(F32)
    mid = r.astype(BF16)
    lo = (r - mid.astype(F32)).astype(BF16)
    return hi, mid, lo


def _softmax_step(s, pv, m, l, acc):
    m_new = jnp.maximum(m, jnp.max(s, axis=-1, keepdims=True))
    alpha = jnp.exp2(m - m_new)
    p = jnp.exp2(s - m_new)
    if l is not None:
        l = alpha * l + jnp.sum(p, axis=-1, keepdims=True)
    return m_new, l, alpha * acc + pv(p.astype(BF16))


def _softmax_update(s, pv, m_ref, l_ref, acc_ref):
    m, l, acc = _softmax_step(s, pv, m_ref[...], l_ref[...], acc_ref[...])
    m_ref[...] = m
    l_ref[...] = l
    acc_ref[...] = acc


def _log_sigmoid(z):
    return jnp.minimum(z, 0.0) - jnp.log1p(jnp.exp(-jnp.abs(z)))


def _qkv_proj_kernel(*refs, rope, logf, scale):
    it = iter(refs)
    h_ref, g_ref, w_ref, gq_ref, gk_ref = next(it), next(it), next(it), next(it), next(it)
    cos_ref = sin_ref = wf_ref = bf_ref = None
    if rope:
        cos_ref, sin_ref = next(it), next(it)
    if logf:
        wf_ref, bf_ref = next(it), next(it)
    q_out, k_out, v_out = next(it), next(it), next(it)
    a = _rms(h_ref[...], g_ref[...]).astype(BF16)
    qkv = _dot(a, w_ref[...])
    q = _head_rms(qkv[:, :D_MODEL], gq_ref[...], 64)
    k = _head_rms(qkv[:, D_MODEL:2 * D_MODEL], gk_ref[...], 64)
    if rope:
        q = _rope(q, cos_ref[...], sin_ref[...])
        k = _rope(k, cos_ref[...], sin_ref[...])
    q_out[...] = (q * scale).astype(BF16)
    k_out[...] = k
    v_out[...] = qkv[:, 2 * D_MODEL:]
    if logf:
        logf_out = next(it)
        lf = _log_sigmoid(_dot(a, wf_ref[...]) + bf_ref[...])
        logf_out[...] = lf[:, :FOX_HEADS]


def _qkv_proj(h, g, w, gq, gk, *, tm, rope_tabs=None, wf=None, bf=None, scale, name):
    m = h.shape[0]
    grid = (m // tm,)
    row = lambda n: pl.BlockSpec((tm, n), lambda i: (i, 0))
    full = lambda a: pl.BlockSpec(a.shape, lambda i: (0,) * a.ndim)
    ins = [h, g, w, gq, gk]
    specs = [row(D_MODEL), full(g), full(w), full(gq), full(gk)]
    if rope_tabs is not None:
        cos, sinp = rope_tabs
        nt = cos.shape[0] // tm
        tab = pl.BlockSpec((tm, LANES), lambda i: (i % nt, 0))
        ins += [cos, sinp]
        specs += [tab, tab]
    if wf is not None:
        ins += [wf, bf]
        specs += [full(wf), full(bf)]
    outs = [jax.ShapeDtypeStruct((m, D_MODEL), BF16), jax.ShapeDtypeStruct((m, D_MODEL), F32),
            jax.ShapeDtypeStruct((m, D_MODEL), F32)]
    ospecs = [row(D_MODEL), row(D_MODEL), row(D_MODEL)]
    if wf is not None:
        outs.append(jax.ShapeDtypeStruct((m, FOX_HEADS), F32))
        ospecs.append(row(FOX_HEADS))
    kern = functools.partial(_qkv_proj_kernel, rope=rope_tabs is not None, logf=wf is not None, scale=scale)
    return pl.pallas_call(kern, out_shape=outs, grid=grid, in_specs=specs, out_specs=ospecs,
                          compiler_params=_cp(1), name=name)(*ins)


def _qkv_proj_t_kernel(*refs, rope, logf, v_tokens, tail, scale):
    it = iter(refs)
    h_ref, g_ref, wq_ref, wkt_ref, wv_ref, gq_ref, gkt_ref = [next(it) for _ in range(7)]
    if rope:
        cos_ref, sin_ref, cost_ref, sint_ref = [next(it) for _ in range(4)]
    if logf:
        wft_ref, bft_ref = next(it), next(it)
    q_out, kt_out, v_out = next(it), next(it), next(it)
    a = _rms(h_ref[...], g_ref[...]).astype(BF16)
    tm = a.shape[0]
    q = _head_rms(_dot(a, wq_ref[...]), gq_ref[...], 64)
    kt = _head_rms_t(_dot_nt(wkt_ref[...], a), gkt_ref[...])
    if rope:
        q = _rope(q, cos_ref[...], sin_ref[...])
        kt = _rope_t(kt, cost_ref[...], sint_ref[...])
    q_out[...] = (q * scale).astype(BF16)
    kt_out[...] = kt
    if v_tokens:
        v = _dot(a, wv_ref[...])
        nh = D_MODEL // LANES
        for hd in range(nh):
            v_out[pl.ds(hd, tm, stride=nh), :] = v[:, LANES * hd:LANES * (hd + 1)]
    else:
        vt = _dot_nt(wv_ref[...], a)
        v_out[...] = vt
    if logf:
        lft_out = next(it)
        lft_out[...] = _log_sigmoid(_dot_nt(wft_ref[...], a) + bft_ref[...])
    if tail:
        kt_tail, vt_tail = next(it), next(it)
        kt_tail[...] = kt
        vt_tail[...] = vt


def _qkv_proj_t(h, g, wq, wkt, wv, gq, gkt, *, b, t, tm, rope_tabs=None, wft=None, bft=None, v_tokens=False,
                tail=False, scale, name):
    m = b * t
    nt = t // tm
    nh = D_MODEL // LANES
    row = lambda n: pl.BlockSpec((tm, n), lambda i: (i, 0))
    full = lambda a: pl.BlockSpec(a.shape, lambda i: (0,) * a.ndim)
    feat = lambda n: pl.BlockSpec((None, n, tm), lambda i: (i // nt, 0, i % nt))
    ins = [h, g, wq, wkt, wv, gq, gkt]
    specs = [row(D_MODEL)] + [full(x) for x in ins[1:]]
    if rope_tabs is not None:
        cos, sinp, cos_t, sin_t = rope_tabs
        ins += [cos, sinp, cos_t, sin_t]
        tab = pl.BlockSpec((tm, LANES), lambda i: (i % nt, 0))
        tab_t = pl.BlockSpec((32, tm), lambda i: (0, i % nt))
        specs += [tab, tab, tab_t, tab_t]
    if wft is not None:
        ins += [wft, bft]
        specs += [full(wft), full(bft)]
    outs = [jax.ShapeDtypeStruct((m, D_MODEL), BF16), jax.ShapeDtypeStruct((b, D_MODEL, t), F32)]
    ospecs = [row(D_MODEL), feat(D_MODEL)]
    if v_tokens:
        outs.append(jax.ShapeDtypeStruct((m * nh, LANES), F32))
        ospecs.append(pl.BlockSpec((tm * nh, LANES), lambda i: (i, 0)))
    else:
        outs.append(jax.ShapeDtypeStruct((b, D_MODEL, t), F32))
        ospecs.append(feat(D_MODEL))
    if wft is not None:
        outs.append(jax.ShapeDtypeStruct((b, FOX_HEADS, t), F32))
        ospecs.append(feat(FOX_HEADS))
    if tail:
        assert not v_tokens
        outs += [jax.ShapeDtypeStruct((b, D_MODEL, tm), F32)] * 2
        ospecs += [pl.BlockSpec((None, D_MODEL, tm), lambda i: (i // nt, 0, 0))] * 2
    kern = functools.partial(_qkv_proj_t_kernel, rope=rope_tabs is not None, logf=wft is not None,
                             v_tokens=v_tokens, tail=tail, scale=scale)
    return pl.pallas_call(kern, out_shape=outs, grid=(m // tm,), in_specs=specs, out_specs=ospecs,
                          compiler_params=_cp(1), name=name)(*ins)


def _mla_proj_kernel(*refs, with_kv, scale):
    it = iter(refs)
    (h_ref, g_ref, wdq_ref, gq_ref, wuq_ref, wc_ref, wr_ref, gkv_ref, gqn_ref, gqr_ref, gkr_ref, gkn_ref,
     cos_ref, sin_ref) = [next(it) for _ in range(14)]
    if with_kv:
        wuk_ref, wuv_ref, wrt_ref, gkrt_ref, cost_ref, sint_ref = [next(it) for _ in range(6)]
    q_out, ckv_out, kr_out, krd_out = next(it), next(it), next(it), next(it)
    a = _rms(h_ref[...], g_ref[...]).astype(BF16)
    cq = _rms(_dot(a, wdq_ref[...]), gq_ref[...]).astype(BF16)
    q = _dot(cq, wuq_ref[...])
    n_nope = MLA_HEADS * MLA_NOPE
    qn = _head_rms(q[:, :n_nope], gqn_ref[...], MLA_NOPE) * scale
    if not with_kv:
        qn = qn * gkn_ref[...]
    qr = _rope(_head_rms(q[:, n_nope:], gqr_ref[...], MLA_ROPE), cos_ref[...], sin_ref[...]) * scale
    lo = _lane_iota((q.shape[0], LANES)) < MLA_ROPE
    pieces = []
    for hd in range(MLA_HEADS):
        slab = qr[:, LANES * (hd // 2):LANES * (hd // 2 + 1)]
        keep = lo if hd % 2 == 0 else jnp.logical_not(lo)
        pieces += [qn[:, LANES * hd:LANES * (hd + 1)], jnp.where(keep, slab, 0.0)]
    q_out[...] = jnp.concatenate(pieces, axis=-1).astype(BF16)
    ckv = _rms(_dot(a, wc_ref[...]), gkv_ref[...])
    ckv_out[...] = ckv
    kr2 = _rope(_head_rms(_dot(a, wr_ref[...]), gkr_ref[...], MLA_ROPE), cos_ref[...], sin_ref[...])
    kr_out[...] = kr2[:, :MLA_ROPE]
    krd_out[...] = kr2
    if with_kv:
        kc_out, v_out, krt_out = next(it), next(it), next(it)
        c = ckv.astype(BF16)
        kn = _head_rms(_dot(c, wuk_ref[...]), gkn_ref[...], MLA_NOPE)
        pieces = []
        for hd in range(MLA_HEADS):
            pieces += [kn[:, LANES * hd:LANES * (hd + 1)], kr2]
        kc_out[...] = jnp.concatenate(pieces, axis=-1).astype(BF16)
        v_out[...] = _dot(c, wuv_ref[...]).astype(BF16)
        krt_out[...] = _rope_t(_head_rms_t(_dot_nt(wrt_ref[...], a), gkrt_ref[...]), cost_ref[...], sint_ref[...])


def _mla_proj(h, g, w, rope_tabs, *, b, t, tm, with_kv, scale, name):
    m = h.shape[0]
    cos, sinp = rope_tabs[:2]
    nt = cos.shape[0] // tm
    row = lambda n: pl.BlockSpec((tm, n), lambda i: (i, 0))
    full = lambda a: pl.BlockSpec(a.shape, lambda i: (0,) * a.ndim)
    tab = pl.BlockSpec((tm, LANES), lambda i: (i % nt, 0))
    ins = [h, g, w["wdq"], w["gq"], w["wuq"], w["wc"], w["wr"], w["gkv"], w["gqn"], w["gqr"], w["gkr"], w["gkn"],
           cos, sinp]
    specs = [row(D_MODEL)] + [full(x) for x in ins[1:12]] + [tab, tab]
    qw = MLA_HEADS * 2 * LANES
    outs = [jax.ShapeDtypeStruct((m, qw), BF16), jax.ShapeDtypeStruct((m, MLA_KV_LORA), F32),
            jax.ShapeDtypeStruct((m, MLA_ROPE), F32), jax.ShapeDtypeStruct((m, LANES), F32)]
    ospecs = [row(qw), row(MLA_KV_LORA), row(MLA_ROPE), row(LANES)]
    if with_kv:
        cos_t, sin_t = rope_tabs[2:]
        tab_t = pl.BlockSpec((32, tm), lambda i: (0, i % nt))
        extra = [w["wuk"], w["wuv"], w["wrt"], w["gkrt"]]
        ins += extra + [cos_t, sin_t]
        specs += [full(x) for x in extra] + [tab_t, tab_t]
        outs += [jax.ShapeDtypeStruct((m, qw), BF16), jax.ShapeDtypeStruct((m, MLA_HEADS * MLA_V), BF16),
                 jax.ShapeDtypeStruct((b, MLA_ROPE, t), F32)]
        ospecs += [row(qw), row(MLA_HEADS * MLA_V),
                   pl.BlockSpec((None, MLA_ROPE, tm), lambda i: (i // nt, 0, i % nt))]
    kern = functools.partial(_mla_proj_kernel, with_kv=with_kv, scale=scale)
    return pl.pallas_call(kern, out_shape=outs, grid=(m // tm,), in_specs=specs, out_specs=ospecs,
                          compiler_params=_cp(1), name=name)(*ins)


def _cumsum_kernel(x_ref, o_ref, *, tk):
    nh, t = x_ref.shape
    tri = (_row_iota((tk, tk)) <= _lane_iota((tk, tk))).astype(BF16)
    carry = jnp.zeros((nh, 1), F32)
    for c in range(t // tk):
        hi, mid, lo = _split3(x_ref[:, tk * c:tk * (c + 1)])
        y = _dot(jnp.concatenate([hi, mid, lo], axis=0), tri)
        f = (y[:nh] + y[nh:2 * nh]) + y[2 * nh:] + carry
        o_ref[:, tk * c:tk * (c + 1)] = f
        carry = f[:, tk - 1:tk]


def _cumsum_last(x, *, tk, name):
    b, nh, t = x.shape
    spec = pl.BlockSpec((None, nh, t), lambda i: (i, 0, 0))
    return pl.pallas_call(functools.partial(_cumsum_kernel, tk=tk), out_shape=jax.ShapeDtypeStruct(x.shape, F32),
                          grid=(b,), in_specs=[spec], out_specs=spec, compiler_params=_cp(1), name=name)(x)


def _mla_attn_kernel(q_ref, k_ref, v_ref, o_ref, *, tq, hps):
    kw = 2 * LANES
    t = k_ref.shape[0]
    for qi in range(t // tq):
        rows = slice(tq * qi, tq * (qi + 1))
        limit = (((qi * tq + _row_iota((tq, 1))) >> CHUNK_SHIFT) + 1) << CHUNK_SHIFT
        outs = []
        n0 = qi * tq
        for h in range(hps):
            q = q_ref[rows, kw * h:kw * (h + 1)]
            kcols = slice(kw * h, kw * (h + 1))
            vcols = slice(MLA_V * h, MLA_V * (h + 1))
            s_d = jnp.where(n0 + _lane_iota((1, tq)) < limit, _dot_nt(q, k_ref[n0:n0 + tq, kcols]), NEG)
            m = jnp.max(s_d, axis=-1, keepdims=True)
            if qi:
                s_f = _dot_nt(q, k_ref[0:n0, kcols])
                m = jnp.maximum(m, jnp.max(s_f, axis=-1, keepdims=True))
            p_d = jnp.exp2(s_d - m)
            acc = _dot(p_d.astype(BF16), v_ref[n0:n0 + tq, vcols])
            l = jnp.sum(p_d, axis=-1, keepdims=True)
            if qi:
                p_f = jnp.exp2(s_f - m)
                acc = acc + _dot(p_f.astype(BF16), v_ref[0:n0, vcols])
                l = l + jnp.sum(p_f, axis=-1, keepdims=True)
            outs.append(acc / l)
        o_ref[rows, :] = jnp.concatenate(outs, axis=-1).astype(BF16)


def _mla_attn(q, kc, v, *, b, t, tq, hps, name):
    kw = 2 * LANES * hps
    vw = MLA_V * hps
    return pl.pallas_call(
        functools.partial(_mla_attn_kernel, tq=tq, hps=hps),
        out_shape=jax.ShapeDtypeStruct((b * t, MLA_HEADS * MLA_V), BF16),
        grid=(b, MLA_HEADS // hps),
        in_specs=[pl.BlockSpec((t, kw), lambda i, h: (i, h)),
                  pl.BlockSpec((t, kw), lambda i, h: (i, h)),
                  pl.BlockSpec((t, vw), lambda i, h: (i, h))],
        out_specs=pl.BlockSpec((t, vw), lambda i, h: (i, h)),
        compiler_params=_cp(2), name=name)(q, kc, v)


def _ones_row_values(vt, c):
    row = _row_iota(vt.shape)
    if c == 0:
        return jnp.where(row < 64, vt, jnp.where(row == 64, 1.0, 0.0)).astype(BF16)
    return jnp.where(row >= 64, vt, jnp.where(row == 0, 1.0, 0.0)).astype(BF16)


def _merge_pair(acc0, acc1):
    lo = _lane_iota(acc0.shape) < 64
    return jnp.where(lo, acc0 / acc0[:, 64:65], acc1 / acc1[:, 0:1])


def _pair_attn_kernel(*refs, mode, tq, lam_init):
    if mode == "fox":
        q_ref, kt_ref, vt_ref, fk_ref, o_ref, ka, va = refs
    else:
        q_ref, kt_ref, v_ref, lam_ref, gsub_ref, o_ref, ka, vb = refs
    t = kt_ref.shape[1]
    if mode == "fox":
        ka[0:LANES, :] = kt_ref[...].astype(BF16)
        f = fk_ref[...] * (-LOG2E)
        terms = [x.astype(F32) for x in _split3(f)]
        nr = 2 * SUBLANES
        row = _row_iota((nr, t))
        aug = jnp.zeros((nr, t), F32)
        for c in range(2):
            for j in range(3):
                aug = jnp.where(row == 3 * c + j, terms[j][c:c + 1, :], aug)
        ka[LANES:LANES + nr, :] = aug.astype(BF16)
        ka[LANES + nr:, :] = jnp.zeros((LANES - nr, t), BF16)
        vt = vt_ref[...]
        va[0] = _ones_row_values(vt, 0)
        va[1] = _ones_row_values(vt, 1)
    else:
        ka[...] = kt_ref[...].astype(BF16)
        nhv = v_ref.shape[0] // t
        vb[0:LANES, :] = v_ref[pl.ds(pl.program_id(1), t, stride=nhv), :].T.astype(BF16)
        vb[LANES:, :] = jnp.where(_row_iota((LANES, t)) == 0, 1.0, 0.0).astype(BF16)

    def q_tile(qi, q):
        lane = _lane_iota((tq, LANES))
        zero = jnp.zeros_like(q)
        qa = [jnp.where(lane < 64, q, zero), jnp.where(lane < 64, zero, q)]
        qpos = qi * tq + _row_iota((tq, 1))
        if mode == "fox":
            limit = qpos + 1
            pick = [jnp.where(lane < 3 * c, 0.0, jnp.where(lane < 3 * c + 3, 1.0, 0.0)).astype(BF16)
                    for c in range(2)]
            qa = [jnp.concatenate([qa[c], pick[c]], axis=-1) for c in range(2)]
        else:
            limit = ((qpos >> CHUNK_SHIFT) + 1) << CHUNK_SHIFT

        n0 = qi * tq
        accs = []
        for c in range(2):
            s_d = jnp.where(n0 + _lane_iota((1, tq)) < limit, _dot(qa[c], ka[:, n0:n0 + tq]), NEG)
            m = jnp.max(s_d, axis=-1, keepdims=True)
            if qi:
                s_f = _dot(qa[c], ka[:, 0:n0])
                m = jnp.maximum(m, jnp.max(s_f, axis=-1, keepdims=True))
            parts = [(jnp.exp2(s_d - m), n0, n0 + tq)] + ([(jnp.exp2(s_f - m), 0, n0)] if qi else [])
            acc = 0.0
            for p, lo_k, hi_k in parts:
                vals = va[c, :, lo_k:hi_k] if mode == "fox" else vb[:, lo_k:hi_k]
                acc = acc + _dot_nt(p.astype(BF16), vals)
            accs.append(acc)
        if mode == "fox":
            o = _merge_pair(*accs)
        else:
            lv = lam_ref[...]
            lam = (jnp.exp(jnp.sum(lv[0:1] * lv[1:2], axis=-1, keepdims=True))
                   - jnp.exp(jnp.sum(lv[2:3] * lv[3:4], axis=-1, keepdims=True)) + lam_init)
            o0, o1 = [a[:, :LANES] / a[:, LANES:LANES + 1] for a in accs]
            o = _rms(o0 - lam * o1, gsub_ref[...]) * (1.0 - lam_init)
        return o.astype(BF16)

    for qi in range(t // tq):
        o_ref[tq * qi:tq * (qi + 1), :] = q_tile(qi, q_ref[tq * qi:tq * (qi + 1), :])


def _pair_attn(q, kt, v, extra, *, mode, b, t, tq, lam_init=0.0, name):
    npair = D_MODEL // LANES
    qspec = pl.BlockSpec((t, LANES), lambda i, p: (i, p))
    ktspec = pl.BlockSpec((None, LANES, t), lambda i, p: (i, p, 0))
    if mode == "fox":
        vspec = ktspec
        especs = [pl.BlockSpec((None, None, 2, t), lambda i, p: (i, p, 0, 0))]
        scratch = [pltpu.VMEM((2 * LANES, t), BF16), pltpu.VMEM((2, LANES, t), BF16)]
    else:
        vspec = pl.BlockSpec((t * npair, LANES), lambda i, p: (i, 0))
        especs = [pl.BlockSpec(x.shape, lambda i, p: (0, 0)) for x in extra]
        scratch = [pltpu.VMEM((LANES, t), BF16), pltpu.VMEM((2 * LANES, t), BF16)]
    return pl.pallas_call(
        functools.partial(_pair_attn_kernel, mode=mode, tq=tq, lam_init=lam_init),
        out_shape=jax.ShapeDtypeStruct((b * t, D_MODEL), BF16),
        grid=(b, npair),
        in_specs=[qspec, ktspec, vspec] + especs,
        out_specs=qspec,
        scratch_shapes=scratch,
        compiler_params=_cp(2), name=name)(q, kt, v, *extra)


def _rel_gather(tab, width, center):
    idx = jnp.clip(center - _lane_iota((REL_PAD, width)), -REL_CLIP, REL_CLIP) + REL_CLIP
    onehot = (_row_iota((REL_PAD, width)) == idx).astype(BF16)
    hi, mid, lo = _split3(tab)
    return (_dot(hi, onehot) + _dot(mid, onehot)) + _dot(lo, onehot)


def _band_attn_kernel(q_ref, kt_ref, vt_ref, tab_ref, o_ref, kb, va, bias_ref, *, tq, win, bw):
    gw = bw + tq

    @pl.when(pl.program_id(1) == 0)
    def _():
        g = _rel_gather(tab_ref[...], gw, BAND_LEFT + tq) * LOG2E
        ii = _row_iota((tq, bw)) >> CHUNK_SHIFT
        jj = _lane_iota((tq, bw)) >> CHUNK_SHIFT
        allowed = (jj >= ii) & (jj <= ii + BAND_LEFT_CHUNKS)
        for c in range(2):
            rows = jnp.broadcast_to(g[c:c + 1, :], (tq, gw))
            skew = pltpu.roll(rows, gw - tq, axis=1, stride=1, stride_axis=0)
            bias_ref[c] = jnp.where(allowed, skew[:, :bw], NEG)

    kb[...] = kt_ref[...].astype(BF16)
    vt = vt_ref[...]
    va[0] = _ones_row_values(vt, 0)
    va[1] = _ones_row_values(vt, 1)

    lo = _lane_iota((tq, LANES)) < 64
    for r in range(q_ref.shape[0] // tq):
        q0 = r * tq
        ws = max(q0 - BAND_LEFT, 0)
        d = BAND_LEFT - q0 + ws
        k = kb[:, pl.ds(ws, win)]
        q = q_ref[tq * r:tq * (r + 1), :]
        zero = jnp.zeros_like(q)
        accs = []
        for c in range(2):
            qc = jnp.where(lo, q, zero) if c == 0 else jnp.where(lo, zero, q)
            s = _dot(qc, k) + bias_ref[c, :, pl.ds(d, win)]
            p = jnp.exp2(s - jnp.max(s, axis=-1, keepdims=True))
            accs.append(_dot_nt(p.astype(BF16), va[c, :, pl.ds(ws, win)]))
        o_ref[tq * r:tq * (r + 1), :] = _merge_pair(*accs).astype(BF16)


def _band_attn(q, kt, vt, tab, *, b, t, tq, name):
    npair = D_MODEL // LANES
    win = BAND_LEFT + tq
    bw = win + BAND_LEFT
    assert t >= win and tq % CHUNK == 0 and BAND_LEFT % tq == 0
    qspec = pl.BlockSpec((t, LANES), lambda p, i: (i, p))
    kvspec = pl.BlockSpec((None, LANES, t), lambda p, i: (i, p, 0))
    return pl.pallas_call(
        functools.partial(_band_attn_kernel, tq=tq, win=win, bw=bw),
        out_shape=jax.ShapeDtypeStruct((b * t, D_MODEL), BF16),
        grid=(npair, b),
        in_specs=[qspec, kvspec, kvspec, pl.BlockSpec((None, SUBLANES, REL_PAD), lambda p, i: (p, 0, 0))],
        out_specs=qspec,
        scratch_shapes=[pltpu.VMEM((LANES, t), BF16), pltpu.VMEM((2, LANES, t), BF16), pltpu.VMEM((2, tq, bw), F32)],
        compiler_params=_cp(2), name=name)(q, kt, vt, tab)


def _block_diag_q(q, nh, width):
    tq = q.shape[0]
    rep = jnp.concatenate([q] * nh, axis=0)
    keep = (_row_iota(rep.shape) >> _log2(tq)) == (_lane_iota(rep.shape) >> _log2(width))
    return jnp.where(keep, rep, jnp.zeros_like(rep))


def _expand_rows(f, tq):
    return jnp.concatenate([jnp.broadcast_to(f[h:h + 1, :], (tq, f.shape[1])) for h in range(f.shape[0])], axis=0)


def _dec_attn_kernel(*refs, mode, n_cache, tq, past, lam_init):
    if mode == "fox":
        (q_ref, kc_ref, vc_ref, kn_ref, vn_ref, fkc_ref, fkn_ref, o_ref, qb, m_ref, l_ref, acc_ref) = refs
    else:
        (q_ref, kc_ref, vc_ref, kn_ref, vn_ref, lam_ref, gsub_ref, o_ref, qb, m_ref, l_ref, acc_ref) = refs
    t = pl.program_id(1)
    nh = D_MODEL // 64
    rows = nh * tq

    @pl.when(t == 0)
    def _():
        qb[...] = _block_diag_q(q_ref[...], nh, 64)
        m_ref[...] = jnp.full_like(m_ref, NEG)
        l_ref[...] = jnp.zeros_like(l_ref)
        acc_ref[...] = jnp.zeros_like(acc_ref)

    def cache_step():
        s = _dot(qb[...], kc_ref[...].astype(BF16))
        if mode == "fox":
            s = s - _expand_rows(fkc_ref[...] * LOG2E, tq)
            v = vc_ref[...].astype(BF16)
            _softmax_update(s, lambda pb: _dot_nt(pb, v), m_ref, l_ref, acc_ref)
        else:
            tk = kc_ref.shape[1]
            nhv = D_MODEL // LANES
            v = jnp.concatenate([vc_ref[pl.ds(h, tk, stride=nhv), :].astype(BF16) for h in range(nhv)], axis=-1)
            _softmax_update(s, lambda pb: _dot(pb, v), m_ref, l_ref, acc_ref)

    cache_step()

    @pl.when(t == n_cache - 1)
    def _():
        s = _dot_nt(qb[...], kn_ref[...].astype(BF16))
        qpos = past + (_row_iota((rows, tq)) & (tq - 1))
        kpos = past + _lane_iota((rows, tq))
        if mode == "fox":
            s = s - _expand_rows(fkn_ref[...][:, :tq] * LOG2E, tq)
            allowed = kpos <= qpos
        else:
            allowed = (kpos >> CHUNK_SHIFT) <= (qpos >> CHUNK_SHIFT)
        vn = vn_ref[...].astype(BF16)
        _softmax_update(jnp.where(allowed, s, NEG), lambda pb: _dot(pb, vn), m_ref, l_ref, acc_ref)
        o_all = acc_ref[...] / l_ref[...]
        if mode == "fox":
            o = jnp.zeros((tq, D_MODEL), F32)
            hl = _lane_iota((tq, D_MODEL)) >> _log2(FOX_DH)
            for h in range(nh):
                o = jnp.where(hl == h, o_all[h * tq:(h + 1) * tq, :], o)
        else:
            lv = lam_ref[...]
            lam = (jnp.exp(jnp.sum(lv[0:1] * lv[1:2], axis=-1, keepdims=True))
                   - jnp.exp(jnp.sum(lv[2:3] * lv[3:4], axis=-1, keepdims=True)) + lam_init)
            pieces = []
            for h in range(nh // 2):
                a0 = o_all[(2 * h) * tq:(2 * h + 1) * tq, LANES * h:LANES * (h + 1)]
                a1 = o_all[(2 * h + 1) * tq:(2 * h + 2) * tq, LANES * h:LANES * (h + 1)]
                pieces.append(_rms(a0 - lam * a1, gsub_ref[...]) * (1.0 - lam_init))
            o = jnp.concatenate(pieces, axis=-1)
        o_ref[...] = o.astype(BF16)


def _dec_attn(q, kc, vc, kn, vn, extra, *, mode, b, tq, past, tk, lam_init=0.0, name):
    n_cache = past // tk
    nh = D_MODEL // 64
    new = pl.BlockSpec((tq, D_MODEL), lambda i, t: (i, 0))
    cache_t = pl.BlockSpec((None, D_MODEL, tk), lambda i, t: (i, 0, t))
    if mode == "fox":
        (fk,) = extra
        vspec = cache_t
        especs = [pl.BlockSpec((None, nh, tk), lambda i, t: (i, 0, t)),
                  pl.BlockSpec((None, nh, LANES), lambda i, t: (i, 0, past // LANES))]
        ins = [fk, fk]
    else:
        nhv = D_MODEL // LANES
        vspec = pl.BlockSpec((tk * nhv, LANES), lambda i, t: (i * n_cache + t, 0))
        especs = [pl.BlockSpec(x.shape, lambda i, t: (0, 0)) for x in extra]
        ins = list(extra)
    return pl.pallas_call(
        functools.partial(_dec_attn_kernel, mode=mode, n_cache=n_cache, tq=tq, past=past, lam_init=lam_init),
        out_shape=jax.ShapeDtypeStruct((b * tq, D_MODEL), BF16),
        grid=(b, n_cache),
        in_specs=[new, cache_t, vspec, new, new] + especs,
        out_specs=new,
        scratch_shapes=[pltpu.VMEM((nh * tq, D_MODEL), BF16), pltpu.VMEM((nh * tq, 1), F32),
                        pltpu.VMEM((nh * tq, 1), F32), pltpu.VMEM((nh * tq, D_MODEL), F32)],
        compiler_params=_cp(2), name=name)(q, kc, vc, kn, vn, *ins)


def _mla_dec_kernel(q_ref, cc_ref, rc_ref, cn_ref, rn_ref, wukt_ref, wuv_ref, o_ref,
                    qn, qr, m_ref, l_ref, acc_ref, *, n_cache, tq, past):
    t = pl.program_id(1)
    nh = MLA_HEADS
    rows = nh * tq

    @pl.when(t == 0)
    def _():
        q = q_ref[...]
        zero = jnp.zeros((tq, LANES), BF16)
        for h in range(nh):
            qn[h * tq:(h + 1) * tq, :] = jnp.concatenate(
                [q[:, 2 * LANES * h:2 * LANES * h + LANES] if c == h else zero for c in range(nh)], axis=-1)
            qr[h * tq:(h + 1) * tq, :] = q[:, 2 * LANES * h + LANES:2 * LANES * (h + 1)]
        m_ref[...] = jnp.full_like(m_ref, NEG)
        l_ref[...] = jnp.zeros_like(l_ref)
        acc_ref[...] = jnp.zeros_like(acc_ref)

    def step(ckv, s_rope, mask_new):
        c = ckv.astype(BF16)
        knt = _dot_nt(wukt_ref[...], c)
        ms = jnp.concatenate([jnp.mean(jnp.square(knt[MLA_NOPE * h:MLA_NOPE * (h + 1)]), axis=0, keepdims=True)
                              for h in range(nh)], axis=0)
        s = _dot(qn[...], knt.astype(BF16)) * _expand_rows(lax.rsqrt(ms + NORM_EPS), tq) + s_rope
        if mask_new:
            n = ckv.shape[0]
            qpos = past + (_row_iota((rows, n)) & (tq - 1))
            kpos = past + _lane_iota((rows, n))
            s = jnp.where((kpos >> CHUNK_SHIFT) <= (qpos >> CHUNK_SHIFT), s, NEG)
        _softmax_update(s, lambda pb: _dot(pb, c), m_ref, l_ref, acc_ref)

    def cache_step():
        kr = rc_ref[...].astype(BF16)
        step(cc_ref[...], _dot(qr[...], jnp.concatenate([kr, kr], axis=0)), False)

    cache_step()

    @pl.when(t == n_cache - 1)
    def _():
        step(cn_ref[...], _dot_nt(qr[...], rn_ref[...].astype(BF16)), True)
        lat = (acc_ref[...] / l_ref[...]).astype(BF16)
        o_ref[...] = jnp.concatenate(
            [_dot(lat[h * tq:(h + 1) * tq, :], wuv_ref[:, MLA_V * h:MLA_V * (h + 1)]) for h in range(nh)],
            axis=-1).astype(BF16)


def _mla_dec(q, cc, rc, cn, rn, wukt, wuv, *, b, tq, past, tk, name):
    n_cache = past // tk
    nh = MLA_HEADS
    new = lambda n: pl.BlockSpec((tq, n), lambda i, t: (i, 0))
    full = lambda a: pl.BlockSpec(a.shape, lambda i, t: (0,) * a.ndim)
    return pl.pallas_call(
        functools.partial(_mla_dec_kernel, n_cache=n_cache, tq=tq, past=past),
        out_shape=jax.ShapeDtypeStruct((b * tq, nh * MLA_V), BF16),
        grid=(b, n_cache),
        in_specs=[new(nh * 2 * LANES),
                  pl.BlockSpec((tk, MLA_KV_LORA), lambda i, t: (i * n_cache + t, 0)),
                  pl.BlockSpec((None, MLA_ROPE, tk), lambda i, t: (i, 0, t)),
                  new(MLA_KV_LORA), new(LANES), full(wukt), full(wuv)],
        out_specs=new(nh * MLA_V),
        scratch_shapes=[pltpu.VMEM((nh * tq, nh * MLA_NOPE), BF16), pltpu.VMEM((nh * tq, LANES), BF16),
                        pltpu.VMEM((nh * tq, 1), F32), pltpu.VMEM((nh * tq, 1), F32),
                        pltpu.VMEM((nh * tq, MLA_KV_LORA), F32)],
        compiler_params=_cp(2), name=name)(q, cc, rc, cn, rn, wukt, wuv)


def _band_dec_kernel(q_ref, kc_ref, vc_ref, kn_ref, vn_ref, knt_ref, vnt_ref, tab_ref, o_ref, kr_out, vr_out,
                     *, tq, w, past):
    keep_old = _lane_iota((D_MODEL, LANES)) < LANES - tq
    for c_ref, nt_ref, r_out in ((kc_ref, knt_ref, kr_out), (vc_ref, vnt_ref, vr_out)):
        rolled = pltpu.roll(c_ref[...], w - tq, axis=1)
        r_out[:, :w - LANES] = rolled[:, :w - LANES]
        r_out[:, w - LANES:] = jnp.where(keep_old, rolled[:, w - LANES:], nt_ref[...])
    nh = BAND_HEADS
    rows = nh * tq
    bwid = ((w + tq + LANES - 1) // LANES) * LANES
    qb = _block_diag_q(q_ref[...], nh, BAND_DH)
    g = _rel_gather(tab_ref[...], bwid, w + tq) * LOG2E
    bias = jnp.concatenate(
        [pltpu.roll(jnp.broadcast_to(g[h:h + 1, :], (tq, bwid)), bwid - tq, axis=1, stride=1, stride_axis=0)
         for h in range(nh)], axis=0)
    qc = (past + (_row_iota((rows, bwid)) & (tq - 1))) >> CHUNK_SHIFT
    kc = (past - w + _lane_iota((rows, bwid))) >> CHUNK_SHIFT
    bias = jnp.where((kc <= qc) & (kc >= qc - BAND_LEFT_CHUNKS), bias, NEG)
    s_c = _dot(qb, kc_ref[...].astype(BF16)) + bias[:, :w]
    s_n = _dot_nt(qb, kn_ref[...].astype(BF16)) + bias[:, w:w + tq]
    m = jnp.maximum(jnp.max(s_c, axis=-1, keepdims=True), jnp.max(s_n, axis=-1, keepdims=True))
    p_c = jnp.exp2(s_c - m)
    p_n = jnp.exp2(s_n - m)
    l = jnp.sum(p_c, axis=-1, keepdims=True) + jnp.sum(p_n, axis=-1, keepdims=True)
    o_all = (_dot_nt(p_c.astype(BF16), vc_ref[...].astype(BF16))
             + _dot(p_n.astype(BF16), vn_ref[...].astype(BF16))) / l
    o = jnp.zeros((tq, D_MODEL), F32)
    hl = _lane_iota((tq, D_MODEL)) >> _log2(BAND_DH)
    for h in range(nh):
        o = jnp.where(hl == h, o_all[h * tq:(h + 1) * tq, :], o)
    o_ref[...] = o.astype(BF16)


def _band_dec(q, kc, vc, kn, vn, tab, *, b, tq, w, past, name):
    assert w % LANES == 0 and tq <= LANES
    new = pl.BlockSpec((tq, D_MODEL), lambda i: (i, 0))
    cache = pl.BlockSpec((None, D_MODEL, w), lambda i: (i, 0, 0))
    slab = pl.BlockSpec((None, D_MODEL, LANES), lambda i: (i, 0, 0))
    right = lambda x: jnp.pad(jnp.swapaxes(x.reshape(b, tq, D_MODEL), 1, 2), ((0, 0), (0, 0), (LANES - tq, 0)))
    return pl.pallas_call(
        functools.partial(_band_dec_kernel, tq=tq, w=w, past=past),
        out_shape=[jax.ShapeDtypeStruct((b * tq, D_MODEL), BF16), jax.ShapeDtypeStruct(kc.shape, F32),
                   jax.ShapeDtypeStruct(vc.shape, F32)],
        grid=(b,),
        in_specs=[new, cache, cache, new, new, slab, slab, pl.BlockSpec(tab.shape, lambda i: (0, 0))],
        out_specs=[new, cache, cache],
        compiler_params=_cp(1), name=name)(q, kc, vc, kn, vn, right(kn), right(vn), tab)


def _ffn_rows_kernel(*refs, mode, tps, seq, final, tf):
    it = iter(refs)
    h_ref, o_ref, wo_ref, gf_ref, wup_ref, cw_ref, cb_ref, wd_ref = [next(it) for _ in range(8)]
    if mode == "state":
        s1_ref, s2_ref = next(it), next(it)
    if final:
        gfin_ref = next(it)
    out_ref, ug_out, uh_out = next(it), next(it), next(it)
    if mode == "carry":
        carry_ref = next(it)
    tm = h_ref.shape[0]
    h1 = h_ref[...] + _dot(o_ref[...], wo_ref[...])
    xn = _rms(h1, gf_ref[...]).astype(BF16)
    if mode == "carry":
        @pl.when(pl.program_id(0) % tps == 0)
        def _():
            carry_ref[...] = jnp.zeros_like(carry_ref)
    else:
        tpos = _row_iota((tm, tf)) & (seq - 1)
    acts = []
    for j in range(D_FF // tf):
        cs = []
        for part, u_out in enumerate((ug_out, uh_out)):
            tile = slice(tf * j, tf * (j + 1))
            cols = slice(part * D_FF + tf * j, part * D_FF + tf * (j + 1))
            u = _dot(xn, wup_ref[:, cols])
            if mode == "carry":
                tail = u[tm - SUBLANES:]
                u_out[:, tile] = tail
                ext = jnp.concatenate([carry_ref[part, :, tile], u], axis=0)
                carry_ref[part, :, tile] = tail
                um1 = pltpu.roll(ext, 1, axis=0)[SUBLANES:]
                um2 = pltpu.roll(ext, 2, axis=0)[SUBLANES:]
            else:
                u_out[:, tile] = u
                um1 = jnp.where(tpos >= 1, pltpu.roll(u, 1, axis=0), s1_ref[:, cols])
                um2 = jnp.where(tpos >= 2, pltpu.roll(u, 2, axis=0), s2_ref[:, cols])
            cw = cw_ref[:, cols]
            cs.append(((cb_ref[:, cols] + u * cw[2:3]) + um2 * cw[0:1]) + um1 * cw[1:2])
        acts.append((cs[0] * jax.nn.sigmoid(cs[0]) * cs[1]).astype(BF16))
    acc = h1 + _dot(jnp.concatenate(acts, axis=-1), wd_ref[...])
    if final:
        acc = _rms(acc, gfin_ref[...])
    out_ref[...] = acc


def _ffn_rows(h, o, wo, gf, wup, cw, cb, wd, *, layer, tm, tf, seq, state=None, final_g=None, name):
    m = h.shape[0]
    mode = "carry" if state is None else "state"
    tps = seq // tm if mode == "carry" else 1
    row = lambda n: pl.BlockSpec((tm, n), lambda i: (i, 0))
    once = lambda a: pl.BlockSpec(a.shape, lambda i: (0,) * a.ndim, pipeline_mode=pl.Buffered(1))
    of_layer = lambda a: pl.BlockSpec((None,) + a.shape[1:], lambda i: (layer,) + (0,) * (a.ndim - 1),
                                      pipeline_mode=pl.Buffered(1))
    ins = [h, o, wo, gf, wup, cw, cb, wd]
    specs = [row(D_MODEL), row(D_MODEL), once(wo), once(gf), of_layer(wup), once(cw), once(cb), of_layer(wd)]
    scratch = []
    if mode == "state":
        assert m == tm and seq & (seq - 1) == 0
        ins += list(state)
        specs += [row(2 * D_FF), row(2 * D_FF)]
        u_shape = jax.ShapeDtypeStruct((m, D_FF), F32)
        u_spec = row(D_FF)
    else:
        assert seq % tm == 0
        scratch.append(pltpu.VMEM((2, SUBLANES, D_FF), F32))
        u_shape = jax.ShapeDtypeStruct((m // tm, SUBLANES, D_FF), F32)
        u_spec = pl.BlockSpec((None, SUBLANES, D_FF), lambda i: (i, 0, 0))
    if final_g is not None:
        ins.append(final_g)
        specs.append(once(final_g))
    kern = functools.partial(_ffn_rows_kernel, mode=mode, tps=tps, seq=seq, final=final_g is not None, tf=tf)
    return pl.pallas_call(
        kern, out_shape=[jax.ShapeDtypeStruct((m, D_MODEL), F32), u_shape, u_shape],
        grid=(m // tm,), in_specs=specs, out_specs=[row(D_MODEL), u_spec, u_spec], scratch_shapes=scratch,
        compiler_params=_cp(1), name=name)(*ins)


def _rope_tables(pos):
    inv = 1.0 / (ROPE_THETA ** (jnp.arange(0, 64, 2, dtype=F32) / 64))
    ang = pos.astype(F32)[:, None] * inv[None, :]
    c, s = jnp.cos(ang), jnp.sin(ang)
    return jnp.tile(c, (1, 4)), jnp.tile(jnp.concatenate([-s, s], axis=1), (1, 2)), c.T, s.T


def _tile_gain(g, n):
    return jnp.tile(g.astype(F32), n // g.shape[0]).reshape(1, n)


def _gain_t(g, n, w):
    return jnp.broadcast_to(jnp.tile(g.astype(F32), n // g.shape[0])[:, None], (n, w))


def _to_tokens(xt, heads):
    b, n, t = xt.shape
    dh = n // math.prod(heads)
    nd = len(heads)
    return xt.reshape(b, *heads, dh, t).transpose(0, nd + 2, *range(1, nd + 2))


def _to_features(x):
    b, t = x.shape[:2]
    nd = x.ndim
    return x.transpose(0, *range(2, nd), 1).reshape(b, -1, t)


def kernel(x_prompt, x_sample, cache_mla_ckv, cache_mla_krope, cache_fox_k, cache_fox_v, cache_fox_logf,
           cache_diff_k, cache_diff_v, cache_band_k, cache_band_v, state_ffn_conv,
           attn_norm_g, ffn_norm_g, final_norm_g,
           mla_w_dq, mla_g_q, mla_w_uq, mla_w_dkv, mla_g_kv, mla_w_uk, mla_w_uv,
           mla_g_qn, mla_g_qr, mla_g_kn, mla_g_kr, mla_w_o,
           fox_w_qkv, fox_w_f, fox_b_f, fox_g_q, fox_g_k, fox_w_o,
           diff_w_qkv, diff_g_q, diff_g_k, diff_lq1, diff_lk1, diff_lq2, diff_lk2, diff_g_sub, diff_w_o,
           band_w_qkv, band_g_q, band_g_k, band_rel_bias, band_w_o,
           ffn_w_up, ffn_conv_w, ffn_conv_b, ffn_w_down):
    bp, tp, d = x_prompt.shape
    bs, ts, _ = x_sample.shape
    past = cache_mla_ckv.shape[1]
    depth = attn_norm_g.shape[0]
    mp, ms = bp * tp, bs * ts
    tm_p = min(512, tp)
    tm_f = min(512, tp)
    tq = 512
    tq_band = 256
    tkd = min(1024, past)
    tkw = min(2048, past)
    tf = 256
    assert tp % tq == 0 and past % tkd == 0 and past % tkw == 0 and past % CHUNK == 0

    tobf = lambda a: a.astype(BF16)
    rowv = lambda g: g.astype(F32).reshape(1, -1)
    pos_p = jnp.arange(tp, dtype=jnp.int32)
    pos_s = past + jnp.arange(ts, dtype=jnp.int32)
    tabs_p = _rope_tables(pos_p)
    tabs_s = tuple(jnp.tile(a, (bs, 1)) for a in _rope_tables(pos_s)[:2])

    nope_cols = jnp.arange(MLA_HEADS)[:, None] * (MLA_NOPE + MLA_ROPE) + jnp.arange(MLA_NOPE)[None, :]
    rope_cols = jnp.arange(MLA_HEADS)[:, None] * (MLA_NOPE + MLA_ROPE) + MLA_NOPE + jnp.arange(MLA_ROPE)[None, :]
    wuq_perm = jnp.concatenate([mla_w_uq[:, nope_cols.reshape(-1)], mla_w_uq[:, rope_cols.reshape(-1)]], axis=1)
    wr = mla_w_dkv[:, MLA_KV_LORA:]
    mla_w = dict(
        wdq=tobf(mla_w_dq), gq=rowv(mla_g_q), wuq=tobf(wuq_perm), wc=tobf(mla_w_dkv[:, :MLA_KV_LORA]),
        wr=tobf(jnp.concatenate([wr, wr], axis=1)), gkv=rowv(mla_g_kv),
        gqn=_tile_gain(mla_g_qn, MLA_HEADS * MLA_NOPE), gqr=_tile_gain(mla_g_qr, MLA_HEADS * MLA_ROPE),
        gkr=_tile_gain(mla_g_kr, LANES), wuk=tobf(mla_w_uk), wuv=tobf(mla_w_uv),
        gkn=_tile_gain(mla_g_kn, MLA_HEADS * MLA_NOPE), wrt=tobf(wr.T), gkrt=_gain_t(mla_g_kr, MLA_ROPE, tm_p))

    def split_qkv(w):
        wq, wk, wv = w[:, :d], w[:, d:2 * d], w[:, 2 * d:]
        return tobf(wq), tobf(wk.T), tobf(wv), tobf(wv.T)

    fox_wf = tobf(jnp.pad(fox_w_f, ((0, 0), (0, LANES - FOX_HEADS))))
    fox_bf = jnp.pad(fox_b_f.astype(F32), (0, LANES - FOX_HEADS)).reshape(1, LANES)
    fox_wft = tobf(fox_w_f.T)
    fox_bft = jnp.broadcast_to(fox_b_f.astype(F32)[:, None], (FOX_HEADS, tm_p))
    lamv = jnp.stack([diff_lq1, diff_lk1, diff_lq2, diff_lk2]).astype(F32)
    gsub = rowv(diff_g_sub)
    tab_pad = jnp.pad(band_rel_bias.astype(F32), ((0, 0), (0, REL_PAD - band_rel_bias.shape[1])))
    wo = [tobf(mla_w_o), tobf(fox_w_o), tobf(diff_w_o), tobf(band_w_o)]
    wup, wdn = tobf(ffn_w_up), tobf(ffn_w_down)
    cwf, cbf = ffn_conv_w.astype(F32), ffn_conv_b.astype(F32)

    h_p = x_prompt.reshape(mp, d)
    h_s = x_sample.reshape(ms, d)
    outs = {}
    conv_p, conv_s = [], []
    for i in range(depth):
        kind = i % 4
        ga = rowv(attn_norm_g[i])
        if kind == 0:
            scale = (MLA_NOPE + MLA_ROPE) ** -0.5 * LOG2E
            q_p, ckv_p, _, _, kc_p, v_p, krt_p = _mla_proj(h_p, ga, mla_w, tabs_p, b=bp, t=tp, tm=tm_p, with_kv=True,
                                                           scale=scale, name="mla_proj_p")
            o_p = _mla_attn(q_p, kc_p, v_p, b=bp, t=tp, tq=tq, hps=2, name="mla_attn_p")
            q_s, ckv_s, kr_s, krd_s = _mla_proj(h_s, ga, mla_w, tabs_s, b=bs, t=ts, tm=ms, with_kv=False,
                                                scale=scale, name="mla_proj_s")
            o_s = _mla_dec(q_s, cache_mla_ckv.astype(F32).reshape(bs * past, MLA_KV_LORA),
                           _to_features(cache_mla_krope.astype(F32)), ckv_s, krd_s,
                           tobf(mla_w_uk.T), mla_w["wuv"], b=bs, tq=ts, past=past, tk=tkd, name="mla_attn_s")
            outs["mla"] = (ckv_p.reshape(bp, tp, MLA_KV_LORA), jnp.swapaxes(krt_p, 1, 2),
                           ckv_s.reshape(bs, ts, MLA_KV_LORA), kr_s.reshape(bs, ts, MLA_ROPE))
        elif kind == 1:
            scale = FOX_DH ** -0.5 * LOG2E
            gq, gk = _tile_gain(fox_g_q, d), _tile_gain(fox_g_k, d)
            wq, wkt, wv, wvt = split_qkv(fox_w_qkv)
            q_p, kt_p, vt_p, lft_p = _qkv_proj_t(h_p, ga, wq, wkt, wvt, gq, _gain_t(fox_g_k, d, tm_p), b=bp, t=tp,
                                                 tm=tm_p, wft=fox_wft, bft=fox_bft, scale=scale, name="fox_proj_p")
            fk_p = _cumsum_last(lft_p, tk=tq, name="fox_cumsum_p")
            o_p = _pair_attn(q_p, kt_p, vt_p, (fk_p.reshape(bp, FOX_HEADS // 2, 2, tp),), mode="fox",
                             b=bp, t=tp, tq=tq, name="fox_attn_p")
            q_s, k_s, v_s, lf_s = _qkv_proj(h_s, ga, tobf(fox_w_qkv), gq, gk, tm=ms, wf=fox_wf, bf=fox_bf,
                                            scale=scale, name="fox_proj_s")
            lf_all = jnp.concatenate([jnp.swapaxes(cache_fox_logf.astype(F32), 1, 2),
                                      jnp.swapaxes(lf_s.reshape(bs, ts, FOX_HEADS), 1, 2)], axis=2)
            lf_all = jnp.pad(lf_all, ((0, 0), (0, 0), (0, tkw - ts)))
            f_all = _cumsum_last(lf_all, tk=tq, name="fox_cumsum_s")
            o_s = _dec_attn(q_s, _to_features(cache_fox_k.astype(F32)), _to_features(cache_fox_v.astype(F32)),
                            k_s, v_s, (f_all,), mode="fox", b=bs, tq=ts, past=past, tk=tkw, name="fox_attn_s")
            sh = (FOX_HEADS, FOX_DH)
            outs["fox"] = (_to_tokens(kt_p, (FOX_HEADS,)), _to_tokens(vt_p, (FOX_HEADS,)),
                           jnp.swapaxes(lft_p, 1, 2),
                           k_s.reshape(bs, ts, *sh), v_s.reshape(bs, ts, *sh), lf_s.reshape(bs, ts, FOX_HEADS))
        elif kind == 2:
            scale = DIFF_DH ** -0.5 * LOG2E
            lam_init = 0.8 - 0.6 * math.exp(-0.3 * i)
            gq, gk = _tile_gain(diff_g_q, d), _tile_gain(diff_g_k, d)
            wq, wkt, wv, wvt = split_qkv(diff_w_qkv)
            q_p, kt_p, v4_p = _qkv_proj_t(h_p, ga, wq, wkt, wv, gq, _gain_t(diff_g_k, d, tm_p), b=bp, t=tp, tm=tm_p,
                                          rope_tabs=tabs_p, v_tokens=True, scale=scale, name="diff_proj_p")
            v_p = v4_p.reshape(bp, tp, DIFF_HEADS, 2 * DIFF_DH)
            o_p = _pair_attn(q_p, kt_p, v4_p, (lamv, gsub), mode="diff", b=bp, t=tp, tq=tq, lam_init=lam_init,
                             name="diff_attn_p")
            q_s, k_s, v_s = _qkv_proj(h_s, ga, tobf(diff_w_qkv), gq, gk, tm=ms, rope_tabs=tabs_s, scale=scale,
                                      name="diff_proj_s")
            o_s = _dec_attn(q_s, _to_features(cache_diff_k.astype(F32)),
                            cache_diff_v.astype(F32).reshape(bs * past * DIFF_HEADS, 2 * DIFF_DH), k_s, v_s,
                            (lamv, gsub), mode="diff", b=bs, tq=ts, past=past, tk=tkw, lam_init=lam_init,
                            name="diff_attn_s")
            outs["diff"] = (_to_tokens(kt_p, (DIFF_HEADS, 2)), v_p,
                            k_s.reshape(bs, ts, DIFF_HEADS, 2, DIFF_DH), v_s.reshape(bs, ts, DIFF_HEADS, 2 * DIFF_DH))
        else:
            scale = BAND_DH ** -0.5 * LOG2E
            gq, gk = _tile_gain(band_g_q, d), _tile_gain(band_g_k, d)
            wq, wkt, wv, wvt = split_qkv(band_w_qkv)
            keep = min(BAND_LEFT, tp)
            assert keep == tm_p
            q_p, kt_p, vt_p, kt_keep, vt_keep = _qkv_proj_t(
                h_p, ga, wq, wkt, wvt, gq, _gain_t(band_g_k, d, tm_p), b=bp, t=tp, tm=tm_p, tail=True, scale=scale,
                name="band_proj_p")
            tab_pairs = jnp.pad(tab_pad.reshape(BAND_HEADS // 2, 2, REL_PAD), ((0, 0), (0, SUBLANES - 2), (0, 0)))
            o_p = _band_attn(q_p, kt_p, vt_p, tab_pairs, b=bp, t=tp, tq=tq_band, name="band_attn_p")
            q_s, k_s, v_s = _qkv_proj(h_s, ga, tobf(band_w_qkv), gq, gk, tm=ms, scale=scale, name="band_proj_s")
            w = cache_band_k.shape[1]
            kct = _to_features(cache_band_k.astype(F32))
            vct = _to_features(cache_band_v.astype(F32))
            o_s, k_roll, v_roll = _band_dec(q_s, kct, vct, k_s, v_s, tab_pad, b=bs, tq=ts, w=w, past=past,
                                            name="band_attn_s")
            outs["band"] = (_to_tokens(kt_keep, (BAND_HEADS,)), _to_tokens(vt_keep, (BAND_HEADS,)),
                            _to_tokens(k_roll, (BAND_HEADS,)), _to_tokens(v_roll, (BAND_HEADS,)))

        last = i == depth - 1
        gfin = rowv(final_norm_g) if last else None
        gfn = rowv(ffn_norm_g[i])
        h_p, ug, uh = _ffn_rows(h_p, o_p, wo[kind], gfn, wup, cwf[i], cbf[i].reshape(1, -1), wdn, layer=i, tm=tm_f,
                                tf=tf, seq=tp, final_g=gfin, name="ffn_p")
        tps = tp // tm_f
        conv_p.append(jnp.concatenate([ug, uh], axis=-1)[tps - 1::tps, SUBLANES - (CONV_W - 1):])
        st = state_ffn_conv[i].astype(F32)
        s1 = jnp.pad(st[:, 1:], ((0, 0), (0, ts - 1), (0, 0))).reshape(ms, 2 * D_FF)
        s2 = jnp.pad(st, ((0, 0), (0, ts - 2), (0, 0))).reshape(ms, 2 * D_FF)
        h_s, ug, uh = _ffn_rows(h_s, o_s, wo[kind], gfn, wup, cwf[i], cbf[i].reshape(1, -1), wdn, layer=i, tm=ms,
                                tf=tf, seq=ts, state=(s1, s2), final_g=gfin, name="ffn_s")
        u_s = jnp.concatenate([ug, uh], axis=-1).reshape(bs, ts, 2 * D_FF)
        conv_s.append(jnp.concatenate([st, u_s], axis=1)[:, ts:])

    y_prompt = h_p.reshape(bp, tp, d)
    y_sample = h_s.reshape(bs, ts, d)
    return (y_prompt, y_sample) + outs["mla"] + outs["fox"] + outs["diff"] + outs["band"] + (
        jnp.stack(conv_p, axis=0), jnp.stack(conv_s, axis=0))
```

```python
import functools
import math

import jax
import jax.numpy as jnp
from jax import lax
from jax.experimental import pallas as pl
from jax.experimental.pallas import tpu as pltpu

F32 = jnp.float32
BF16 = jnp.bfloat16

D_MODEL = 1024
CHUNK = 64
CHUNK_SHIFT = CHUNK.bit_length() - 1
ROPE_DIM = 64
ROPE_THETA = 10000.0
NORM_EPS = 1e-6
MLA_HEADS, MLA_Q_LORA, MLA_KV_LORA, MLA_NOPE, MLA_ROPE, MLA_V = 8, 384, 256, 128, 64, 128
FOX_HEADS, FOX_DH = 16, 64
DIFF_HEADS, DIFF_DH = 8, 64
BAND_HEADS, BAND_DH = 16, 64
BAND_LEFT_CHUNKS = 8
BAND_LEFT = BAND_LEFT_CHUNKS * CHUNK
REL_CLIP = 128
D_FF = 2816
CONV_W = 3

LANES = 128
SUBLANES = 8
NEG = -1e30
REL_PAD = 384
LOG2E = math.log2(math.e)
VMEM_LIMIT = 56 * 1024 * 1024


def _cp(n_axes):
    return pltpu.CompilerParams(dimension_semantics=("arbitrary",) * n_axes, vmem_limit_bytes=VMEM_LIMIT)


def _dot(a, b):
    return jnp.dot(a, b, preferred_element_type=F32)


def _dot_nt(a, b):
    return lax.dot_general(a, b, (((1,), (1,)), ((), ())), preferred_element_type=F32)


def _rms(x, g):
    ms = jnp.mean(x * x, axis=-1, keepdims=True)
    return x * lax.rsqrt(ms + NORM_EPS) * g


def _lane_iota(shape):
    return lax.broadcasted_iota(jnp.int32, shape, len(shape) - 1)


def _row_iota(shape):
    return lax.broadcasted_iota(jnp.int32, shape, len(shape) - 2)


def _log2(n):
    assert n & (n - 1) == 0, n
    return n.bit_length() - 1


def _head_rms(x, g, seg):
    n = x.shape[1]
    outs = []
    for c in range(n // LANES):
        xs = x[:, LANES * c:LANES * (c + 1)]
        sq = xs * xs
        if seg == LANES:
            r = lax.rsqrt(jnp.sum(sq, axis=-1, keepdims=True) * (1.0 / seg) + NORM_EPS)
        else:
            lo = _lane_iota(xs.shape) < seg
            s_lo = jnp.sum(jnp.where(lo, sq, 0.0), axis=-1, keepdims=True)
            s_hi = jnp.sum(jnp.where(lo, 0.0, sq), axis=-1, keepdims=True)
            r = jnp.where(lo, lax.rsqrt(s_lo * (1.0 / seg) + NORM_EPS), lax.rsqrt(s_hi * (1.0 / seg) + NORM_EPS))
        outs.append(xs * r * g[:, LANES * c:LANES * (c + 1)])
    return jnp.concatenate(outs, axis=-1) if len(outs) > 1 else outs[0]


def _head_rms_t(x, g, seg=64):
    outs = []
    for hd in range(x.shape[0] // seg):
        xs = x[seg * hd:seg * (hd + 1)]
        r = lax.rsqrt(jnp.sum(xs * xs, axis=0, keepdims=True) * (1.0 / seg) + NORM_EPS)
        outs.append(xs * r * g[seg * hd:seg * (hd + 1)])
    return jnp.concatenate(outs, axis=0) if len(outs) > 1 else outs[0]


def _rope(x, cos, sinp):
    half = ROPE_DIM // 2
    first_half = (_lane_iota((x.shape[0], LANES)) & (ROPE_DIM - 1)) < half
    outs = []
    for c in range(x.shape[1] // LANES):
        xs = x[:, LANES * c:LANES * (c + 1)]
        fwd = pltpu.roll(xs, half, axis=1)
        bwd = pltpu.roll(xs, LANES - half, axis=1)
        outs.append(xs * cos + jnp.where(first_half, bwd, fwd) * sinp)
    return jnp.concatenate(outs, axis=-1) if len(outs) > 1 else outs[0]


def _rope_t(x, cos_t, sin_t):
    outs = []
    for hd in range(x.shape[0] // 64):
        x1 = x[64 * hd:64 * hd + 32]
        x2 = x[64 * hd + 32:64 * (hd + 1)]
        outs += [x1 * cos_t - x2 * sin_t, x2 * cos_t + x1 * sin_t]
    return jnp.concatenate(outs, axis=0)


def _split3(x):
    hi = x.astype(BF16)
    r = x - hi.astype(F32)
    mid = r.astype(BF16)
    lo = (r - mid.astype(F32)).astype(BF16)
    return hi, mid, lo


def _softmax_step(s, pv, m, l, acc):
    m_new = jnp.maximum(m, jnp.max(s, axis=-1, keepdims=True))
    alpha = jnp.exp2(m - m_new)
    p = jnp.exp2(s - m_new)
    if l is not None:
        l = alpha * l + jnp.sum(p, axis=-1, keepdims=True)
    return m_new, l, alpha * acc + pv(p.astype(BF16))


def _softmax_update(s, pv, m_ref, l_ref, acc_ref):
    m, l, acc = _softmax_step(s, pv, m_ref[...], l_ref[...], acc_ref[...])
    m_ref[...] = m
    l_ref[...] = l
    acc_ref[...] = acc


def _log_sigmoid(z):
    return jnp.minimum(z, 0.0) - jnp.log1p(jnp.exp(-jnp.abs(z)))


def _qkv_proj_kernel(*refs, rope, logf, scale):
    it = iter(refs)
    h_ref, g_ref, w_ref, gq_ref, gk_ref = next(it), next(it), next(it), next(it), next(it)
    cos_ref = sin_ref = wf_ref = bf_ref = None
    if rope:
        cos_ref, sin_ref = next(it), next(it)
    if logf:
        wf_ref, bf_ref = next(it), next(it)
    q_out, k_out, v_out = next(it), next(it), next(it)
    a = _rms(h_ref[...], g_ref[...]).astype(BF16)
    qkv = _dot(a, w_ref[...])
    q = _head_rms(qkv[:, :D_MODEL], gq_ref[...], 64)
    k = _head_rms(qkv[:, D_MODEL:2 * D_MODEL], gk_ref[...], 64)
    if rope:
        q = _rope(q, cos_ref[...], sin_ref[...])
        k = _rope(k, cos_ref[...], sin_ref[...])
    q_out[...] = (q * scale).astype(BF16)
    k_out[...] = k
    v_out[...] = qkv[:, 2 * D_MODEL:]
    if logf:
        logf_out = next(it)
        lf = _log_sigmoid(_dot(a, wf_ref[...]) + bf_ref[...])
        logf_out[...] = lf[:, :FOX_HEADS]


def _qkv_proj(h, g, w, gq, gk, *, tm, rope_tabs=None, wf=None, bf=None, scale, name):
    m = h.shape[0]
    grid = (m // tm,)
    row = lambda n: pl.BlockSpec((tm, n), lambda i: (i, 0))
    full = lambda a: pl.BlockSpec(a.shape, lambda i: (0,) * a.ndim)
    ins = [h, g, w, gq, gk]
    specs = [row(D_MODEL), full(g), full(w), full(gq), full(gk)]
    if rope_tabs is not None:
        cos, sinp = rope_tabs
        nt = cos.shape[0] // tm
        tab = pl.BlockSpec((tm, LANES), lambda i: (i % nt, 0))
        ins += [cos, sinp]
        specs += [tab, tab]
    if wf is not None:
        ins += [wf, bf]
        specs += [full(wf), full(bf)]
    outs = [jax.ShapeDtypeStruct((m, D_MODEL), BF16), jax.ShapeDtypeStruct((m, D_MODEL), F32),
            jax.ShapeDtypeStruct((m, D_MODEL), F32)]
    ospecs = [row(D_MODEL), row(D_MODEL), row(D_MODEL)]
    if wf is not None:
        outs.append(jax.ShapeDtypeStruct((m, FOX_HEADS), F32))
        ospecs.append(row(FOX_HEADS))
    kern = functools.partial(_qkv_proj_kernel, rope=rope_tabs is not None, logf=wf is not None, scale=scale)
    return pl.pallas_call(kern, out_shape=outs, grid=grid, in_specs=specs, out_specs=ospecs,
                          compiler_params=_cp(1), name=name)(*ins)


def _qkv_proj_t_kernel(*refs, rope, logf, v_tokens, tail, scale):
    it = iter(refs)
    h_ref, g_ref, wq_ref, wkt_ref, wv_ref, gq_ref, gkt_ref = [next(it) for _ in range(7)]
    if rope:
        cos_ref, sin_ref, cost_ref, sint_ref = [next(it) for _ in range(4)]
    if logf:
        wft_ref, bft_ref = next(it), next(it)
    q_out, kt_out, v_out = next(it), next(it), next(it)
    a = _rms(h_ref[...], g_ref[...]).astype(BF16)
    tm = a.shape[0]
    q = _head_rms(_dot(a, wq_ref[...]), gq_ref[...], 64)
    kt = _head_rms_t(_dot_nt(wkt_ref[...], a), gkt_ref[...])
    if rope:
        q = _rope(q, cos_ref[...], sin_ref[...])
        kt = _rope_t(kt, cost_ref[...], sint_ref[...])
    q_out[...] = (q * scale).astype(BF16)
    kt_out[...] = kt
    if v_tokens:
        v = _dot(a, wv_ref[...])
        nh = D_MODEL // LANES
        for hd in range(nh):
            v_out[pl.ds(hd, tm, stride=nh), :] = v[:, LANES * hd:LANES * (hd + 1)]
    else:
        vt = _dot_nt(wv_ref[...], a)
        v_out[...] = vt
    if logf:
        lft_out = next(it)
        lft_out[...] = _log_sigmoid(_dot_nt(wft_ref[...], a) + bft_ref[...])
    if tail:
        kt_tail, vt_tail = next(it), next(it)
        kt_tail[...] = kt
        vt_tail[...] = vt


def _qkv_proj_t(h, g, wq, wkt, wv, gq, gkt, *, b, t, tm, rope_tabs=None, wft=None, bft=None, v_tokens=False,
                tail=False, scale, name):
    m = b * t
    nt = t // tm
    nh = D_MODEL // LANES
    row = lambda n: pl.BlockSpec((tm, n), lambda i: (i, 0))
    full = lambda a: pl.BlockSpec(a.shape, lambda i: (0,) * a.ndim)
    feat = lambda n: pl.BlockSpec((None, n, tm), lambda i: (i // nt, 0, i % nt))
    ins = [h, g, wq, wkt, wv, gq, gkt]
    specs = [row(D_MODEL)] + [full(x) for x in ins[1:]]
    if rope_tabs is not None:
        cos, sinp, cos_t, sin_t = rope_tabs
        ins += [cos, sinp, cos_t, sin_t]
        tab = pl.BlockSpec((tm, LANES), lambda i: (i % nt, 0))
        tab_t = pl.BlockSpec((32, tm), lambda i: (0, i % nt))
        specs += [tab, tab, tab_t, tab_t]
    if wft is not None:
        ins += [wft, bft]
        specs += [full(wft), full(bft)]
    outs = [jax.ShapeDtypeStruct((m, D_MODEL), BF16), jax.ShapeDtypeStruct((b, D_MODEL, t), F32)]
    ospecs = [row(D_MODEL), feat(D_MODEL)]
    if v_tokens:
        outs.append(jax.ShapeDtypeStruct((m * nh, LANES), F32))
        ospecs.append(pl.BlockSpec((tm * nh, LANES), lambda i: (i, 0)))
    else:
        outs.append(jax.ShapeDtypeStruct((b, D_MODEL, t), F32))
        ospecs.append(feat(D_MODEL))
    if wft is not None:
        outs.append(jax.ShapeDtypeStruct((b, FOX_HEADS, t), F32))
        ospecs.append(feat(FOX_HEADS))
    if tail:
        assert not v_tokens
        outs += [jax.ShapeDtypeStruct((b, D_MODEL, tm), F32)] * 2
        ospecs += [pl.BlockSpec((None, D_MODEL, tm), lambda i: (i // nt, 0, 0))] * 2
    kern = functools.partial(_qkv_proj_t_kernel, rope=rope_tabs is not None, logf=wft is not None,
                             v_tokens=v_tokens, tail=tail, scale=scale)
    return pl.pallas_call(kern, out_shape=outs, grid=(m // tm,), in_specs=specs, out_specs=ospecs,
                          compiler_params=_cp(1), name=name)(*ins)


def _mla_proj_kernel(*refs, with_kv, scale):
    it = iter(refs)
    (h_ref, g_ref, wdq_ref, gq_ref, wuq_ref, wc_ref, wr_ref, gkv_ref, gqn_ref, gqr_ref, gkr_ref, gkn_ref,
     cos_ref, sin_ref) = [next(it) for _ in range(14)]
    if with_kv:
        wuk_ref, wuv_ref, wrt_ref, gkrt_ref, cost_ref, sint_ref = [next(it) for _ in range(6)]
    q_out, ckv_out, kr_out, krd_out = next(it), next(it), next(it), next(it)
    a = _rms(h_ref[...], g_ref[...]).astype(BF16)
    cq = _rms(_dot(a, wdq_ref[...]), gq_ref[...]).astype(BF16)
    q = _dot(cq, wuq_ref[...])
    n_nope = MLA_HEADS * MLA_NOPE
    qn = _head_rms(q[:, :n_nope], gqn_ref[...], MLA_NOPE) * scale
    if not with_kv:
        qn = qn * gkn_ref[...]
    qr = _rope(_head_rms(q[:, n_nope:], gqr_ref[...], MLA_ROPE), cos_ref[...], sin_ref[...]) * scale
    lo = _lane_iota((q.shape[0], LANES)) < MLA_ROPE
    pieces = []
    for hd in range(MLA_HEADS):
        slab = qr[:, LANES * (hd // 2):LANES * (hd // 2 + 1)]
        keep = lo if hd % 2 == 0 else jnp.logical_not(lo)
        pieces += [qn[:, LANES * hd:LANES * (hd + 1)], jnp.where(keep, slab, 0.0)]
    q_out[...] = jnp.concatenate(pieces, axis=-1).astype(BF16)
    ckv = _rms(_dot(a, wc_ref[...]), gkv_ref[...])
    ckv_out[...] = ckv
    kr2 = _rope(_head_rms(_dot(a, wr_ref[...]), gkr_ref[...], MLA_ROPE), cos_ref[...], sin_ref[...])
    kr_out[...] = kr2[:, :MLA_ROPE]
    krd_out[...] = kr2
    if with_kv:
        kc_out, v_out, krt_out = next(it), next(it), next(it)
        c = ckv.astype(BF16)
        kn = _head_rms(_dot(c, wuk_ref[...]), gkn_ref[...], MLA_NOPE)
        pieces = []
        for hd in range(MLA_HEADS):
            pieces += [kn[:, LANES * hd:LANES * (hd + 1)], kr2]
        kc_out[...] = jnp.concatenate(pieces, axis=-1).astype(BF16)
        v_out[...] = _dot(c, wuv_ref[...]).astype(BF16)
        krt_out[...] = _rope_t(_head_rms_t(_dot_nt(wrt_ref[...], a), gkrt_ref[...]), cost_ref[...], sint_ref[...])


def _mla_proj(h, g, w, rope_tabs, *, b, t, tm, with_kv, scale, name):
    m = h.shape[0]
    cos, sinp = rope_tabs[:2]
    nt = cos.shape[0] // tm
    row = lambda n: pl.BlockSpec((tm, n), lambda i: (i, 0))
    full = lambda a: pl.BlockSpec(a.shape, lambda i: (0,) * a.ndim)
    tab = pl.BlockSpec((tm, LANES), lambda i: (i % nt, 0))
    ins = [h, g, w["wdq"], w["gq"], w["wuq"], w["wc"], w["wr"], w["gkv"], w["gqn"], w["gqr"], w["gkr"], w["gkn"],
           cos, sinp]
    specs = [row(D_MODEL)] + [full(x) for x in ins[1:12]] + [tab, tab]
    qw = MLA_HEADS * 2 * LANES
    outs = [jax.ShapeDtypeStruct((m, qw), BF16), jax.ShapeDtypeStruct((m, MLA_KV_LORA), F32),
            jax.ShapeDtypeStruct((m, MLA_ROPE), F32), jax.ShapeDtypeStruct((m, LANES), F32)]
    ospecs = [row(qw), row(MLA_KV_LORA), row(MLA_ROPE), row(LANES)]
    if with_kv:
        cos_t, sin_t = rope_tabs[2:]
        tab_t = pl.BlockSpec((32, tm), lambda i: (0, i % nt))
        extra = [w["wuk"], w["wuv"], w["wrt"], w["gkrt"]]
        ins += extra + [cos_t, sin_t]
        specs += [full(x) for x in extra] + [tab_t, tab_t]
        outs += [jax.ShapeDtypeStruct((m, qw), BF16), jax.ShapeDtypeStruct((m, MLA_HEADS * MLA_V), BF16),
                 jax.ShapeDtypeStruct((b, MLA_ROPE, t), F32)]
        ospecs += [row(qw), row(MLA_HEADS * MLA_V),
                   pl.BlockSpec((None, MLA_ROPE, tm), lambda i: (i // nt, 0, i % nt))]
    kern = functools.partial(_mla_proj_kernel, with_kv=with_kv, scale=scale)
    return pl.pallas_call(kern, out_shape=outs, grid=(m // tm,), in_specs=specs, out_specs=ospecs,
                          compiler_params=_cp(1), name=name)(*ins)


def _cumsum_kernel(x_ref, o_ref, *, tk):
    nh, t = x_ref.shape
    tri = (_row_iota((tk, tk)) <= _lane_iota((tk, tk))).astype(BF16)
    carry = jnp.zeros((nh, 1), F32)
    for c in range(t // tk):
        hi, mid, lo = _split3(x_ref[:, tk * c:tk * (c + 1)])
        y = _dot(jnp.concatenate([hi, mid, lo], axis=0), tri)
        f = (y[:nh] + y[nh:2 * nh]) + y[2 * nh:] + carry
        o_ref[:, tk * c:tk * (c + 1)] = f
        carry = f[:, tk - 1:tk]


def _cumsum_last(x, *, tk, name):
    b, nh, t = x.shape
    spec = pl.BlockSpec((None, nh, t), lambda i: (i, 0, 0))
    return pl.pallas_call(functools.partial(_cumsum_kernel, tk=tk), out_shape=jax.ShapeDtypeStruct(x.shape, F32),
                          grid=(b,), in_specs=[spec], out_specs=spec, compiler_params=_cp(1), name=name)(x)


def _mla_attn_kernel(q_ref, k_ref, v_ref, o_ref, *, tq, hps):
    kw = 2 * LANES
    t = k_ref.shape[0]
    for qi in range(t // tq):
        rows = slice(tq * qi, tq * (qi + 1))
        limit = (((qi * tq + _row_iota((tq, 1))) >> CHUNK_SHIFT) + 1) << CHUNK_SHIFT
        outs = []
        n0 = qi * tq
        for h in range(hps):
            q = q_ref[rows, kw * h:kw * (h + 1)]
            kcols = slice(kw * h, kw * (h + 1))
            vcols = slice(MLA_V * h, MLA_V * (h + 1))
            s_d = jnp.where(n0 + _lane_iota((1, tq)) < limit, _dot_nt(q, k_ref[n0:n0 + tq, kcols]), NEG)
            m = jnp.max(s_d, axis=-1, keepdims=True)
            if qi:
                s_f = _dot_nt(q, k_ref[0:n0, kcols])
                m = jnp.maximum(m, jnp.max(s_f, axis=-1, keepdims=True))
            p_d = jnp.exp2(s_d - m)
            acc = _dot(p_d.astype(BF16), v_ref[n0:n0 + tq, vcols])
            l = jnp.sum(p_d, axis=-1, keepdims=True)
            if qi:
                p_f = jnp.exp2(s_f - m)
                acc = acc + _dot(p_f.astype(BF16), v_ref[0:n0, vcols])
                l = l + jnp.sum(p_f, axis=-1, keepdims=True)
            outs.append(acc / l)
        o_ref[rows, :] = jnp.concatenate(outs, axis=-1).astype(BF16)


def _mla_attn(q, kc, v, *, b, t, tq, hps, name):
    kw = 2 * LANES * hps
    vw = MLA_V * hps
    return pl.pallas_call(
        functools.partial(_mla_attn_kernel, tq=tq, hps=hps),
        out_shape=jax.ShapeDtypeStruct((b * t, MLA_HEADS * MLA_V), BF16),
        grid=(b, MLA_HEADS // hps),
        in_specs=[pl.BlockSpec((t, kw), lambda i, h: (i, h)),
                  pl.BlockSpec((t, kw), lambda i, h: (i, h)),
                  pl.BlockSpec((t, vw), lambda i, h: (i, h))],
        out_specs=pl.BlockSpec((t, vw), lambda i, h: (i, h)),
        compiler_params=_cp(2), name=name)(q, kc, v)


def _ones_row_values(vt, c):
    row = _row_iota(vt.shape)
    if c == 0:
        return jnp.where(row < 64, vt, jnp.where(row == 64, 1.0, 0.0)).astype(BF16)
    return jnp.where(row >= 64, vt, jnp.where(row == 0, 1.0, 0.0)).astype(BF16)


def _merge_pair(acc0, acc1):
    lo = _lane_iota(acc0.shape) < 64
    return jnp.where(lo, acc0 / acc0[:, 64:65], acc1 / acc1[:, 0:1])


def _pair_attn_kernel(*refs, mode, tq, lam_init):
    if mode == "fox":
        q_ref, kt_ref, vt_ref, fk_ref, o_ref, ka, va = refs
    else:
        q_ref, kt_ref, v_ref, lam_ref, gsub_ref, o_ref, ka, vb = refs
    t = kt_ref.shape[1]
    if mode == "fox":
        ka[0:LANES, :] = kt_ref[...].astype(BF16)
        f = fk_ref[...] * (-LOG2E)
        terms = [x.astype(F32) for x in _split3(f)]
        nr = 2 * SUBLANES
        row = _row_iota((nr, t))
        aug = jnp.zeros((nr, t), F32)
        for c in range(2):
            for j in range(3):
                aug = jnp.where(row == 3 * c + j, terms[j][c:c + 1, :], aug)
        ka[LANES:LANES + nr, :] = aug.astype(BF16)
        ka[LANES + nr:, :] = jnp.zeros((LANES - nr, t), BF16)
        vt = vt_ref[...]
        va[0] = _ones_row_values(vt, 0)
        va[1] = _ones_row_values(vt, 1)
    else:
        ka[...] = kt_ref[...].astype(BF16)
        nhv = v_ref.shape[0] // t
        vb[0:LANES, :] = v_ref[pl.ds(pl.program_id(1), t, stride=nhv), :].T.astype(BF16)
        vb[LANES:, :] = jnp.where(_row_iota((LANES, t)) == 0, 1.0, 0.0).astype(BF16)

    def q_tile(qi, q):
        lane = _lane_iota((tq, LANES))
        zero = jnp.zeros_like(q)
        qa = [jnp.where(lane < 64, q, zero), jnp.where(lane < 64, zero, q)]
        qpos = qi * tq + _row_iota((tq, 1))
        if mode == "fox":
            limit = qpos + 1
            pick = [jnp.where(lane < 3 * c, 0.0, jnp.where(lane < 3 * c + 3, 1.0, 0.0)).astype(BF16)
                    for c in range(2)]
            qa = [jnp.concatenate([qa[c], pick[c]], axis=-1) for c in range(2)]
        else:
            limit = ((qpos >> CHUNK_SHIFT) + 1) << CHUNK_SHIFT

        n0 = qi * tq
        accs = []
        for c in range(2):
            s_d = jnp.where(n0 + _lane_iota((1, tq)) < limit, _dot(qa[c], ka[:, n0:n0 + tq]), NEG)
            m = jnp.max(s_d, axis=-1, keepdims=True)
            if qi:
                s_f = _dot(qa[c], ka[:, 0:n0])
                m = jnp.maximum(m, jnp.max(s_f, axis=-1, keepdims=True))
            parts = [(jnp.exp2(s_d - m), n0, n0 + tq)] + ([(jnp.exp2(s_f - m), 0, n0)] if qi else [])
            acc = 0.0
            for p, lo_k, hi_k in parts:
                vals = va[c, :, lo_k:hi_k] if mode == "fox" else vb[:, lo_k:hi_k]
                acc = acc + _dot_nt(p.astype(BF16), vals)
            accs.append(acc)
        if mode == "fox":
            o = _merge_pair(*accs)
        else:
            lv = lam_ref[...]
            lam = (jnp.exp(jnp.sum(lv[0:1] * lv[1:2], axis=-1, keepdims=True))
                   - jnp.exp(jnp.sum(lv[2:3] * lv[3:4], axis=-1, keepdims=True)) + lam_init)
            o0, o1 = [a[:, :LANES] / a[:, LANES:LANES + 1] for a in accs]
            o = _rms(o0 - lam * o1, gsub_ref[...]) * (1.0 - lam_init)
        return o.astype(BF16)

    for qi in range(t // tq):
        o_ref[tq * qi:tq * (qi + 1), :] = q_tile(qi, q_ref[tq * qi:tq * (qi + 1), :])


def _pair_attn(q, kt, v, extra, *, mode, b, t, tq, lam_init=0.0, name):
    npair = D_MODEL // LANES
    qspec = pl.BlockSpec((t, LANES), lambda i, p: (i, p))
    ktspec = pl.BlockSpec((None, LANES, t), lambda i, p: (i, p, 0))
    if mode == "fox":
        vspec = ktspec
        especs = [pl.BlockSpec((None, None, 2, t), lambda i, p: (i, p, 0, 0))]
        scratch = [pltpu.VMEM((2 * LANES, t), BF16), pltpu.VMEM((2, LANES, t), BF16)]
    else:
        vspec = pl.BlockSpec((t * npair, LANES), lambda i, p: (i, 0))
        especs = [pl.BlockSpec(x.shape, lambda i, p: (0, 0)) for x in extra]
        scratch = [pltpu.VMEM((LANES, t), BF16), pltpu.VMEM((2 * LANES, t), BF16)]
    return pl.pallas_call(
        functools.partial(_pair_attn_kernel, mode=mode, tq=tq, lam_init=lam_init),
        out_shape=jax.ShapeDtypeStruct((b * t, D_MODEL), BF16),
        grid=(b, npair),
        in_specs=[qspec, ktspec, vspec] + especs,
        out_specs=qspec,
        scratch_shapes=scratch,
        compiler_params=_cp(2), name=name)(q, kt, v, *extra)


def _rel_gather(tab, width, center):
    idx = jnp.clip(center - _lane_iota((REL_PAD, width)), -REL_CLIP, REL_CLIP) + REL_CLIP
    onehot = (_row_iota((REL_PAD, width)) == idx).astype(BF16)
    hi, mid, lo = _split3(tab)
    return (_dot(hi, onehot) + _dot(mid, onehot)) + _dot(lo, onehot)


def _band_attn_kernel(q_ref, kt_ref, vt_ref, tab_ref, o_ref, kb, va, bias_ref, *, tq, win, bw):
    gw = bw + tq

    @pl.when(pl.program_id(1) == 0)
    def _():
        g = _rel_gather(tab_ref[...], gw, BAND_LEFT + tq) * LOG2E
        ii = _row_iota((tq, bw)) >> CHUNK_SHIFT
        jj = _lane_iota((tq, bw)) >> CHUNK_SHIFT
        allowed = (jj >= ii) & (jj <= ii + BAND_LEFT_CHUNKS)
        for c in range(2):
            rows = jnp.broadcast_to(g[c:c + 1, :], (tq, gw))
            skew = pltpu.roll(rows, gw - tq, axis=1, stride=1, stride_axis=0)
            bias_ref[c] = jnp.where(allowed, skew[:, :bw], NEG)

    kb[...] = kt_ref[...].astype(BF16)
    vt = vt_ref[...]
    va[0] = _ones_row_values(vt, 0)
    va[1] = _ones_row_values(vt, 1)

    lo = _lane_iota((tq, LANES)) < 64
    for r in range(q_ref.shape[0] // tq):
        q0 = r * tq
        ws = max(q0 - BAND_LEFT, 0)
        d = BAND_LEFT - q0 + ws
        k = kb[:, pl.ds(ws, win)]
        q = q_ref[tq * r:tq * (r + 1), :]
        zero = jnp.zeros_like(q)
        accs = []
        for c in range(2):
            qc = jnp.where(lo, q, zero) if c == 0 else jnp.where(lo, zero, q)
            s = _dot(qc, k) + bias_ref[c, :, pl.ds(d, win)]
            p = jnp.exp2(s - jnp.max(s, axis=-1, keepdims=True))
            accs.append(_dot_nt(p.astype(BF16), va[c, :, pl.ds(ws, win)]))
        o_ref[tq * r:tq * (r + 1), :] = _merge_pair(*accs).astype(BF16)


def _band_attn(q, kt, vt, tab, *, b, t, tq, name):
    npair = D_MODEL // LANES
    win = BAND_LEFT + tq
    bw = win + BAND_LEFT
    assert t >= win and tq % CHUNK == 0 and BAND_LEFT % tq == 0
    qspec = pl.BlockSpec((t, LANES), lambda p, i: (i, p))
    kvspec = pl.BlockSpec((None, LANES, t), lambda p, i: (i, p, 0))
    return pl.pallas_call(
        functools.partial(_band_attn_kernel, tq=tq, win=win, bw=bw),
        out_shape=jax.ShapeDtypeStruct((b * t, D_MODEL), BF16),
        grid=(npair, b),
        in_specs=[qspec, kvspec, kvspec, pl.BlockSpec((None, SUBLANES, REL_PAD), lambda p, i: (p, 0, 0))],
        out_specs=qspec,
        scratch_shapes=[pltpu.VMEM((LANES, t), BF16), pltpu.VMEM((2, LANES, t), BF16), pltpu.VMEM((2, tq, bw), F32)],
        compiler_params=_cp(2), name=name)(q, kt, vt, tab)


def _block_diag_q(q, nh, width):
    tq = q.shape[0]
    rep = jnp.concatenate([q] * nh, axis=0)
    keep = (_row_iota(rep.shape) >> _log2(tq)) == (_lane_iota(rep.shape) >> _log2(width))
    return jnp.where(keep, rep, jnp.zeros_like(rep))


def _expand_rows(f, tq):
    return jnp.concatenate([jnp.broadcast_to(f[h:h + 1, :], (tq, f.shape[1])) for h in range(f.shape[0])], axis=0)


def _dec_attn_kernel(*refs, mode, n_cache, tq, past, lam_init):
    if mode == "fox":
        (q_ref, kc_ref, vc_ref, kn_ref, vn_ref, fkc_ref, fkn_ref, o_ref, qb, m_ref, l_ref, acc_ref) = refs
    else:
        (q_ref, kc_ref, vc_ref, kn_ref, vn_ref, lam_ref, gsub_ref, o_ref, qb, m_ref, l_ref, acc_ref) = refs
    t = pl.program_id(1)
    nh = D_MODEL // 64
    rows = nh * tq

    @pl.when(t == 0)
    def _():
        qb[...] = _block_diag_q(q_ref[...], nh, 64)
        m_ref[...] = jnp.full_like(m_ref, NEG)
        l_ref[...] = jnp.zeros_like(l_ref)
        acc_ref[...] = jnp.zeros_like(acc_ref)

    def cache_step():
        s = _dot(qb[...], kc_ref[...].astype(BF16))
        if mode == "fox":
            s = s - _expand_rows(fkc_ref[...] * LOG2E, tq)
            v = vc_ref[...].astype(BF16)
            _softmax_update(s, lambda pb: _dot_nt(pb, v), m_ref, l_ref, acc_ref)
        else:
            tk = kc_ref.shape[1]
            nhv = D_MODEL // LANES
            v = jnp.concatenate([vc_ref[pl.ds(h, tk, stride=nhv), :].astype(BF16) for h in range(nhv)], axis=-1)
            _softmax_update(s, lambda pb: _dot(pb, v), m_ref, l_ref, acc_ref)

    cache_step()

    @pl.when(t == n_cache - 1)
    def _():
        s = _dot_nt(qb[...], kn_ref[...].astype(BF16))
        qpos = past + (_row_iota((rows, tq)) & (tq - 1))
        kpos = past + _lane_iota((rows, tq))
        if mode == "fox":
            s = s - _expand_rows(fkn_ref[...][:, :tq] * LOG2E, tq)
            allowed = kpos <= qpos
        else:
            allowed = (kpos >> CHUNK_SHIFT) <= (qpos >> CHUNK_SHIFT)
        vn = vn_ref[...].astype(BF16)
        _softmax_update(jnp.where(allowed, s, NEG), lambda pb: _dot(pb, vn), m_ref, l_ref, acc_ref)
        o_all = acc_ref[...] / l_ref[...]
        if mode == "fox":
            o = jnp.zeros((tq, D_MODEL), F32)
            hl = _lane_iota((tq, D_MODEL)) >> _log2(FOX_DH)
            for h in range(nh):
                o = jnp.where(hl == h, o_all[h * tq:(h + 1) * tq, :], o)
        else:
            lv = lam_ref[...]
            lam = (jnp.exp(jnp.sum(lv[0:1] * lv[1:2], axis=-1, keepdims=True))
                   - jnp.exp(jnp.sum(lv[2:3] * lv[3:4], axis=-1, keepdims=True)) + lam_init)
            pieces = []
            for h in range(nh // 2):
                a0 = o_all[(2 * h) * tq:(2 * h + 1) * tq, LANES * h:LANES * (h + 1)]
                a1 = o_all[(2 * h + 1) * tq:(2 * h + 2) * tq, LANES * h:LANES * (h + 1)]
                pieces.append(_rms(a0 - lam * a1, gsub_ref[...]) * (1.0 - lam_init))
            o = jnp.concatenate(pieces, axis=-1)
        o_ref[...] = o.astype(BF16)


def _dec_attn(q, kc, vc, kn, vn, extra, *, mode, b, tq, past, tk, lam_init=0.0, name):
    n_cache = past // tk
    nh = D_MODEL // 64
    new = pl.BlockSpec((tq, D_MODEL), lambda i, t: (i, 0))
    cache_t = pl.BlockSpec((None, D_MODEL, tk), lambda i, t: (i, 0, t))
    if mode == "fox":
        (fk,) = extra
        vspec = cache_t
        especs = [pl.BlockSpec((None, nh, tk), lambda i, t: (i, 0, t)),
                  pl.BlockSpec((None, nh, LANES), lambda i, t: (i, 0, past // LANES))]
        ins = [fk, fk]
    else:
        nhv = D_MODEL // LANES
        vspec = pl.BlockSpec((tk * nhv, LANES), lambda i, t: (i * n_cache + t, 0))
        especs = [pl.BlockSpec(x.shape, lambda i, t: (0, 0)) for x in extra]
        ins = list(extra)
    return pl.pallas_call(
        functools.partial(_dec_attn_kernel, mode=mode, n_cache=n_cache, tq=tq, past=past, lam_init=lam_init),
        out_shape=jax.ShapeDtypeStruct((b * tq, D_MODEL), BF16),
        grid=(b, n_cache),
        in_specs=[new, cache_t, vspec, new, new] + especs,
        out_specs=new,
        scratch_shapes=[pltpu.VMEM((nh * tq, D_MODEL), BF16), pltpu.VMEM((nh * tq, 1), F32),
                        pltpu.VMEM((nh * tq, 1), F32), pltpu.VMEM((nh * tq, D_MODEL), F32)],
        compiler_params=_cp(2), name=name)(q, kc, vc, kn, vn, *ins)


def _mla_dec_kernel(q_ref, cc_ref, rc_ref, cn_ref, rn_ref, wukt_ref, wuv_ref, o_ref,
                    qn, qr, m_ref, l_ref, acc_ref, *, n_cache, tq, past):
    t = pl.program_id(1)
    nh = MLA_HEADS
    rows = nh * tq

    @pl.when(t == 0)
    def _():
        q = q_ref[...]
        zero = jnp.zeros((tq, LANES), BF16)
        for h in range(nh):
            qn[h * tq:(h + 1) * tq, :] = jnp.concatenate(
                [q[:, 2 * LANES * h:2 * LANES * h + LANES] if c == h else zero for c in range(nh)], axis=-1)
            qr[h * tq:(h + 1) * tq, :] = q[:, 2 * LANES * h + LANES:2 * LANES * (h + 1)]
        m_ref[...] = jnp.full_like(m_ref, NEG)
        l_ref[...] = jnp.zeros_like(l_ref)
        acc_ref[...] = jnp.zeros_like(acc_ref)

    def step(ckv, s_rope, mask_new):
        c = ckv.astype(BF16)
        knt = _dot_nt(wukt_ref[...], c)
        ms = jnp.concatenate([jnp.mean(jnp.square(knt[MLA_NOPE * h:MLA_NOPE * (h + 1)]), axis=0, keepdims=True)
                              for h in range(nh)], axis=0)
        s = _dot(qn[...], knt.astype(BF16)) * _expand_rows(lax.rsqrt(ms + NORM_EPS), tq) + s_rope
        if mask_new:
            n = ckv.shape[0]
            qpos = past + (_row_iota((rows, n)) & (tq - 1))
            kpos = past + _lane_iota((rows, n))
            s = jnp.where((kpos >> CHUNK_SHIFT) <= (qpos >> CHUNK_SHIFT), s, NEG)
        _softmax_update(s, lambda pb: _dot(pb, c), m_ref, l_ref, acc_ref)

    def cache_step():
        kr = rc_ref[...].astype(BF16)
        step(cc_ref[...], _dot(qr[...], jnp.concatenate([kr, kr], axis=0)), False)

    cache_step()

    @pl.when(t == n_cache - 1)
    def _():
        step(cn_ref[...], _dot_nt(qr[...], rn_ref[...].astype(BF16)), True)
        lat = (acc_ref[...] / l_ref[...]).astype(BF16)
        o_ref[...] = jnp.concatenate(
            [_dot(lat[h * tq:(h + 1) * tq, :], wuv_ref[:, MLA_V * h:MLA_V * (h + 1)]) for h in range(nh)],
            axis=-1).astype(BF16)


def _mla_dec(q, cc, rc, cn, rn, wukt, wuv, *, b, tq, past, tk, name):
    n_cache = past // tk
    nh = MLA_HEADS
    new = lambda n: pl.BlockSpec((tq, n), lambda i, t: (i, 0))
    full = lambda a: pl.BlockSpec(a.shape, lambda i, t: (0,) * a.ndim)
    return pl.pallas_call(
        functools.partial(_mla_dec_kernel, n_cache=n_cache, tq=tq, past=past),
        out_shape=jax.ShapeDtypeStruct((b * tq, nh * MLA_V), BF16),
        grid=(b, n_cache),
        in_specs=[new(nh * 2 * LANES),
                  pl.BlockSpec((tk, MLA_KV_LORA), lambda i, t: (i * n_cache + t, 0)),
                  pl.BlockSpec((None, MLA_ROPE, tk), lambda i, t: (i, 0, t)),
                  new(MLA_KV_LORA), new(LANES), full(wukt), full(wuv)],
        out_specs=new(nh * MLA_V),
        scratch_shapes=[pltpu.VMEM((nh * tq, nh * MLA_NOPE), BF16), pltpu.VMEM((nh * tq, LANES), BF16),
                        pltpu.VMEM((nh * tq, 1), F32), pltpu.VMEM((nh * tq, 1), F32),
                        pltpu.VMEM((nh * tq, MLA_KV_LORA), F32)],
        compiler_params=_cp(2), name=name)(q, cc, rc, cn, rn, wukt, wuv)


def _band_dec_kernel(q_ref, kc_ref, vc_ref, kn_ref, vn_ref, knt_ref, vnt_ref, tab_ref, o_ref, kr_out, vr_out,
                     *, tq, w, past):
    keep_old = _lane_iota((D_MODEL, LANES)) < LANES - tq
    for c_ref, nt_ref, r_out in ((kc_ref, knt_ref, kr_out), (vc_ref, vnt_ref, vr_out)):
        rolled = pltpu.roll(c_ref[...], w - tq, axis=1)
        r_out[:, :w - LANES] = rolled[:, :w - LANES]
        r_out[:, w - LANES:] = jnp.where(keep_old, rolled[:, w - LANES:], nt_ref[...])
    nh = BAND_HEADS
    rows = nh * tq
    bwid = ((w + tq + LANES - 1) // LANES) * LANES
    qb = _block_diag_q(q_ref[...], nh, BAND_DH)
    g = _rel_gather(tab_ref[...], bwid, w + tq) * LOG2E
    bias = jnp.concatenate(
        [pltpu.roll(jnp.broadcast_to(g[h:h + 1, :], (tq, bwid)), bwid - tq, axis=1, stride=1, stride_axis=0)
         for h in range(nh)], axis=0)
    qc = (past + (_row_iota((rows, bwid)) & (tq - 1))) >> CHUNK_SHIFT
    kc = (past - w + _lane_iota((rows, bwid))) >> CHUNK_SHIFT
    bias = jnp.where((kc <= qc) & (kc >= qc - BAND_LEFT_CHUNKS), bias, NEG)
    s_c = _dot(qb, kc_ref[...].astype(BF16)) + bias[:, :w]
    s_n = _dot_nt(qb, kn_ref[...].astype(BF16)) + bias[:, w:w + tq]
    m = jnp.maximum(jnp.max(s_c, axis=-1, keepdims=True), jnp.max(s_n, axis=-1, keepdims=True))
    p_c = jnp.exp2(s_c - m)
    p_n = jnp.exp2(s_n - m)
    l = jnp.sum(p_c, axis=-1, keepdims=True) + jnp.sum(p_n, axis=-1, keepdims=True)
    o_all = (_dot_nt(p_c.astype(BF16), vc_ref[...].astype(BF16))
             + _dot(p_n.astype(BF16), vn_ref[...].astype(BF16))) / l
    o = jnp.zeros((tq, D_MODEL), F32)
    hl = _lane_iota((tq, D_MODEL)) >> _log2(BAND_DH)
    for h in range(nh):
        o = jnp.where(hl == h, o_all[h * tq:(h + 1) * tq, :], o)
    o_ref[...] = o.astype(BF16)


def _band_dec(q, kc, vc, kn, vn, tab, *, b, tq, w, past, name):
    assert w % LANES == 0 and tq <= LANES
    new = pl.BlockSpec((tq, D_MODEL), lambda i: (i, 0))
    cache = pl.BlockSpec((None, D_MODEL, w), lambda i: (i, 0, 0))
    slab = pl.BlockSpec((None, D_MODEL, LANES), lambda i: (i, 0, 0))
    right = lambda x: jnp.pad(jnp.swapaxes(x.reshape(b, tq, D_MODEL), 1, 2), ((0, 0), (0, 0), (LANES - tq, 0)))
    return pl.pallas_call(
        functools.partial(_band_dec_kernel, tq=tq, w=w, past=past),
        out_shape=[jax.ShapeDtypeStruct((b * tq, D_MODEL), BF16), jax.ShapeDtypeStruct(kc.shape, F32),
                   jax.ShapeDtypeStruct(vc.shape, F32)],
        grid=(b,),
        in_specs=[new, cache, cache, new, new, slab, slab, pl.BlockSpec(tab.shape, lambda i: (0, 0))],
        out_specs=[new, cache, cache],
        compiler_params=_cp(1), name=name)(q, kc, vc, kn, vn, right(kn), right(vn), tab)


def _ffn_rows_kernel(*refs, mode, tps, seq, final, tf):
    it = iter(refs)
    h_ref, o_ref, wo_ref, gf_ref, wup_ref, cw_ref, cb_ref, wd_ref = [next(it) for _ in range(8)]
    if mode == "state":
        s1_ref, s2_ref = next(it), next(it)
    if final:
        gfin_ref = next(it)
    out_ref, ug_out, uh_out = next(it), next(it), next(it)
    if mode == "carry":
        carry_ref = next(it)
    tm = h_ref.shape[0]
    h1 = h_ref[...] + _dot(o_ref[...], wo_ref[...])
    xn = _rms(h1, gf_ref[...]).astype(BF16)
    if mode == "carry":
        @pl.when(pl.program_id(0) % tps == 0)
        def _():
            carry_ref[...] = jnp.zeros_like(carry_ref)
    else:
        tpos = _row_iota((tm, tf)) & (seq - 1)
    acts = []
    for j in range(D_FF // tf):
        cs = []
        for part, u_out in enumerate((ug_out, uh_out)):
            tile = slice(tf * j, tf * (j + 1))
            cols = slice(part * D_FF + tf * j, part * D_FF + tf * (j + 1))
            u = _dot(xn, wup_ref[:, cols])
            if mode == "carry":
                tail = u[tm - SUBLANES:]
                u_out[:, tile] = tail
                ext = jnp.concatenate([carry_ref[part, :, tile], u], axis=0)
                carry_ref[part, :, tile] = tail
                um1 = pltpu.roll(ext, 1, axis=0)[SUBLANES:]
                um2 = pltpu.roll(ext, 2, axis=0)[SUBLANES:]
            else:
                u_out[:, tile] = u
                um1 = jnp.where(tpos >= 1, pltpu.roll(u, 1, axis=0), s1_ref[:, cols])
                um2 = jnp.where(tpos >= 2, pltpu.roll(u, 2, axis=0), s2_ref[:, cols])
            cw = cw_ref[:, cols]
            cs.append(((cb_ref[:, cols] + u * cw[2:3]) + um2 * cw[0:1]) + um1 * cw[1:2])
        acts.append((cs[0] * jax.nn.sigmoid(cs[0]) * cs[1]).astype(BF16))
    acc = h1 + _dot(jnp.concatenate(acts, axis=-1), wd_ref[...])
    if final:
        acc = _rms(acc, gfin_ref[...])
    out_ref[...] = acc


def _ffn_rows(h, o, wo, gf, wup, cw, cb, wd, *, layer, tm, tf, seq, state=None, final_g=None, name):
    m = h.shape[0]
    mode = "carry" if state is None else "state"
    tps = seq // tm if mode == "carry" else 1
    row = lambda n: pl.BlockSpec((tm, n), lambda i: (i, 0))
    once = lambda a: pl.BlockSpec(a.shape, lambda i: (0,) * a.ndim, pipeline_mode=pl.Buffered(1))
    of_layer = lambda a: pl.BlockSpec((None,) + a.shape[1:], lambda i: (layer,) + (0,) * (a.ndim - 1),
                                      pipeline_mode=pl.Buffered(1))
    ins = [h, o, wo, gf, wup, cw, cb, wd]
    specs = [row(D_MODEL), row(D_MODEL), once(wo), once(gf), of_layer(wup), once(cw), once(cb), of_layer(wd)]
    scratch = []
    if mode == "state":
        assert m == tm and seq & (seq - 1) == 0
        ins += list(state)
        specs += [row(2 * D_FF), row(2 * D_FF)]
        u_shape = jax.ShapeDtypeStruct((m, D_FF), F32)
        u_spec = row(D_FF)
    else:
        assert seq % tm == 0
        scratch.append(pltpu.VMEM((2, SUBLANES, D_FF), F32))
        u_shape = jax.ShapeDtypeStruct((m // tm, SUBLANES, D_FF), F32)
        u_spec = pl.BlockSpec((None, SUBLANES, D_FF), lambda i: (i, 0, 0))
    if final_g is not None:
        ins.append(final_g)
        specs.append(once(final_g))
    kern = functools.partial(_ffn_rows_kernel, mode=mode, tps=tps, seq=seq, final=final_g is not None, tf=tf)
    return pl.pallas_call(
        kern, out_shape=[jax.ShapeDtypeStruct((m, D_MODEL), F32), u_shape, u_shape],
        grid=(m // tm,), in_specs=specs, out_specs=[row(D_MODEL), u_spec, u_spec], scratch_shapes=scratch,
        compiler_params=_cp(1), name=name)(*ins)


def _rope_tables(pos):
    inv = 1.0 / (ROPE_THETA ** (jnp.arange(0, 64, 2, dtype=F32) / 64))
    ang = pos.astype(F32)[:, None] * inv[None, :]
    c, s = jnp.cos(ang), jnp.sin(ang)
    return jnp.tile(c, (1, 4)), jnp.tile(jnp.concatenate([-s, s], axis=1), (1, 2)), c.T, s.T


def _tile_gain(g, n):
    return jnp.tile(g.astype(F32), n // g.shape[0]).reshape(1, n)


def _gain_t(g, n, w):
    return jnp.broadcast_to(jnp.tile(g.astype(F32), n // g.shape[0])[:, None], (n, w))


def _to_tokens(xt, heads):
    b, n, t = xt.shape
    dh = n // math.prod(heads)
    nd = len(heads)
    return xt.reshape(b, *heads, dh, t).transpose(0, nd + 2, *range(1, nd + 2))


def _to_features(x):
    b, t = x.shape[:2]
    nd = x.ndim
    return x.transpose(0, *range(2, nd), 1).reshape(b, -1, t)


def kernel(x_prompt, x_sample, cache_mla_ckv, cache_mla_krope, cache_fox_k, cache_fox_v, cache_fox_logf,
           cache_diff_k, cache_diff_v, cache_band_k, cache_band_v, state_ffn_conv,
           attn_norm_g, ffn_norm_g, final_norm_g,
           mla_w_dq, mla_g_q, mla_w_uq, mla_w_dkv, mla_g_kv, mla_w_uk, mla_w_uv,
           mla_g_qn, mla_g_qr, mla_g_kn, mla_g_kr, mla_w_o,
           fox_w_qkv, fox_w_f, fox_b_f, fox_g_q, fox_g_k, fox_w_o,
           diff_w_qkv, diff_g_q, diff_g_k, diff_lq1, diff_lk1, diff_lq2, diff_lk2, diff_g_sub, diff_w_o,
           band_w_qkv, band_g_q, band_g_k, band_rel_bias, band_w_o,
           ffn_w_up, ffn_conv_w, ffn_conv_b, ffn_w_down):
    bp, tp, d = x_prompt.shape
    bs, ts, _ = x_sample.shape
    past = cache_mla_ckv.shape[1]
    depth = attn_norm_g.shape[0]
    mp, ms = bp * tp, bs * ts
    tm_p = min(512, tp)
    tm_f = min(1024, tp)
    tq = 512
    tq_band = 256
    tkd = min(1024, past)
    tkw = min(2048, past)
    tf = 256
    assert tp % tq == 0 and past % tkd == 0 and past % tkw == 0 and past % CHUNK == 0

    tobf = lambda a: a.astype(BF16)
    rowv = lambda g: g.astype(F32).reshape(1, -1)
    pos_p = jnp.arange(tp, dtype=jnp.int32)
    pos_s = past + jnp.arange(ts, dtype=jnp.int32)
    tabs_p = _rope_tables(pos_p)
    tabs_s = tuple(jnp.tile(a, (bs, 1)) for a in _rope_tables(pos_s)[:2])

    nope_cols = jnp.arange(MLA_HEADS)[:, None] * (MLA_NOPE + MLA_ROPE) + jnp.arange(MLA_NOPE)[None, :]
    rope_cols = jnp.arange(MLA_HEADS)[:, None] * (MLA_NOPE + MLA_ROPE) + MLA_NOPE + jnp.arange(MLA_ROPE)[None, :]
    wuq_perm = jnp.concatenate([mla_w_uq[:, nope_cols.reshape(-1)], mla_w_uq[:, rope_cols.reshape(-1)]], axis=1)
    wr = mla_w_dkv[:, MLA_KV_LORA:]
    mla_w = dict(
        wdq=tobf(mla_w_dq), gq=rowv(mla_g_q), wuq=tobf(wuq_perm), wc=tobf(mla_w_dkv[:, :MLA_KV_LORA]),
        wr=tobf(jnp.concatenate([wr, wr], axis=1)), gkv=rowv(mla_g_kv),
        gqn=_tile_gain(mla_g_qn, MLA_HEADS * MLA_NOPE), gqr=_tile_gain(mla_g_qr, MLA_HEADS * MLA_ROPE),
        gkr=_tile_gain(mla_g_kr, LANES), wuk=tobf(mla_w_uk), wuv=tobf(mla_w_uv),
        gkn=_tile_gain(mla_g_kn, MLA_HEADS * MLA_NOPE), wrt=tobf(wr.T), gkrt=_gain_t(mla_g_kr, MLA_ROPE, tm_p))

    def split_qkv(w):
        wq, wk, wv = w[:, :d], w[:, d:2 * d], w[:, 2 * d:]
        return tobf(wq), tobf(wk.T), tobf(wv), tobf(wv.T)

    fox_wf = tobf(jnp.pad(fox_w_f, ((0, 0), (0, LANES - FOX_HEADS))))
    fox_bf = jnp.pad(fox_b_f.astype(F32), (0, LANES - FOX_HEADS)).reshape(1, LANES)
    fox_wft = tobf(fox_w_f.T)
    fox_bft = jnp.broadcast_to(fox_b_f.astype(F32)[:, None], (FOX_HEADS, tm_p))
    lamv = jnp.stack([diff_lq1, diff_lk1, diff_lq2, diff_lk2]).astype(F32)
    gsub = rowv(diff_g_sub)
    tab_pad = jnp.pad(band_rel_bias.astype(F32), ((0, 0), (0, REL_PAD - band_rel_bias.shape[1])))
    wo = [tobf(mla_w_o), tobf(fox_w_o), tobf(diff_w_o), tobf(band_w_o)]
    wup, wdn = tobf(ffn_w_up), tobf(ffn_w_down)
    cwf, cbf = ffn_conv_w.astype(F32), ffn_conv_b.astype(F32)

    h_p = x_prompt.reshape(mp, d)
    h_s = x_sample.reshape(ms, d)
    outs = {}
    conv_p, conv_s = [], []
    for i in range(depth):
        kind = i % 4
        ga = rowv(attn_norm_g[i])
        if kind == 0:
            scale = (MLA_NOPE + MLA_ROPE) ** -0.5 * LOG2E
            q_p, ckv_p, _, _, kc_p, v_p, krt_p = _mla_proj(h_p, ga, mla_w, tabs_p, b=bp, t=tp, tm=tm_p, with_kv=True,
                                                           scale=scale, name="mla_proj_p")
            o_p = _mla_attn(q_p, kc_p, v_p, b=bp, t=tp, tq=tq, hps=2, name="mla_attn_p")
            q_s, ckv_s, kr_s, krd_s = _mla_proj(h_s, ga, mla_w, tabs_s, b=bs, t=ts, tm=ms, with_kv=False,
                                                scale=scale, name="mla_proj_s")
            o_s = _mla_dec(q_s, cache_mla_ckv.astype(F32).reshape(bs * past, MLA_KV_LORA),
                           _to_features(cache_mla_krope.astype(F32)), ckv_s, krd_s,
                           tobf(mla_w_uk.T), mla_w["wuv"], b=bs, tq=ts, past=past, tk=tkd, name="mla_attn_s")
            outs["mla"] = (ckv_p.reshape(bp, tp, MLA_KV_LORA), jnp.swapaxes(krt_p, 1, 2),
                           ckv_s.reshape(bs, ts, MLA_KV_LORA), kr_s.reshape(bs, ts, MLA_ROPE))
        elif kind == 1:
            scale = FOX_DH ** -0.5 * LOG2E
            gq, gk = _tile_gain(fox_g_q, d), _tile_gain(fox_g_k, d)
            wq, wkt, wv, wvt = split_qkv(fox_w_qkv)
            q_p, kt_p, vt_p, lft_p = _qkv_proj_t(h_p, ga, wq, wkt, wvt, gq, _gain_t(fox_g_k, d, tm_p), b=bp, t=tp,
                                                 tm=tm_p, wft=fox_wft, bft=fox_bft, scale=scale, name="fox_proj_p")
            fk_p = _cumsum_last(lft_p, tk=tq, name="fox_cumsum_p")
            o_p = _pair_attn(q_p, kt_p, vt_p, (fk_p.reshape(bp, FOX_HEADS // 2, 2, tp),), mode="fox",
                             b=bp, t=tp, tq=tq, name="fox_attn_p")
            q_s, k_s, v_s, lf_s = _qkv_proj(h_s, ga, tobf(fox_w_qkv), gq, gk, tm=ms, wf=fox_wf, bf=fox_bf,
                                            scale=scale, name="fox_proj_s")
            lf_all = jnp.concatenate([jnp.swapaxes(cache_fox_logf.astype(F32), 1, 2),
                                      jnp.swapaxes(lf_s.reshape(bs, ts, FOX_HEADS), 1, 2)], axis=2)
            lf_all = jnp.pad(lf_all, ((0, 0), (0, 0), (0, tkw - ts)))
            f_all = _cumsum_last(lf_all, tk=tq, name="fox_cumsum_s")
            o_s = _dec_attn(q_s, _to_features(cache_fox_k.astype(F32)), _to_features(cache_fox_v.astype(F32)),
                            k_s, v_s, (f_all,), mode="fox", b=bs, tq=ts, past=past, tk=tkw, name="fox_attn_s")
            sh = (FOX_HEADS, FOX_DH)
            outs["fox"] = (_to_tokens(kt_p, (FOX_HEADS,)), _to_tokens(vt_p, (FOX_HEADS,)),
                           jnp.swapaxes(lft_p, 1, 2),
                           k_s.reshape(bs, ts, *sh), v_s.reshape(bs, ts, *sh), lf_s.reshape(bs, ts, FOX_HEADS))
        elif kind == 2:
            scale = DIFF_DH ** -0.5 * LOG2E
            lam_init = 0.8 - 0.6 * math.exp(-0.3 * i)
            gq, gk = _tile_gain(diff_g_q, d), _tile_gain(diff_g_k, d)
            wq, wkt, wv, wvt = split_qkv(diff_w_qkv)
            q_p, kt_p, v4_p = _qkv_proj_t(h_p, ga, wq, wkt, wv, gq, _gain_t(diff_g_k, d, tm_p), b=bp, t=tp, tm=tm_p,
                                          rope_tabs=tabs_p, v_tokens=True, scale=scale, name="diff_proj_p")
            v_p = v4_p.reshape(bp, tp, DIFF_HEADS, 2 * DIFF_DH)
            o_p = _pair_attn(q_p, kt_p, v4_p, (lamv, gsub), mode="diff", b=bp, t=tp, tq=tq, lam_init=lam_init,
                             name="diff_attn_p")
            q_s, k_s, v_s = _qkv_proj(h_s, ga, tobf(diff_w_qkv), gq, gk, tm=ms, rope_tabs=tabs_s, scale=scale,
                                      name="diff_proj_s")
            o_s = _dec_attn(q_s, _to_features(cache_diff_k.astype(F32)),
                            cache_diff_v.astype(F32).reshape(bs * past * DIFF_HEADS, 2 * DIFF_DH), k_s, v_s,
                            (lamv, gsub), mode="diff", b=bs, tq=ts, past=past, tk=tkw, lam_init=lam_init,
                            name="diff_attn_s")
            outs["diff"] = (_to_tokens(kt_p, (DIFF_HEADS, 2)), v_p,
                            k_s.reshape(bs, ts, DIFF_HEADS, 2, DIFF_DH), v_s.reshape(bs, ts, DIFF_HEADS, 2 * DIFF_DH))
        else:
            scale = BAND_DH ** -0.5 * LOG2E
            gq, gk = _tile_gain(band_g_q, d), _tile_gain(band_g_k, d)
            wq, wkt, wv, wvt = split_qkv(band_w_qkv)
            keep = min(BAND_LEFT, tp)
            assert keep == tm_p
            q_p, kt_p, vt_p, kt_keep, vt_keep = _qkv_proj_t(
                h_p, ga, wq, wkt, wvt, gq, _gain_t(band_g_k, d, tm_p), b=bp, t=tp, tm=tm_p, tail=True, scale=scale,
                name="band_proj_p")
            tab_pairs = jnp.pad(tab_pad.reshape(BAND_HEADS // 2, 2, REL_PAD), ((0, 0), (0, SUBLANES - 2), (0, 0)))
            o_p = _band_attn(q_p, kt_p, vt_p, tab_pairs, b=bp, t=tp, tq=tq_band, name="band_attn_p")
            q_s, k_s, v_s = _qkv_proj(h_s, ga, tobf(band_w_qkv), gq, gk, tm=ms, scale=scale, name="band_proj_s")
            w = cache_band_k.shape[1]
            kct = _to_features(cache_band_k.astype(F32))
            vct = _to_features(cache_band_v.astype(F32))
            o_s, k_roll, v_roll = _band_dec(q_s, kct, vct, k_s, v_s, tab_pad, b=bs, tq=ts, w=w, past=past,
                                            name="band_attn_s")
            outs["band"] = (_to_tokens(kt_keep, (BAND_HEADS,)), _to_tokens(vt_keep, (BAND_HEADS,)),
                            _to_tokens(k_roll, (BAND_HEADS,)), _to_tokens(v_roll, (BAND_HEADS,)))

        last = i == depth - 1
        gfin = rowv(final_norm_g) if last else None
        gfn = rowv(ffn_norm_g[i])
        h_p, ug, uh = _ffn_rows(h_p, o_p, wo[kind], gfn, wup, cwf[i], cbf[i].reshape(1, -1), wdn, layer=i, tm=tm_f,
                                tf=tf, seq=tp, final_g=gfin, name="ffn_p")
        tps = tp // tm_f
        conv_p.append(jnp.concatenate([ug, uh], axis=-1)[tps - 1::tps, SUBLANES - (CONV_W - 1):])
        st = state_ffn_conv[i].astype(F32)
        s1 = jnp.pad(st[:, 1:], ((0, 0), (0, ts - 1), (0, 0))).reshape(ms, 2 * D_FF)
        s2 = jnp.pad(st, ((0, 0), (0, ts - 2), (0, 0))).reshape(ms, 2 * D_FF)
        h_s, ug, uh = _ffn_rows(h_s, o_s, wo[kind], gfn, wup, cwf[i], cbf[i].reshape(1, -1), wdn, layer=i, tm=ms,
                                tf=tf, seq=ts, state=(s1, s2), final_g=gfin, name="ffn_s")
        u_s = jnp.concatenate([ug, uh], axis=-1).reshape(bs, ts, 2 * D_FF)
        conv_s.append(jnp.concatenate([st, u_s], axis=1)[:, ts:])

    y_prompt = h_p.reshape(bp, tp, d)
    y_sample = h_s.reshape(bs, ts, d)
    return (y_prompt, y_sample) + outs["mla"] + outs["fox"] + outs["diff"] + outs["band"] + (
        jnp.stack(conv_p, axis=0), jnp.stack(conv_s, axis=0))
```

```python
import functools
import math

import jax
import jax.numpy as jnp
from jax import lax
from jax.experimental import pallas as pl
from jax.experimental.pallas import tpu as pltpu

F32 = jnp.float32
BF16 = jnp.bfloat16

D_MODEL = 1024
CHUNK = 64
CHUNK_SHIFT = CHUNK.bit_length() - 1
ROPE_DIM = 64
ROPE_THETA = 10000.0
NORM_EPS = 1e-6
MLA_HEADS, MLA_Q_LORA, MLA_KV_LORA, MLA_NOPE, MLA_ROPE, MLA_V = 8, 384, 256, 128, 64, 128
FOX_HEADS, FOX_DH = 16, 64
DIFF_HEADS, DIFF_DH = 8, 64
BAND_HEADS, BAND_DH = 16, 64
BAND_LEFT_CHUNKS = 8
BAND_LEFT = BAND_LEFT_CHUNKS * CHUNK
REL_CLIP = 128
D_FF = 2816
CONV_W = 3

LANES = 128
SUBLANES = 8
NEG = -1e30
REL_PAD = 384
LOG2E = math.log2(math.e)
VMEM_LIMIT = 56 * 1024 * 1024


def _cp(n_axes):
    return pltpu.CompilerParams(dimension_semantics=("arbitrary",) * n_axes, vmem_limit_bytes=VMEM_LIMIT)


def _dot(a, b):
    return jnp.dot(a, b, preferred_element_type=F32)


def _dot_nt(a, b):
    return lax.dot_general(a, b, (((1,), (1,)), ((), ())), preferred_element_type=F32)


def _rms(x, g):
    ms = jnp.mean(x * x, axis=-1, keepdims=True)
    return x * lax.rsqrt(ms + NORM_EPS) * g


def _lane_iota(shape):
    return lax.broadcasted_iota(jnp.int32, shape, len(shape) - 1)


def _row_iota(shape):
    return lax.broadcasted_iota(jnp.int32, shape, len(shape) - 2)


def _log2(n):
    assert n & (n - 1) == 0, n
    return n.bit_length() - 1


def _head_rms(x, g, seg):
    n = x.shape[1]
    outs = []
    for c in range(n // LANES):
        xs = x[:, LANES * c:LANES * (c + 1)]
        sq = xs * xs
        if seg == LANES:
            r = lax.rsqrt(jnp.sum(sq, axis=-1, keepdims=True) * (1.0 / seg) + NORM_EPS)
        else:
            lo = _lane_iota(xs.shape) < seg
            s_lo = jnp.sum(jnp.where(lo, sq, 0.0), axis=-1, keepdims=True)
            s_hi = jnp.sum(jnp.where(lo, 0.0, sq), axis=-1, keepdims=True)
            r = jnp.where(lo, lax.rsqrt(s_lo * (1.0 / seg) + NORM_EPS), lax.rsqrt(s_hi * (1.0 / seg) + NORM_EPS))
        outs.append(xs * r * g[:, LANES * c:LANES * (c + 1)])
    return jnp.concatenate(outs, axis=-1) if len(outs) > 1 else outs[0]


def _head_rms_t(x, g, seg=64):
    outs = []
    for hd in range(x.shape[0] // seg):
        xs = x[seg * hd:seg * (hd + 1)]
        r = lax.rsqrt(jnp.sum(xs * xs, axis=0, keepdims=True) * (1.0 / seg) + NORM_EPS)
        outs.append(xs * r * g[seg * hd:seg * (hd + 1)])
    return jnp.concatenate(outs, axis=0) if len(outs) > 1 else outs[0]


def _rope(x, cos, sinp):
    half = ROPE_DIM // 2
    first_half = (_lane_iota((x.shape[0], LANES)) & (ROPE_DIM - 1)) < half
    outs = []
    for c in range(x.shape[1] // LANES):
        xs = x[:, LANES * c:LANES * (c + 1)]
        fwd = pltpu.roll(xs, half, axis=1)
        bwd = pltpu.roll(xs, LANES - half, axis=1)
        outs.append(xs * cos + jnp.where(first_half, bwd, fwd) * sinp)
    return jnp.concatenate(outs, axis=-1) if len(outs) > 1 else outs[0]


def _rope_t(x, cos_t, sin_t):
    outs = []
    for hd in range(x.shape[0] // 64):
        x1 = x[64 * hd:64 * hd + 32]
        x2 = x[64 * hd + 32:64 * (hd + 1)]
        outs += [x1 * cos_t - x2 * sin_t, x2 * cos_t + x1 * sin_t]
    return jnp.concatenate(outs, axis=0)


def _split3(x):
    hi = x.astype(BF16)
    r = x - hi.astype(F32)
    mid = r.astype(BF16)
    lo = (r - mid.astype(F32)).astype(BF16)
    return hi, mid, lo


def _softmax_step(s, pv, m, l, acc):
    m_new = jnp.maximum(m, jnp.max(s, axis=-1, keepdims=True))
    alpha = jnp.exp2(m - m_new)
    p = jnp.exp2(s - m_new)
    if l is not None:
        l = alpha * l + jnp.sum(p, axis=-1, keepdims=True)
    return m_new, l, alpha * acc + pv(p.astype(BF16))


def _softmax_update(s, pv, m_ref, l_ref, acc_ref):
    m, l, acc = _softmax_step(s, pv, m_ref[...], l_ref[...], acc_ref[...])
    m_ref[...] = m
    l_ref[...] = l
    acc_ref[...] = acc


def _log_sigmoid(z):
    return jnp.minimum(z, 0.0) - jnp.log1p(jnp.exp(-jnp.abs(z)))


def _qkv_proj_kernel(*refs, rope, logf, scale):
    it = iter(refs)
    h_ref, g_ref, w_ref, gq_ref, gk_ref = next(it), next(it), next(it), next(it), next(it)
    cos_ref = sin_ref = wf_ref = bf_ref = None
    if rope:
        cos_ref, sin_ref = next(it), next(it)
    if logf:
        wf_ref, bf_ref = next(it), next(it)
    q_out, k_out, v_out = next(it), next(it), next(it)
    a = _rms(h_ref[...], g_ref[...]).astype(BF16)
    qkv = _dot(a, w_ref[...])
    q = _head_rms(qkv[:, :D_MODEL], gq_ref[...], 64)
    k = _head_rms(qkv[:, D_MODEL:2 * D_MODEL], gk_ref[...], 64)
    if rope:
        q = _rope(q, cos_ref[...], sin_ref[...])
        k = _rope(k, cos_ref[...], sin_ref[...])
    q_out[...] = (q * scale).astype(BF16)
    k_out[...] = k
    v_out[...] = qkv[:, 2 * D_MODEL:]
    if logf:
        logf_out = next(it)
        lf = _log_sigmoid(_dot(a, wf_ref[...]) + bf_ref[...])
        logf_out[...] = lf[:, :FOX_HEADS]


def _qkv_proj(h, g, w, gq, gk, *, tm, rope_tabs=None, wf=None, bf=None, scale, name):
    m = h.shape[0]
    grid = (m // tm,)
    row = lambda n: pl.BlockSpec((tm, n), lambda i: (i, 0))
    full = lambda a: pl.BlockSpec(a.shape, lambda i: (0,) * a.ndim)
    ins = [h, g, w, gq, gk]
    specs = [row(D_MODEL), full(g), full(w), full(gq), full(gk)]
    if rope_tabs is not None:
        cos, sinp = rope_tabs
        nt = cos.shape[0] // tm
        tab = pl.BlockSpec((tm, LANES), lambda i: (i % nt, 0))
        ins += [cos, sinp]
        specs += [tab, tab]
    if wf is not None:
        ins += [wf, bf]
        specs += [full(wf), full(bf)]
    outs = [jax.ShapeDtypeStruct((m, D_MODEL), BF16), jax.ShapeDtypeStruct((m, D_MODEL), F32),
            jax.ShapeDtypeStruct((m, D_MODEL), F32)]
    ospecs = [row(D_MODEL), row(D_MODEL), row(D_MODEL)]
    if wf is not None:
        outs.append(jax.ShapeDtypeStruct((m, FOX_HEADS), F32))
        ospecs.append(row(FOX_HEADS))
    kern = functools.partial(_qkv_proj_kernel, rope=rope_tabs is not None, logf=wf is not None, scale=scale)
    return pl.pallas_call(kern, out_shape=outs, grid=grid, in_specs=specs, out_specs=ospecs,
                          compiler_params=_cp(1), name=name)(*ins)


def _qkv_proj_t_kernel(*refs, rope, logf, v_tokens, tail, scale):
    it = iter(refs)
    h_ref, g_ref, wq_ref, wkt_ref, wv_ref, gq_ref, gkt_ref = [next(it) for _ in range(7)]
    if rope:
        cos_ref, sin_ref, cost_ref, sint_ref = [next(it) for _ in range(4)]
    if logf:
        wft_ref, bft_ref = next(it), next(it)
    q_out, kt_out, v_out = next(it), next(it), next(it)
    a = _rms(h_ref[...], g_ref[...]).astype(BF16)
    tm = a.shape[0]
    q = _head_rms(_dot(a, wq_ref[...]), gq_ref[...], 64)
    kt = _head_rms_t(_dot_nt(wkt_ref[...], a), gkt_ref[...])
    if rope:
        q = _rope(q, cos_ref[...], sin_ref[...])
        kt = _rope_t(kt, cost_ref[...], sint_ref[...])
    q_out[...] = (q * scale).astype(BF16)
    kt_out[...] = kt
    if v_tokens:
        v = _dot(a, wv_ref[...])
        nh = D_MODEL // LANES
        for hd in range(nh):
            v_out[pl.ds(hd, tm, stride=nh), :] = v[:, LANES * hd:LANES * (hd + 1)]
    else:
        vt = _dot_nt(wv_ref[...], a)
        v_out[...] = vt
    if logf:
        lft_out = next(it)
        lft_out[...] = _log_sigmoid(_dot_nt(wft_ref[...], a) + bft_ref[...])
    if tail:
        kt_tail, vt_tail = next(it), next(it)
        kt_tail[...] = kt
        vt_tail[...] = vt


def _qkv_proj_t(h, g, wq, wkt, wv, gq, gkt, *, b, t, tm, rope_tabs=None, wft=None, bft=None, v_tokens=False,
                tail=False, scale, name):
    m = b * t
    nt = t // tm
    nh = D_MODEL // LANES
    row = lambda n: pl.BlockSpec((tm, n), lambda i: (i, 0))
    full = lambda a: pl.BlockSpec(a.shape, lambda i: (0,) * a.ndim)
    feat = lambda n: pl.BlockSpec((None, n, tm), lambda i: (i // nt, 0, i % nt))
    ins = [h, g, wq, wkt, wv, gq, gkt]
    specs = [row(D_MODEL)] + [full(x) for x in ins[1:]]
    if rope_tabs is not None:
        cos, sinp, cos_t, sin_t = rope_tabs
        ins += [cos, sinp, cos_t, sin_t]
        tab = pl.BlockSpec((tm, LANES), lambda i: (i % nt, 0))
        tab_t = pl.BlockSpec((32, tm), lambda i: (0, i % nt))
        specs += [tab, tab, tab_t, tab_t]
    if wft is not None:
        ins += [wft, bft]
        specs += [full(wft), full(bft)]
    outs = [jax.ShapeDtypeStruct((m, D_MODEL), BF16), jax.ShapeDtypeStruct((b, D_MODEL, t), F32)]
    ospecs = [row(D_MODEL), feat(D_MODEL)]
    if v_tokens:
        outs.append(jax.ShapeDtypeStruct((m * nh, LANES), F32))
        ospecs.append(pl.BlockSpec((tm * nh, LANES), lambda i: (i, 0)))
    else:
        outs.append(jax.ShapeDtypeStruct((b, D_MODEL, t), F32))
        ospecs.append(feat(D_MODEL))
    if wft is not None:
        outs.append(jax.ShapeDtypeStruct((b, FOX_HEADS, t), F32))
        ospecs.append(feat(FOX_HEADS))
    if tail:
        assert not v_tokens
        outs += [jax.ShapeDtypeStruct((b, D_MODEL, tm), F32)] * 2
        ospecs += [pl.BlockSpec((None, D_MODEL, tm), lambda i: (i // nt, 0, 0))] * 2
    kern = functools.partial(_qkv_proj_t_kernel, rope=rope_tabs is not None, logf=wft is not None,
                             v_tokens=v_tokens, tail=tail, scale=scale)
    return pl.pallas_call(kern, out_shape=outs, grid=(m // tm,), in_specs=specs, out_specs=ospecs,
                          compiler_params=_cp(1), name=name)(*ins)


def _mla_proj_kernel(*refs, with_kv, scale):
    it = iter(refs)
    (h_ref, g_ref, wdq_ref, gq_ref, wuq_ref, wc_ref, wr_ref, gkv_ref, gqn_ref, gqr_ref, gkr_ref, gkn_ref,
     cos_ref, sin_ref) = [next(it) for _ in range(14)]
    if with_kv:
        wuk_ref, wuv_ref, wrt_ref, gkrt_ref, cost_ref, sint_ref = [next(it) for _ in range(6)]
    q_out, ckv_out, kr_out, krd_out = next(it), next(it), next(it), next(it)
    a = _rms(h_ref[...], g_ref[...]).astype(BF16)
    cq = _rms(_dot(a, wdq_ref[...]), gq_ref[...]).astype(BF16)
    q = _dot(cq, wuq_ref[...])
    n_nope = MLA_HEADS * MLA_NOPE
    qn = _head_rms(q[:, :n_nope], gqn_ref[...], MLA_NOPE) * scale
    if not with_kv:
        qn = qn * gkn_ref[...]
    qr = _rope(_head_rms(q[:, n_nope:], gqr_ref[...], MLA_ROPE), cos_ref[...], sin_ref[...]) * scale
    lo = _lane_iota((q.shape[0], LANES)) < MLA_ROPE
    pieces = []
    for hd in range(MLA_HEADS):
        slab = qr[:, LANES * (hd // 2):LANES * (hd // 2 + 1)]
        keep = lo if hd % 2 == 0 else jnp.logical_not(lo)
        pieces += [qn[:, LANES * hd:LANES * (hd + 1)], jnp.where(keep, slab, 0.0)]
    q_out[...] = jnp.concatenate(pieces, axis=-1).astype(BF16)
    ckv = _rms(_dot(a, wc_ref[...]), gkv_ref[...])
    ckv_out[...] = ckv
    kr2 = _rope(_head_rms(_dot(a, wr_ref[...]), gkr_ref[...], MLA_ROPE), cos_ref[...], sin_ref[...])
    kr_out[...] = kr2[:, :MLA_ROPE]
    krd_out[...] = kr2
    if with_kv:
        kc_out, v_out, krt_out = next(it), next(it), next(it)
        c = ckv.astype(BF16)
        kn = _head_rms(_dot(c, wuk_ref[...]), gkn_ref[...], MLA_NOPE)
        pieces = []
        for hd in range(MLA_HEADS):
            pieces += [kn[:, LANES * hd:LANES * (hd + 1)], kr2]
        kc_out[...] = jnp.concatenate(pieces, axis=-1).astype(BF16)
        v_out[...] = _dot(c, wuv_ref[...]).astype(BF16)
        krt_out[...] = _rope_t(_head_rms_t(_dot_nt(wrt_ref[...], a), gkrt_ref[...]), cost_ref[...], sint_ref[...])


def _mla_proj(h, g, w, rope_tabs, *, b, t, tm, with_kv, scale, name):
    m = h.shape[0]
    cos, sinp = rope_tabs[:2]
    nt = cos.shape[0] // tm
    row = lambda n: pl.BlockSpec((tm, n), lambda i: (i, 0))
    full = lambda a: pl.BlockSpec(a.shape, lambda i: (0,) * a.ndim)
    tab = pl.BlockSpec((tm, LANES), lambda i: (i % nt, 0))
    ins = [h, g, w["wdq"], w["gq"], w["wuq"], w["wc"], w["wr"], w["gkv"], w["gqn"], w["gqr"], w["gkr"], w["gkn"],
           cos, sinp]
    specs = [row(D_MODEL)] + [full(x) for x in ins[1:12]] + [tab, tab]
    qw = MLA_HEADS * 2 * LANES
    outs = [jax.ShapeDtypeStruct((m, qw), BF16), jax.ShapeDtypeStruct((m, MLA_KV_LORA), F32),
            jax.ShapeDtypeStruct((m, MLA_ROPE), F32), jax.ShapeDtypeStruct((m, LANES), F32)]
    ospecs = [row(qw), row(MLA_KV_LORA), row(MLA_ROPE), row(LANES)]
    if with_kv:
        cos_t, sin_t = rope_tabs[2:]
        tab_t = pl.BlockSpec((32, tm), lambda i: (0, i % nt))
        extra = [w["wuk"], w["wuv"], w["wrt"], w["gkrt"]]
        ins += extra + [cos_t, sin_t]
        specs += [full(x) for x in extra] + [tab_t, tab_t]
        outs += [jax.ShapeDtypeStruct((m, qw), BF16), jax.ShapeDtypeStruct((m, MLA_HEADS * MLA_V), BF16),
                 jax.ShapeDtypeStruct((b, MLA_ROPE, t), F32)]
        ospecs += [row(qw), row(MLA_HEADS * MLA_V),
                   pl.BlockSpec((None, MLA_ROPE, tm), lambda i: (i // nt, 0, i % nt))]
    kern = functools.partial(_mla_proj_kernel, with_kv=with_kv, scale=scale)
    return pl.pallas_call(kern, out_shape=outs, grid=(m // tm,), in_specs=specs, out_specs=ospecs,
                          compiler_params=_cp(1), name=name)(*ins)


def _cumsum_kernel(x_ref, o_ref, *, tk):
    nh, t = x_ref.shape
    tri = (_row_iota((tk, tk)) <= _lane_iota((tk, tk))).astype(BF16)
    carry = jnp.zeros((nh, 1), F32)
    for c in range(t // tk):
        hi, mid, lo = _split3(x_ref[:, tk * c:tk * (c + 1)])
        y = _dot(jnp.concatenate([hi, mid, lo], axis=0), tri)
        f = (y[:nh] + y[nh:2 * nh]) + y[2 * nh:] + carry
        o_ref[:, tk * c:tk * (c + 1)] = f
        carry = f[:, tk - 1:tk]


def _cumsum_last(x, *, tk, name):
    b, nh, t = x.shape
    spec = pl.BlockSpec((None, nh, t), lambda i: (i, 0, 0))
    return pl.pallas_call(functools.partial(_cumsum_kernel, tk=tk), out_shape=jax.ShapeDtypeStruct(x.shape, F32),
                          grid=(b,), in_specs=[spec], out_specs=spec, compiler_params=_cp(1), name=name)(x)


def _mla_attn_kernel(q_ref, k_ref, v_ref, o_ref, *, tq, hps):
    kw = 2 * LANES
    t = k_ref.shape[0]
    for qi in range(t // tq):
        rows = slice(tq * qi, tq * (qi + 1))
        limit = (((qi * tq + _row_iota((tq, 1))) >> CHUNK_SHIFT) + 1) << CHUNK_SHIFT
        outs = []
        n0 = qi * tq
        for h in range(hps):
            q = q_ref[rows, kw * h:kw * (h + 1)]
            kcols = slice(kw * h, kw * (h + 1))
            vcols = slice(MLA_V * h, MLA_V * (h + 1))
            s_d = jnp.where(n0 + _lane_iota((1, tq)) < limit, _dot_nt(q, k_ref[n0:n0 + tq, kcols]), NEG)
            m = jnp.max(s_d, axis=-1, keepdims=True)
            if qi:
                s_f = _dot_nt(q, k_ref[0:n0, kcols])
                m = jnp.maximum(m, jnp.max(s_f, axis=-1, keepdims=True))
            p_d = jnp.exp2(s_d - m)
            acc = _dot(p_d.astype(BF16), v_ref[n0:n0 + tq, vcols])
            l = jnp.sum(p_d, axis=-1, keepdims=True)
            if qi:
                p_f = jnp.exp2(s_f - m)
                acc = acc + _dot(p_f.astype(BF16), v_ref[0:n0, vcols])
                l = l + jnp.sum(p_f, axis=-1, keepdims=True)
            outs.append(acc / l)
        o_ref[rows, :] = jnp.concatenate(outs, axis=-1).astype(BF16)


def _mla_attn(q, kc, v, *, b, t, tq, hps, name):
    kw = 2 * LANES * hps
    vw = MLA_V * hps
    return pl.pallas_call(
        functools.partial(_mla_attn_kernel, tq=tq, hps=hps),
        out_shape=jax.ShapeDtypeStruct((b * t, MLA_HEADS * MLA_V), BF16),
        grid=(b, MLA_HEADS // hps),
        in_specs=[pl.BlockSpec((t, kw), lambda i, h: (i, h)),
                  pl.BlockSpec((t, kw), lambda i, h: (i, h)),
                  pl.BlockSpec((t, vw), lambda i, h: (i, h))],
        out_specs=pl.BlockSpec((t, vw), lambda i, h: (i, h)),
        compiler_params=_cp(2), name=name)(q, kc, v)


def _ones_row_values(vt, c):
    row = _row_iota(vt.shape)
    if c == 0:
        return jnp.where(row < 64, vt, jnp.where(row == 64, 1.0, 0.0)).astype(BF16)
    return jnp.where(row >= 64, vt, jnp.where(row == 0, 1.0, 0.0)).astype(BF16)


def _merge_pair(acc0, acc1):
    lo = _lane_iota(acc0.shape) < 64
    return jnp.where(lo, acc0 / acc0[:, 64:65], acc1 / acc1[:, 0:1])


def _pair_attn_kernel(*refs, mode, tq, lam_init):
    if mode == "fox":
        q_all, kt_all, vt_all, fk_all, o_all, ka_all, va_all = refs
    else:
        q_all, kt_all, v_ref, lam_ref, gsub_ref, o_all, ka_all, vb_all = refs
    t = kt_all.shape[1]
    pps = kt_all.shape[0] // LANES
    for j in range(pps):
        _pair_setup(mode, j, pps, t, kt_all, ka_all, *((vt_all, fk_all, va_all) if mode == "fox" else
                                                        (v_ref, None, vb_all)))
    for qi in range(t // tq):
        for j in range(pps):
            cols = slice(LANES * j, LANES * (j + 1))
            o_all[tq * qi:tq * (qi + 1), cols] = _pair_q_tile(
                mode, qi, tq, lam_init, q_all[tq * qi:tq * (qi + 1), cols], ka_all.at[j],
                va_all.at[j] if mode == "fox" else vb_all.at[j], *(() if mode == "fox" else (lam_ref, gsub_ref)))


def _pair_setup(mode, j, pps, t, kt_all, ka_all, v_src, fk_all, v_dst):
    kt_ref = kt_all.at[LANES * j:LANES * (j + 1), :]
    ka = ka_all.at[j]
    if mode == "fox":
        fk_ref, va = fk_all.at[j], v_dst.at[j]
        vt_ref = v_src.at[LANES * j:LANES * (j + 1), :]
        ka[0:LANES, :] = kt_ref[...].astype(BF16)
        f = fk_ref[...] * (-LOG2E)
        terms = [x.astype(F32) for x in _split3(f)]
        nr = 2 * SUBLANES
        row = _row_iota((nr, t))
        aug = jnp.zeros((nr, t), F32)
        for c in range(2):
            for j in range(3):
                aug = jnp.where(row == 3 * c + j, terms[j][c:c + 1, :], aug)
        ka[LANES:LANES + nr, :] = aug.astype(BF16)
        ka[LANES + nr:, :] = jnp.zeros((LANES - nr, t), BF16)
        vt = vt_ref[...]
        va[0] = _ones_row_values(vt, 0)
        va[1] = _ones_row_values(vt, 1)
    else:
        vb = v_dst.at[j]
        ka[...] = kt_ref[...].astype(BF16)
        nhv = v_src.shape[0] // t
        vb[0:LANES, :] = v_src[pl.ds(pl.program_id(1) * pps + j, t, stride=nhv), :].T.astype(BF16)
        vb[LANES:, :] = jnp.where(_row_iota((LANES, t)) == 0, 1.0, 0.0).astype(BF16)


def _pair_q_tile(mode, qi, tq, lam_init, q, ka, vals_ref, lam_ref=None, gsub_ref=None):
    lane = _lane_iota((tq, LANES))
    zero = jnp.zeros_like(q)
    qa = [jnp.where(lane < 64, q, zero), jnp.where(lane < 64, zero, q)]
    qpos = qi * tq + _row_iota((tq, 1))
    if mode == "fox":
        limit = qpos + 1
        pick = [jnp.where(lane < 3 * c, 0.0, jnp.where(lane < 3 * c + 3, 1.0, 0.0)).astype(BF16) for c in range(2)]
        qa = [jnp.concatenate([qa[c], pick[c]], axis=-1) for c in range(2)]
    else:
        limit = ((qpos >> CHUNK_SHIFT) + 1) << CHUNK_SHIFT

    n0 = qi * tq
    accs = []
    for c in range(2):
        s_d = jnp.where(n0 + _lane_iota((1, tq)) < limit, _dot(qa[c], ka[:, n0:n0 + tq]), NEG)
        m = jnp.max(s_d, axis=-1, keepdims=True)
        if qi:
            s_f = _dot(qa[c], ka[:, 0:n0])
            m = jnp.maximum(m, jnp.max(s_f, axis=-1, keepdims=True))
        parts = [(jnp.exp2(s_d - m), n0, n0 + tq)] + ([(jnp.exp2(s_f - m), 0, n0)] if qi else [])
        acc = 0.0
        for p, lo_k, hi_k in parts:
            vals = vals_ref[c, :, lo_k:hi_k] if mode == "fox" else vals_ref[:, lo_k:hi_k]
            acc = acc + _dot_nt(p.astype(BF16), vals)
        accs.append(acc)
    if mode == "fox":
        o = _merge_pair(*accs)
    else:
        lv = lam_ref[...]
        lam = (jnp.exp(jnp.sum(lv[0:1] * lv[1:2], axis=-1, keepdims=True))
               - jnp.exp(jnp.sum(lv[2:3] * lv[3:4], axis=-1, keepdims=True)) + lam_init)
        o0, o1 = [a[:, :LANES] / a[:, LANES:LANES + 1] for a in accs]
        o = _rms(o0 - lam * o1, gsub_ref[...]) * (1.0 - lam_init)
    return o.astype(BF16)


def _pair_attn(q, kt, v, extra, *, mode, b, t, tq, pps, lam_init=0.0, name):
    npair = D_MODEL // LANES
    qspec = pl.BlockSpec((t, LANES * pps), lambda i, p: (i, p))
    ktspec = pl.BlockSpec((None, LANES * pps, t), lambda i, p: (i, p, 0))
    if mode == "fox":
        vspec = ktspec
        especs = [pl.BlockSpec((None, pps, 2, t), lambda i, p: (i, p, 0, 0))]
        scratch = [pltpu.VMEM((pps, 2 * LANES, t), BF16), pltpu.VMEM((pps, 2, LANES, t), BF16)]
    else:
        vspec = pl.BlockSpec((t * npair, LANES), lambda i, p: (i, 0))
        especs = [pl.BlockSpec(x.shape, lambda i, p: (0, 0)) for x in extra]
        scratch = [pltpu.VMEM((pps, LANES, t), BF16), pltpu.VMEM((pps, 2 * LANES, t), BF16)]
    return pl.pallas_call(
        functools.partial(_pair_attn_kernel, mode=mode, tq=tq, lam_init=lam_init),
        out_shape=jax.ShapeDtypeStruct((b * t, D_MODEL), BF16),
        grid=(b, npair // pps),
        in_specs=[qspec, ktspec, vspec] + especs,
        out_specs=qspec,
        scratch_shapes=scratch,
        compiler_params=_cp(2), name=name)(q, kt, v, *extra)


def _rel_gather(tab, width, center):
    idx = jnp.clip(center - _lane_iota((REL_PAD, width)), -REL_CLIP, REL_CLIP) + REL_CLIP
    onehot = (_row_iota((REL_PAD, width)) == idx).astype(BF16)
    hi, mid, lo = _split3(tab)
    return (_dot(hi, onehot) + _dot(mid, onehot)) + _dot(lo, onehot)


def _band_attn_kernel(q_ref, kt_ref, vt_ref, tab_ref, o_ref, kb, va, bias_ref, *, tq, win, bw):
    gw = bw + tq

    @pl.when(pl.program_id(1) == 0)
    def _():
        g = _rel_gather(tab_ref[...], gw, BAND_LEFT + tq) * LOG2E
        ii = _row_iota((tq, bw)) >> CHUNK_SHIFT
        jj = _lane_iota((tq, bw)) >> CHUNK_SHIFT
        allowed = (jj >= ii) & (jj <= ii + BAND_LEFT_CHUNKS)
        for c in range(2):
            rows = jnp.broadcast_to(g[c:c + 1, :], (tq, gw))
            skew = pltpu.roll(rows, gw - tq, axis=1, stride=1, stride_axis=0)
            bias_ref[c] = jnp.where(allowed, skew[:, :bw], NEG)

    kb[...] = kt_ref[...].astype(BF16)
    vt = vt_ref[...]
    va[0] = _ones_row_values(vt, 0)
    va[1] = _ones_row_values(vt, 1)

    lo = _lane_iota((tq, LANES)) < 64
    for r in range(q_ref.shape[0] // tq):
        q0 = r * tq
        ws = max(q0 - BAND_LEFT, 0)
        d = BAND_LEFT - q0 + ws
        k = kb[:, pl.ds(ws, win)]
        q = q_ref[tq * r:tq * (r + 1), :]
        zero = jnp.zeros_like(q)
        accs = []
        for c in range(2):
            qc = jnp.where(lo, q, zero) if c == 0 else jnp.where(lo, zero, q)
            s = _dot(qc, k) + bias_ref[c, :, pl.ds(d, win)]
            p = jnp.exp2(s - jnp.max(s, axis=-1, keepdims=True))
            accs.append(_dot_nt(p.astype(BF16), va[c, :, pl.ds(ws, win)]))
        o_ref[tq * r:tq * (r + 1), :] = _merge_pair(*accs).astype(BF16)


def _band_attn(q, kt, vt, tab, *, b, t, tq, name):
    npair = D_MODEL // LANES
    win = BAND_LEFT + tq
    bw = win + BAND_LEFT
    assert t >= win and tq % CHUNK == 0 and BAND_LEFT % tq == 0
    qspec = pl.BlockSpec((t, LANES), lambda p, i: (i, p))
    kvspec = pl.BlockSpec((None, LANES, t), lambda p, i: (i, p, 0))
    return pl.pallas_call(
        functools.partial(_band_attn_kernel, tq=tq, win=win, bw=bw),
        out_shape=jax.ShapeDtypeStruct((b * t, D_MODEL), BF16),
        grid=(npair, b),
        in_specs=[qspec, kvspec, kvspec, pl.BlockSpec((None, SUBLANES, REL_PAD), lambda p, i: (p, 0, 0))],
        out_specs=qspec,
        scratch_shapes=[pltpu.VMEM((LANES, t), BF16), pltpu.VMEM((2, LANES, t), BF16), pltpu.VMEM((2, tq, bw), F32)],
        compiler_params=_cp(2), name=name)(q, kt, vt, tab)


def _block_diag_q(q, nh, width):
    tq = q.shape[0]
    rep = jnp.concatenate([q] * nh, axis=0)
    keep = (_row_iota(rep.shape) >> _log2(tq)) == (_lane_iota(rep.shape) >> _log2(width))
    return jnp.where(keep, rep, jnp.zeros_like(rep))


def _expand_rows(f, tq):
    return jnp.concatenate([jnp.broadcast_to(f[h:h + 1, :], (tq, f.shape[1])) for h in range(f.shape[0])], axis=0)


def _dec_attn_kernel(*refs, mode, n_cache, tq, past, lam_init):
    if mode == "fox":
        (q_ref, kc_ref, vc_ref, kn_ref, vn_ref, fkc_ref, fkn_ref, o_ref, qb, m_ref, l_ref, acc_ref) = refs
    else:
        (q_ref, kc_ref, vc_ref, kn_ref, vn_ref, lam_ref, gsub_ref, o_ref, qb, m_ref, l_ref, acc_ref) = refs
    t = pl.program_id(1)
    nh = D_MODEL // 64
    rows = nh * tq

    @pl.when(t == 0)
    def _():
        qb[...] = _block_diag_q(q_ref[...], nh, 64)
        m_ref[...] = jnp.full_like(m_ref, NEG)
        l_ref[...] = jnp.zeros_like(l_ref)
        acc_ref[...] = jnp.zeros_like(acc_ref)

    def cache_step():
        s = _dot(qb[...], kc_ref[...].astype(BF16))
        if mode == "fox":
            s = s - _expand_rows(fkc_ref[...] * LOG2E, tq)
            v = vc_ref[...].astype(BF16)
            _softmax_update(s, lambda pb: _dot_nt(pb, v), m_ref, l_ref, acc_ref)
        else:
            tk = kc_ref.shape[1]
            nhv = D_MODEL // LANES
            v = jnp.concatenate([vc_ref[pl.ds(h, tk, stride=nhv), :].astype(BF16) for h in range(nhv)], axis=-1)
            _softmax_update(s, lambda pb: _dot(pb, v), m_ref, l_ref, acc_ref)

    cache_step()

    @pl.when(t == n_cache - 1)
    def _():
        s = _dot_nt(qb[...], kn_ref[...].astype(BF16))
        qpos = past + (_row_iota((rows, tq)) & (tq - 1))
        kpos = past + _lane_iota((rows, tq))
        if mode == "fox":
            s = s - _expand_rows(fkn_ref[...][:, :tq] * LOG2E, tq)
            allowed = kpos <= qpos
        else:
            allowed = (kpos >> CHUNK_SHIFT) <= (qpos >> CHUNK_SHIFT)
        vn = vn_ref[...].astype(BF16)
        _softmax_update(jnp.where(allowed, s, NEG), lambda pb: _dot(pb, vn), m_ref, l_ref, acc_ref)
        o_all = acc_ref[...] / l_ref[...]
        if mode == "fox":
            o = jnp.zeros((tq, D_MODEL), F32)
            hl = _lane_iota((tq, D_MODEL)) >> _log2(FOX_DH)
            for h in range(nh):
                o = jnp.where(hl == h, o_all[h * tq:(h + 1) * tq, :], o)
        else:
            lv = lam_ref[...]
            lam = (jnp.exp(jnp.sum(lv[0:1] * lv[1:2], axis=-1, keepdims=True))
                   - jnp.exp(jnp.sum(lv[2:3] * lv[3:4], axis=-1, keepdims=True)) + lam_init)
            pieces = []
            for h in range(nh // 2):
                a0 = o_all[(2 * h) * tq:(2 * h + 1) * tq, LANES * h:LANES * (h + 1)]
                a1 = o_all[(2 * h + 1) * tq:(2 * h + 2) * tq, LANES * h:LANES * (h + 1)]
                pieces.append(_rms(a0 - lam * a1, gsub_ref[...]) * (1.0 - lam_init))
            o = jnp.concatenate(pieces, axis=-1)
        o_ref[...] = o.astype(BF16)


def _dec_attn(q, kc, vc, kn, vn, extra, *, mode, b, tq, past, tk, lam_init=0.0, name):
    n_cache = past // tk
    nh = D_MODEL // 64
    new = pl.BlockSpec((tq, D_MODEL), lambda i, t: (i, 0))
    cache_t = pl.BlockSpec((None, D_MODEL, tk), lambda i, t: (i, 0, t))
    if mode == "fox":
        (fk,) = extra
        vspec = cache_t
        especs = [pl.BlockSpec((None, nh, tk), lambda i, t: (i, 0, t)),
                  pl.BlockSpec((None, nh, LANES), lambda i, t: (i, 0, past // LANES))]
        ins = [fk, fk]
    else:
        nhv = D_MODEL // LANES
        vspec = pl.BlockSpec((tk * nhv, LANES), lambda i, t: (i * n_cache + t, 0))
        especs = [pl.BlockSpec(x.shape, lambda i, t: (0, 0)) for x in extra]
        ins = list(extra)
    return pl.pallas_call(
        functools.partial(_dec_attn_kernel, mode=mode, n_cache=n_cache, tq=tq, past=past, lam_init=lam_init),
        out_shape=jax.ShapeDtypeStruct((b * tq, D_MODEL), BF16),
        grid=(b, n_cache),
        in_specs=[new, cache_t, vspec, new, new] + especs,
        out_specs=new,
        scratch_shapes=[pltpu.VMEM((nh * tq, D_MODEL), BF16), pltpu.VMEM((nh * tq, 1), F32),
                        pltpu.VMEM((nh * tq, 1), F32), pltpu.VMEM((nh * tq, D_MODEL), F32)],
        compiler_params=_cp(2), name=name)(q, kc, vc, kn, vn, *ins)


def _mla_dec_kernel(q_ref, cc_ref, rc_ref, cn_ref, rn_ref, wukt_ref, wuv_ref, o_ref,
                    qn, qr, m_ref, l_ref, acc_ref, *, n_cache, tq, past):
    t = pl.program_id(1)
    nh = MLA_HEADS
    rows = nh * tq

    @pl.when(t == 0)
    def _():
        q = q_ref[...]
        zero = jnp.zeros((tq, LANES), BF16)
        for h in range(nh):
            qn[h * tq:(h + 1) * tq, :] = jnp.concatenate(
                [q[:, 2 * LANES * h:2 * LANES * h + LANES] if c == h else zero for c in range(nh)], axis=-1)
            qr[h * tq:(h + 1) * tq, :] = q[:, 2 * LANES * h + LANES:2 * LANES * (h + 1)]
        m_ref[...] = jnp.full_like(m_ref, NEG)
        l_ref[...] = jnp.zeros_like(l_ref)
        acc_ref[...] = jnp.zeros_like(acc_ref)

    def step(ckv, s_rope, mask_new):
        c = ckv.astype(BF16)
        knt = _dot_nt(wukt_ref[...], c)
        ms = jnp.concatenate([jnp.mean(jnp.square(knt[MLA_NOPE * h:MLA_NOPE * (h + 1)]), axis=0, keepdims=True)
                              for h in range(nh)], axis=0)
        s = _dot(qn[...], knt.astype(BF16)) * _expand_rows(lax.rsqrt(ms + NORM_EPS), tq) + s_rope
        if mask_new:
            n = ckv.shape[0]
            qpos = past + (_row_iota((rows, n)) & (tq - 1))
            kpos = past + _lane_iota((rows, n))
            s = jnp.where((kpos >> CHUNK_SHIFT) <= (qpos >> CHUNK_SHIFT), s, NEG)
        _softmax_update(s, lambda pb: _dot(pb, c), m_ref, l_ref, acc_ref)

    def cache_step():
        kr = rc_ref[...].astype(BF16)
        step(cc_ref[...], _dot(qr[...], jnp.concatenate([kr, kr], axis=0)), False)

    cache_step()

    @pl.when(t == n_cache - 1)
    def _():
        step(cn_ref[...], _dot_nt(qr[...], rn_ref[...].astype(BF16)), True)
        lat = (acc_ref[...] / l_ref[...]).astype(BF16)
        o_ref[...] = jnp.concatenate(
            [_dot(lat[h * tq:(h + 1) * tq, :], wuv_ref[:, MLA_V * h:MLA_V * (h + 1)]) for h in range(nh)],
            axis=-1).astype(BF16)


def _mla_dec(q, cc, rc, cn, rn, wukt, wuv, *, b, tq, past, tk, name):
    n_cache = past // tk
    nh = MLA_HEADS
    new = lambda n: pl.BlockSpec((tq, n), lambda i, t: (i, 0))
    full = lambda a: pl.BlockSpec(a.shape, lambda i, t: (0,) * a.ndim)
    return pl.pallas_call(
        functools.partial(_mla_dec_kernel, n_cache=n_cache, tq=tq, past=past),
        out_shape=jax.ShapeDtypeStruct((b * tq, nh * MLA_V), BF16),
        grid=(b, n_cache),
        in_specs=[new(nh * 2 * LANES),
                  pl.BlockSpec((tk, MLA_KV_LORA), lambda i, t: (i * n_cache + t, 0)),
                  pl.BlockSpec((None, MLA_ROPE, tk), lambda i, t: (i, 0, t)),
                  new(MLA_KV_LORA), new(LANES), full(wukt), full(wuv)],
        out_specs=new(nh * MLA_V),
        scratch_shapes=[pltpu.VMEM((nh * tq, nh * MLA_NOPE), BF16), pltpu.VMEM((nh * tq, LANES), BF16),
                        pltpu.VMEM((nh * tq, 1), F32), pltpu.VMEM((nh * tq, 1), F32),
                        pltpu.VMEM((nh * tq, MLA_KV_LORA), F32)],
        compiler_params=_cp(2), name=name)(q, cc, rc, cn, rn, wukt, wuv)


def _band_dec_kernel(q_ref, kc_ref, vc_ref, kn_ref, vn_ref, knt_ref, vnt_ref, tab_ref, o_ref, kr_out, vr_out,
                     *, tq, w, past):
    keep_old = _lane_iota((D_MODEL, LANES)) < LANES - tq
    for c_ref, nt_ref, r_out in ((kc_ref, knt_ref, kr_out), (vc_ref, vnt_ref, vr_out)):
        rolled = pltpu.roll(c_ref[...], w - tq, axis=1)
        r_out[:, :w - LANES] = rolled[:, :w - LANES]
        r_out[:, w - LANES:] = jnp.where(keep_old, rolled[:, w - LANES:], nt_ref[...])
    nh = BAND_HEADS
    rows = nh * tq
    bwid = ((w + tq + LANES - 1) // LANES) * LANES
    qb = _block_diag_q(q_ref[...], nh, BAND_DH)
    g = _rel_gather(tab_ref[...], bwid, w + tq) * LOG2E
    bias = jnp.concatenate(
        [pltpu.roll(jnp.broadcast_to(g[h:h + 1, :], (tq, bwid)), bwid - tq, axis=1, stride=1, stride_axis=0)
         for h in range(nh)], axis=0)
    qc = (past + (_row_iota((rows, bwid)) & (tq - 1))) >> CHUNK_SHIFT
    kc = (past - w + _lane_iota((rows, bwid))) >> CHUNK_SHIFT
    bias = jnp.where((kc <= qc) & (kc >= qc - BAND_LEFT_CHUNKS), bias, NEG)
    s_c = _dot(qb, kc_ref[...].astype(BF16)) + bias[:, :w]
    s_n = _dot_nt(qb, kn_ref[...].astype(BF16)) + bias[:, w:w + tq]
    m = jnp.maximum(jnp.max(s_c, axis=-1, keepdims=True), jnp.max(s_n, axis=-1, keepdims=True))
    p_c = jnp.exp2(s_c - m)
    p_n = jnp.exp2(s_n - m)
    l = jnp.sum(p_c, axis=-1, keepdims=True) + jnp.sum(p_n, axis=-1, keepdims=True)
    o_all = (_dot_nt(p_c.astype(BF16), vc_ref[...].astype(BF16))
             + _dot(p_n.astype(BF16), vn_ref[...].astype(BF16))) / l
    o = jnp.zeros((tq, D_MODEL), F32)
    hl = _lane_iota((tq, D_MODEL)) >> _log2(BAND_DH)
    for h in range(nh):
        o = jnp.where(hl == h, o_all[h * tq:(h + 1) * tq, :], o)
    o_ref[...] = o.astype(BF16)


def _band_dec(q, kc, vc, kn, vn, tab, *, b, tq, w, past, name):
    assert w % LANES == 0 and tq <= LANES
    new = pl.BlockSpec((tq, D_MODEL), lambda i: (i, 0))
    cache = pl.BlockSpec((None, D_MODEL, w), lambda i: (i, 0, 0))
    slab = pl.BlockSpec((None, D_MODEL, LANES), lambda i: (i, 0, 0))
    right = lambda x: jnp.pad(jnp.swapaxes(x.reshape(b, tq, D_MODEL), 1, 2), ((0, 0), (0, 0), (LANES - tq, 0)))
    return pl.pallas_call(
        functools.partial(_band_dec_kernel, tq=tq, w=w, past=past),
        out_shape=[jax.ShapeDtypeStruct((b * tq, D_MODEL), BF16), jax.ShapeDtypeStruct(kc.shape, F32),
                   jax.ShapeDtypeStruct(vc.shape, F32)],
        grid=(b,),
        in_specs=[new, cache, cache, new, new, slab, slab, pl.BlockSpec(tab.shape, lambda i: (0, 0))],
        out_specs=[new, cache, cache],
        compiler_params=_cp(1), name=name)(q, kc, vc, kn, vn, right(kn), right(vn), tab)


def _ffn_rows_kernel(*refs, mode, tps, seq, final, tf):
    it = iter(refs)
    h_ref, o_ref, wo_ref, gf_ref, wup_ref, cw_ref, cb_ref, wd_ref = [next(it) for _ in range(8)]
    if mode == "state":
        s1_ref, s2_ref = next(it), next(it)
    if final:
        gfin_ref = next(it)
    out_ref, ug_out, uh_out = next(it), next(it), next(it)
    if mode == "carry":
        carry_ref = next(it)
    tm = h_ref.shape[0]
    h1 = h_ref[...] + _dot(o_ref[...], wo_ref[...])
    xn = _rms(h1, gf_ref[...]).astype(BF16)
    if mode == "carry":
        @pl.when(pl.program_id(0) % tps == 0)
        def _():
            carry_ref[...] = jnp.zeros_like(carry_ref)
    else:
        tpos = _row_iota((tm, tf)) & (seq - 1)
    acts = []
    for j in range(D_FF // tf):
        cs = []
        for part, u_out in enumerate((ug_out, uh_out)):
            tile = slice(tf * j, tf * (j + 1))
            cols = slice(part * D_FF + tf * j, part * D_FF + tf * (j + 1))
            u = _dot(xn, wup_ref[:, cols])
            if mode == "carry":
                tail = u[tm - SUBLANES:]
                u_out[:, tile] = tail
                ext = jnp.concatenate([carry_ref[part, :, tile], u], axis=0)
                carry_ref[part, :, tile] = tail
                um1 = pltpu.roll(ext, 1, axis=0)[SUBLANES:]
                um2 = pltpu.roll(ext, 2, axis=0)[SUBLANES:]
            else:
                u_out[:, tile] = u
                um1 = jnp.where(tpos >= 1, pltpu.roll(u, 1, axis=0), s1_ref[:, cols])
                um2 = jnp.where(tpos >= 2, pltpu.roll(u, 2, axis=0), s2_ref[:, cols])
            cw = cw_ref[:, cols]
            cs.append(((cb_ref[:, cols] + u * cw[2:3]) + um2 * cw[0:1]) + um1 * cw[1:2])
        acts.append((cs[0] * jax.nn.sigmoid(cs[0]) * cs[1]).astype(BF16))
    acc = h1 + _dot(jnp.concatenate(acts, axis=-1), wd_ref[...])
    if final:
        acc = _rms(acc, gfin_ref[...])
    out_ref[...] = acc


def _ffn_rows(h, o, wo, gf, wup, cw, cb, wd, *, layer, tm, tf, seq, state=None, final_g=None, name):
    m = h.shape[0]
    mode = "carry" if state is None else "state"
    tps = seq // tm if mode == "carry" else 1
    row = lambda n: pl.BlockSpec((tm, n), lambda i: (i, 0))
    once = lambda a: pl.BlockSpec(a.shape, lambda i: (0,) * a.ndim, pipeline_mode=pl.Buffered(1))
    of_layer = lambda a: pl.BlockSpec((None,) + a.shape[1:], lambda i: (layer,) + (0,) * (a.ndim - 1),
                                      pipeline_mode=pl.Buffered(1))
    ins = [h, o, wo, gf, wup, cw, cb, wd]
    specs = [row(D_MODEL), row(D_MODEL), once(wo), once(gf), of_layer(wup), once(cw), once(cb), of_layer(wd)]
    scratch = []
    if mode == "state":
        assert m == tm and seq & (seq - 1) == 0
        ins += list(state)
        specs += [row(2 * D_FF), row(2 * D_FF)]
        u_shape = jax.ShapeDtypeStruct((m, D_FF), F32)
        u_spec = row(D_FF)
    else:
        assert seq % tm == 0
        scratch.append(pltpu.VMEM((2, SUBLANES, D_FF), F32))
        u_shape = jax.ShapeDtypeStruct((m // tm, SUBLANES, D_FF), F32)
        u_spec = pl.BlockSpec((None, SUBLANES, D_FF), lambda i: (i, 0, 0))
    if final_g is not None:
        ins.append(final_g)
        specs.append(once(final_g))
    kern = functools.partial(_ffn_rows_kernel, mode=mode, tps=tps, seq=seq, final=final_g is not None, tf=tf)
    return pl.pallas_call(
        kern, out_shape=[jax.ShapeDtypeStruct((m, D_MODEL), F32), u_shape, u_shape],
        grid=(m // tm,), in_specs=specs, out_specs=[row(D_MODEL), u_spec, u_spec], scratch_shapes=scratch,
        compiler_params=_cp(1), name=name)(*ins)


def _rope_tables(pos):
    inv = 1.0 / (ROPE_THETA ** (jnp.arange(0, 64, 2, dtype=F32) / 64))
    ang = pos.astype(F32)[:, None] * inv[None, :]
    c, s = jnp.cos(ang), jnp.sin(ang)
    return jnp.tile(c, (1, 4)), jnp.tile(jnp.concatenate([-s, s], axis=1), (1, 2)), c.T, s.T


def _tile_gain(g, n):
    return jnp.tile(g.astype(F32), n // g.shape[0]).reshape(1, n)


def _gain_t(g, n, w):
    return jnp.broadcast_to(jnp.tile(g.astype(F32), n // g.shape[0])[:, None], (n, w))


def _to_tokens(xt, heads):
    b, n, t = xt.shape
    dh = n // math.prod(heads)
    nd = len(heads)
    return xt.reshape(b, *heads, dh, t).transpose(0, nd + 2, *range(1, nd + 2))


def _to_features(x):
    b, t = x.shape[:2]
    nd = x.ndim
    return x.transpose(0, *range(2, nd), 1).reshape(b, -1, t)


def kernel(x_prompt, x_sample, cache_mla_ckv, cache_mla_krope, cache_fox_k, cache_fox_v, cache_fox_logf,
           cache_diff_k, cache_diff_v, cache_band_k, cache_band_v, state_ffn_conv,
           attn_norm_g, ffn_norm_g, final_norm_g,
           mla_w_dq, mla_g_q, mla_w_uq, mla_w_dkv, mla_g_kv, mla_w_uk, mla_w_uv,
           mla_g_qn, mla_g_qr, mla_g_kn, mla_g_kr, mla_w_o,
           fox_w_qkv, fox_w_f, fox_b_f, fox_g_q, fox_g_k, fox_w_o,
           diff_w_qkv, diff_g_q, diff_g_k, diff_lq1, diff_lk1, diff_lq2, diff_lk2, diff_g_sub, diff_w_o,
           band_w_qkv, band_g_q, band_g_k, band_rel_bias, band_w_o,
           ffn_w_up, ffn_conv_w, ffn_conv_b, ffn_w_down):
    bp, tp, d = x_prompt.shape
    bs, ts, _ = x_sample.shape
    past = cache_mla_ckv.shape[1]
    depth = attn_norm_g.shape[0]
    mp, ms = bp * tp, bs * ts
    tm_p = min(512, tp)
    tm_f = min(1024, tp)
    tq = 512
    tq_band = 256
    tkd = min(1024, past)
    tkw = min(2048, past)
    tf = 256
    assert tp % tq == 0 and past % tkd == 0 and past % tkw == 0 and past % CHUNK == 0

    tobf = lambda a: a.astype(BF16)
    rowv = lambda g: g.astype(F32).reshape(1, -1)
    pos_p = jnp.arange(tp, dtype=jnp.int32)
    pos_s = past + jnp.arange(ts, dtype=jnp.int32)
    tabs_p = _rope_tables(pos_p)
    tabs_s = tuple(jnp.tile(a, (bs, 1)) for a in _rope_tables(pos_s)[:2])

    nope_cols = jnp.arange(MLA_HEADS)[:, None] * (MLA_NOPE + MLA_ROPE) + jnp.arange(MLA_NOPE)[None, :]
    rope_cols = jnp.arange(MLA_HEADS)[:, None] * (MLA_NOPE + MLA_ROPE) + MLA_NOPE + jnp.arange(MLA_ROPE)[None, :]
    wuq_perm = jnp.concatenate([mla_w_uq[:, nope_cols.reshape(-1)], mla_w_uq[:, rope_cols.reshape(-1)]], axis=1)
    wr = mla_w_dkv[:, MLA_KV_LORA:]
    mla_w = dict(
        wdq=tobf(mla_w_dq), gq=rowv(mla_g_q), wuq=tobf(wuq_perm), wc=tobf(mla_w_dkv[:, :MLA_KV_LORA]),
        wr=tobf(jnp.concatenate([wr, wr], axis=1)), gkv=rowv(mla_g_kv),
        gqn=_tile_gain(mla_g_qn, MLA_HEADS * MLA_NOPE), gqr=_tile_gain(mla_g_qr, MLA_HEADS * MLA_ROPE),
        gkr=_tile_gain(mla_g_kr, LANES), wuk=tobf(mla_w_uk), wuv=tobf(mla_w_uv),
        gkn=_tile_gain(mla_g_kn, MLA_HEADS * MLA_NOPE), wrt=tobf(wr.T), gkrt=_gain_t(mla_g_kr, MLA_ROPE, tm_p))

    def split_qkv(w):
        wq, wk, wv = w[:, :d], w[:, d:2 * d], w[:, 2 * d:]
        return tobf(wq), tobf(wk.T), tobf(wv), tobf(wv.T)

    fox_wf = tobf(jnp.pad(fox_w_f, ((0, 0), (0, LANES - FOX_HEADS))))
    fox_bf = jnp.pad(fox_b_f.astype(F32), (0, LANES - FOX_HEADS)).reshape(1, LANES)
    fox_wft = tobf(fox_w_f.T)
    fox_bft = jnp.broadcast_to(fox_b_f.astype(F32)[:, None], (FOX_HEADS, tm_p))
    lamv = jnp.stack([diff_lq1, diff_lk1, diff_lq2, diff_lk2]).astype(F32)
    gsub = rowv(diff_g_sub)
    tab_pad = jnp.pad(band_rel_bias.astype(F32), ((0, 0), (0, REL_PAD - band_rel_bias.shape[1])))
    wo = [tobf(mla_w_o), tobf(fox_w_o), tobf(diff_w_o), tobf(band_w_o)]
    wup, wdn = tobf(ffn_w_up), tobf(ffn_w_down)
    cwf, cbf = ffn_conv_w.astype(F32), ffn_conv_b.astype(F32)

    h_p = x_prompt.reshape(mp, d)
    h_s = x_sample.reshape(ms, d)
    outs = {}
    conv_p, conv_s = [], []
    for i in range(depth):
        kind = i % 4
        ga = rowv(attn_norm_g[i])
        if kind == 0:
            scale = (MLA_NOPE + MLA_ROPE) ** -0.5 * LOG2E
            q_p, ckv_p, _, _, kc_p, v_p, krt_p = _mla_proj(h_p, ga, mla_w, tabs_p, b=bp, t=tp, tm=tm_p, with_kv=True,
                                                           scale=scale, name="mla_proj_p")
            o_p = _mla_attn(q_p, kc_p, v_p, b=bp, t=tp, tq=tq, hps=2, name="mla_attn_p")
            q_s, ckv_s, kr_s, krd_s = _mla_proj(h_s, ga, mla_w, tabs_s, b=bs, t=ts, tm=ms, with_kv=False,
                                                scale=scale, name="mla_proj_s")
            o_s = _mla_dec(q_s, cache_mla_ckv.astype(F32).reshape(bs * past, MLA_KV_LORA),
                           _to_features(cache_mla_krope.astype(F32)), ckv_s, krd_s,
                           tobf(mla_w_uk.T), mla_w["wuv"], b=bs, tq=ts, past=past, tk=tkd, name="mla_attn_s")
            outs["mla"] = (ckv_p.reshape(bp, tp, MLA_KV_LORA), jnp.swapaxes(krt_p, 1, 2),
                           ckv_s.reshape(bs, ts, MLA_KV_LORA), kr_s.reshape(bs, ts, MLA_ROPE))
        elif kind == 1:
            scale = FOX_DH ** -0.5 * LOG2E
            gq, gk = _tile_gain(fox_g_q, d), _tile_gain(fox_g_k, d)
            wq, wkt, wv, wvt = split_qkv(fox_w_qkv)
            q_p, kt_p, vt_p, lft_p = _qkv_proj_t(h_p, ga, wq, wkt, wvt, gq, _gain_t(fox_g_k, d, tm_p), b=bp, t=tp,
                                                 tm=tm_p, wft=fox_wft, bft=fox_bft, scale=scale, name="fox_proj_p")
            fk_p = _cumsum_last(lft_p, tk=tq, name="fox_cumsum_p")
            o_p = _pair_attn(q_p, kt_p, vt_p, (fk_p.reshape(bp, FOX_HEADS // 2, 2, tp),), mode="fox",
                             b=bp, t=tp, tq=tq, pps=4, name="fox_attn_p")
            q_s, k_s, v_s, lf_s = _qkv_proj(h_s, ga, tobf(fox_w_qkv), gq, gk, tm=ms, wf=fox_wf, bf=fox_bf,
                                            scale=scale, name="fox_proj_s")
            lf_all = jnp.concatenate([jnp.swapaxes(cache_fox_logf.astype(F32), 1, 2),
                                      jnp.swapaxes(lf_s.reshape(bs, ts, FOX_HEADS), 1, 2)], axis=2)
            lf_all = jnp.pad(lf_all, ((0, 0), (0, 0), (0, tkw - ts)))
            f_all = _cumsum_last(lf_all, tk=tq, name="fox_cumsum_s")
            o_s = _dec_attn(q_s, _to_features(cache_fox_k.astype(F32)), _to_features(cache_fox_v.astype(F32)),
                            k_s, v_s, (f_all,), mode="fox", b=bs, tq=ts, past=past, tk=tkw, name="fox_attn_s")
            sh = (FOX_HEADS, FOX_DH)
            outs["fox"] = (_to_tokens(kt_p, (FOX_HEADS,)), _to_tokens(vt_p, (FOX_HEADS,)),
                           jnp.swapaxes(lft_p, 1, 2),
                           k_s.reshape(bs, ts, *sh), v_s.reshape(bs, ts, *sh), lf_s.reshape(bs, ts, FOX_HEADS))
        elif kind == 2:
            scale = DIFF_DH ** -0.5 * LOG2E
            lam_init = 0.8 - 0.6 * math.exp(-0.3 * i)
            gq, gk = _tile_gain(diff_g_q, d), _tile_gain(diff_g_k, d)
            wq, wkt, wv, wvt = split_qkv(diff_w_qkv)
            q_p, kt_p, v4_p = _qkv_proj_t(h_p, ga, wq, wkt, wv, gq, _gain_t(diff_g_k, d, tm_p), b=bp, t=tp, tm=tm_p,
                                          rope_tabs=tabs_p, v_tokens=True, scale=scale, name="diff_proj_p")
            v_p = v4_p.reshape(bp, tp, DIFF_HEADS, 2 * DIFF_DH)
            o_p = _pair_attn(q_p, kt_p, v4_p, (lamv, gsub), mode="diff", b=bp, t=tp, tq=tq, pps=2, lam_init=lam_init,
                             name="diff_attn_p")
            q_s, k_s, v_s = _qkv_proj(h_s, ga, tobf(diff_w_qkv), gq, gk, tm=ms, rope_tabs=tabs_s, scale=scale,
                                      name="diff_proj_s")
            o_s = _dec_attn(q_s, _to_features(cache_diff_k.astype(F32)),
                            cache_diff_v.astype(F32).reshape(bs * past * DIFF_HEADS, 2 * DIFF_DH), k_s, v_s,
                            (lamv, gsub), mode="diff", b=bs, tq=ts, past=past, tk=tkw, lam_init=lam_init,
                            name="diff_attn_s")
            outs["diff"] = (_to_tokens(kt_p, (DIFF_HEADS, 2)), v_p,
                            k_s.reshape(bs, ts, DIFF_HEADS, 2, DIFF_DH), v_s.reshape(bs, ts, DIFF_HEADS, 2 * DIFF_DH))
        else:
            scale = BAND_DH ** -0.5 * LOG2E
            gq, gk = _tile_gain(band_g_q, d), _tile_gain(band_g_k, d)
            wq, wkt, wv, wvt = split_qkv(band_w_qkv)
            keep = min(BAND_LEFT, tp)
            assert keep == tm_p
            q_p, kt_p, vt_p, kt_keep, vt_keep = _qkv_proj_t(
                h_p, ga, wq, wkt, wvt, gq, _gain_t(band_g_k, d, tm_p), b=bp, t=tp, tm=tm_p, tail=True, scale=scale,
                name="band_proj_p")
            tab_pairs = jnp.pad(tab_pad.reshape(BAND_HEADS // 2, 2, REL_PAD), ((0, 0), (0, SUBLANES - 2), (0, 0)))
            o_p = _band_attn(q_p, kt_p, vt_p, tab_pairs, b=bp, t=tp, tq=tq_band, name="band_attn_p")
            q_s, k_s, v_s = _qkv_proj(h_s, ga, tobf(band_w_qkv), gq, gk, tm=ms, scale=scale, name="band_proj_s")
            w = cache_band_k.shape[1]
            kct = _to_features(cache_band_k.astype(F32))
            vct = _to_features(cache_band_v.astype(F32))
            o_s, k_roll, v_roll = _band_dec(q_s, kct, vct, k_s, v_s, tab_pad, b=bs, tq=ts, w=w, past=past,
                                            name="band_attn_s")
            outs["band"] = (_to_tokens(kt_keep, (BAND_HEADS,)), _to_tokens(vt_keep, (BAND_HEADS,)),
                            _to_tokens(k_roll, (BAND_HEADS,)), _to_tokens(v_roll, (BAND_HEADS,)))

        last = i == depth - 1
        gfin = rowv(final_norm_g) if last else None
        gfn = rowv(ffn_norm_g[i])
        h_p, ug, uh = _ffn_rows(h_p, o_p, wo[kind], gfn, wup, cwf[i], cbf[i].reshape(1, -1), wdn, layer=i, tm=tm_f,
                                tf=tf, seq=tp, final_g=gfin, name="ffn_p")
        tps = tp // tm_f
        conv_p.append(jnp.concatenate([ug, uh], axis=-1)[tps - 1::tps, SUBLANES - (CONV_W - 1):])
        st = state_ffn_conv[i].astype(F32)
        s1 = jnp.pad(st[:, 1:], ((0, 0), (0, ts - 1), (0, 0))).reshape(ms, 2 * D_FF)
        s2 = jnp.pad(st, ((0, 0), (0, ts - 2), (0, 0))).reshape(ms, 2 * D_FF)
        h_s, ug, uh = _ffn_rows(h_s, o_s, wo[kind], gfn, wup, cwf[i], cbf[i].reshape(1, -1), wdn, layer=i, tm=ms,
                                tf=tf, seq=ts, state=(s1, s2), final_g=gfin, name="ffn_s")
        u_s = jnp.concatenate([ug, uh], axis=-1).reshape(bs, ts, 2 * D_FF)
        conv_s.append(jnp.concatenate([st, u_s], axis=1)[:, ts:])

    y_prompt = h_p.reshape(bp, tp, d)
    y_sample = h_s.reshape(bs, ts, d)
    return (y_prompt, y_sample) + outs["mla"] + outs["fox"] + outs["diff"] + outs["band"] + (
        jnp.stack(conv_p, axis=0), jnp.stack(conv_s, axis=0))
```

```python
import functools
import math

import jax
import jax.numpy as jnp
from jax import lax
from jax.experimental import pallas as pl
from jax.experimental.pallas import tpu as pltpu

F32 = jnp.float32
BF16 = jnp.bfloat16

D_MODEL = 1024
CHUNK = 64
CHUNK_SHIFT = CHUNK.bit_length() - 1
ROPE_DIM = 64
ROPE_THETA = 10000.0
NORM_EPS = 1e-6
MLA_HEADS, MLA_Q_LORA, MLA_KV_LORA, MLA_NOPE, MLA_ROPE, MLA_V = 8, 384, 256, 128, 64, 128
FOX_HEADS, FOX_DH = 16, 64
DIFF_HEADS, DIFF_DH = 8, 64
BAND_HEADS, BAND_DH = 16, 64
BAND_LEFT_CHUNKS = 8
BAND_LEFT = BAND_LEFT_CHUNKS * CHUNK
REL_CLIP = 128
D_FF = 2816
CONV_W = 3

LANES = 128
SUBLANES = 8
NEG = -1e30
REL_PAD = 384
LOG2E = math.log2(math.e)
VMEM_LIMIT = 56 * 1024 * 1024


def _cp(n_axes):
    return pltpu.CompilerParams(dimension_semantics=("arbitrary",) * n_axes, vmem_limit_bytes=VMEM_LIMIT)


def _dot(a, b):
    return jnp.dot(a, b, preferred_element_type=F32)


def _dot_nt(a, b):
    return lax.dot_general(a, b, (((1,), (1,)), ((), ())), preferred_element_type=F32)


def _rms(x, g):
    ms = jnp.mean(x * x, axis=-1, keepdims=True)
    return x * lax.rsqrt(ms + NORM_EPS) * g


def _lane_iota(shape):
    return lax.broadcasted_iota(jnp.int32, shape, len(shape) - 1)


def _row_iota(shape):
    return lax.broadcasted_iota(jnp.int32, shape, len(shape) - 2)


def _log2(n):
    assert n & (n - 1) == 0, n
    return n.bit_length() - 1


def _head_rms(x, g, seg):
    n = x.shape[1]
    outs = []
    for c in range(n // LANES):
        xs = x[:, LANES * c:LANES * (c + 1)]
        sq = xs * xs
        if seg == LANES:
            r = lax.rsqrt(jnp.sum(sq, axis=-1, keepdims=True) * (1.0 / seg) + NORM_EPS)
        else:
            lo = _lane_iota(xs.shape) < seg
            s_lo = jnp.sum(jnp.where(lo, sq, 0.0), axis=-1, keepdims=True)
            s_hi = jnp.sum(jnp.where(lo, 0.0, sq), axis=-1, keepdims=True)
            r = jnp.where(lo, lax.rsqrt(s_lo * (1.0 / seg) + NORM_EPS), lax.rsqrt(s_hi * (1.0 / seg) + NORM_EPS))
        outs.append(xs * r * g[:, LANES * c:LANES * (c + 1)])
    return jnp.concatenate(outs, axis=-1) if len(outs) > 1 else outs[0]


def _head_rms_t(x, g, seg=64):
    outs = []
    for hd in range(x.shape[0] // seg):
        xs = x[seg * hd:seg * (hd + 1)]
        r = lax.rsqrt(jnp.sum(xs * xs, axis=0, keepdims=True) * (1.0 / seg) + NORM_EPS)
        outs.append(xs * r * g[seg * hd:seg * (hd + 1)])
    return jnp.concatenate(outs, axis=0) if len(outs) > 1 else outs[0]


def _rope(x, cos, sinp):
    half = ROPE_DIM // 2
    first_half = (_lane_iota((x.shape[0], LANES)) & (ROPE_DIM - 1)) < half
    outs = []
    for c in range(x.shape[1] // LANES):
        xs = x[:, LANES * c:LANES * (c + 1)]
        fwd = pltpu.roll(xs, half, axis=1)
        bwd = pltpu.roll(xs, LANES - half, axis=1)
        outs.append(xs * cos + jnp.where(first_half, bwd, fwd) * sinp)
    return jnp.concatenate(outs, axis=-1) if len(outs) > 1 else outs[0]


def _rope_t(x, cos_t, sin_t):
    outs = []
    for hd in range(x.shape[0] // 64):
        x1 = x[64 * hd:64 * hd + 32]
        x2 = x[64 * hd + 32:64 * (hd + 1)]
        outs += [x1 * cos_t - x2 * sin_t, x2 * cos_t + x1 * sin_t]
    return jnp.concatenate(outs, axis=0)


def _split3(x):
    hi = x.astype(BF16)
    r = x - hi.astype(F32)
    mid = r.astype(BF16)
    lo = (r - mid.astype(F32)).astype(BF16)
    return hi, mid, lo


def _softmax_step(s, pv, m, l, acc):
    m_new = jnp.maximum(m, jnp.max(s, axis=-1, keepdims=True))
    alpha = jnp.exp2(m - m_new)
    p = jnp.exp2(s - m_new)
    if l is not None:
        l = alpha * l + jnp.sum(p, axis=-1, keepdims=True)
    return m_new, l, alpha * acc + pv(p.astype(BF16))


def _softmax_update(s, pv, m_ref, l_ref, acc_ref):
    m, l, acc = _softmax_step(s, pv, m_ref[...], l_ref[...], acc_ref[...])
    m_ref[...] = m
    l_ref[...] = l
    acc_ref[...] = acc


def _log_sigmoid(z):
    return jnp.minimum(z, 0.0) - jnp.log1p(jnp.exp(-jnp.abs(z)))


def _qkv_proj_kernel(*refs, rope, logf, scale):
    it = iter(refs)
    h_ref, g_ref, w_ref, gq_ref, gk_ref = next(it), next(it), next(it), next(it), next(it)
    cos_ref = sin_ref = wf_ref = bf_ref = None
    if rope:
        cos_ref, sin_ref = next(it), next(it)
    if logf:
        wf_ref, bf_ref = next(it), next(it)
    q_out, k_out, v_out = next(it), next(it), next(it)
    a = _rms(h_ref[...], g_ref[...]).astype(BF16)
    qkv = _dot(a, w_ref[...])
    q = _head_rms(qkv[:, :D_MODEL], gq_ref[...], 64)
    k = _head_rms(qkv[:, D_MODEL:2 * D_MODEL], gk_ref[...], 64)
    if rope:
        q = _rope(q, cos_ref[...], sin_ref[...])
        k = _rope(k, cos_ref[...], sin_ref[...])
    q_out[...] = (q * scale).astype(BF16)
    k_out[...] = k
    v_out[...] = qkv[:, 2 * D_MODEL:]
    if logf:
        logf_out = next(it)
        lf = _log_sigmoid(_dot(a, wf_ref[...]) + bf_ref[...])
        logf_out[...] = lf[:, :FOX_HEADS]


def _qkv_proj(h, g, w, gq, gk, *, tm, rope_tabs=None, wf=None, bf=None, scale, name):
    m = h.shape[0]
    grid = (m // tm,)
    row = lambda n: pl.BlockSpec((tm, n), lambda i: (i, 0))
    full = lambda a: pl.BlockSpec(a.shape, lambda i: (0,) * a.ndim)
    ins = [h, g, w, gq, gk]
    specs = [row(D_MODEL), full(g), full(w), full(gq), full(gk)]
    if rope_tabs is not None:
        cos, sinp = rope_tabs
        nt = cos.shape[0] // tm
        tab = pl.BlockSpec((tm, LANES), lambda i: (i % nt, 0))
        ins += [cos, sinp]
        specs += [tab, tab]
    if wf is not None:
        ins += [wf, bf]
        specs += [full(wf), full(bf)]
    outs = [jax.ShapeDtypeStruct((m, D_MODEL), BF16), jax.ShapeDtypeStruct((m, D_MODEL), F32),
            jax.ShapeDtypeStruct((m, D_MODEL), F32)]
    ospecs = [row(D_MODEL), row(D_MODEL), row(D_MODEL)]
    if wf is not None:
        outs.append(jax.ShapeDtypeStruct((m, FOX_HEADS), F32))
        ospecs.append(row(FOX_HEADS))
    kern = functools.partial(_qkv_proj_kernel, rope=rope_tabs is not None, logf=wf is not None, scale=scale)
    return pl.pallas_call(kern, out_shape=outs, grid=grid, in_specs=specs, out_specs=ospecs,
                          compiler_params=_cp(1), name=name)(*ins)


def _qkv_proj_t_kernel(*refs, rope, logf, v_tokens, tail, scale):
    it = iter(refs)
    h_ref, g_ref, wq_ref, wkt_ref, wv_ref, gq_ref, gkt_ref = [next(it) for _ in range(7)]
    if rope:
        cos_ref, sin_ref, cost_ref, sint_ref = [next(it) for _ in range(4)]
    if logf:
        wft_ref, bft_ref = next(it), next(it)
    q_out, kt_out, v_out = next(it), next(it), next(it)
    a = _rms(h_ref[...], g_ref[...]).astype(BF16)
    tm = a.shape[0]
    q = _head_rms(_dot(a, wq_ref[...]), gq_ref[...], 64)
    kt = _head_rms_t(_dot_nt(wkt_ref[...], a), gkt_ref[...])
    if rope:
        q = _rope(q, cos_ref[...], sin_ref[...])
        kt = _rope_t(kt, cost_ref[...], sint_ref[...])
    q_out[...] = (q * scale).astype(BF16)
    kt_out[...] = kt
    if v_tokens:
        v = _dot(a, wv_ref[...])
        nh = D_MODEL // LANES
        for hd in range(nh):
            v_out[pl.ds(hd, tm, stride=nh), :] = v[:, LANES * hd:LANES * (hd + 1)]
    else:
        vt = _dot_nt(wv_ref[...], a)
        v_out[...] = vt
    if logf:
        lft_out = next(it)
        lft_out[...] = _log_sigmoid(_dot_nt(wft_ref[...], a) + bft_ref[...])
    if tail:
        kt_tail, vt_tail = next(it), next(it)
        kt_tail[...] = kt
        vt_tail[...] = vt


def _qkv_proj_t(h, g, wq, wkt, wv, gq, gkt, *, b, t, tm, rope_tabs=None, wft=None, bft=None, v_tokens=False,
                tail=False, scale, name):
    m = b * t
    nt = t // tm
    nh = D_MODEL // LANES
    row = lambda n: pl.BlockSpec((tm, n), lambda i: (i, 0))
    full = lambda a: pl.BlockSpec(a.shape, lambda i: (0,) * a.ndim)
    feat = lambda n: pl.BlockSpec((None, n, tm), lambda i: (i // nt, 0, i % nt))
    ins = [h, g, wq, wkt, wv, gq, gkt]
    specs = [row(D_MODEL)] + [full(x) for x in ins[1:]]
    if rope_tabs is not None:
        cos, sinp, cos_t, sin_t = rope_tabs
        ins += [cos, sinp, cos_t, sin_t]
        tab = pl.BlockSpec((tm, LANES), lambda i: (i % nt, 0))
        tab_t = pl.BlockSpec((32, tm), lambda i: (0, i % nt))
        specs += [tab, tab, tab_t, tab_t]
    if wft is not None:
        ins += [wft, bft]
        specs += [full(wft), full(bft)]
    outs = [jax.ShapeDtypeStruct((m, D_MODEL), BF16), jax.ShapeDtypeStruct((b, D_MODEL, t), F32)]
    ospecs = [row(D_MODEL), feat(D_MODEL)]
    if v_tokens:
        outs.append(jax.ShapeDtypeStruct((m * nh, LANES), F32))
        ospecs.append(pl.BlockSpec((tm * nh, LANES), lambda i: (i, 0)))
    else:
        outs.append(jax.ShapeDtypeStruct((b, D_MODEL, t), F32))
        ospecs.append(feat(D_MODEL))
    if wft is not None:
        outs.append(jax.ShapeDtypeStruct((b, FOX_HEADS, t), F32))
        ospecs.append(feat(FOX_HEADS))
    if tail:
        assert not v_tokens
        outs += [jax.ShapeDtypeStruct((b, D_MODEL, tm), F32)] * 2
        ospecs += [pl.BlockSpec((None, D_MODEL, tm), lambda i: (i // nt, 0, 0))] * 2
    kern = functools.partial(_qkv_proj_t_kernel, rope=rope_tabs is not None, logf=wft is not None,
                             v_tokens=v_tokens, tail=tail, scale=scale)
    return pl.pallas_call(kern, out_shape=outs, grid=(m // tm,), in_specs=specs, out_specs=ospecs,
                          compiler_params=_cp(1), name=name)(*ins)


def _mla_proj_kernel(*refs, with_kv, scale):
    it = iter(refs)
    (h_ref, g_ref, wdq_ref, gq_ref, wuq_ref, wc_ref, wr_ref, gkv_ref, gqn_ref, gqr_ref, gkr_ref, gkn_ref,
     cos_ref, sin_ref) = [next(it) for _ in range(14)]
    if with_kv:
        wuk_ref, wuv_ref, wrt_ref, gkrt_ref, cost_ref, sint_ref = [next(it) for _ in range(6)]
    q_out, ckv_out, kr_out, krd_out = next(it), next(it), next(it), next(it)
    a = _rms(h_ref[...], g_ref[...]).astype(BF16)
    cq = _rms(_dot(a, wdq_ref[...]), gq_ref[...]).astype(BF16)
    q = _dot(cq, wuq_ref[...])
    n_nope = MLA_HEADS * MLA_NOPE
    qn = _head_rms(q[:, :n_nope], gqn_ref[...], MLA_NOPE) * scale
    if not with_kv:
        qn = qn * gkn_ref[...]
    qr = _rope(_head_rms(q[:, n_nope:], gqr_ref[...], MLA_ROPE), cos_ref[...], sin_ref[...]) * scale
    lo = _lane_iota((q.shape[0], LANES)) < MLA_ROPE
    pieces = []
    for hd in range(MLA_HEADS):
        slab = qr[:, LANES * (hd // 2):LANES * (hd // 2 + 1)]
        keep = lo if hd % 2 == 0 else jnp.logical_not(lo)
        pieces += [qn[:, LANES * hd:LANES * (hd + 1)], jnp.where(keep, slab, 0.0)]
    q_out[...] = jnp.concatenate(pieces, axis=-1).astype(BF16)
    ckv = _rms(_dot(a, wc_ref[...]), gkv_ref[...])
    ckv_out[...] = ckv
    kr2 = _rope(_head_rms(_dot(a, wr_ref[...]), gkr_ref[...], MLA_ROPE), cos_ref[...], sin_ref[...])
    kr_out[...] = kr2[:, :MLA_ROPE]
    krd_out[...] = kr2
    if with_kv:
        kc_out, v_out, krt_out = next(it), next(it), next(it)
        c = ckv.astype(BF16)
        kn = _head_rms(_dot(c, wuk_ref[...]), gkn_ref[...], MLA_NOPE)
        pieces = []
        for hd in range(MLA_HEADS):
            pieces += [kn[:, LANES * hd:LANES * (hd + 1)], kr2]
        kc_out[...] = jnp.concatenate(pieces, axis=-1).astype(BF16)
        v_out[...] = _dot(c, wuv_ref[...]).astype(BF16)
        krt_out[...] = _rope_t(_head_rms_t(_dot_nt(wrt_ref[...], a), gkrt_ref[...]), cost_ref[...], sint_ref[...])


def _mla_proj(h, g, w, rope_tabs, *, b, t, tm, with_kv, scale, name):
    m = h.shape[0]
    cos, sinp = rope_tabs[:2]
    nt = cos.shape[0] // tm
    row = lambda n: pl.BlockSpec((tm, n), lambda i: (i, 0))
    full = lambda a: pl.BlockSpec(a.shape, lambda i: (0,) * a.ndim)
    tab = pl.BlockSpec((tm, LANES), lambda i: (i % nt, 0))
    ins = [h, g, w["wdq"], w["gq"], w["wuq"], w["wc"], w["wr"], w["gkv"], w["gqn"], w["gqr"], w["gkr"], w["gkn"],
           cos, sinp]
    specs = [row(D_MODEL)] + [full(x) for x in ins[1:12]] + [tab, tab]
    qw = MLA_HEADS * 2 * LANES
    outs = [jax.ShapeDtypeStruct((m, qw), BF16), jax.ShapeDtypeStruct((m, MLA_KV_LORA), F32),
            jax.ShapeDtypeStruct((m, MLA_ROPE), F32), jax.ShapeDtypeStruct((m, LANES), F32)]
    ospecs = [row(qw), row(MLA_KV_LORA), row(MLA_ROPE), row(LANES)]
    if with_kv:
        cos_t, sin_t = rope_tabs[2:]
        tab_t = pl.BlockSpec((32, tm), lambda i: (0, i % nt))
        extra = [w["wuk"], w["wuv"], w["wrt"], w["gkrt"]]
        ins += extra + [cos_t, sin_t]
        specs += [full(x) for x in extra] + [tab_t, tab_t]
        outs += [jax.ShapeDtypeStruct((m, qw), BF16), jax.ShapeDtypeStruct((m, MLA_HEADS * MLA_V), BF16),
                 jax.ShapeDtypeStruct((b, MLA_ROPE, t), F32)]
        ospecs += [row(qw), row(MLA_HEADS * MLA_V),
                   pl.BlockSpec((None, MLA_ROPE, tm), lambda i: (i // nt, 0, i % nt))]
    kern = functools.partial(_mla_proj_kernel, with_kv=with_kv, scale=scale)
    return pl.pallas_call(kern, out_shape=outs, grid=(m // tm,), in_specs=specs, out_specs=ospecs,
                          compiler_params=_cp(1), name=name)(*ins)


def _cumsum_kernel(x_ref, o_ref, *, tk):
    nh, t = x_ref.shape
    tri = (_row_iota((tk, tk)) <= _lane_iota((tk, tk))).astype(BF16)
    carry = jnp.zeros((nh, 1), F32)
    for c in range(t // tk):
        hi, mid, lo = _split3(x_ref[:, tk * c:tk * (c + 1)])
        y = _dot(jnp.concatenate([hi, mid, lo], axis=0), tri)
        f = (y[:nh] + y[nh:2 * nh]) + y[2 * nh:] + carry
        o_ref[:, tk * c:tk * (c + 1)] = f
        carry = f[:, tk - 1:tk]


def _cumsum_last(x, *, tk, name):
    b, nh, t = x.shape
    spec = pl.BlockSpec((None, nh, t), lambda i: (i, 0, 0))
    return pl.pallas_call(functools.partial(_cumsum_kernel, tk=tk), out_shape=jax.ShapeDtypeStruct(x.shape, F32),
                          grid=(b,), in_specs=[spec], out_specs=spec, compiler_params=_cp(1), name=name)(x)


def _mla_attn_kernel(q_ref, k_ref, v_ref, o_ref, *, tq, hps):
    kw = 2 * LANES
    t = k_ref.shape[0]
    for qi in range(t // tq):
        rows = slice(tq * qi, tq * (qi + 1))
        limit = (((qi * tq + _row_iota((tq, 1))) >> CHUNK_SHIFT) + 1) << CHUNK_SHIFT
        outs = []
        n0 = qi * tq
        for h in range(hps):
            q = q_ref[rows, kw * h:kw * (h + 1)]
            kcols = slice(kw * h, kw * (h + 1))
            vcols = slice(MLA_V * h, MLA_V * (h + 1))
            s_d = jnp.where(n0 + _lane_iota((1, tq)) < limit, _dot_nt(q, k_ref[n0:n0 + tq, kcols]), NEG)
            m = jnp.max(s_d, axis=-1, keepdims=True)
            if qi:
                s_f = _dot_nt(q, k_ref[0:n0, kcols])
                m = jnp.maximum(m, jnp.max(s_f, axis=-1, keepdims=True))
            p_d = jnp.exp2(s_d - m)
            acc = _dot(p_d.astype(BF16), v_ref[n0:n0 + tq, vcols])
            l = jnp.sum(p_d, axis=-1, keepdims=True)
            if qi:
                p_f = jnp.exp2(s_f - m)
                acc = acc + _dot(p_f.astype(BF16), v_ref[0:n0, vcols])
                l = l + jnp.sum(p_f, axis=-1, keepdims=True)
            outs.append(acc / l)
        o_ref[rows, :] = jnp.concatenate(outs, axis=-1).astype(BF16)


def _mla_attn(q, kc, v, *, b, t, tq, hps, name):
    kw = 2 * LANES * hps
    vw = MLA_V * hps
    return pl.pallas_call(
        functools.partial(_mla_attn_kernel, tq=tq, hps=hps),
        out_shape=jax.ShapeDtypeStruct((b * t, MLA_HEADS * MLA_V), BF16),
        grid=(b, MLA_HEADS // hps),
        in_specs=[pl.BlockSpec((t, kw), lambda i, h: (i, h)),
                  pl.BlockSpec((t, kw), lambda i, h: (i, h)),
                  pl.BlockSpec((t, vw), lambda i, h: (i, h))],
        out_specs=pl.BlockSpec((t, vw), lambda i, h: (i, h)),
        compiler_params=_cp(2), name=name)(q, kc, v)


def _ones_row_values(vt, c):
    row = _row_iota(vt.shape)
    if c == 0:
        return jnp.where(row < 64, vt, jnp.where(row == 64, 1.0, 0.0)).astype(BF16)
    return jnp.where(row >= 64, vt, jnp.where(row == 0, 1.0, 0.0)).astype(BF16)


def _merge_pair(acc0, acc1):
    lo = _lane_iota(acc0.shape) < 64
    return jnp.where(lo, acc0 / acc0[:, 64:65], acc1 / acc1[:, 0:1])


def _pair_attn_kernel(*refs, mode, tq, lam_init):
    if mode == "fox":
        q_all, kt_all, vt_all, fk_all, o_all, ka_all, va_all = refs
    else:
        q_all, kt_all, v_ref, lam_ref, gsub_ref, o_all, ka_all, vb_all = refs
    t = kt_all.shape[1]
    pps = kt_all.shape[0] // LANES
    for j in range(pps):
        _pair_setup(mode, j, pps, t, kt_all, ka_all, *((vt_all, fk_all, va_all) if mode == "fox" else
                                                        (v_ref, None, vb_all)))
    for qi in range(t // tq):
        for j in range(pps):
            cols = slice(LANES * j, LANES * (j + 1))
            o_all[tq * qi:tq * (qi + 1), cols] = _pair_q_tile(
                mode, qi, tq, lam_init, q_all[tq * qi:tq * (qi + 1), cols], ka_all.at[j],
                va_all.at[j] if mode == "fox" else vb_all.at[j], *(() if mode == "fox" else (lam_ref, gsub_ref)))


def _pair_setup(mode, j, pps, t, kt_all, ka_all, v_src, fk_all, v_dst):
    kt_ref = kt_all.at[LANES * j:LANES * (j + 1), :]
    ka = ka_all.at[j]
    if mode == "fox":
        fk_ref, va = fk_all.at[j], v_dst.at[j]
        vt_ref = v_src.at[LANES * j:LANES * (j + 1), :]
        ka[0:LANES, :] = kt_ref[...].astype(BF16)
        f = fk_ref[...] * (-LOG2E)
        terms = [x.astype(F32) for x in _split3(f)]
        nr = 2 * SUBLANES
        row = _row_iota((nr, t))
        aug = jnp.zeros((nr, t), F32)
        for c in range(2):
            for part in range(3):
                aug = jnp.where(row == 3 * c + part, terms[part][c:c + 1, :], aug)
        ka[LANES:LANES + nr, :] = aug.astype(BF16)
        ka[LANES + nr:, :] = jnp.zeros((LANES - nr, t), BF16)
        vt = vt_ref[...]
        va[0] = _ones_row_values(vt, 0)
        va[1] = _ones_row_values(vt, 1)
    else:
        vb = v_dst.at[j]
        ka[...] = kt_ref[...].astype(BF16)
        nhv = v_src.shape[0] // t
        vb[0:LANES, :] = v_src[pl.ds(pl.program_id(1) * pps + j, t, stride=nhv), :].T.astype(BF16)
        vb[LANES:, :] = jnp.where(_row_iota((LANES, t)) == 0, 1.0, 0.0).astype(BF16)


def _pair_q_tile(mode, qi, tq, lam_init, q, ka, vals_ref, lam_ref=None, gsub_ref=None):
    lane = _lane_iota((tq, LANES))
    zero = jnp.zeros_like(q)
    qa = [jnp.where(lane < 64, q, zero), jnp.where(lane < 64, zero, q)]
    qpos = qi * tq + _row_iota((tq, 1))
    if mode == "fox":
        limit = qpos + 1
        pick = [jnp.where(lane < 3 * c, 0.0, jnp.where(lane < 3 * c + 3, 1.0, 0.0)).astype(BF16) for c in range(2)]
        qa = [jnp.concatenate([qa[c], pick[c]], axis=-1) for c in range(2)]
    else:
        limit = ((qpos >> CHUNK_SHIFT) + 1) << CHUNK_SHIFT

    n0 = qi * tq
    accs = []
    for c in range(2):
        s_d = jnp.where(n0 + _lane_iota((1, tq)) < limit, _dot(qa[c], ka[:, n0:n0 + tq]), NEG)
        m = jnp.max(s_d, axis=-1, keepdims=True)
        if qi:
            s_f = _dot(qa[c], ka[:, 0:n0])
            m = jnp.maximum(m, jnp.max(s_f, axis=-1, keepdims=True))
        parts = [(jnp.exp2(s_d - m), n0, n0 + tq)] + ([(jnp.exp2(s_f - m), 0, n0)] if qi else [])
        acc = 0.0
        for p, lo_k, hi_k in parts:
            vals = vals_ref[c, :, lo_k:hi_k] if mode == "fox" else vals_ref[:, lo_k:hi_k]
            acc = acc + _dot_nt(p.astype(BF16), vals)
        accs.append(acc)
    if mode == "fox":
        o = _merge_pair(*accs)
    else:
        lv = lam_ref[...]
        lam = (jnp.exp(jnp.sum(lv[0:1] * lv[1:2], axis=-1, keepdims=True))
               - jnp.exp(jnp.sum(lv[2:3] * lv[3:4], axis=-1, keepdims=True)) + lam_init)
        o0, o1 = [a[:, :LANES] / a[:, LANES:LANES + 1] for a in accs]
        o = _rms(o0 - lam * o1, gsub_ref[...]) * (1.0 - lam_init)
    return o.astype(BF16)


def _pair_attn(q, kt, v, extra, *, mode, b, t, tq, pps, lam_init=0.0, name):
    npair = D_MODEL // LANES
    qspec = pl.BlockSpec((t, LANES * pps), lambda i, p: (i, p))
    ktspec = pl.BlockSpec((None, LANES * pps, t), lambda i, p: (i, p, 0))
    if mode == "fox":
        vspec = ktspec
        especs = [pl.BlockSpec((None, pps, 2, t), lambda i, p: (i, p, 0, 0))]
        scratch = [pltpu.VMEM((pps, 2 * LANES, t), BF16), pltpu.VMEM((pps, 2, LANES, t), BF16)]
    else:
        vspec = pl.BlockSpec((t * npair, LANES), lambda i, p: (i, 0))
        especs = [pl.BlockSpec(x.shape, lambda i, p: (0, 0)) for x in extra]
        scratch = [pltpu.VMEM((pps, LANES, t), BF16), pltpu.VMEM((pps, 2 * LANES, t), BF16)]
    return pl.pallas_call(
        functools.partial(_pair_attn_kernel, mode=mode, tq=tq, lam_init=lam_init),
        out_shape=jax.ShapeDtypeStruct((b * t, D_MODEL), BF16),
        grid=(b, npair // pps),
        in_specs=[qspec, ktspec, vspec] + especs,
        out_specs=qspec,
        scratch_shapes=scratch,
        compiler_params=_cp(2), name=name)(q, kt, v, *extra)


def _rel_gather(tab, width, center):
    idx = jnp.clip(center - _lane_iota((REL_PAD, width)), -REL_CLIP, REL_CLIP) + REL_CLIP
    onehot = (_row_iota((REL_PAD, width)) == idx).astype(BF16)
    hi, mid, lo = _split3(tab)
    return (_dot(hi, onehot) + _dot(mid, onehot)) + _dot(lo, onehot)


def _band_attn_kernel(q_ref, kt_ref, vt_ref, tab_ref, o_ref, kb, va, bias_ref, *, tq, win, bw):
    gw = bw + tq

    @pl.when(pl.program_id(1) == 0)
    def _():
        g = _rel_gather(tab_ref[...], gw, BAND_LEFT + tq) * LOG2E
        ii = _row_iota((tq, bw)) >> CHUNK_SHIFT
        jj = _lane_iota((tq, bw)) >> CHUNK_SHIFT
        allowed = (jj >= ii) & (jj <= ii + BAND_LEFT_CHUNKS)
        for c in range(2):
            rows = jnp.broadcast_to(g[c:c + 1, :], (tq, gw))
            skew = pltpu.roll(rows, gw - tq, axis=1, stride=1, stride_axis=0)
            bias_ref[c] = jnp.where(allowed, skew[:, :bw], NEG)

    kb[...] = kt_ref[...].astype(BF16)
    vt = vt_ref[...]
    va[0] = _ones_row_values(vt, 0)
    va[1] = _ones_row_values(vt, 1)

    lo = _lane_iota((tq, LANES)) < 64
    for r in range(q_ref.shape[0] // tq):
        q0 = r * tq
        ws = max(q0 - BAND_LEFT, 0)
        d = BAND_LEFT - q0 + ws
        k = kb[:, pl.ds(ws, win)]
        q = q_ref[tq * r:tq * (r + 1), :]
        zero = jnp.zeros_like(q)
        accs = []
        for c in range(2):
            qc = jnp.where(lo, q, zero) if c == 0 else jnp.where(lo, zero, q)
            s = _dot(qc, k) + bias_ref[c, :, pl.ds(d, win)]
            p = jnp.exp2(s - jnp.max(s, axis=-1, keepdims=True))
            accs.append(_dot_nt(p.astype(BF16), va[c, :, pl.ds(ws, win)]))
        o_ref[tq * r:tq * (r + 1), :] = _merge_pair(*accs).astype(BF16)


def _band_attn(q, kt, vt, tab, *, b, t, tq, name):
    npair = D_MODEL // LANES
    win = BAND_LEFT + tq
    bw = win + BAND_LEFT
    assert t >= win and tq % CHUNK == 0 and BAND_LEFT % tq == 0
    qspec = pl.BlockSpec((t, LANES), lambda p, i: (i, p))
    kvspec = pl.BlockSpec((None, LANES, t), lambda p, i: (i, p, 0))
    return pl.pallas_call(
        functools.partial(_band_attn_kernel, tq=tq, win=win, bw=bw),
        out_shape=jax.ShapeDtypeStruct((b * t, D_MODEL), BF16),
        grid=(npair, b),
        in_specs=[qspec, kvspec, kvspec, pl.BlockSpec((None, SUBLANES, REL_PAD), lambda p, i: (p, 0, 0))],
        out_specs=qspec,
        scratch_shapes=[pltpu.VMEM((LANES, t), BF16), pltpu.VMEM((2, LANES, t), BF16), pltpu.VMEM((2, tq, bw), F32)],
        compiler_params=_cp(2), name=name)(q, kt, vt, tab)


def _block_diag_q(q, nh, width):
    tq = q.shape[0]
    rep = jnp.concatenate([q] * nh, axis=0)
    keep = (_row_iota(rep.shape) >> _log2(tq)) == (_lane_iota(rep.shape) >> _log2(width))
    return jnp.where(keep, rep, jnp.zeros_like(rep))


def _expand_rows(f, tq):
    return jnp.concatenate([jnp.broadcast_to(f[h:h + 1, :], (tq, f.shape[1])) for h in range(f.shape[0])], axis=0)


def _dec_attn_kernel(*refs, mode, n_cache, tq, past, lam_init):
    if mode == "fox":
        (q_ref, kc_ref, vc_ref, kn_ref, vn_ref, fkc_ref, fkn_ref, o_ref, qb, m_ref, l_ref, acc_ref) = refs
    else:
        (q_ref, kc_ref, vc_ref, kn_ref, vn_ref, lam_ref, gsub_ref, o_ref, qb, m_ref, l_ref, acc_ref) = refs
    t = pl.program_id(1)
    nh = D_MODEL // 64
    rows = nh * tq

    @pl.when(t == 0)
    def _():
        qb[...] = _block_diag_q(q_ref[...], nh, 64)
        m_ref[...] = jnp.full_like(m_ref, NEG)
        l_ref[...] = jnp.zeros_like(l_ref)
        acc_ref[...] = jnp.zeros_like(acc_ref)

    def cache_step():
        s = _dot(qb[...], kc_ref[...].astype(BF16))
        if mode == "fox":
            s = s - _expand_rows(fkc_ref[...] * LOG2E, tq)
            v = vc_ref[...].astype(BF16)
            _softmax_update(s, lambda pb: _dot_nt(pb, v), m_ref, l_ref, acc_ref)
        else:
            tk = kc_ref.shape[1]
            nhv = D_MODEL // LANES
            v = jnp.concatenate([vc_ref[pl.ds(h, tk, stride=nhv), :].astype(BF16) for h in range(nhv)], axis=-1)
            _softmax_update(s, lambda pb: _dot(pb, v), m_ref, l_ref, acc_ref)

    cache_step()

    @pl.when(t == n_cache - 1)
    def _():
        s = _dot_nt(qb[...], kn_ref[...].astype(BF16))
        qpos = past + (_row_iota((rows, tq)) & (tq - 1))
        kpos = past + _lane_iota((rows, tq))
        if mode == "fox":
            s = s - _expand_rows(fkn_ref[...][:, :tq] * LOG2E, tq)
            allowed = kpos <= qpos
        else:
            allowed = (kpos >> CHUNK_SHIFT) <= (qpos >> CHUNK_SHIFT)
        vn = vn_ref[...].astype(BF16)
        _softmax_update(jnp.where(allowed, s, NEG), lambda pb: _dot(pb, vn), m_ref, l_ref, acc_ref)
        o_all = acc_ref[...] / l_ref[...]
        if mode == "fox":
            o = jnp.zeros((tq, D_MODEL), F32)
            hl = _lane_iota((tq, D_MODEL)) >> _log2(FOX_DH)
            for h in range(nh):
                o = jnp.where(hl == h, o_all[h * tq:(h + 1) * tq, :], o)
        else:
            lv = lam_ref[...]
            lam = (jnp.exp(jnp.sum(lv[0:1] * lv[1:2], axis=-1, keepdims=True))
                   - jnp.exp(jnp.sum(lv[2:3] * lv[3:4], axis=-1, keepdims=True)) + lam_init)
            pieces = []
            for h in range(nh // 2):
                a0 = o_all[(2 * h) * tq:(2 * h + 1) * tq, LANES * h:LANES * (h + 1)]
                a1 = o_all[(2 * h + 1) * tq:(2 * h + 2) * tq, LANES * h:LANES * (h + 1)]
                pieces.append(_rms(a0 - lam * a1, gsub_ref[...]) * (1.0 - lam_init))
            o = jnp.concatenate(pieces, axis=-1)
        o_ref[...] = o.astype(BF16)


def _dec_attn(q, kc, vc, kn, vn, extra, *, mode, b, tq, past, tk, lam_init=0.0, name):
    n_cache = past // tk
    nh = D_MODEL // 64
    new = pl.BlockSpec((tq, D_MODEL), lambda i, t: (i, 0))
    cache_t = pl.BlockSpec((None, D_MODEL, tk), lambda i, t: (i, 0, t))
    if mode == "fox":
        (fk,) = extra
        vspec = cache_t
        especs = [pl.BlockSpec((None, nh, tk), lambda i, t: (i, 0, t)),
                  pl.BlockSpec((None, nh, LANES), lambda i, t: (i, 0, past // LANES))]
        ins = [fk, fk]
    else:
        nhv = D_MODEL // LANES
        vspec = pl.BlockSpec((tk * nhv, LANES), lambda i, t: (i * n_cache + t, 0))
        especs = [pl.BlockSpec(x.shape, lambda i, t: (0, 0)) for x in extra]
        ins = list(extra)
    return pl.pallas_call(
        functools.partial(_dec_attn_kernel, mode=mode, n_cache=n_cache, tq=tq, past=past, lam_init=lam_init),
        out_shape=jax.ShapeDtypeStruct((b * tq, D_MODEL), BF16),
        grid=(b, n_cache),
        in_specs=[new, cache_t, vspec, new, new] + especs,
        out_specs=new,
        scratch_shapes=[pltpu.VMEM((nh * tq, D_MODEL), BF16), pltpu.VMEM((nh * tq, 1), F32),
                        pltpu.VMEM((nh * tq, 1), F32), pltpu.VMEM((nh * tq, D_MODEL), F32)],
        compiler_params=_cp(2), name=name)(q, kc, vc, kn, vn, *ins)


def _mla_dec_kernel(q_ref, cc_ref, rc_ref, cn_ref, rn_ref, wukt_ref, wuv_ref, o_ref,
                    qn, qr, m_ref, l_ref, acc_ref, *, n_cache, tq, past):
    t = pl.program_id(1)
    nh = MLA_HEADS
    rows = nh * tq

    @pl.when(t == 0)
    def _():
        q = q_ref[...]
        zero = jnp.zeros((tq, LANES), BF16)
        for h in range(nh):
            qn[h * tq:(h + 1) * tq, :] = jnp.concatenate(
                [q[:, 2 * LANES * h:2 * LANES * h + LANES] if c == h else zero for c in range(nh)], axis=-1)
            qr[h * tq:(h + 1) * tq, :] = q[:, 2 * LANES * h + LANES:2 * LANES * (h + 1)]
        m_ref[...] = jnp.full_like(m_ref, NEG)
        l_ref[...] = jnp.zeros_like(l_ref)
        acc_ref[...] = jnp.zeros_like(acc_ref)

    def step(ckv, s_rope, mask_new):
        c = ckv.astype(BF16)
        knt = _dot_nt(wukt_ref[...], c)
        ms = jnp.concatenate([jnp.mean(jnp.square(knt[MLA_NOPE * h:MLA_NOPE * (h + 1)]), axis=0, keepdims=True)
                              for h in range(nh)], axis=0)
        s = _dot(qn[...], knt.astype(BF16)) * _expand_rows(lax.rsqrt(ms + NORM_EPS), tq) + s_rope
        if mask_new:
            n = ckv.shape[0]
            qpos = past + (_row_iota((rows, n)) & (tq - 1))
            kpos = past + _lane_iota((rows, n))
            s = jnp.where((kpos >> CHUNK_SHIFT) <= (qpos >> CHUNK_SHIFT), s, NEG)
        _softmax_update(s, lambda pb: _dot(pb, c), m_ref, l_ref, acc_ref)

    def cache_step():
        kr = rc_ref[...].astype(BF16)
        step(cc_ref[...], _dot(qr[...], jnp.concatenate([kr, kr], axis=0)), False)

    cache_step()

    @pl.when(t == n_cache - 1)
    def _():
        step(cn_ref[...], _dot_nt(qr[...], rn_ref[...].astype(BF16)), True)
        lat = (acc_ref[...] / l_ref[...]).astype(BF16)
        o_ref[...] = jnp.concatenate(
            [_dot(lat[h * tq:(h + 1) * tq, :], wuv_ref[:, MLA_V * h:MLA_V * (h + 1)]) for h in range(nh)],
            axis=-1).astype(BF16)


def _mla_dec(q, cc, rc, cn, rn, wukt, wuv, *, b, tq, past, tk, name):
    n_cache = past // tk
    nh = MLA_HEADS
    new = lambda n: pl.BlockSpec((tq, n), lambda i, t: (i, 0))
    full = lambda a: pl.BlockSpec(a.shape, lambda i, t: (0,) * a.ndim)
    return pl.pallas_call(
        functools.partial(_mla_dec_kernel, n_cache=n_cache, tq=tq, past=past),
        out_shape=jax.ShapeDtypeStruct((b * tq, nh * MLA_V), BF16),
        grid=(b, n_cache),
        in_specs=[new(nh * 2 * LANES),
                  pl.BlockSpec((tk, MLA_KV_LORA), lambda i, t: (i * n_cache + t, 0)),
                  pl.BlockSpec((None, MLA_ROPE, tk), lambda i, t: (i, 0, t)),
                  new(MLA_KV_LORA), new(LANES), full(wukt), full(wuv)],
        out_specs=new(nh * MLA_V),
        scratch_shapes=[pltpu.VMEM((nh * tq, nh * MLA_NOPE), BF16), pltpu.VMEM((nh * tq, LANES), BF16),
                        pltpu.VMEM((nh * tq, 1), F32), pltpu.VMEM((nh * tq, 1), F32),
                        pltpu.VMEM((nh * tq, MLA_KV_LORA), F32)],
        compiler_params=_cp(2), name=name)(q, cc, rc, cn, rn, wukt, wuv)


def _band_dec_kernel(q_ref, kc_ref, vc_ref, kn_ref, vn_ref, knt_ref, vnt_ref, tab_ref, o_ref, kr_out, vr_out,
                     *, tq, w, past):
    keep_old = _lane_iota((D_MODEL, LANES)) < LANES - tq
    for c_ref, nt_ref, r_out in ((kc_ref, knt_ref, kr_out), (vc_ref, vnt_ref, vr_out)):
        rolled = pltpu.roll(c_ref[...], w - tq, axis=1)
        r_out[:, :w - LANES] = rolled[:, :w - LANES]
        r_out[:, w - LANES:] = jnp.where(keep_old, rolled[:, w - LANES:], nt_ref[...])
    nh = BAND_HEADS
    rows = nh * tq
    bwid = ((w + tq + LANES - 1) // LANES) * LANES
    qb = _block_diag_q(q_ref[...], nh, BAND_DH)
    g = _rel_gather(tab_ref[...], bwid, w + tq) * LOG2E
    bias = jnp.concatenate(
        [pltpu.roll(jnp.broadcast_to(g[h:h + 1, :], (tq, bwid)), bwid - tq, axis=1, stride=1, stride_axis=0)
         for h in range(nh)], axis=0)
    qc = (past + (_row_iota((rows, bwid)) & (tq - 1))) >> CHUNK_SHIFT
    kc = (past - w + _lane_iota((rows, bwid))) >> CHUNK_SHIFT
    bias = jnp.where((kc <= qc) & (kc >= qc - BAND_LEFT_CHUNKS), bias, NEG)
    s_c = _dot(qb, kc_ref[...].astype(BF16)) + bias[:, :w]
    s_n = _dot_nt(qb, kn_ref[...].astype(BF16)) + bias[:, w:w + tq]
    m = jnp.maximum(jnp.max(s_c, axis=-1, keepdims=True), jnp.max(s_n, axis=-1, keepdims=True))
    p_c = jnp.exp2(s_c - m)
    p_n = jnp.exp2(s_n - m)
    l = jnp.sum(p_c, axis=-1, keepdims=True) + jnp.sum(p_n, axis=-1, keepdims=True)
    o_all = (_dot_nt(p_c.astype(BF16), vc_ref[...].astype(BF16))
             + _dot(p_n.astype(BF16), vn_ref[...].astype(BF16))) / l
    o = jnp.zeros((tq, D_MODEL), F32)
    hl = _lane_iota((tq, D_MODEL)) >> _log2(BAND_DH)
    for h in range(nh):
        o = jnp.where(hl == h, o_all[h * tq:(h + 1) * tq, :], o)
    o_ref[...] = o.astype(BF16)


def _band_dec(q, kc, vc, kn, vn, tab, *, b, tq, w, past, name):
    assert w % LANES == 0 and tq <= LANES
    new = pl.BlockSpec((tq, D_MODEL), lambda i: (i, 0))
    cache = pl.BlockSpec((None, D_MODEL, w), lambda i: (i, 0, 0))
    slab = pl.BlockSpec((None, D_MODEL, LANES), lambda i: (i, 0, 0))
    right = lambda x: jnp.pad(jnp.swapaxes(x.reshape(b, tq, D_MODEL), 1, 2), ((0, 0), (0, 0), (LANES - tq, 0)))
    return pl.pallas_call(
        functools.partial(_band_dec_kernel, tq=tq, w=w, past=past),
        out_shape=[jax.ShapeDtypeStruct((b * tq, D_MODEL), BF16), jax.ShapeDtypeStruct(kc.shape, F32),
                   jax.ShapeDtypeStruct(vc.shape, F32)],
        grid=(b,),
        in_specs=[new, cache, cache, new, new, slab, slab, pl.BlockSpec(tab.shape, lambda i: (0, 0))],
        out_specs=[new, cache, cache],
        compiler_params=_cp(1), name=name)(q, kc, vc, kn, vn, right(kn), right(vn), tab)


def _ffn_rows_kernel(*refs, mode, tps, seq, final, tf):
    it = iter(refs)
    h_ref, o_ref, wo_ref, gf_ref, wup_ref, cw_ref, cb_ref, wd_ref = [next(it) for _ in range(8)]
    if mode == "state":
        s1_ref, s2_ref = next(it), next(it)
    if final:
        gfin_ref = next(it)
    out_ref, ug_out, uh_out = next(it), next(it), next(it)
    if mode == "carry":
        carry_ref = next(it)
    tm = h_ref.shape[0]
    h1 = h_ref[...] + _dot(o_ref[...], wo_ref[...])
    xn = _rms(h1, gf_ref[...]).astype(BF16)
    if mode == "carry":
        @pl.when(pl.program_id(0) % tps == 0)
        def _():
            carry_ref[...] = jnp.zeros_like(carry_ref)
    else:
        tpos = _row_iota((tm, tf)) & (seq - 1)
    acts = []
    for j in range(D_FF // tf):
        cs = []
        for part, u_out in enumerate((ug_out, uh_out)):
            tile = slice(tf * j, tf * (j + 1))
            cols = slice(part * D_FF + tf * j, part * D_FF + tf * (j + 1))
            u = _dot(xn, wup_ref[:, cols])
            if mode == "carry":
                tail = u[tm - SUBLANES:]
                u_out[:, tile] = tail
                ext = jnp.concatenate([carry_ref[part, :, tile], u], axis=0)
                carry_ref[part, :, tile] = tail
                um1 = pltpu.roll(ext, 1, axis=0)[SUBLANES:]
                um2 = pltpu.roll(ext, 2, axis=0)[SUBLANES:]
            else:
                u_out[:, tile] = u
                um1 = jnp.where(tpos >= 1, pltpu.roll(u, 1, axis=0), s1_ref[:, cols])
                um2 = jnp.where(tpos >= 2, pltpu.roll(u, 2, axis=0), s2_ref[:, cols])
            cw = cw_ref[:, cols]
            cs.append(((cb_ref[:, cols] + u * cw[2:3]) + um2 * cw[0:1]) + um1 * cw[1:2])
        acts.append((cs[0] * jax.nn.sigmoid(cs[0]) * cs[1]).astype(BF16))
    acc = h1 + _dot(jnp.concatenate(acts, axis=-1), wd_ref[...])
    if final:
        acc = _rms(acc, gfin_ref[...])
    out_ref[...] = acc


def _ffn_rows(h, o, wo, gf, wup, cw, cb, wd, *, layer, tm, tf, seq, state=None, final_g=None, name):
    m = h.shape[0]
    mode = "carry" if state is None else "state"
    tps = seq // tm if mode == "carry" else 1
    row = lambda n: pl.BlockSpec((tm, n), lambda i: (i, 0))
    once = lambda a: pl.BlockSpec(a.shape, lambda i: (0,) * a.ndim, pipeline_mode=pl.Buffered(1))
    of_layer = lambda a: pl.BlockSpec((None,) + a.shape[1:], lambda i: (layer,) + (0,) * (a.ndim - 1),
                                      pipeline_mode=pl.Buffered(1))
    ins = [h, o, wo, gf, wup, cw, cb, wd]
    specs = [row(D_MODEL), row(D_MODEL), once(wo), once(gf), of_layer(wup), once(cw), once(cb), of_layer(wd)]
    scratch = []
    if mode == "state":
        assert m == tm and seq & (seq - 1) == 0
        ins += list(state)
        specs += [row(2 * D_FF), row(2 * D_FF)]
        u_shape = jax.ShapeDtypeStruct((m, D_FF), F32)
        u_spec = row(D_FF)
    else:
        assert seq % tm == 0
        scratch.append(pltpu.VMEM((2, SUBLANES, D_FF), F32))
        u_shape = jax.ShapeDtypeStruct((m // tm, SUBLANES, D_FF), F32)
        u_spec = pl.BlockSpec((None, SUBLANES, D_FF), lambda i: (i, 0, 0))
    if final_g is not None:
        ins.append(final_g)
        specs.append(once(final_g))
    kern = functools.partial(_ffn_rows_kernel, mode=mode, tps=tps, seq=seq, final=final_g is not None, tf=tf)
    return pl.pallas_call(
        kern, out_shape=[jax.ShapeDtypeStruct((m, D_MODEL), F32), u_shape, u_shape],
        grid=(m // tm,), in_specs=specs, out_specs=[row(D_MODEL), u_spec, u_spec], scratch_shapes=scratch,
        compiler_params=_cp(1), name=name)(*ins)


def _rope_tables(pos):
    inv = 1.0 / (ROPE_THETA ** (jnp.arange(0, 64, 2, dtype=F32) / 64))
    ang = pos.astype(F32)[:, None] * inv[None, :]
    c, s = jnp.cos(ang), jnp.sin(ang)
    return jnp.tile(c, (1, 4)), jnp.tile(jnp.concatenate([-s, s], axis=1), (1, 2)), c.T, s.T


def _tile_gain(g, n):
    return jnp.tile(g.astype(F32), n // g.shape[0]).reshape(1, n)


def _gain_t(g, n, w):
    return jnp.broadcast_to(jnp.tile(g.astype(F32), n // g.shape[0])[:, None], (n, w))


def _to_tokens(xt, heads):
    b, n, t = xt.shape
    dh = n // math.prod(heads)
    nd = len(heads)
    return xt.reshape(b, *heads, dh, t).transpose(0, nd + 2, *range(1, nd + 2))


def _to_features(x):
    b, t = x.shape[:2]
    nd = x.ndim
    return x.transpose(0, *range(2, nd), 1).reshape(b, -1, t)


def kernel(x_prompt, x_sample, cache_mla_ckv, cache_mla_krope, cache_fox_k, cache_fox_v, cache_fox_logf,
           cache_diff_k, cache_diff_v, cache_band_k, cache_band_v, state_ffn_conv,
           attn_norm_g, ffn_norm_g, final_norm_g,
           mla_w_dq, mla_g_q, mla_w_uq, mla_w_dkv, mla_g_kv, mla_w_uk, mla_w_uv,
           mla_g_qn, mla_g_qr, mla_g_kn, mla_g_kr, mla_w_o,
           fox_w_qkv, fox_w_f, fox_b_f, fox_g_q, fox_g_k, fox_w_o,
           diff_w_qkv, diff_g_q, diff_g_k, diff_lq1, diff_lk1, diff_lq2, diff_lk2, diff_g_sub, diff_w_o,
           band_w_qkv, band_g_q, band_g_k, band_rel_bias, band_w_o,
           ffn_w_up, ffn_conv_w, ffn_conv_b, ffn_w_down):
    bp, tp, d = x_prompt.shape
    bs, ts, _ = x_sample.shape
    past = cache_mla_ckv.shape[1]
    depth = attn_norm_g.shape[0]
    mp, ms = bp * tp, bs * ts
    tm_p = min(512, tp)
    tm_f = min(1024, tp)
    tq = 512
    tq_band = 256
    tkd = min(1024, past)
    tkw = min(2048, past)
    tf = 256
    assert tp % tq == 0 and past % tkd == 0 and past % tkw == 0 and past % CHUNK == 0

    tobf = lambda a: a.astype(BF16)
    rowv = lambda g: g.astype(F32).reshape(1, -1)
    pos_p = jnp.arange(tp, dtype=jnp.int32)
    pos_s = past + jnp.arange(ts, dtype=jnp.int32)
    tabs_p = _rope_tables(pos_p)
    tabs_s = tuple(jnp.tile(a, (bs, 1)) for a in _rope_tables(pos_s)[:2])

    nope_cols = jnp.arange(MLA_HEADS)[:, None] * (MLA_NOPE + MLA_ROPE) + jnp.arange(MLA_NOPE)[None, :]
    rope_cols = jnp.arange(MLA_HEADS)[:, None] * (MLA_NOPE + MLA_ROPE) + MLA_NOPE + jnp.arange(MLA_ROPE)[None, :]
    wuq_perm = jnp.concatenate([mla_w_uq[:, nope_cols.reshape(-1)], mla_w_uq[:, rope_cols.reshape(-1)]], axis=1)
    wr = mla_w_dkv[:, MLA_KV_LORA:]
    mla_w = dict(
        wdq=tobf(mla_w_dq), gq=rowv(mla_g_q), wuq=tobf(wuq_perm), wc=tobf(mla_w_dkv[:, :MLA_KV_LORA]),
        wr=tobf(jnp.concatenate([wr, wr], axis=1)), gkv=rowv(mla_g_kv),
        gqn=_tile_gain(mla_g_qn, MLA_HEADS * MLA_NOPE), gqr=_tile_gain(mla_g_qr, MLA_HEADS * MLA_ROPE),
        gkr=_tile_gain(mla_g_kr, LANES), wuk=tobf(mla_w_uk), wuv=tobf(mla_w_uv),
        gkn=_tile_gain(mla_g_kn, MLA_HEADS * MLA_NOPE), wrt=tobf(wr.T), gkrt=_gain_t(mla_g_kr, MLA_ROPE, tm_p))

    def split_qkv(w):
        wq, wk, wv = w[:, :d], w[:, d:2 * d], w[:, 2 * d:]
        return tobf(wq), tobf(wk.T), tobf(wv), tobf(wv.T)

    fox_wf = tobf(jnp.pad(fox_w_f, ((0, 0), (0, LANES - FOX_HEADS))))
    fox_bf = jnp.pad(fox_b_f.astype(F32), (0, LANES - FOX_HEADS)).reshape(1, LANES)
    fox_wft = tobf(fox_w_f.T)
    fox_bft = jnp.broadcast_to(fox_b_f.astype(F32)[:, None], (FOX_HEADS, tm_p))
    lamv = jnp.stack([diff_lq1, diff_lk1, diff_lq2, diff_lk2]).astype(F32)
    gsub = rowv(diff_g_sub)
    tab_pad = jnp.pad(band_rel_bias.astype(F32), ((0, 0), (0, REL_PAD - band_rel_bias.shape[1])))
    wo = [tobf(mla_w_o), tobf(fox_w_o), tobf(diff_w_o), tobf(band_w_o)]
    wup, wdn = tobf(ffn_w_up), tobf(ffn_w_down)
    cwf, cbf = ffn_conv_w.astype(F32), ffn_conv_b.astype(F32)

    h_p = x_prompt.reshape(mp, d)
    h_s = x_sample.reshape(ms, d)
    outs = {}
    conv_p, conv_s = [], []
    for i in range(depth):
        kind = i % 4
        ga = rowv(attn_norm_g[i])
        if kind == 0:
            scale = (MLA_NOPE + MLA_ROPE) ** -0.5 * LOG2E
            q_p, ckv_p, _, _, kc_p, v_p, krt_p = _mla_proj(h_p, ga, mla_w, tabs_p, b=bp, t=tp, tm=tm_p, with_kv=True,
                                                           scale=scale, name="mla_proj_p")
            o_p = _mla_attn(q_p, kc_p, v_p, b=bp, t=tp, tq=tq, hps=2, name="mla_attn_p")
            q_s, ckv_s, kr_s, krd_s = _mla_proj(h_s, ga, mla_w, tabs_s, b=bs, t=ts, tm=ms, with_kv=False,
                                                scale=scale, name="mla_proj_s")
            o_s = _mla_dec(q_s, cache_mla_ckv.astype(F32).reshape(bs * past, MLA_KV_LORA),
                           _to_features(cache_mla_krope.astype(F32)), ckv_s, krd_s,
                           tobf(mla_w_uk.T), mla_w["wuv"], b=bs, tq=ts, past=past, tk=tkd, name="mla_attn_s")
            outs["mla"] = (ckv_p.reshape(bp, tp, MLA_KV_LORA), jnp.swapaxes(krt_p, 1, 2),
                           ckv_s.reshape(bs, ts, MLA_KV_LORA), kr_s.reshape(bs, ts, MLA_ROPE))
        elif kind == 1:
            scale = FOX_DH ** -0.5 * LOG2E
            gq, gk = _tile_gain(fox_g_q, d), _tile_gain(fox_g_k, d)
            wq, wkt, wv, wvt = split_qkv(fox_w_qkv)
            q_p, kt_p, vt_p, lft_p = _qkv_proj_t(h_p, ga, wq, wkt, wvt, gq, _gain_t(fox_g_k, d, tm_p), b=bp, t=tp,
                                                 tm=tm_p, wft=fox_wft, bft=fox_bft, scale=scale, name="fox_proj_p")
            fk_p = _cumsum_last(lft_p, tk=tq, name="fox_cumsum_p")
            o_p = _pair_attn(q_p, kt_p, vt_p, (fk_p.reshape(bp, FOX_HEADS // 2, 2, tp),), mode="fox",
                             b=bp, t=tp, tq=tq, pps=4, name="fox_attn_p")
            q_s, k_s, v_s, lf_s = _qkv_proj(h_s, ga, tobf(fox_w_qkv), gq, gk, tm=ms, wf=fox_wf, bf=fox_bf,
                                            scale=scale, name="fox_proj_s")
            lf_all = jnp.concatenate([jnp.swapaxes(cache_fox_logf.astype(F32), 1, 2),
                                      jnp.swapaxes(lf_s.reshape(bs, ts, FOX_HEADS), 1, 2)], axis=2)
            lf_all = jnp.pad(lf_all, ((0, 0), (0, 0), (0, tkw - ts)))
            f_all = _cumsum_last(lf_all, tk=tq, name="fox_cumsum_s")
            o_s = _dec_attn(q_s, _to_features(cache_fox_k.astype(F32)), _to_features(cache_fox_v.astype(F32)),
                            k_s, v_s, (f_all,), mode="fox", b=bs, tq=ts, past=past, tk=tkw, name="fox_attn_s")
            sh = (FOX_HEADS, FOX_DH)
            outs["fox"] = (_to_tokens(kt_p, (FOX_HEADS,)), _to_tokens(vt_p, (FOX_HEADS,)),
                           jnp.swapaxes(lft_p, 1, 2),
                           k_s.reshape(bs, ts, *sh), v_s.reshape(bs, ts, *sh), lf_s.reshape(bs, ts, FOX_HEADS))
        elif kind == 2:
            scale = DIFF_DH ** -0.5 * LOG2E
            lam_init = 0.8 - 0.6 * math.exp(-0.3 * i)
            gq, gk = _tile_gain(diff_g_q, d), _tile_gain(diff_g_k, d)
            wq, wkt, wv, wvt = split_qkv(diff_w_qkv)
            q_p, kt_p, v4_p = _qkv_proj_t(h_p, ga, wq, wkt, wv, gq, _gain_t(diff_g_k, d, tm_p), b=bp, t=tp, tm=tm_p,
                                          rope_tabs=tabs_p, v_tokens=True, scale=scale, name="diff_proj_p")
            v_p = v4_p.reshape(bp, tp, DIFF_HEADS, 2 * DIFF_DH)
            o_p = _pair_attn(q_p, kt_p, v4_p, (lamv, gsub), mode="diff", b=bp, t=tp, tq=tq, pps=2, lam_init=lam_init,
                             name="diff_attn_p")
            q_s, k_s, v_s = _qkv_proj(h_s, ga, tobf(diff_w_qkv), gq, gk, tm=ms, rope_tabs=tabs_s, scale=scale,
                                      name="diff_proj_s")
            o_s = _dec_attn(q_s, _to_features(cache_diff_k.astype(F32)),
                            cache_diff_v.astype(F32).reshape(bs * past * DIFF_HEADS, 2 * DIFF_DH), k_s, v_s,
                            (lamv, gsub), mode="diff", b=bs, tq=ts, past=past, tk=tkw, lam_init=lam_init,
                            name="diff_attn_s")
            outs["diff"] = (_to_tokens(kt_p, (DIFF_HEADS, 2)), v_p,
                            k_s.reshape(bs, ts, DIFF_HEADS, 2, DIFF_DH), v_s.reshape(bs, ts, DIFF_HEADS, 2 * DIFF_DH))
        else:
            scale = BAND_DH ** -0.5 * LOG2E
            gq, gk = _tile_gain(band_g_q, d), _tile_gain(band_g_k, d)
            wq, wkt, wv, wvt = split_qkv(band_w_qkv)
            keep = min(BAND_LEFT, tp)
            assert keep == tm_p
            q_p, kt_p, vt_p, kt_keep, vt_keep = _qkv_proj_t(
                h_p, ga, wq, wkt, wvt, gq, _gain_t(band_g_k, d, tm_p), b=bp, t=tp, tm=tm_p, tail=True, scale=scale,
                name="band_proj_p")
            tab_pairs = jnp.pad(tab_pad.reshape(BAND_HEADS // 2, 2, REL_PAD), ((0, 0), (0, SUBLANES - 2), (0, 0)))
            o_p = _band_attn(q_p, kt_p, vt_p, tab_pairs, b=bp, t=tp, tq=tq_band, name="band_attn_p")
            q_s, k_s, v_s = _qkv_proj(h_s, ga, tobf(band_w_qkv), gq, gk, tm=ms, scale=scale, name="band_proj_s")
            w = cache_band_k.shape[1]
            kct = _to_features(cache_band_k.astype(F32))
            vct = _to_features(cache_band_v.astype(F32))
            o_s, k_roll, v_roll = _band_dec(q_s, kct, vct, k_s, v_s, tab_pad, b=bs, tq=ts, w=w, past=past,
                                            name="band_attn_s")
            outs["band"] = (_to_tokens(kt_keep, (BAND_HEADS,)), _to_tokens(vt_keep, (BAND_HEADS,)),
                            _to_tokens(k_roll, (BAND_HEADS,)), _to_tokens(v_roll, (BAND_HEADS,)))

        last = i == depth - 1
        gfin = rowv(final_norm_g) if last else None
        gfn = rowv(ffn_norm_g[i])
        h_p, ug, uh = _ffn_rows(h_p, o_p, wo[kind], gfn, wup, cwf[i], cbf[i].reshape(1, -1), wdn, layer=i, tm=tm_f,
                                tf=tf, seq=tp, final_g=gfin, name="ffn_p")
        tps = tp // tm_f
        conv_p.append(jnp.concatenate([ug, uh], axis=-1)[tps - 1::tps, SUBLANES - (CONV_W - 1):])
        st = state_ffn_conv[i].astype(F32)
        s1 = jnp.pad(st[:, 1:], ((0, 0), (0, ts - 1), (0, 0))).reshape(ms, 2 * D_FF)
        s2 = jnp.pad(st, ((0, 0), (0, ts - 2), (0, 0))).reshape(ms, 2 * D_FF)
        h_s, ug, uh = _ffn_rows(h_s, o_s, wo[kind], gfn, wup, cwf[i], cbf[i].reshape(1, -1), wdn, layer=i, tm=ms,
                                tf=tf, seq=ts, state=(s1, s2), final_g=gfin, name="ffn_s")
        u_s = jnp.concatenate([ug, uh], axis=-1).reshape(bs, ts, 2 * D_FF)
        conv_s.append(jnp.concatenate([st, u_s], axis=1)[:, ts:])

    y_prompt = h_p.reshape(bp, tp, d)
    y_sample = h_s.reshape(bs, ts, d)
    return (y_prompt, y_sample) + outs["mla"] + outs["fox"] + outs["diff"] + outs["band"] + (
        jnp.stack(conv_p, axis=0), jnp.stack(conv_s, axis=0))
```

```python
import functools
import math

import jax
import jax.numpy as jnp
from jax import lax
from jax.experimental import pallas as pl
from jax.experimental.pallas import tpu as pltpu

F32 = jnp.float32
BF16 = jnp.bfloat16

D_MODEL = 1024
CHUNK = 64
CHUNK_SHIFT = CHUNK.bit_length() - 1
ROPE_DIM = 64
ROPE_THETA = 10000.0
NORM_EPS = 1e-6
MLA_HEADS, MLA_Q_LORA, MLA_KV_LORA, MLA_NOPE, MLA_ROPE, MLA_V = 8, 384, 256, 128, 64, 128
FOX_HEADS, FOX_DH = 16, 64
DIFF_HEADS, DIFF_DH = 8, 64
BAND_HEADS, BAND_DH = 16, 64
BAND_LEFT_CHUNKS = 8
BAND_LEFT = BAND_LEFT_CHUNKS * CHUNK
REL_CLIP = 128
D_FF = 2816
CONV_W = 3

LANES = 128
SUBLANES = 8
NEG = -1e30
REL_PAD = 384
LOG2E = math.log2(math.e)
VMEM_LIMIT = 56 * 1024 * 1024


def _cp(n_axes):
    return pltpu.CompilerParams(dimension_semantics=("arbitrary",) * n_axes, vmem_limit_bytes=VMEM_LIMIT)


def _dot(a, b):
    return jnp.dot(a, b, preferred_element_type=F32)


def _dot_nt(a, b):
    return lax.dot_general(a, b, (((1,), (1,)), ((), ())), preferred_element_type=F32)


def _rms(x, g):
    ms = jnp.mean(x * x, axis=-1, keepdims=True)
    return x * lax.rsqrt(ms + NORM_EPS) * g


def _lane_iota(shape):
    return lax.broadcasted_iota(jnp.int32, shape, len(shape) - 1)


def _row_iota(shape):
    return lax.broadcasted_iota(jnp.int32, shape, len(shape) - 2)


def _log2(n):
    assert n & (n - 1) == 0, n
    return n.bit_length() - 1


def _head_rms(x, g, seg):
    n = x.shape[1]
    outs = []
    for c in range(n // LANES):
        xs = x[:, LANES * c:LANES * (c + 1)]
        sq = xs * xs
        if seg == LANES:
            r = lax.rsqrt(jnp.sum(sq, axis=-1, keepdims=True) * (1.0 / seg) + NORM_EPS)
        else:
            lo = _lane_iota(xs.shape) < seg
            s_lo = jnp.sum(jnp.where(lo, sq, 0.0), axis=-1, keepdims=True)
            s_hi = jnp.sum(jnp.where(lo, 0.0, sq), axis=-1, keepdims=True)
            r = jnp.where(lo, lax.rsqrt(s_lo * (1.0 / seg) + NORM_EPS), lax.rsqrt(s_hi * (1.0 / seg) + NORM_EPS))
        outs.append(xs * r * g[:, LANES * c:LANES * (c + 1)])
    return jnp.concatenate(outs, axis=-1) if len(outs) > 1 else outs[0]


def _head_rms_t(x, g, seg=64):
    outs = []
    for hd in range(x.shape[0] // seg):
        xs = x[seg * hd:seg * (hd + 1)]
        r = lax.rsqrt(jnp.sum(xs * xs, axis=0, keepdims=True) * (1.0 / seg) + NORM_EPS)
        outs.append(xs * r * g[seg * hd:seg * (hd + 1)])
    return jnp.concatenate(outs, axis=0) if len(outs) > 1 else outs[0]


def _rope(x, cos, sinp):
    half = ROPE_DIM // 2
    first_half = (_lane_iota((x.shape[0], LANES)) & (ROPE_DIM - 1)) < half
    outs = []
    for c in range(x.shape[1] // LANES):
        xs = x[:, LANES * c:LANES * (c + 1)]
        fwd = pltpu.roll(xs, half, axis=1)
        bwd = pltpu.roll(xs, LANES - half, axis=1)
        outs.append(xs * cos + jnp.where(first_half, bwd, fwd) * sinp)
    return jnp.concatenate(outs, axis=-1) if len(outs) > 1 else outs[0]


def _rope_t(x, cos_t, sin_t):
    outs = []
    for hd in range(x.shape[0] // 64):
        x1 = x[64 * hd:64 * hd + 32]
        x2 = x[64 * hd + 32:64 * (hd + 1)]
        outs += [x1 * cos_t - x2 * sin_t, x2 * cos_t + x1 * sin_t]
    return jnp.concatenate(outs, axis=0)


def _split3(x):
    hi = x.astype(BF16)
    r = x - hi.astype(F32)
    mid = r.astype(BF16)
    lo = (r - mid.astype(F32)).astype(BF16)
    return hi, mid, lo


def _softmax_step(s, pv, m, l, acc):
    m_new = jnp.maximum(m, jnp.max(s, axis=-1, keepdims=True))
    alpha = jnp.exp2(m - m_new)
    p = jnp.exp2(s - m_new)
    if l is not None:
        l = alpha * l + jnp.sum(p, axis=-1, keepdims=True)
    return m_new, l, alpha * acc + pv(p.astype(BF16))


def _softmax_update(s, pv, m_ref, l_ref, acc_ref):
    m, l, acc = _softmax_step(s, pv, m_ref[...], l_ref[...], acc_ref[...])
    m_ref[...] = m
    l_ref[...] = l
    acc_ref[...] = acc


def _log_sigmoid(z):
    return jnp.minimum(z, 0.0) - jnp.log1p(jnp.exp(-jnp.abs(z)))


def _qkv_proj_kernel(*refs, rope, logf, scale):
    it = iter(refs)
    h_ref, g_ref, w_ref, gq_ref, gk_ref = next(it), next(it), next(it), next(it), next(it)
    cos_ref = sin_ref = wf_ref = bf_ref = None
    if rope:
        cos_ref, sin_ref = next(it), next(it)
    if logf:
        wf_ref, bf_ref = next(it), next(it)
    q_out, k_out, v_out = next(it), next(it), next(it)
    a = _rms(h_ref[...], g_ref[...]).astype(BF16)
    qkv = _dot(a, w_ref[...])
    q = _head_rms(qkv[:, :D_MODEL], gq_ref[...], 64)
    k = _head_rms(qkv[:, D_MODEL:2 * D_MODEL], gk_ref[...], 64)
    if rope:
        q = _rope(q, cos_ref[...], sin_ref[...])
        k = _rope(k, cos_ref[...], sin_ref[...])
    q_out[...] = (q * scale).astype(BF16)
    k_out[...] = k
    v_out[...] = qkv[:, 2 * D_MODEL:]
    if logf:
        logf_out = next(it)
        lf = _log_sigmoid(_dot(a, wf_ref[...]) + bf_ref[...])
        logf_out[...] = lf[:, :FOX_HEADS]


def _qkv_proj(h, g, w, gq, gk, *, tm, rope_tabs=None, wf=None, bf=None, scale, name):
    m = h.shape[0]
    grid = (m // tm,)
    row = lambda n: pl.BlockSpec((tm, n), lambda i: (i, 0))
    full = lambda a: pl.BlockSpec(a.shape, lambda i: (0,) * a.ndim)
    ins = [h, g, w, gq, gk]
    specs = [row(D_MODEL), full(g), full(w), full(gq), full(gk)]
    if rope_tabs is not None:
        cos, sinp = rope_tabs
        nt = cos.shape[0] // tm
        tab = pl.BlockSpec((tm, LANES), lambda i: (i % nt, 0))
        ins += [cos, sinp]
        specs += [tab, tab]
    if wf is not None:
        ins += [wf, bf]
        specs += [full(wf), full(bf)]
    outs = [jax.ShapeDtypeStruct((m, D_MODEL), BF16), jax.ShapeDtypeStruct((m, D_MODEL), F32),
            jax.ShapeDtypeStruct((m, D_MODEL), F32)]
    ospecs = [row(D_MODEL), row(D_MODEL), row(D_MODEL)]
    if wf is not None:
        outs.append(jax.ShapeDtypeStruct((m, FOX_HEADS), F32))
        ospecs.append(row(FOX_HEADS))
    kern = functools.partial(_qkv_proj_kernel, rope=rope_tabs is not None, logf=wf is not None, scale=scale)
    return pl.pallas_call(kern, out_shape=outs, grid=grid, in_specs=specs, out_specs=ospecs,
                          compiler_params=_cp(1), name=name)(*ins)


def _qkv_proj_t_kernel(*refs, rope, logf, v_tokens, tail, scale):
    it = iter(refs)
    h_ref, g_ref, wq_ref, wkt_ref, wv_ref, gq_ref, gkt_ref = [next(it) for _ in range(7)]
    if rope:
        cos_ref, sin_ref, cost_ref, sint_ref = [next(it) for _ in range(4)]
    if logf:
        wft_ref, bft_ref = next(it), next(it)
    q_out, kt_out, v_out = next(it), next(it), next(it)
    a = _rms(h_ref[...], g_ref[...]).astype(BF16)
    tm = a.shape[0]
    q = _head_rms(_dot(a, wq_ref[...]), gq_ref[...], 64)
    kt = _head_rms_t(_dot_nt(wkt_ref[...], a), gkt_ref[...])
    if rope:
        q = _rope(q, cos_ref[...], sin_ref[...])
        kt = _rope_t(kt, cost_ref[...], sint_ref[...])
    q_out[...] = (q * scale).astype(BF16)
    kt_out[...] = kt
    if v_tokens:
        v = _dot(a, wv_ref[...])
        nh = D_MODEL // LANES
        for hd in range(nh):
            v_out[pl.ds(hd, tm, stride=nh), :] = v[:, LANES * hd:LANES * (hd + 1)]
    else:
        vt = _dot_nt(wv_ref[...], a)
        v_out[...] = vt
    if logf:
        lft_out = next(it)
        lft_out[...] = _log_sigmoid(_dot_nt(wft_ref[...], a) + bft_ref[...])
    if tail:
        kt_tail, vt_tail = next(it), next(it)
        kt_tail[...] = kt
        vt_tail[...] = vt


def _qkv_proj_t(h, g, wq, wkt, wv, gq, gkt, *, b, t, tm, rope_tabs=None, wft=None, bft=None, v_tokens=False,
                tail=False, scale, name):
    m = b * t
    nt = t // tm
    nh = D_MODEL // LANES
    row = lambda n: pl.BlockSpec((tm, n), lambda i: (i, 0))
    full = lambda a: pl.BlockSpec(a.shape, lambda i: (0,) * a.ndim)
    feat = lambda n: pl.BlockSpec((None, n, tm), lambda i: (i // nt, 0, i % nt))
    ins = [h, g, wq, wkt, wv, gq, gkt]
    specs = [row(D_MODEL)] + [full(x) for x in ins[1:]]
    if rope_tabs is not None:
        cos, sinp, cos_t, sin_t = rope_tabs
        ins += [cos, sinp, cos_t, sin_t]
        tab = pl.BlockSpec((tm, LANES), lambda i: (i % nt, 0))
        tab_t = pl.BlockSpec((32, tm), lambda i: (0, i % nt))
        specs += [tab, tab, tab_t, tab_t]
    if wft is not None:
        ins += [wft, bft]
        specs += [full(wft), full(bft)]
    outs = [jax.ShapeDtypeStruct((m, D_MODEL), BF16), jax.ShapeDtypeStruct((b, D_MODEL, t), F32)]
    ospecs = [row(D_MODEL), feat(D_MODEL)]
    if v_tokens:
        outs.append(jax.ShapeDtypeStruct((m * nh, LANES), F32))
        ospecs.append(pl.BlockSpec((tm * nh, LANES), lambda i: (i, 0)))
    else:
        outs.append(jax.ShapeDtypeStruct((b, D_MODEL, t), F32))
        ospecs.append(feat(D_MODEL))
    if wft is not None:
        outs.append(jax.ShapeDtypeStruct((b, FOX_HEADS, t), F32))
        ospecs.append(feat(FOX_HEADS))
    if tail:
        assert not v_tokens
        outs += [jax.ShapeDtypeStruct((b, D_MODEL, tm), F32)] * 2
        ospecs += [pl.BlockSpec((None, D_MODEL, tm), lambda i: (i // nt, 0, 0))] * 2
    kern = functools.partial(_qkv_proj_t_kernel, rope=rope_tabs is not None, logf=wft is not None,
                             v_tokens=v_tokens, tail=tail, scale=scale)
    return pl.pallas_call(kern, out_shape=outs, grid=(m // tm,), in_specs=specs, out_specs=ospecs,
                          compiler_params=_cp(1), name=name)(*ins)


def _mla_proj_kernel(*refs, with_kv, scale):
    it = iter(refs)
    (h_ref, g_ref, wdq_ref, gq_ref, wuq_ref, wc_ref, wr_ref, gkv_ref, gqn_ref, gqr_ref, gkr_ref, gkn_ref,
     cos_ref, sin_ref) = [next(it) for _ in range(14)]
    if with_kv:
        wuk_ref, wuv_ref, wrt_ref, gkrt_ref, cost_ref, sint_ref = [next(it) for _ in range(6)]
    q_out, ckv_out, kr_out, krd_out = next(it), next(it), next(it), next(it)
    a = _rms(h_ref[...], g_ref[...]).astype(BF16)
    cq = _rms(_dot(a, wdq_ref[...]), gq_ref[...]).astype(BF16)
    q = _dot(cq, wuq_ref[...])
    n_nope = MLA_HEADS * MLA_NOPE
    qn = _head_rms(q[:, :n_nope], gqn_ref[...], MLA_NOPE) * scale
    if not with_kv:
        qn = qn * gkn_ref[...]
    qr = _rope(_head_rms(q[:, n_nope:], gqr_ref[...], MLA_ROPE), cos_ref[...], sin_ref[...]) * scale
    lo = _lane_iota((q.shape[0], LANES)) < MLA_ROPE
    pieces = []
    for hd in range(MLA_HEADS):
        slab = qr[:, LANES * (hd // 2):LANES * (hd // 2 + 1)]
        keep = lo if hd % 2 == 0 else jnp.logical_not(lo)
        pieces += [qn[:, LANES * hd:LANES * (hd + 1)], jnp.where(keep, slab, 0.0)]
    q_out[...] = jnp.concatenate(pieces, axis=-1).astype(BF16)
    ckv = _rms(_dot(a, wc_ref[...]), gkv_ref[...])
    ckv_out[...] = ckv
    kr2 = _rope(_head_rms(_dot(a, wr_ref[...]), gkr_ref[...], MLA_ROPE), cos_ref[...], sin_ref[...])
    kr_out[...] = kr2[:, :MLA_ROPE]
    krd_out[...] = kr2
    if with_kv:
        kc_out, v_out, krt_out = next(it), next(it), next(it)
        c = ckv.astype(BF16)
        kn = _head_rms(_dot(c, wuk_ref[...]), gkn_ref[...], MLA_NOPE)
        pieces = []
        for hd in range(MLA_HEADS):
            pieces += [kn[:, LANES * hd:LANES * (hd + 1)], kr2]
        kc_out[...] = jnp.concatenate(pieces, axis=-1).astype(BF16)
        v_out[...] = _dot(c, wuv_ref[...]).astype(BF16)
        krt_out[...] = _rope_t(_head_rms_t(_dot_nt(wrt_ref[...], a), gkrt_ref[...]), cost_ref[...], sint_ref[...])


def _mla_proj(h, g, w, rope_tabs, *, b, t, tm, with_kv, scale, name):
    m = h.shape[0]
    cos, sinp = rope_tabs[:2]
    nt = cos.shape[0] // tm
    row = lambda n: pl.BlockSpec((tm, n), lambda i: (i, 0))
    full = lambda a: pl.BlockSpec(a.shape, lambda i: (0,) * a.ndim)
    tab = pl.BlockSpec((tm, LANES), lambda i: (i % nt, 0))
    ins = [h, g, w["wdq"], w["gq"], w["wuq"], w["wc"], w["wr"], w["gkv"], w["gqn"], w["gqr"], w["gkr"], w["gkn"],
           cos, sinp]
    specs = [row(D_MODEL)] + [full(x) for x in ins[1:12]] + [tab, tab]
    qw = MLA_HEADS * 2 * LANES
    outs = [jax.ShapeDtypeStruct((m, qw), BF16), jax.ShapeDtypeStruct((m, MLA_KV_LORA), F32),
            jax.ShapeDtypeStruct((m, MLA_ROPE), F32), jax.ShapeDtypeStruct((m, LANES), F32)]
    ospecs = [row(qw), row(MLA_KV_LORA), row(MLA_ROPE), row(LANES)]
    if with_kv:
        cos_t, sin_t = rope_tabs[2:]
        tab_t = pl.BlockSpec((32, tm), lambda i: (0, i % nt))
        extra = [w["wuk"], w["wuv"], w["wrt"], w["gkrt"]]
        ins += extra + [cos_t, sin_t]
        specs += [full(x) for x in extra] + [tab_t, tab_t]
        outs += [jax.ShapeDtypeStruct((m, qw), BF16), jax.ShapeDtypeStruct((m, MLA_HEADS * MLA_V), BF16),
                 jax.ShapeDtypeStruct((b, MLA_ROPE, t), F32)]
        ospecs += [row(qw), row(MLA_HEADS * MLA_V),
                   pl.BlockSpec((None, MLA_ROPE, tm), lambda i: (i // nt, 0, i % nt))]
    kern = functools.partial(_mla_proj_kernel, with_kv=with_kv, scale=scale)
    return pl.pallas_call(kern, out_shape=outs, grid=(m // tm,), in_specs=specs, out_specs=ospecs,
                          compiler_params=_cp(1), name=name)(*ins)


def _cumsum_kernel(x_ref, o_ref, *, tk):
    nh, t = x_ref.shape
    tri = (_row_iota((tk, tk)) <= _lane_iota((tk, tk))).astype(BF16)
    carry = jnp.zeros((nh, 1), F32)
    for c in range(t // tk):
        hi, mid, lo = _split3(x_ref[:, tk * c:tk * (c + 1)])
        y = _dot(jnp.concatenate([hi, mid, lo], axis=0), tri)
        f = (y[:nh] + y[nh:2 * nh]) + y[2 * nh:] + carry
        o_ref[:, tk * c:tk * (c + 1)] = f
        carry = f[:, tk - 1:tk]


def _cumsum_last(x, *, tk, name):
    b, nh, t = x.shape
    spec = pl.BlockSpec((None, nh, t), lambda i: (i, 0, 0))
    return pl.pallas_call(functools.partial(_cumsum_kernel, tk=tk), out_shape=jax.ShapeDtypeStruct(x.shape, F32),
                          grid=(b,), in_specs=[spec], out_specs=spec, compiler_params=_cp(1), name=name)(x)


def _mla_attn_kernel(q_ref, k_ref, v_ref, o_ref, *, tq, hps):
    kw = 2 * LANES
    t = k_ref.shape[0]
    for qi in range(t // tq):
        rows = slice(tq * qi, tq * (qi + 1))
        limit = (((qi * tq + _row_iota((tq, 1))) >> CHUNK_SHIFT) + 1) << CHUNK_SHIFT
        outs = []
        n0 = qi * tq
        for h in range(hps):
            q = q_ref[rows, kw * h:kw * (h + 1)]
            kcols = slice(kw * h, kw * (h + 1))
            vcols = slice(MLA_V * h, MLA_V * (h + 1))
            s_d = jnp.where(n0 + _lane_iota((1, tq)) < limit, _dot_nt(q, k_ref[n0:n0 + tq, kcols]), NEG)
            m = jnp.max(s_d, axis=-1, keepdims=True)
            if qi:
                s_f = _dot_nt(q, k_ref[0:n0, kcols])
                m = jnp.maximum(m, jnp.max(s_f, axis=-1, keepdims=True))
            p_d = jnp.exp2(s_d - m)
            acc = _dot(p_d.astype(BF16), v_ref[n0:n0 + tq, vcols])
            l = jnp.sum(p_d, axis=-1, keepdims=True)
            if qi:
                p_f = jnp.exp2(s_f - m)
                acc = acc + _dot(p_f.astype(BF16), v_ref[0:n0, vcols])
                l = l + jnp.sum(p_f, axis=-1, keepdims=True)
            outs.append(acc / l)
        o_ref[rows, :] = jnp.concatenate(outs, axis=-1).astype(BF16)


def _mla_attn(q, kc, v, *, b, t, tq, hps, name):
    kw = 2 * LANES * hps
    vw = MLA_V * hps
    return pl.pallas_call(
        functools.partial(_mla_attn_kernel, tq=tq, hps=hps),
        out_shape=jax.ShapeDtypeStruct((b * t, MLA_HEADS * MLA_V), BF16),
        grid=(b, MLA_HEADS // hps),
        in_specs=[pl.BlockSpec((t, kw), lambda i, h: (i, h)),
                  pl.BlockSpec((t, kw), lambda i, h: (i, h)),
                  pl.BlockSpec((t, vw), lambda i, h: (i, h))],
        out_specs=pl.BlockSpec((t, vw), lambda i, h: (i, h)),
        compiler_params=_cp(2), name=name)(q, kc, v)


def _ones_row_values(vt, c):
    row = _row_iota(vt.shape)
    if c == 0:
        return jnp.where(row < 64, vt, jnp.where(row == 64, 1.0, 0.0)).astype(BF16)
    return jnp.where(row >= 64, vt, jnp.where(row == 0, 1.0, 0.0)).astype(BF16)


def _merge_pair(acc0, acc1):
    lo = _lane_iota(acc0.shape) < 64
    return jnp.where(lo, acc0 / acc0[:, 64:65], acc1 / acc1[:, 0:1])


def _pair_attn_kernel(*refs, mode, tq, lam_init):
    if mode == "fox":
        q_all, kt_all, vt_all, fk_all, o_all, ka_all, va_all = refs
    else:
        q_all, kt_all, v_ref, lam_ref, gsub_ref, o_all, ka_all, vb_all = refs
    t = kt_all.shape[1]
    pps = kt_all.shape[0] // LANES
    for j in range(pps):
        _pair_setup(mode, j, pps, t, kt_all, ka_all, *((vt_all, fk_all, va_all) if mode == "fox" else
                                                        (v_ref, None, vb_all)))
    for qi in range(t // tq):
        for j in range(pps):
            cols = slice(LANES * j, LANES * (j + 1))
            o_all[tq * qi:tq * (qi + 1), cols] = _pair_q_tile(
                mode, qi, tq, lam_init, q_all[tq * qi:tq * (qi + 1), cols], ka_all.at[j],
                va_all.at[j] if mode == "fox" else vb_all.at[j], *(() if mode == "fox" else (lam_ref, gsub_ref)))


def _pair_setup(mode, j, pps, t, kt_all, ka_all, v_src, fk_all, v_dst):
    kt_ref = kt_all.at[LANES * j:LANES * (j + 1), :]
    ka = ka_all.at[j]
    if mode == "fox":
        fk_ref, va = fk_all.at[j], v_dst.at[j]
        vt_ref = v_src.at[LANES * j:LANES * (j + 1), :]
        ka[0:LANES, :] = kt_ref[...].astype(BF16)
        f = fk_ref[...] * (-LOG2E)
        terms = [x.astype(F32) for x in _split3(f)]
        nr = 2 * SUBLANES
        row = _row_iota((nr, t))
        aug = jnp.zeros((nr, t), F32)
        for c in range(2):
            for part in range(3):
                aug = jnp.where(row == 3 * c + part, terms[part][c:c + 1, :], aug)
        ka[LANES:LANES + nr, :] = aug.astype(BF16)
        ka[LANES + nr:, :] = jnp.zeros((LANES - nr, t), BF16)
        vt = vt_ref[...]
        va[0] = _ones_row_values(vt, 0)
        va[1] = _ones_row_values(vt, 1)
    else:
        vb = v_dst.at[j]
        ka[...] = kt_ref[...].astype(BF16)
        nhv = v_src.shape[0] // t
        vb[0:LANES, :] = v_src[pl.ds(pl.program_id(1) * pps + j, t, stride=nhv), :].T.astype(BF16)
        vb[LANES:, :] = jnp.where(_row_iota((LANES, t)) == 0, 1.0, 0.0).astype(BF16)


def _pair_q_tile(mode, qi, tq, lam_init, q, ka, vals_ref, lam_ref=None, gsub_ref=None):
    lane = _lane_iota((tq, LANES))
    zero = jnp.zeros_like(q)
    qa = [jnp.where(lane < 64, q, zero), jnp.where(lane < 64, zero, q)]
    qpos = qi * tq + _row_iota((tq, 1))
    if mode == "fox":
        limit = qpos + 1
        pick = [jnp.where(lane < 3 * c, 0.0, jnp.where(lane < 3 * c + 3, 1.0, 0.0)).astype(BF16) for c in range(2)]
        qa = [jnp.concatenate([qa[c], pick[c]], axis=-1) for c in range(2)]
    else:
        limit = ((qpos >> CHUNK_SHIFT) + 1) << CHUNK_SHIFT

    n0 = qi * tq
    accs = []
    for c in range(2):
        s_d = jnp.where(n0 + _lane_iota((1, tq)) < limit, _dot(qa[c], ka[:, n0:n0 + tq]), NEG)
        m = jnp.max(s_d, axis=-1, keepdims=True)
        if qi:
            s_f = _dot(qa[c], ka[:, 0:n0])
            m = jnp.maximum(m, jnp.max(s_f, axis=-1, keepdims=True))
        parts = [(jnp.exp2(s_d - m), n0, n0 + tq)] + ([(jnp.exp2(s_f - m), 0, n0)] if qi else [])
        acc = 0.0
        for p, lo_k, hi_k in parts:
            vals = vals_ref[c, :, lo_k:hi_k] if mode == "fox" else vals_ref[:, lo_k:hi_k]
            acc = acc + _dot_nt(p.astype(BF16), vals)
        accs.append(acc)
    if mode == "fox":
        o = _merge_pair(*accs)
    else:
        lv = lam_ref[...]
        lam = (jnp.exp(jnp.sum(lv[0:1] * lv[1:2], axis=-1, keepdims=True))
               - jnp.exp(jnp.sum(lv[2:3] * lv[3:4], axis=-1, keepdims=True)) + lam_init)
        o0, o1 = [a[:, :LANES] / a[:, LANES:LANES + 1] for a in accs]
        o = _rms(o0 - lam * o1, gsub_ref[...]) * (1.0 - lam_init)
    return o.astype(BF16)


def _pair_attn(q, kt, v, extra, *, mode, b, t, tq, pps, lam_init=0.0, name):
    npair = D_MODEL // LANES
    qspec = pl.BlockSpec((t, LANES * pps), lambda i, p: (i, p))
    ktspec = pl.BlockSpec((None, LANES * pps, t), lambda i, p: (i, p, 0))
    if mode == "fox":
        vspec = ktspec
        especs = [pl.BlockSpec((None, pps, 2, t), lambda i, p: (i, p, 0, 0))]
        scratch = [pltpu.VMEM((pps, 2 * LANES, t), BF16), pltpu.VMEM((pps, 2, LANES, t), BF16)]
    else:
        vspec = pl.BlockSpec((t * npair, LANES), lambda i, p: (i, 0))
        especs = [pl.BlockSpec(x.shape, lambda i, p: (0, 0)) for x in extra]
        scratch = [pltpu.VMEM((pps, LANES, t), BF16), pltpu.VMEM((pps, 2 * LANES, t), BF16)]
    return pl.pallas_call(
        functools.partial(_pair_attn_kernel, mode=mode, tq=tq, lam_init=lam_init),
        out_shape=jax.ShapeDtypeStruct((b * t, D_MODEL), BF16),
        grid=(b, npair // pps),
        in_specs=[qspec, ktspec, vspec] + especs,
        out_specs=qspec,
        scratch_shapes=scratch,
        compiler_params=_cp(2), name=name)(q, kt, v, *extra)


def _rel_gather(tab, width, center):
    idx = jnp.clip(center - _lane_iota((REL_PAD, width)), -REL_CLIP, REL_CLIP) + REL_CLIP
    onehot = (_row_iota((REL_PAD, width)) == idx).astype(BF16)
    hi, mid, lo = _split3(tab)
    return (_dot(hi, onehot) + _dot(mid, onehot)) + _dot(lo, onehot)


def _band_attn_kernel(q_ref, kt_ref, vt_ref, tab_ref, o_ref, kb, va, bias_ref, *, tq, win, bw):
    gw = bw + tq

    pps = kb.shape[0]

    @pl.when(pl.program_id(1) == 0)
    def _():
        ii = _row_iota((tq, bw)) >> CHUNK_SHIFT
        jj = _lane_iota((tq, bw)) >> CHUNK_SHIFT
        allowed = (jj >= ii) & (jj <= ii + BAND_LEFT_CHUNKS)
        for pj in range(pps):
            g = _rel_gather(tab_ref[pj], gw, BAND_LEFT + tq) * LOG2E
            for c in range(2):
                rows = jnp.broadcast_to(g[c:c + 1, :], (tq, gw))
                skew = pltpu.roll(rows, gw - tq, axis=1, stride=1, stride_axis=0)
                bias_ref[2 * pj + c] = jnp.where(allowed, skew[:, :bw], NEG)

    for pj in range(pps):
        kb[pj] = kt_ref[LANES * pj:LANES * (pj + 1), :].astype(BF16)
        vt = vt_ref[LANES * pj:LANES * (pj + 1), :]
        va[2 * pj] = _ones_row_values(vt, 0)
        va[2 * pj + 1] = _ones_row_values(vt, 1)

    lo = _lane_iota((tq, LANES)) < 64
    for r in range(q_ref.shape[0] // tq):
        q0 = r * tq
        ws = max(q0 - BAND_LEFT, 0)
        d = BAND_LEFT - q0 + ws
        for pj in range(pps):
            k = kb[pj, :, pl.ds(ws, win)]
            q = q_ref[tq * r:tq * (r + 1), LANES * pj:LANES * (pj + 1)]
            zero = jnp.zeros_like(q)
            accs = []
            for c in range(2):
                qc = jnp.where(lo, q, zero) if c == 0 else jnp.where(lo, zero, q)
                s = _dot(qc, k) + bias_ref[2 * pj + c, :, pl.ds(d, win)]
                p = jnp.exp2(s - jnp.max(s, axis=-1, keepdims=True))
                accs.append(_dot_nt(p.astype(BF16), va[2 * pj + c, :, pl.ds(ws, win)]))
            o_ref[tq * r:tq * (r + 1), LANES * pj:LANES * (pj + 1)] = _merge_pair(*accs).astype(BF16)


def _band_attn(q, kt, vt, tab, *, b, t, tq, name):
    npair = D_MODEL // LANES
    win = BAND_LEFT + tq
    bw = win + BAND_LEFT
    assert t >= win and tq % CHUNK == 0 and BAND_LEFT % tq == 0
    pps = 2
    qspec = pl.BlockSpec((t, LANES * pps), lambda p, i: (i, p))
    kvspec = pl.BlockSpec((None, LANES * pps, t), lambda p, i: (i, p, 0))
    return pl.pallas_call(
        functools.partial(_band_attn_kernel, tq=tq, win=win, bw=bw),
        out_shape=jax.ShapeDtypeStruct((b * t, D_MODEL), BF16),
        grid=(npair // pps, b),
        in_specs=[qspec, kvspec, kvspec, pl.BlockSpec((pps, SUBLANES, REL_PAD), lambda p, i: (p, 0, 0))],
        out_specs=qspec,
        scratch_shapes=[pltpu.VMEM((pps, LANES, t), BF16), pltpu.VMEM((2 * pps, LANES, t), BF16),
                        pltpu.VMEM((2 * pps, tq, bw), F32)],
        compiler_params=_cp(2), name=name)(q, kt, vt, tab)


def _block_diag_q(q, nh, width):
    tq = q.shape[0]
    rep = jnp.concatenate([q] * nh, axis=0)
    keep = (_row_iota(rep.shape) >> _log2(tq)) == (_lane_iota(rep.shape) >> _log2(width))
    return jnp.where(keep, rep, jnp.zeros_like(rep))


def _expand_rows(f, tq):
    return jnp.concatenate([jnp.broadcast_to(f[h:h + 1, :], (tq, f.shape[1])) for h in range(f.shape[0])], axis=0)


def _dec_attn_kernel(*refs, mode, n_cache, tq, past, lam_init):
    if mode == "fox":
        (q_ref, kc_ref, vc_ref, kn_ref, vn_ref, fkc_ref, fkn_ref, o_ref, qb, m_ref, l_ref, acc_ref) = refs
    else:
        (q_ref, kc_ref, vc_ref, kn_ref, vn_ref, lam_ref, gsub_ref, o_ref, qb, m_ref, l_ref, acc_ref) = refs
    t = pl.program_id(1)
    nh = D_MODEL // 64
    rows = nh * tq

    @pl.when(t == 0)
    def _():
        qb[...] = _block_diag_q(q_ref[...], nh, 64)
        m_ref[...] = jnp.full_like(m_ref, NEG)
        l_ref[...] = jnp.zeros_like(l_ref)
        acc_ref[...] = jnp.zeros_like(acc_ref)

    def cache_step():
        s = _dot(qb[...], kc_ref[...].astype(BF16))
        if mode == "fox":
            s = s - _expand_rows(fkc_ref[...] * LOG2E, tq)
            v = vc_ref[...].astype(BF16)
            _softmax_update(s, lambda pb: _dot_nt(pb, v), m_ref, l_ref, acc_ref)
        else:
            tk = kc_ref.shape[1]
            nhv = D_MODEL // LANES
            v = jnp.concatenate([vc_ref[pl.ds(h, tk, stride=nhv), :].astype(BF16) for h in range(nhv)], axis=-1)
            _softmax_update(s, lambda pb: _dot(pb, v), m_ref, l_ref, acc_ref)

    cache_step()

    @pl.when(t == n_cache - 1)
    def _():
        s = _dot_nt(qb[...], kn_ref[...].astype(BF16))
        qpos = past + (_row_iota((rows, tq)) & (tq - 1))
        kpos = past + _lane_iota((rows, tq))
        if mode == "fox":
            s = s - _expand_rows(fkn_ref[...][:, :tq] * LOG2E, tq)
            allowed = kpos <= qpos
        else:
            allowed = (kpos >> CHUNK_SHIFT) <= (qpos >> CHUNK_SHIFT)
        vn = vn_ref[...].astype(BF16)
        _softmax_update(jnp.where(allowed, s, NEG), lambda pb: _dot(pb, vn), m_ref, l_ref, acc_ref)
        o_all = acc_ref[...] / l_ref[...]
        if mode == "fox":
            o = jnp.zeros((tq, D_MODEL), F32)
            hl = _lane_iota((tq, D_MODEL)) >> _log2(FOX_DH)
            for h in range(nh):
                o = jnp.where(hl == h, o_all[h * tq:(h + 1) * tq, :], o)
        else:
            lv = lam_ref[...]
            lam = (jnp.exp(jnp.sum(lv[0:1] * lv[1:2], axis=-1, keepdims=True))
                   - jnp.exp(jnp.sum(lv[2:3] * lv[3:4], axis=-1, keepdims=True)) + lam_init)
            pieces = []
            for h in range(nh // 2):
                a0 = o_all[(2 * h) * tq:(2 * h + 1) * tq, LANES * h:LANES * (h + 1)]
                a1 = o_all[(2 * h + 1) * tq:(2 * h + 2) * tq, LANES * h:LANES * (h + 1)]
                pieces.append(_rms(a0 - lam * a1, gsub_ref[...]) * (1.0 - lam_init))
            o = jnp.concatenate(pieces, axis=-1)
        o_ref[...] = o.astype(BF16)


def _dec_attn(q, kc, vc, kn, vn, extra, *, mode, b, tq, past, tk, lam_init=0.0, name):
    n_cache = past // tk
    nh = D_MODEL // 64
    new = pl.BlockSpec((tq, D_MODEL), lambda i, t: (i, 0))
    cache_t = pl.BlockSpec((None, D_MODEL, tk), lambda i, t: (i, 0, t))
    if mode == "fox":
        (fk,) = extra
        vspec = cache_t
        especs = [pl.BlockSpec((None, nh, tk), lambda i, t: (i, 0, t)),
                  pl.BlockSpec((None, nh, LANES), lambda i, t: (i, 0, past // LANES))]
        ins = [fk, fk]
    else:
        nhv = D_MODEL // LANES
        vspec = pl.BlockSpec((tk * nhv, LANES), lambda i, t: (i * n_cache + t, 0))
        especs = [pl.BlockSpec(x.shape, lambda i, t: (0, 0)) for x in extra]
        ins = list(extra)
    return pl.pallas_call(
        functools.partial(_dec_attn_kernel, mode=mode, n_cache=n_cache, tq=tq, past=past, lam_init=lam_init),
        out_shape=jax.ShapeDtypeStruct((b * tq, D_MODEL), BF16),
        grid=(b, n_cache),
        in_specs=[new, cache_t, vspec, new, new] + especs,
        out_specs=new,
        scratch_shapes=[pltpu.VMEM((nh * tq, D_MODEL), BF16), pltpu.VMEM((nh * tq, 1), F32),
                        pltpu.VMEM((nh * tq, 1), F32), pltpu.VMEM((nh * tq, D_MODEL), F32)],
        compiler_params=_cp(2), name=name)(q, kc, vc, kn, vn, *ins)


def _mla_dec_kernel(q_ref, cc_ref, rc_ref, cn_ref, rn_ref, wukt_ref, wuv_ref, o_ref,
                    qn, qr, m_ref, l_ref, acc_ref, *, n_cache, tq, past):
    t = pl.program_id(1)
    nh = MLA_HEADS
    rows = nh * tq

    @pl.when(t == 0)
    def _():
        q = q_ref[...]
        zero = jnp.zeros((tq, LANES), BF16)
        for h in range(nh):
            qn[h * tq:(h + 1) * tq, :] = jnp.concatenate(
                [q[:, 2 * LANES * h:2 * LANES * h + LANES] if c == h else zero for c in range(nh)], axis=-1)
            qr[h * tq:(h + 1) * tq, :] = q[:, 2 * LANES * h + LANES:2 * LANES * (h + 1)]
        m_ref[...] = jnp.full_like(m_ref, NEG)
        l_ref[...] = jnp.zeros_like(l_ref)
        acc_ref[...] = jnp.zeros_like(acc_ref)

    def step(ckv, s_rope, mask_new):
        c = ckv.astype(BF16)
        knt = _dot_nt(wukt_ref[...], c)
        ms = jnp.concatenate([jnp.mean(jnp.square(knt[MLA_NOPE * h:MLA_NOPE * (h + 1)]), axis=0, keepdims=True)
                              for h in range(nh)], axis=0)
        s = _dot(qn[...], knt.astype(BF16)) * _expand_rows(lax.rsqrt(ms + NORM_EPS), tq) + s_rope
        if mask_new:
            n = ckv.shape[0]
            qpos = past + (_row_iota((rows, n)) & (tq - 1))
            kpos = past + _lane_iota((rows, n))
            s = jnp.where((kpos >> CHUNK_SHIFT) <= (qpos >> CHUNK_SHIFT), s, NEG)
        _softmax_update(s, lambda pb: _dot(pb, c), m_ref, l_ref, acc_ref)

    def cache_step():
        kr = rc_ref[...].astype(BF16)
        step(cc_ref[...], _dot(qr[...], jnp.concatenate([kr, kr], axis=0)), False)

    cache_step()

    @pl.when(t == n_cache - 1)
    def _():
        step(cn_ref[...], _dot_nt(qr[...], rn_ref[...].astype(BF16)), True)
        lat = (acc_ref[...] / l_ref[...]).astype(BF16)
        o_ref[...] = jnp.concatenate(
            [_dot(lat[h * tq:(h + 1) * tq, :], wuv_ref[:, MLA_V * h:MLA_V * (h + 1)]) for h in range(nh)],
            axis=-1).astype(BF16)


def _mla_dec(q, cc, rc, cn, rn, wukt, wuv, *, b, tq, past, tk, name):
    n_cache = past // tk
    nh = MLA_HEADS
    new = lambda n: pl.BlockSpec((tq, n), lambda i, t: (i, 0))
    full = lambda a: pl.BlockSpec(a.shape, lambda i, t: (0,) * a.ndim)
    return pl.pallas_call(
        functools.partial(_mla_dec_kernel, n_cache=n_cache, tq=tq, past=past),
        out_shape=jax.ShapeDtypeStruct((b * tq, nh * MLA_V), BF16),
        grid=(b, n_cache),
        in_specs=[new(nh * 2 * LANES),
                  pl.BlockSpec((tk, MLA_KV_LORA), lambda i, t: (i * n_cache + t, 0)),
                  pl.BlockSpec((None, MLA_ROPE, tk), lambda i, t: (i, 0, t)),
                  new(MLA_KV_LORA), new(LANES), full(wukt), full(wuv)],
        out_specs=new(nh * MLA_V),
        scratch_shapes=[pltpu.VMEM((nh * tq, nh * MLA_NOPE), BF16), pltpu.VMEM((nh * tq, LANES), BF16),
                        pltpu.VMEM((nh * tq, 1), F32), pltpu.VMEM((nh * tq, 1), F32),
                        pltpu.VMEM((nh * tq, MLA_KV_LORA), F32)],
        compiler_params=_cp(2), name=name)(q, cc, rc, cn, rn, wukt, wuv)


def _band_dec_kernel(q_ref, kc_ref, vc_ref, kn_ref, vn_ref, knt_ref, vnt_ref, tab_ref, o_ref, kr_out, vr_out,
                     *, tq, w, past):
    keep_old = _lane_iota((D_MODEL, LANES)) < LANES - tq
    for c_ref, nt_ref, r_out in ((kc_ref, knt_ref, kr_out), (vc_ref, vnt_ref, vr_out)):
        rolled = pltpu.roll(c_ref[...], w - tq, axis=1)
        r_out[:, :w - LANES] = rolled[:, :w - LANES]
        r_out[:, w - LANES:] = jnp.where(keep_old, rolled[:, w - LANES:], nt_ref[...])
    nh = BAND_HEADS
    rows = nh * tq
    bwid = ((w + tq + LANES - 1) // LANES) * LANES
    qb = _block_diag_q(q_ref[...], nh, BAND_DH)
    g = _rel_gather(tab_ref[...], bwid, w + tq) * LOG2E
    bias = jnp.concatenate(
        [pltpu.roll(jnp.broadcast_to(g[h:h + 1, :], (tq, bwid)), bwid - tq, axis=1, stride=1, stride_axis=0)
         for h in range(nh)], axis=0)
    qc = (past + (_row_iota((rows, bwid)) & (tq - 1))) >> CHUNK_SHIFT
    kc = (past - w + _lane_iota((rows, bwid))) >> CHUNK_SHIFT
    bias = jnp.where((kc <= qc) & (kc >= qc - BAND_LEFT_CHUNKS), bias, NEG)
    s_c = _dot(qb, kc_ref[...].astype(BF16)) + bias[:, :w]
    s_n = _dot_nt(qb, kn_ref[...].astype(BF16)) + bias[:, w:w + tq]
    m = jnp.maximum(jnp.max(s_c, axis=-1, keepdims=True), jnp.max(s_n, axis=-1, keepdims=True))
    p_c = jnp.exp2(s_c - m)
    p_n = jnp.exp2(s_n - m)
    l = jnp.sum(p_c, axis=-1, keepdims=True) + jnp.sum(p_n, axis=-1, keepdims=True)
    o_all = (_dot_nt(p_c.astype(BF16), vc_ref[...].astype(BF16))
             + _dot(p_n.astype(BF16), vn_ref[...].astype(BF16))) / l
    o = jnp.zeros((tq, D_MODEL), F32)
    hl = _lane_iota((tq, D_MODEL)) >> _log2(BAND_DH)
    for h in range(nh):
        o = jnp.where(hl == h, o_all[h * tq:(h + 1) * tq, :], o)
    o_ref[...] = o.astype(BF16)


def _band_dec(q, kc, vc, kn, vn, tab, *, b, tq, w, past, name):
    assert w % LANES == 0 and tq <= LANES
    new = pl.BlockSpec((tq, D_MODEL), lambda i: (i, 0))
    cache = pl.BlockSpec((None, D_MODEL, w), lambda i: (i, 0, 0))
    slab = pl.BlockSpec((None, D_MODEL, LANES), lambda i: (i, 0, 0))
    right = lambda x: jnp.pad(jnp.swapaxes(x.reshape(b, tq, D_MODEL), 1, 2), ((0, 0), (0, 0), (LANES - tq, 0)))
    return pl.pallas_call(
        functools.partial(_band_dec_kernel, tq=tq, w=w, past=past),
        out_shape=[jax.ShapeDtypeStruct((b * tq, D_MODEL), BF16), jax.ShapeDtypeStruct(kc.shape, F32),
                   jax.ShapeDtypeStruct(vc.shape, F32)],
        grid=(b,),
        in_specs=[new, cache, cache, new, new, slab, slab, pl.BlockSpec(tab.shape, lambda i: (0, 0))],
        out_specs=[new, cache, cache],
        compiler_params=_cp(1), name=name)(q, kc, vc, kn, vn, right(kn), right(vn), tab)


def _ffn_rows_kernel(*refs, mode, tps, seq, final, tf):
    it = iter(refs)
    h_ref, o_ref, wo_ref, gf_ref, wup_ref, cw_ref, cb_ref, wd_ref = [next(it) for _ in range(8)]
    if mode == "state":
        s1_ref, s2_ref = next(it), next(it)
    if final:
        gfin_ref = next(it)
    out_ref, ug_out, uh_out = next(it), next(it), next(it)
    if mode == "carry":
        carry_ref = next(it)
    tm = h_ref.shape[0]
    h1 = h_ref[...] + _dot(o_ref[...], wo_ref[...])
    xn = _rms(h1, gf_ref[...]).astype(BF16)
    if mode == "carry":
        @pl.when(pl.program_id(0) % tps == 0)
        def _():
            carry_ref[...] = jnp.zeros_like(carry_ref)
    else:
        tpos = _row_iota((tm, tf)) & (seq - 1)
    acts = []
    for j in range(D_FF // tf):
        cs = []
        for part, u_out in enumerate((ug_out, uh_out)):
            tile = slice(tf * j, tf * (j + 1))
            cols = slice(part * D_FF + tf * j, part * D_FF + tf * (j + 1))
            u = _dot(xn, wup_ref[:, cols])
            if mode == "carry":
                tail = u[tm - SUBLANES:]
                u_out[:, tile] = tail
                ext = jnp.concatenate([carry_ref[part, :, tile], u], axis=0)
                carry_ref[part, :, tile] = tail
                um1 = pltpu.roll(ext, 1, axis=0)[SUBLANES:]
                um2 = pltpu.roll(ext, 2, axis=0)[SUBLANES:]
            else:
                u_out[:, tile] = u
                um1 = jnp.where(tpos >= 1, pltpu.roll(u, 1, axis=0), s1_ref[:, cols])
                um2 = jnp.where(tpos >= 2, pltpu.roll(u, 2, axis=0), s2_ref[:, cols])
            cw = cw_ref[:, cols]
            cs.append(((cb_ref[:, cols] + u * cw[2:3]) + um2 * cw[0:1]) + um1 * cw[1:2])
        acts.append((cs[0] * jax.nn.sigmoid(cs[0]) * cs[1]).astype(BF16))
    acc = h1 + _dot(jnp.concatenate(acts, axis=-1), wd_ref[...])
    if final:
        acc = _rms(acc, gfin_ref[...])
    out_ref[...] = acc


def _ffn_rows(h, o, wo, gf, wup, cw, cb, wd, *, layer, tm, tf, seq, state=None, final_g=None, name):
    m = h.shape[0]
    mode = "carry" if state is None else "state"
    tps = seq // tm if mode == "carry" else 1
    row = lambda n: pl.BlockSpec((tm, n), lambda i: (i, 0))
    once = lambda a: pl.BlockSpec(a.shape, lambda i: (0,) * a.ndim, pipeline_mode=pl.Buffered(1))
    of_layer = lambda a: pl.BlockSpec((None,) + a.shape[1:], lambda i: (layer,) + (0,) * (a.ndim - 1),
                                      pipeline_mode=pl.Buffered(1))
    ins = [h, o, wo, gf, wup, cw, cb, wd]
    specs = [row(D_MODEL), row(D_MODEL), once(wo), once(gf), of_layer(wup), once(cw), once(cb), of_layer(wd)]
    scratch = []
    if mode == "state":
        assert m == tm and seq & (seq - 1) == 0
        ins += list(state)
        specs += [row(2 * D_FF), row(2 * D_FF)]
        u_shape = jax.ShapeDtypeStruct((m, D_FF), F32)
        u_spec = row(D_FF)
    else:
        assert seq % tm == 0
        scratch.append(pltpu.VMEM((2, SUBLANES, D_FF), F32))
        u_shape = jax.ShapeDtypeStruct((m // tm, SUBLANES, D_FF), F32)
        u_spec = pl.BlockSpec((None, SUBLANES, D_FF), lambda i: (i, 0, 0))
    if final_g is not None:
        ins.append(final_g)
        specs.append(once(final_g))
    kern = functools.partial(_ffn_rows_kernel, mode=mode, tps=tps, seq=seq, final=final_g is not None, tf=tf)
    return pl.pallas_call(
        kern, out_shape=[jax.ShapeDtypeStruct((m, D_MODEL), F32), u_shape, u_shape],
        grid=(m // tm,), in_specs=specs, out_specs=[row(D_MODEL), u_spec, u_spec], scratch_shapes=scratch,
        compiler_params=_cp(1), name=name)(*ins)


def _rope_tables(pos):
    inv = 1.0 / (ROPE_THETA ** (jnp.arange(0, 64, 2, dtype=F32) / 64))
    ang = pos.astype(F32)[:, None] * inv[None, :]
    c, s = jnp.cos(ang), jnp.sin(ang)
    return jnp.tile(c, (1, 4)), jnp.tile(jnp.concatenate([-s, s], axis=1), (1, 2)), c.T, s.T


def _tile_gain(g, n):
    return jnp.tile(g.astype(F32), n // g.shape[0]).reshape(1, n)


def _gain_t(g, n, w):
    return jnp.broadcast_to(jnp.tile(g.astype(F32), n // g.shape[0])[:, None], (n, w))


def _to_tokens(xt, heads):
    b, n, t = xt.shape
    dh = n // math.prod(heads)
    nd = len(heads)
    return xt.reshape(b, *heads, dh, t).transpose(0, nd + 2, *range(1, nd + 2))


def _to_features(x):
    b, t = x.shape[:2]
    nd = x.ndim
    return x.transpose(0, *range(2, nd), 1).reshape(b, -1, t)


def kernel(x_prompt, x_sample, cache_mla_ckv, cache_mla_krope, cache_fox_k, cache_fox_v, cache_fox_logf,
           cache_diff_k, cache_diff_v, cache_band_k, cache_band_v, state_ffn_conv,
           attn_norm_g, ffn_norm_g, final_norm_g,
           mla_w_dq, mla_g_q, mla_w_uq, mla_w_dkv, mla_g_kv, mla_w_uk, mla_w_uv,
           mla_g_qn, mla_g_qr, mla_g_kn, mla_g_kr, mla_w_o,
           fox_w_qkv, fox_w_f, fox_b_f, fox_g_q, fox_g_k, fox_w_o,
           diff_w_qkv, diff_g_q, diff_g_k, diff_lq1, diff_lk1, diff_lq2, diff_lk2, diff_g_sub, diff_w_o,
           band_w_qkv, band_g_q, band_g_k, band_rel_bias, band_w_o,
           ffn_w_up, ffn_conv_w, ffn_conv_b, ffn_w_down):
    bp, tp, d = x_prompt.shape
    bs, ts, _ = x_sample.shape
    past = cache_mla_ckv.shape[1]
    depth = attn_norm_g.shape[0]
    mp, ms = bp * tp, bs * ts
    tm_p = min(512, tp)
    tm_f = min(1024, tp)
    tq = 512
    tq_band = 256
    tkd = min(1024, past)
    tkw = min(2048, past)
    tf = 256
    assert tp % tq == 0 and past % tkd == 0 and past % tkw == 0 and past % CHUNK == 0

    tobf = lambda a: a.astype(BF16)
    rowv = lambda g: g.astype(F32).reshape(1, -1)
    pos_p = jnp.arange(tp, dtype=jnp.int32)
    pos_s = past + jnp.arange(ts, dtype=jnp.int32)
    tabs_p = _rope_tables(pos_p)
    tabs_s = tuple(jnp.tile(a, (bs, 1)) for a in _rope_tables(pos_s)[:2])

    nope_cols = jnp.arange(MLA_HEADS)[:, None] * (MLA_NOPE + MLA_ROPE) + jnp.arange(MLA_NOPE)[None, :]
    rope_cols = jnp.arange(MLA_HEADS)[:, None] * (MLA_NOPE + MLA_ROPE) + MLA_NOPE + jnp.arange(MLA_ROPE)[None, :]
    wuq_perm = jnp.concatenate([mla_w_uq[:, nope_cols.reshape(-1)], mla_w_uq[:, rope_cols.reshape(-1)]], axis=1)
    wr = mla_w_dkv[:, MLA_KV_LORA:]
    mla_w = dict(
        wdq=tobf(mla_w_dq), gq=rowv(mla_g_q), wuq=tobf(wuq_perm), wc=tobf(mla_w_dkv[:, :MLA_KV_LORA]),
        wr=tobf(jnp.concatenate([wr, wr], axis=1)), gkv=rowv(mla_g_kv),
        gqn=_tile_gain(mla_g_qn, MLA_HEADS * MLA_NOPE), gqr=_tile_gain(mla_g_qr, MLA_HEADS * MLA_ROPE),
        gkr=_tile_gain(mla_g_kr, LANES), wuk=tobf(mla_w_uk), wuv=tobf(mla_w_uv),
        gkn=_tile_gain(mla_g_kn, MLA_HEADS * MLA_NOPE), wrt=tobf(wr.T), gkrt=_gain_t(mla_g_kr, MLA_ROPE, tm_p))

    def split_qkv(w):
        wq, wk, wv = w[:, :d], w[:, d:2 * d], w[:, 2 * d:]
        return tobf(wq), tobf(wk.T), tobf(wv), tobf(wv.T)

    fox_wf = tobf(jnp.pad(fox_w_f, ((0, 0), (0, LANES - FOX_HEADS))))
    fox_bf = jnp.pad(fox_b_f.astype(F32), (0, LANES - FOX_HEADS)).reshape(1, LANES)
    fox_wft = tobf(fox_w_f.T)
    fox_bft = jnp.broadcast_to(fox_b_f.astype(F32)[:, None], (FOX_HEADS, tm_p))
    lamv = jnp.stack([diff_lq1, diff_lk1, diff_lq2, diff_lk2]).astype(F32)
    gsub = rowv(diff_g_sub)
    tab_pad = jnp.pad(band_rel_bias.astype(F32), ((0, 0), (0, REL_PAD - band_rel_bias.shape[1])))
    wo = [tobf(mla_w_o), tobf(fox_w_o), tobf(diff_w_o), tobf(band_w_o)]
    wup, wdn = tobf(ffn_w_up), tobf(ffn_w_down)
    cwf, cbf = ffn_conv_w.astype(F32), ffn_conv_b.astype(F32)

    h_p = x_prompt.reshape(mp, d)
    h_s = x_sample.reshape(ms, d)
    outs = {}
    conv_p, conv_s = [], []
    for i in range(depth):
        kind = i % 4
        ga = rowv(attn_norm_g[i])
        if kind == 0:
            scale = (MLA_NOPE + MLA_ROPE) ** -0.5 * LOG2E
            q_p, ckv_p, _, _, kc_p, v_p, krt_p = _mla_proj(h_p, ga, mla_w, tabs_p, b=bp, t=tp, tm=tm_p, with_kv=True,
                                                           scale=scale, name="mla_proj_p")
            o_p = _mla_attn(q_p, kc_p, v_p, b=bp, t=tp, tq=tq, hps=2, name="mla_attn_p")
            q_s, ckv_s, kr_s, krd_s = _mla_proj(h_s, ga, mla_w, tabs_s, b=bs, t=ts, tm=ms, with_kv=False,
                                                scale=scale, name="mla_proj_s")
            o_s = _mla_dec(q_s, cache_mla_ckv.astype(F32).reshape(bs * past, MLA_KV_LORA),
                           _to_features(cache_mla_krope.astype(F32)), ckv_s, krd_s,
                           tobf(mla_w_uk.T), mla_w["wuv"], b=bs, tq=ts, past=past, tk=tkd, name="mla_attn_s")
            outs["mla"] = (ckv_p.reshape(bp, tp, MLA_KV_LORA), jnp.swapaxes(krt_p, 1, 2),
                           ckv_s.reshape(bs, ts, MLA_KV_LORA), kr_s.reshape(bs, ts, MLA_ROPE))
        elif kind == 1:
            scale = FOX_DH ** -0.5 * LOG2E
            gq, gk = _tile_gain(fox_g_q, d), _tile_gain(fox_g_k, d)
            wq, wkt, wv, wvt = split_qkv(fox_w_qkv)
            q_p, kt_p, vt_p, lft_p = _qkv_proj_t(h_p, ga, wq, wkt, wvt, gq, _gain_t(fox_g_k, d, tm_p), b=bp, t=tp,
                                                 tm=tm_p, wft=fox_wft, bft=fox_bft, scale=scale, name="fox_proj_p")
            fk_p = _cumsum_last(lft_p, tk=tq, name="fox_cumsum_p")
            o_p = _pair_attn(q_p, kt_p, vt_p, (fk_p.reshape(bp, FOX_HEADS // 2, 2, tp),), mode="fox",
                             b=bp, t=tp, tq=tq, pps=4, name="fox_attn_p")
            q_s, k_s, v_s, lf_s = _qkv_proj(h_s, ga, tobf(fox_w_qkv), gq, gk, tm=ms, wf=fox_wf, bf=fox_bf,
                                            scale=scale, name="fox_proj_s")
            lf_all = jnp.concatenate([jnp.swapaxes(cache_fox_logf.astype(F32), 1, 2),
                                      jnp.swapaxes(lf_s.reshape(bs, ts, FOX_HEADS), 1, 2)], axis=2)
            lf_all = jnp.pad(lf_all, ((0, 0), (0, 0), (0, tkw - ts)))
            f_all = _cumsum_last(lf_all, tk=tq, name="fox_cumsum_s")
            o_s = _dec_attn(q_s, _to_features(cache_fox_k.astype(F32)), _to_features(cache_fox_v.astype(F32)),
                            k_s, v_s, (f_all,), mode="fox", b=bs, tq=ts, past=past, tk=tkw, name="fox_attn_s")
            sh = (FOX_HEADS, FOX_DH)
            outs["fox"] = (_to_tokens(kt_p, (FOX_HEADS,)), _to_tokens(vt_p, (FOX_HEADS,)),
                           jnp.swapaxes(lft_p, 1, 2),
                           k_s.reshape(bs, ts, *sh), v_s.reshape(bs, ts, *sh), lf_s.reshape(bs, ts, FOX_HEADS))
        elif kind == 2:
            scale = DIFF_DH ** -0.5 * LOG2E
            lam_init = 0.8 - 0.6 * math.exp(-0.3 * i)
            gq, gk = _tile_gain(diff_g_q, d), _tile_gain(diff_g_k, d)
            wq, wkt, wv, wvt = split_qkv(diff_w_qkv)
            q_p, kt_p, v4_p = _qkv_proj_t(h_p, ga, wq, wkt, wv, gq, _gain_t(diff_g_k, d, tm_p), b=bp, t=tp, tm=tm_p,
                                          rope_tabs=tabs_p, v_tokens=True, scale=scale, name="diff_proj_p")
            v_p = v4_p.reshape(bp, tp, DIFF_HEADS, 2 * DIFF_DH)
            o_p = _pair_attn(q_p, kt_p, v4_p, (lamv, gsub), mode="diff", b=bp, t=tp, tq=tq, pps=2, lam_init=lam_init,
                             name="diff_attn_p")
            q_s, k_s, v_s = _qkv_proj(h_s, ga, tobf(diff_w_qkv), gq, gk, tm=ms, rope_tabs=tabs_s, scale=scale,
                                      name="diff_proj_s")
            o_s = _dec_attn(q_s, _to_features(cache_diff_k.astype(F32)),
                            cache_diff_v.astype(F32).reshape(bs * past * DIFF_HEADS, 2 * DIFF_DH), k_s, v_s,
                            (lamv, gsub), mode="diff", b=bs, tq=ts, past=past, tk=tkw, lam_init=lam_init,
                            name="diff_attn_s")
            outs["diff"] = (_to_tokens(kt_p, (DIFF_HEADS, 2)), v_p,
                            k_s.reshape(bs, ts, DIFF_HEADS, 2, DIFF_DH), v_s.reshape(bs, ts, DIFF_HEADS, 2 * DIFF_DH))
        else:
            scale = BAND_DH ** -0.5 * LOG2E
            gq, gk = _tile_gain(band_g_q, d), _tile_gain(band_g_k, d)
            wq, wkt, wv, wvt = split_qkv(band_w_qkv)
            keep = min(BAND_LEFT, tp)
            assert keep == tm_p
            q_p, kt_p, vt_p, kt_keep, vt_keep = _qkv_proj_t(
                h_p, ga, wq, wkt, wvt, gq, _gain_t(band_g_k, d, tm_p), b=bp, t=tp, tm=tm_p, tail=True, scale=scale,
                name="band_proj_p")
            tab_pairs = jnp.pad(tab_pad.reshape(BAND_HEADS // 2, 2, REL_PAD), ((0, 0), (0, SUBLANES - 2), (0, 0)))
            o_p = _band_attn(q_p, kt_p, vt_p, tab_pairs, b=bp, t=tp, tq=tq_band, name="band_attn_p")
            q_s, k_s, v_s = _qkv_proj(h_s, ga, tobf(band_w_qkv), gq, gk, tm=ms, scale=scale, name="band_proj_s")
            w = cache_band_k.shape[1]
            kct = _to_features(cache_band_k.astype(F32))
            vct = _to_features(cache_band_v.astype(F32))
            o_s, k_roll, v_roll = _band_dec(q_s, kct, vct, k_s, v_s, tab_pad, b=bs, tq=ts, w=w, past=past,
                                            name="band_attn_s")
            outs["band"] = (_to_tokens(kt_keep, (BAND_HEADS,)), _to_tokens(vt_keep, (BAND_HEADS,)),
                            _to_tokens(k_roll, (BAND_HEADS,)), _to_tokens(v_roll, (BAND_HEADS,)))

        last = i == depth - 1
        gfin = rowv(final_norm_g) if last else None
        gfn = rowv(ffn_norm_g[i])
        h_p, ug, uh = _ffn_rows(h_p, o_p, wo[kind], gfn, wup, cwf[i], cbf[i].reshape(1, -1), wdn, layer=i, tm=tm_f,
                                tf=tf, seq=tp, final_g=gfin, name="ffn_p")
        tps = tp // tm_f
        conv_p.append(jnp.concatenate([ug, uh], axis=-1)[tps - 1::tps, SUBLANES - (CONV_W - 1):])
        st = state_ffn_conv[i].astype(F32)
        s1 = jnp.pad(st[:, 1:], ((0, 0), (0, ts - 1), (0, 0))).reshape(ms, 2 * D_FF)
        s2 = jnp.pad(st, ((0, 0), (0, ts - 2), (0, 0))).reshape(ms, 2 * D_FF)
        h_s, ug, uh = _ffn_rows(h_s, o_s, wo[kind], gfn, wup, cwf[i], cbf[i].reshape(1, -1), wdn, layer=i, tm=ms,
                                tf=tf, seq=ts, state=(s1, s2), final_g=gfin, name="ffn_s")
        u_s = jnp.concatenate([ug, uh], axis=-1).reshape(bs, ts, 2 * D_FF)
        conv_s.append(jnp.concatenate([st, u_s], axis=1)[:, ts:])

    y_prompt = h_p.reshape(bp, tp, d)
    y_sample = h_s.reshape(bs, ts, d)
    return (y_prompt, y_sample) + outs["mla"] + outs["fox"] + outs["diff"] + outs["band"] + (
        jnp.stack(conv_p, axis=0), jnp.stack(conv_s, axis=0))
```
